```python
import math
import jax, jax.numpy as jnp
from jax import lax
import numpy as np

D_MODEL = 1024
BATCH = 4
SEQ = 4096
DEPTH = 4
DEC_BATCH = 128
DEC_SEQ = 8
PAST_LEN = 2048
PAGE_SIZE = 128

N_MIXERS = 4
DEEPNORM_ALPHA = (2 * DEPTH) ** 0.25
DEEPNORM_BETA = (8 * DEPTH) ** -0.25
LN_EPS = 1e-5
NEG = -1e30

RWKV_HEAD = 64
RWKV_HEADS = D_MODEL // RWKV_HEAD
RWKV_DECAY_LORA = 64
RWKV_AAA_LORA = 64
RWKV_GATE_LORA = 128
RWKV_GN_EPS = 64e-5

NSA_HEADS = 16
NSA_KV_HEADS = 4
NSA_HEAD_DIM = D_MODEL // NSA_HEADS
NSA_GROUP = NSA_HEADS // NSA_KV_HEADS
CMP_BLK = 32
SEL_BLK = 64
SEL_RATIO = SEL_BLK // CMP_BLK
N_SEL = 16
WINDOW = 512
NSA_QBLK = 64
NSA_ROWS = 256
FORCE_SCORE = 1e4

REL_BUCKETS = 32
REL_MAX_DIST = 128

FOX_HEADS = 16
FOX_HEAD_DIM = D_MODEL // FOX_HEADS
QBLK = 128

GDN_HEADS = 8
GDN_DK = 128
GDN_DV = 128
GDN_CONV = 4
GDN_CHUNK = 64
GDN_CONV_CH = 2 * GDN_HEADS * GDN_DK + GDN_HEADS * GDN_DV

MOE_GROUPS = 4
MOE_EPG = 8
MOE_EXPERTS = MOE_GROUPS * MOE_EPG
MOE_TOPK = 2
MOE_FF = D_MODEL // 2
MOE_BLK = 256

kernel_name = 'hybrid_rwkv_nsa_fox_gdn_hmoe_step'


def layer_norm(x, g, b):
    xf = x.astype(jnp.float32)
    mu = xf.mean(-1, keepdims=True)
    var = jnp.square(xf - mu).mean(-1, keepdims=True)
    return ((xf - mu) * lax.rsqrt(var + LN_EPS) * g + b).astype(x.dtype)


def l2norm(x, eps=1e-6):
    xf = x.astype(jnp.float32)
    return xf * lax.rsqrt(jnp.sum(xf * xf, -1, keepdims=True) + eps)


def masked_softmax(logits, mask):
    p = jax.nn.softmax(jnp.where(mask, logits.astype(jnp.float32), NEG), axis=-1)
    return p * mask


def ada_mod(c, w, b):
    m = jax.nn.silu(c) @ w + b
    return jnp.split(m[:, None, :], 6, axis=-1)


def gather_pages(pool, page_table):
    g = pool[page_table]
    return g.reshape((g.shape[0], g.shape[1] * g.shape[2]) + g.shape[3:])


def t5_bucket(dist):
    exact = REL_BUCKETS // 2
    d = jnp.maximum(dist, 0)
    far = exact + (jnp.log(jnp.maximum(d, 1).astype(jnp.float32) / exact) / math.log(REL_MAX_DIST / exact) * (REL_BUCKETS - exact)).astype(jnp.int32)
    return jnp.where(d < exact, d, jnp.minimum(far, REL_BUCKETS - 1))


def _bias_qk(table, dist):
    b = table[t5_bucket(dist)].reshape(dist.shape + (NSA_KV_HEADS, NSA_GROUP))
    return jnp.transpose(b, (2, 3, 0, 1))


def _nsa_qblock(T, B):
    qb = 1
    while qb < NSA_QBLK and T % (2 * qb) == 0 and B * 2 * qb <= NSA_ROWS:
        qb *= 2
    return qb


def rwkv7_mixer(h, shift_prev, wkv0, mu, w_rkv, w0, w1, w2, a0, a1, a2, g1, g2, k_k, k_a, r_k, ln_w, ln_b, w_o):
    B, T, D = h.shape
    H, N = RWKV_HEADS, RWKV_HEAD
    f32 = jnp.float32
    x_prev = jnp.concatenate([shift_prev[:, None, :].astype(h.dtype), h[:, :-1]], axis=1)
    xx = x_prev - h
    xr, xw, xk, xv, xa, xg = [h + xx * mu[i] for i in range(6)]
    r = xr @ w_rkv[0]
    k = xk @ w_rkv[1]
    v = xv @ w_rkv[2]
    logw = -jax.nn.softplus(-(w0 + jnp.tanh(xw @ w1) @ w2).astype(f32)) - 0.5
    decay = jnp.exp(-jnp.exp(logw))
    a = jax.nn.sigmoid((a0 + (xa @ a1) @ a2).astype(f32))
    g = jax.nn.sigmoid(xg @ g1) @ g2
    heads = lambda t: t.astype(f32).reshape(B, T, H, N)
    r, k, v, decay, a = heads(r), heads(k), heads(v), heads(decay), heads(a)
    kk = l2norm(k * k_k.reshape(H, N))
    k = k * (1.0 + (a - 1.0) * k_a.reshape(H, N))

    def step(S, inp):
        r_t, w_t, k_t, v_t, kk_t, a_t = inp
        sa = jnp.einsum('bhvk,bhk->bhv', S, -kk_t)
        S = S * w_t[:, :, None, :] + sa[..., None] * (kk_t * a_t)[:, :, None, :] + v_t[..., None] * k_t[:, :, None, :]
        return S, jnp.einsum('bhvk,bhk->bhv', S, r_t)

    tm = lambda t: jnp.moveaxis(t, 1, 0)
    S, o = lax.scan(step, wkv0.astype(f32), (tm(r), tm(decay), tm(k), tm(v), tm(kk), tm(a)))
    o = jnp.moveaxis(o, 0, 1)
    mean = o.mean(-1, keepdims=True)
    var = jnp.square(o - mean).mean(-1, keepdims=True)
    o = ((o - mean) * lax.rsqrt(var + RWKV_GN_EPS)).reshape(B, T, D) * ln_w + ln_b
    bonus = jnp.sum(r * k * r_k, axis=-1, keepdims=True) * v
    o = (o + bonus.reshape(B, T, D)) * g
    return o.astype(h.dtype) @ w_o, S, h[:, -1]


def nsa_attend(q, cmp_all, slc_all, win_all, offset, w0, cmp_w1, cmp_b1, cmp_w2, table):
    B, T = q.shape[:2]
    L = cmp_all.shape[1]
    KVH, G, HD = NSA_KV_HEADS, NSA_GROUP, NSA_HEAD_DIM
    scale = HD ** -0.5
    nc = L // CMP_BLK

    def compress(raw, j):
        blocks = raw[:, :nc * CMP_BLK].reshape(B, nc, CMP_BLK, KVH, HD)
        hid = jax.nn.gelu(jnp.einsum('bnlkd,lde->bnke', blocks, cmp_w1[j]) + cmp_b1[j])
        return hid @ cmp_w2[j]

    kc, vc = compress(cmp_all[:, :, 0], 0), compress(cmp_all[:, :, 1], 1)
    cmp_end = jnp.arange(nc) * CMP_BLK + CMP_BLK - 1
    ns = -(-L // SEL_BLK)
    n_sel = min(N_SEL, ns)

    def to_blocks(raw):
        raw = jnp.pad(raw, ((0, 0), (0, ns * SEL_BLK - L), (0, 0), (0, 0)))
        return jnp.transpose(raw.reshape(B, ns, SEL_BLK, KVH, HD), (0, 3, 1, 2, 4))

    ks_blk, vs_blk = to_blocks(slc_all[:, :, 0]), to_blocks(slc_all[:, :, 1])
    kw_all, vw_all = win_all[:, :, 0], win_all[:, :, 1]
    qb = _nsa_qblock(T, B)
    nb = T // qb
    wlen = win_all.shape[1] - (nb - 1) * qb
    qg = q.reshape(B, T, KVH, G, HD)
    tab = jnp.transpose(table.reshape(REL_BUCKETS, KVH, G), (1, 2, 0))
    bar = jnp.arange(B)[:, None, None, None]
    kar = jnp.arange(KVH)[None, :, None, None]
    blk_ids = jnp.arange(ns)

    def block(i):
        qi = lax.dynamic_slice_in_dim(qg, i * qb, qb, axis=1)
        tpos = offset + i * qb + jnp.arange(qb)
        lg = jnp.einsum('bqkgd,bnkd->bkgqn', qi, kc) * scale + _bias_qk(table, tpos[:, None] - cmp_end[None, :])
        p_c = masked_softmax(lg, cmp_end[None, :] <= tpos[:, None])
        o_c = jnp.einsum('bkgqn,bnkd->bqkgd', p_c.astype(vc.dtype), vc)
        imp = jnp.pad(p_c.sum(2), ((0, 0), (0, 0), (0, 0), (0, SEL_RATIO * ns - nc)))
        imp = imp.reshape(B, KVH, qb, ns, SEL_RATIO).sum(-1)
        cur = (tpos // SEL_BLK)[:, None]
        forced = (blk_ids == 0) | (blk_ids == cur) | (blk_ids == cur - 1)
        score = jnp.where(forced, FORCE_SCORE, imp)
        score = jnp.where(blk_ids * SEL_BLK > tpos[:, None], -1.0, score)
        _, idx = lax.top_k(score, n_sel)
        kg = ks_blk[bar, kar, idx].reshape(B, KVH, qb, n_sel * SEL_BLK, HD)
        vg = vs_blk[bar, kar, idx].reshape(B, KVH, qb, n_sel * SEL_BLK, HD)
        spos = (idx[..., None] * SEL_BLK + jnp.arange(SEL_BLK)).reshape(B, KVH, qb, n_sel * SEL_BLK)
        dist_s = tpos[:, None] - spos
        lg = jnp.einsum('bqkgd,bkqsd->bkgqs', qi, kg) * scale + jnp.moveaxis(tab[kar, :, t5_bucket(dist_s)], -1, 2)
        p_s = masked_softmax(lg, (dist_s >= 0)[:, :, None])
        o_s = jnp.einsum('bkgqs,bkqsd->bqkgd', p_s.astype(vg.dtype), vg)
        kw = lax.dynamic_slice_in_dim(kw_all, i * qb, wlen, axis=1)
        vw = lax.dynamic_slice_in_dim(vw_all, i * qb, wlen, axis=1)
        wpos = w0 + i * qb + jnp.arange(wlen)
        dist_w = tpos[:, None] - wpos[None, :]
        mask_w = (dist_w >= 0) & (dist_w < WINDOW) & (wpos[None, :] >= 0)
        lg = jnp.einsum('bqkgd,bwkd->bkgqw', qi, kw) * scale + _bias_qk(table, dist_w)
        p_w = masked_softmax(lg, mask_w)
        o_w = jnp.einsum('bkgqw,bwkd->bqkgd', p_w.astype(vw.dtype), vw)
        return o_c, o_s, o_w

    o_c, o_s, o_w = lax.map(block, jnp.arange(nb))
    fold = lambda o: jnp.moveaxis(o, 0, 1).reshape(B, T, NSA_HEADS, HD)
    return fold(o_c), fold(o_s), fold(o_w)


def nsa_mixer(h, past_cmp, past_slc, win_buf, w_in, cmp_w1, cmp_b1, cmp_w2, w_o, table):
    B, T, _ = h.shape
    offset = past_cmp.shape[1]
    kvw = NSA_KV_HEADS * NSA_HEAD_DIM
    qw = NSA_HEADS * NSA_HEAD_DIM
    parts = jnp.split(h @ w_in, [qw + j * kvw for j in range(7)], axis=-1)
    q = parts[0].reshape(B, T, NSA_HEADS, NSA_HEAD_DIM)
    rows = [p.reshape(B, T, NSA_KV_HEADS, NSA_HEAD_DIM) for p in parts[1:7]]
    cmp_rows = jnp.stack(rows[0:2], axis=2)
    slc_rows = jnp.stack(rows[2:4], axis=2)
    win_rows = jnp.stack(rows[4:6], axis=2)
    gates = jax.nn.sigmoid(parts[7].astype(jnp.float32)).reshape(B, T, 3, NSA_HEADS, 1)
    cmp_all = jnp.concatenate([past_cmp.astype(h.dtype), cmp_rows], axis=1)
    slc_all = jnp.concatenate([past_slc.astype(h.dtype), slc_rows], axis=1)
    win_all = jnp.concatenate([win_buf.astype(h.dtype), win_rows], axis=1)
    o_c, o_s, o_w = nsa_attend(q, cmp_all, slc_all, win_all, offset, offset - win_buf.shape[1], cmp_w1, cmp_b1, cmp_w2, table)
    o = gates[:, :, 0] * o_c + gates[:, :, 1] * o_s + gates[:, :, 2] * o_w
    y = o.reshape(B, T, qw).astype(h.dtype) @ w_o
    keep = min(WINDOW, offset + T)
    return y, cmp_rows, slc_rows, win_all[:, win_all.shape[1] - keep:]


def fox_attend(q, k, v, cum, offset, qb):
    B, T, H, HD = q.shape
    L = k.shape[1]
    kpos = jnp.arange(L)
    cum_h = jnp.moveaxis(cum, 1, 2)
    cq = cum_h[:, :, offset:]

    def block(i):
        qi = lax.dynamic_slice_in_dim(q, i * qb, qb, axis=1)
        ci = lax.dynamic_slice_in_dim(cq, i * qb, qb, axis=2)
        tpos = offset + i * qb + jnp.arange(qb)
        lg = jnp.einsum('bqhd,bkhd->bhqk', qi, k).astype(jnp.float32) * HD ** -0.5 + ci[..., None] - cum_h[:, :, None, :]
        p = masked_softmax(lg, kpos[None, :] <= tpos[:, None])
        return jnp.einsum('bhqk,bkhd->bqhd', p.astype(v.dtype), v)

    o = lax.map(block, jnp.arange(T // qb))
    return jnp.moveaxis(o, 0, 1).reshape(B, T, H, HD)


def fox_mixer(h, past_kv, past_logf, w_in, b_f, w_o):
    B, T, _ = h.shape
    offset = past_kv.shape[1]
    hw = FOX_HEADS * FOX_HEAD_DIM
    q, k, v, fl = jnp.split(h @ w_in, [hw, 2 * hw, 3 * hw], axis=-1)
    shp = (B, T, FOX_HEADS, FOX_HEAD_DIM)
    q, k, v = q.reshape(shp), k.reshape(shp), v.reshape(shp)
    logf = jax.nn.log_sigmoid(fl.astype(jnp.float32) + b_f)
    kv_rows = jnp.stack([k, v], axis=2)
    kv_all = jnp.concatenate([past_kv.astype(h.dtype), kv_rows], axis=1)
    cum = jnp.cumsum(jnp.concatenate([past_logf.astype(jnp.float32), logf], axis=1), axis=1)
    qb = QBLK if T % QBLK == 0 else T
    o = fox_attend(q, kv_all[:, :, 0], kv_all[:, :, 1], cum, offset, qb)
    return o.reshape(B, T, hw).astype(h.dtype) @ w_o, kv_rows, logf


def gated_delta_chunked(q, k, v, g, beta, S0, chunk):
    B, T, H, DK = q.shape
    DV = v.shape[-1]
    n = T // chunk
    f32 = jnp.float32

    def chunks(t):
        t = t.astype(f32).reshape((B, n, chunk, H) + t.shape[3:])
        return jnp.moveaxis(jnp.moveaxis(t, 1, 0), 3, 2)

    qc, kc, vc, bc = chunks(q), chunks(k), chunks(v), chunks(beta)
    gc = jnp.cumsum(chunks(g), axis=-1)
    lower = jnp.tril(jnp.ones((chunk, chunk), bool))
    strict = jnp.tril(jnp.ones((chunk, chunk), bool), -1)
    gamma = jnp.exp(jnp.where(lower, gc[..., :, None] - gc[..., None, :], -jnp.inf))
    kb = kc * bc[..., None]
    a_mat = jnp.where(strict, jnp.einsum('nbhid,nbhjd->nbhij', kb, kc) * gamma, 0.0)
    m = a_mat + jnp.eye(chunk, dtype=f32)
    u = lax.linalg.triangular_solve(m, vc * bc[..., None], left_side=True, lower=True, unit_diagonal=True)
    w = lax.linalg.triangular_solve(m, kb * jnp.exp(gc)[..., None], left_side=True, lower=True, unit_diagonal=True)
    qk = jnp.einsum('nbhid,nbhjd->nbhij', qc, kc) * gamma

    def step(S, inp):
        q_i, k_i, u_i, w_i, g_i, qk_i = inp
        v_new = u_i - w_i @ S
        o = (q_i * jnp.exp(g_i)[..., None]) @ S + qk_i @ v_new
        g_last = g_i[..., -1:]
        S = S * jnp.exp(g_last)[..., None] + jnp.einsum('bhck,bhcv->bhkv', k_i * jnp.exp(g_last - g_i)[..., None], v_new)
        return S, o

    S, o = lax.scan(step, S0.astype(f32), (qc, kc, u, w, gc, qk))
    o = jnp.moveaxis(jnp.moveaxis(o, 2, 3), 0, 1).reshape(B, T, H, DV)
    return o, S


def gdn_mixer(h, conv_buf, S0, w_in, conv_w, A_log, dt_bias, norm_w, w_o):
    B, T, _ = h.shape
    H, DK, DV = GDN_HEADS, GDN_DK, GDN_DV
    cq = GDN_CONV_CH
    pre, z, b_lin, a_lin = jnp.split(h @ w_in, [cq, cq + H * DV, cq + H * DV + H], axis=-1)
    xpad = jnp.concatenate([conv_buf.astype(pre.dtype), pre], axis=1)
    conv = xpad[:, 0:T] * conv_w[0]
    for i in range(1, GDN_CONV):
        conv = conv + xpad[:, i:i + T] * conv_w[i]
    conv = jax.nn.silu(conv)
    q, k, v = jnp.split(conv, [H * DK, 2 * H * DK], axis=-1)
    q = l2norm(q.reshape(B, T, H, DK)) * DK ** -0.5
    k = l2norm(k.reshape(B, T, H, DK))
    v = v.reshape(B, T, H, DV)
    beta = jax.nn.sigmoid(b_lin.astype(jnp.float32))
    g = -jnp.exp(A_log) * jax.nn.softplus(a_lin.astype(jnp.float32) + dt_bias)
    chunk = GDN_CHUNK if T % GDN_CHUNK == 0 else T
    o, S = gated_delta_chunked(q, k, v, g, beta, S0, chunk)
    o = o * lax.rsqrt(jnp.mean(o * o, -1, keepdims=True) + 1e-6) * norm_w * jax.nn.silu(z.reshape(B, T, H, DV).astype(jnp.float32))
    y = o.reshape(B, T, H * DV).astype(h.dtype) @ w_o
    return y, S, xpad[:, xpad.shape[1] - (GDN_CONV - 1):]


def moe_ffn(h, w_group, b_group, w_router, b_router, w1, w3, w2):
    B, T, D = h.shape
    N = B * T
    xf = h.reshape(N, D)
    glogit = (xf @ w_group + b_group).astype(jnp.float32)
    gprob = jax.nn.softmax(glogit, -1)
    gsel = jnp.argmax(glogit, -1)
    gw = jnp.take_along_axis(gprob, gsel[:, None], -1)[:, 0]
    elogit = (xf @ w_router + b_router).astype(jnp.float32).reshape(N, MOE_GROUPS, MOE_EPG)
    elogit = jnp.take_along_axis(elogit, gsel[:, None, None], 1)[:, 0]
    topv, topi = lax.top_k(jax.nn.softmax(elogit, -1), MOE_TOPK)
    wts = gw[:, None] * topv / jnp.sum(topv, -1, keepdims=True)
    eid = (gsel[:, None] * MOE_EPG + topi).reshape(-1)
    tok = jnp.repeat(jnp.arange(N), MOE_TOPK)
    S = N * MOE_TOPK
    order = jnp.argsort(eid)
    e_sorted, tok_sorted, w_sorted = eid[order], tok[order], wts.reshape(-1)[order]
    counts = jnp.bincount(eid, length=MOE_EXPERTS)
    padded = (counts + MOE_BLK - 1) // MOE_BLK * MOE_BLK
    pad_end = jnp.cumsum(padded)
    pad_start = pad_end - padded
    start = jnp.cumsum(counts) - counts
    dest = pad_start[e_sorted] + jnp.arange(S) - start[e_sorted]
    nblk = -(-S // MOE_BLK) + MOE_EXPERTS
    buf = jnp.zeros((nblk * MOE_BLK, D), h.dtype).at[dest].set(xf[tok_sorted])
    blk_expert = jnp.minimum(jnp.searchsorted(pad_end, jnp.arange(nblk) * MOE_BLK, side='right'), MOE_EXPERTS - 1)

    def expert_block(args):
        xb, e = args
        return (jax.nn.silu(xb @ w1[e]) * (xb @ w3[e])) @ w2[e]

    yb = lax.map(expert_block, (buf.reshape(nblk, MOE_BLK, D), blk_expert))
    y_slot = yb.reshape(-1, D)[dest]
    out = jnp.zeros((N, D), jnp.float32).at[tok_sorted].add(w_sorted[:, None] * y_slot)
    return out.reshape(B, T, D).astype(h.dtype)


def setup_inputs(seed: int = 0) -> dict:
    key = jax.random.key(seed)
    keys = iter(jax.random.split(key, 80))

    def nrm(shape, scale):
        return jax.random.normal(next(keys), shape, jnp.float32) * scale

    def unif(shape, lo, hi):
        return jax.random.uniform(next(keys), shape, jnp.float32, lo, hi)

    D = D_MODEL
    n_pages = PAST_LEN // PAGE_SIZE
    n_pool = (DEC_BATCH * n_pages * 5) // 4
    wbuf = min(WINDOW, PAST_LEN)
    perm = jax.random.permutation(next(keys), n_pool)
    page_table = perm[:DEC_BATCH * n_pages].reshape(DEC_BATCH, n_pages).astype(jnp.int32)
    kvw = NSA_KV_HEADS * NSA_HEAD_DIM
    nsa_in = NSA_HEADS * NSA_HEAD_DIM + 6 * kvw + 3 * NSA_HEADS
    fox_in = 3 * FOX_HEADS * FOX_HEAD_DIM + FOX_HEADS
    gdn_in = GDN_CONV_CH + GDN_HEADS * GDN_DV + 2 * GDN_HEADS
    dt = jnp.exp(unif((GDN_HEADS,), math.log(1e-3), math.log(1e-1)))
    out_s = D ** -0.5 * DEEPNORM_BETA
    hd = NSA_HEAD_DIM
    return {
        'x_prompt': nrm((BATCH, SEQ, D), 1.0),
        'x_sample': nrm((DEC_BATCH, DEC_SEQ, D), 1.0),
        'state_rwkv_wkv': nrm((DEC_BATCH, RWKV_HEADS, RWKV_HEAD, RWKV_HEAD), 0.3),
        'state_rwkv_shift': nrm((DEC_BATCH, D), 1.0),
        'cache_nsa_cmp': nrm((n_pool, PAGE_SIZE, 2, NSA_KV_HEADS, hd), 1.0),
        'cache_nsa_slc': nrm((n_pool, PAGE_SIZE, 2, NSA_KV_HEADS, hd), 1.0),
        'cache_nsa_win': nrm((DEC_BATCH, wbuf, 2, NSA_KV_HEADS, hd), 1.0),
        'cache_fox_kv': nrm((n_pool, PAGE_SIZE, 2, FOX_HEADS, FOX_HEAD_DIM), 1.0),
        'cache_fox_logf': jax.nn.log_sigmoid(nrm((n_pool, PAGE_SIZE, FOX_HEADS), 1.0) + 3.0),
        'state_gdn_S': nrm((DEC_BATCH, GDN_HEADS, GDN_DK, GDN_DV), 0.1),
        'state_gdn_conv': nrm((DEC_BATCH, GDN_CONV - 1, GDN_CONV_CH), 1.0),
        'page_table': page_table,
        'c_prompt': nrm((BATCH, D), 1.0),
        'c_sample': nrm((DEC_BATCH, D), 1.0),
        'w_mod': nrm((DEPTH, D, 6 * D), 0.3 * D ** -0.5),
        'b_mod': nrm((DEPTH, 6 * D), 0.02),
        'ln_g': 1.0 + nrm((DEPTH, 2, D), 0.02),
        'ln_b': nrm((DEPTH, 2, D), 0.02),
        'moe_w_group': nrm((DEPTH, D, MOE_GROUPS), D ** -0.5),
        'moe_b_group': nrm((DEPTH, MOE_GROUPS), 0.01),
        'moe_w_router': nrm((DEPTH, D, MOE_EXPERTS), D ** -0.5),
        'moe_b_router': nrm((DEPTH, MOE_EXPERTS), 0.01),
        'moe_w1': nrm((DEPTH, MOE_EXPERTS, D, MOE_FF), D ** -0.5),
        'moe_w3': nrm((DEPTH, MOE_EXPERTS, D, MOE_FF), D ** -0.5),
        'moe_w2': nrm((DEPTH, MOE_EXPERTS, MOE_FF, D), MOE_FF ** -0.5 * DEEPNORM_BETA),
        'rwkv_mu': unif((6, D), 0.0, 1.0),
        'rwkv_w_rkv': nrm((3, D, D), D ** -0.5),
        'rwkv_w0': unif((D,), -6.0, -1.0),
        'rwkv_w1': nrm((D, RWKV_DECAY_LORA), D ** -0.5),
        'rwkv_w2': nrm((RWKV_DECAY_LORA, D), 0.3 * RWKV_DECAY_LORA ** -0.5),
        'rwkv_a0': nrm((D,), 0.1),
        'rwkv_a1': nrm((D, RWKV_AAA_LORA), D ** -0.5),
        'rwkv_a2': nrm((RWKV_AAA_LORA, D), RWKV_AAA_LORA ** -0.5),
        'rwkv_g1': nrm((D, RWKV_GATE_LORA), D ** -0.5),
        'rwkv_g2': nrm((RWKV_GATE_LORA, D), RWKV_GATE_LORA ** -0.5),
        'rwkv_k_k': 0.85 + nrm((D,), 0.05),
        'rwkv_k_a': 1.0 + nrm((D,), 0.05),
        'rwkv_r_k': nrm((RWKV_HEADS, RWKV_HEAD), 0.1),
        'rwkv_ln_w': 1.0 + nrm((D,), 0.02),
        'rwkv_ln_b': nrm((D,), 0.02),
        'rwkv_w_o': nrm((D, D), out_s),
        'nsa_w_in': nrm((D, nsa_in), D ** -0.5),
        'nsa_cmp_w1': nrm((2, CMP_BLK, hd, hd), (CMP_BLK * hd) ** -0.5),
        'nsa_cmp_b1': nrm((2, hd), 0.02),
        'nsa_cmp_w2': nrm((2, hd, hd), hd ** -0.5),
        'nsa_w_o': nrm((NSA_HEADS * hd, D), out_s),
        'rel_bias': nrm((REL_BUCKETS, NSA_HEADS), 0.1),
        'fox_w_in': nrm((D, fox_in), D ** -0.5),
        'fox_b_f': unif((FOX_HEADS,), 1.0, 4.0),
        'fox_w_o': nrm((FOX_HEADS * FOX_HEAD_DIM, D), out_s),
        'gdn_w_in': nrm((D, gdn_in), D ** -0.5),
        'gdn_conv_w': nrm((GDN_CONV, GDN_CONV_CH), GDN_CONV ** -0.5),
        'gdn_A_log': jnp.log(unif((GDN_HEADS,), 1.0, 16.0)),
        'gdn_dt_bias': dt + jnp.log(-jnp.expm1(-dt)),
        'gdn_norm_w': 1.0 + nrm((GDN_DV,), 0.02),
        'gdn_w_o': nrm((GDN_HEADS * GDN_DV, D), out_s),
    }


def reference(x_prompt, x_sample, state_rwkv_wkv, state_rwkv_shift, cache_nsa_cmp, cache_nsa_slc, cache_nsa_win, cache_fox_kv, cache_fox_logf, state_gdn_S, state_gdn_conv, page_table, c_prompt, c_sample, w_mod, b_mod, ln_g, ln_b, moe_w_group, moe_b_group, moe_w_router, moe_b_router, moe_w1, moe_w3, moe_w2, rwkv_mu, rwkv_w_rkv, rwkv_w0, rwkv_w1, rwkv_w2, rwkv_a0, rwkv_a1, rwkv_a2, rwkv_g1, rwkv_g2, rwkv_k_k, rwkv_k_a, rwkv_r_k, rwkv_ln_w, rwkv_ln_b, rwkv_w_o, nsa_w_in, nsa_cmp_w1, nsa_cmp_b1, nsa_cmp_w2, nsa_w_o, rel_bias, fox_w_in, fox_b_f, fox_w_o, gdn_w_in, gdn_conv_w, gdn_A_log, gdn_dt_bias, gdn_norm_w, gdn_w_o):
    Bp = x_prompt.shape[0]
    xp, xs = x_prompt, x_sample
    rw = (rwkv_mu, rwkv_w_rkv, rwkv_w0, rwkv_w1, rwkv_w2, rwkv_a0, rwkv_a1, rwkv_a2, rwkv_g1, rwkv_g2, rwkv_k_k, rwkv_k_a, rwkv_r_k, rwkv_ln_w, rwkv_ln_b, rwkv_w_o)
    nsa_p = (nsa_w_in, nsa_cmp_w1, nsa_cmp_b1, nsa_cmp_w2, nsa_w_o, rel_bias)
    gdn_p = (gdn_w_in, gdn_conv_w, gdn_A_log, gdn_dt_bias, gdn_norm_w, gdn_w_o)
    for layer in range(DEPTH):
        kind = layer % N_MIXERS
        mp = ada_mod(c_prompt, w_mod[layer], b_mod[layer])
        ms = ada_mod(c_sample, w_mod[layer], b_mod[layer])
        hp = xp * (1.0 + mp[1]) + mp[0]
        hs = xs * (1.0 + ms[1]) + ms[0]
        if kind == 0:
            yp, rwkv_wkv_p, rwkv_shift_p = rwkv7_mixer(hp, jnp.zeros((Bp, D_MODEL), hp.dtype), jnp.zeros((Bp, RWKV_HEADS, RWKV_HEAD, RWKV_HEAD), jnp.float32), *rw)
            ys, rwkv_wkv_s, rwkv_shift_s = rwkv7_mixer(hs, state_rwkv_shift, state_rwkv_wkv, *rw)
        elif kind == 1:
            empty = jnp.zeros((Bp, 0, 2, NSA_KV_HEADS, NSA_HEAD_DIM), hp.dtype)
            win0 = jnp.zeros((Bp, WINDOW, 2, NSA_KV_HEADS, NSA_HEAD_DIM), hp.dtype)
            yp, nsa_cmp_p, nsa_slc_p, nsa_win_p = nsa_mixer(hp, empty, empty, win0, *nsa_p)
            ys, nsa_cmp_s, nsa_slc_s, nsa_win_s = nsa_mixer(hs, gather_pages(cache_nsa_cmp, page_table), gather_pages(cache_nsa_slc, page_table), cache_nsa_win, *nsa_p)
        elif kind == 2:
            yp, fox_kv_p, fox_logf_p = fox_mixer(hp, jnp.zeros((Bp, 0, 2, FOX_HEADS, FOX_HEAD_DIM), hp.dtype), jnp.zeros((Bp, 0, FOX_HEADS), jnp.float32), fox_w_in, fox_b_f, fox_w_o)
            ys, fox_kv_s, fox_logf_s = fox_mixer(hs, gather_pages(cache_fox_kv, page_table), gather_pages(cache_fox_logf, page_table), fox_w_in, fox_b_f, fox_w_o)
        else:
            yp, gdn_S_p, gdn_conv_p = gdn_mixer(hp, jnp.zeros((Bp, GDN_CONV - 1, GDN_CONV_CH), hp.dtype), jnp.zeros((Bp, GDN_HEADS, GDN_DK, GDN_DV), jnp.float32), *gdn_p)
            ys, gdn_S_s, gdn_conv_s = gdn_mixer(hs, state_gdn_conv, state_gdn_S, *gdn_p)
        xp = layer_norm(DEEPNORM_ALPHA * xp + (1.0 + mp[2]) * yp, ln_g[layer, 0], ln_b[layer, 0])
        xs = layer_norm(DEEPNORM_ALPHA * xs + (1.0 + ms[2]) * ys, ln_g[layer, 0], ln_b[layer, 0])
        moe_p = (moe_w_group[layer], moe_b_group[layer], moe_w_router[layer], moe_b_router[layer], moe_w1[layer], moe_w3[layer], moe_w2[layer])
        hp = xp * (1.0 + mp[4]) + mp[3]
        hs = xs * (1.0 + ms[4]) + ms[3]
        xp = layer_norm(DEEPNORM_ALPHA * xp + (1.0 + mp[5]) * moe_ffn(hp, *moe_p), ln_g[layer, 1], ln_b[layer, 1])
        xs = layer_norm(DEEPNORM_ALPHA * xs + (1.0 + ms[5]) * moe_ffn(hs, *moe_p), ln_g[layer, 1], ln_b[layer, 1])
    return (xp, xs, rwkv_wkv_p, rwkv_wkv_s, rwkv_shift_p, rwkv_shift_s, nsa_cmp_p, nsa_cmp_s, nsa_slc_p, nsa_slc_s, nsa_win_p, nsa_win_s, fox_kv_p, fox_kv_s, fox_logf_p, fox_logf_s, gdn_S_p, gdn_S_s, gdn_conv_p, gdn_conv_s)
```

```python
import functools
import math

import numpy as np
import jax
import jax.numpy as jnp
from jax import lax
from jax.experimental import pallas as pl
from jax.experimental.pallas import tpu as pltpu

F32 = jnp.float32
BF16 = jnp.bfloat16
I32 = jnp.int32
NEG = -1e30
LN_EPS = 1e-5
D = 1024
LANES = 128
SUBLANES = 8
MXU_TILE = 256
VMEM_LIMIT_MB = 56

RWKV_HSZ = 64
RWKV_GN_EPS = 64e-5
NSA_HEADS, NSA_KVH, HD = 16, 4, 64
NSA_G = NSA_HEADS // NSA_KVH
CMP_BLK, SEL_BLK, N_SEL, WINDOW = 32, 64, 16, 512
FORCE_SCORE = 1e4
REL_BUCKETS, REL_MAX_DIST = 32, 128
FOX_HEADS = 16
GDN_HEADS, GDN_HSZ, GDN_CONV = 8, 128, 4
MOE_GROUPS, MOE_EPG, MOE_BLK = 4, 8, 256
MOE_EXPERTS = MOE_GROUPS * MOE_EPG
ATT_T = 128
SEL_NEG = -65536.0
PAGES_PER_STEP = 4


def _d(a, b):
    return jnp.dot(a.astype(BF16), b.astype(BF16), preferred_element_type=F32)


def _d_nt(a, b):
    return lax.dot_general(a.astype(BF16), b.astype(BF16), (((1,), (1,)), ((), ())),
                           preferred_element_type=F32)


def _split3(x):
    h = x.astype(BF16)
    r1 = x - h.astype(F32)
    m = r1.astype(BF16)
    l = (r1 - m.astype(F32)).astype(BF16)
    return h, m, l


def _d_x3(x, sel):
    h, m, l = _split3(x)
    return _d(h, sel) + _d(m, sel) + _d(l, sel)


def _d_3x(sel, x):
    h, m, l = _split3(x)
    return _d(sel, h) + _d(sel, m) + _d(sel, l)


def _d_f32(x, w):
    xh, xm, xl = _split3(x)
    wh, wm, wl = _split3(w)
    return (_d(xh, wh) + _d(xh, wm) + _d(xm, wh)) + (_d(xh, wl) + _d(xl, wh) + _d(xm, wm))


def _sigmoid(x):
    return 1.0 / (1.0 + jnp.exp(-x))


def _softplus(x):
    return jnp.maximum(x, 0.0) + jnp.log(1.0 + jnp.exp(-jnp.abs(x)))


def _silu(x):
    return x * _sigmoid(x)


def _gelu_tanh(x):
    return 0.5 * x * (1.0 + jnp.tanh(math.sqrt(2.0 / math.pi) * (x + 0.044715 * (x * x * x))))


def _layer_norm(z, g, b):
    mu = jnp.mean(z, axis=-1, keepdims=True)
    zc = z - mu
    var = jnp.mean(zc * zc, axis=-1, keepdims=True)
    return zc * lax.rsqrt(var + LN_EPS) * g + b


def _modulate(x, sc, sh):
    return x * (1.0 + sc) + sh


def _res_ln(alpha, y, xres, gate, g, b):
    return _layer_norm(alpha * xres + (1.0 + gate) * y, g, b)


@functools.lru_cache(maxsize=None)
def _seg_np(hsz):
    head = np.arange(D) // hsz
    hs = (head[:, None] == np.arange(LANES)[None, :]).astype(np.float32)
    return hs


def _seg_consts(hsz):
    hs = _seg_np(hsz)
    return jnp.asarray(hs, BF16), jnp.asarray(hs.T, BF16)


def _scan_consts(hsz):
    nh = D // hsz
    head = np.arange(D) // hsz
    slot_j = np.arange(LANES) // 16
    slot_h = np.arange(LANES) % 16
    hexp = np.zeros((SUBLANES, LANES, D), np.float32)
    for j in range(SUBLANES):
        hexp[j] = ((slot_j[:, None] == j) & (slot_h[:, None] == head[None, :]) & (slot_h[:, None] < nh))
    hsum = np.transpose(hexp, (0, 2, 1))
    blk = np.arange(MXU_TILE) // hsz
    bd = (blk[:, None] == blk[None, :]).astype(np.float32)
    return jnp.asarray(hexp, BF16), jnp.asarray(hsum, BF16), jnp.asarray(bd, BF16)


def _call(body, grid, ins, outs, scratch=(), name=None, prefetch=None, aliases=None):
    arrays = [a for a, _ in ins]
    in_specs = [s for _, s in ins]
    out_shape = [jax.ShapeDtypeStruct(s, d) for s, d, _ in outs]
    out_specs = [s for _, _, s in outs]
    params = pltpu.CompilerParams(dimension_semantics=("arbitrary",) * len(grid),
                                  vmem_limit_bytes=VMEM_LIMIT_MB << 20)
    kw = {}
    if aliases:
        kw["input_output_aliases"] = aliases
    if prefetch is None:
        fn = pl.pallas_call(body, grid=grid, in_specs=in_specs, out_specs=out_specs, out_shape=out_shape,
                            scratch_shapes=list(scratch), compiler_params=params, name=name, **kw)
        res = fn(*arrays)
    else:
        gs = pltpu.PrefetchScalarGridSpec(num_scalar_prefetch=len(prefetch), grid=grid, in_specs=in_specs,
                                          out_specs=out_specs, scratch_shapes=list(scratch))
        fn = pl.pallas_call(body, grid_spec=gs, out_shape=out_shape, compiler_params=params, name=name, **kw)
        res = fn(*prefetch, *arrays)
    return list(res)


def _full(a):
    nd = a.ndim
    return (a, pl.BlockSpec(a.shape, lambda *_: (0,) * nd))


def _rows(a, tm):
    return (a, pl.BlockSpec((tm, a.shape[1]), lambda i, *_: (i, 0)))


def _rows_out(M, C, tm, dtype=F32):
    return ((M, C), dtype, pl.BlockSpec((tm, C), lambda i, *_: (i, 0)))


class _Stream:
    def __init__(self, x, B, T, tm):
        self.x, self.B, self.T, self.tm = x, B, T, tm
        self.M = B * T
        self.m6 = None

    def set_mods(self, m6):
        self.m6 = m6
        self.rep = jnp.repeat(m6, self.T, axis=0) if self.T < self.tm else None

    def mod(self, c, tm=None):
        tm = tm or self.tm
        if self.T % tm == 0:
            tpb = self.T // tm
            a = self.m6[:, c * D:(c + 1) * D].reshape(self.B, 1, D)
            return (a, pl.BlockSpec((None, 1, D), lambda i, *_: (i // tpb, 0, 0)))
        assert tm % self.T == 0 and self.M % tm == 0
        a = self.rep[:, c * D:(c + 1) * D].reshape(self.M // tm, tm, D)
        return (a, pl.BlockSpec((None, tm, D), lambda i, *_: (i, 0, 0)))


def _mm(x, w_in, *, tm, pro=None, pro_ins=(), epi=None, epi_ins=(), outs=None, name="mm"):
    M, K = x.shape
    n_pro, n_epi = len(pro_ins), len(epi_ins)
    n_out = len(outs)

    def body(*refs):
        x_ref = refs[0]
        pro_refs = refs[1:1 + n_pro]
        w_ref = refs[1 + n_pro]
        epi_refs = refs[2 + n_pro:2 + n_pro + n_epi]
        out_refs = refs[2 + n_pro + n_epi:2 + n_pro + n_epi + n_out]
        a = x_ref[...]
        if pro is not None:
            a = pro(a, *[r[...] for r in pro_refs])
        acc = _d(a, w_ref[...])
        res = epi(acc, *[r[...] for r in epi_refs]) if epi is not None else (acc,)
        for o, r in zip(out_refs, res):
            o[...] = r.astype(o.dtype)

    ins = [_rows(x, tm)] + list(pro_ins) + [w_in] + list(epi_ins)
    return _call(body, (M // tm,), ins, outs, name=name)


def _ada(c_all, w_mod, b_mod, layer):
    Mp = c_all.shape[0]
    N = w_mod.shape[2]
    tn = 1536

    def body(c_ref, w_ref, b_ref, o_ref):
        o_ref[...] = _d(_silu(c_ref[...]), w_ref[...]) + b_ref[...]

    ins = [(c_all, pl.BlockSpec((Mp, D), lambda j: (0, 0))),
           (w_mod, pl.BlockSpec((None, D, tn), lambda j: (layer, 0, j))),
           (b_mod.reshape(b_mod.shape[0], 1, N), pl.BlockSpec((None, 1, tn), lambda j: (layer, 0, j)))]
    outs = [((Mp, N), F32, pl.BlockSpec((Mp, tn), lambda j: (0, j)))]
    return _call(body, (N // tn,), ins, outs, name="ada_mod")[0]


def _pick_tile(*sizes, cap=512):
    t = cap
    while t > SUBLANES and any(s % t for s in sizes):
        t //= 2
    assert all(s % t == 0 for s in sizes), sizes
    return t


def _route(lg):
    lane = lax.broadcasted_iota(I32, lg.shape, 1)
    big = jnp.int32(1 << 20)
    isg = lane < MOE_GROUPS
    gl = jnp.where(isg, lg, NEG)
    gmax = jnp.max(gl, axis=-1, keepdims=True)
    gsel = jnp.min(jnp.where(gl == gmax, lane, big), axis=-1, keepdims=True)
    gsum = jnp.sum(jnp.where(isg, jnp.exp(gl - gmax), 0.0), axis=-1, keepdims=True)
    gw = 1.0 / gsum
    lo = MOE_GROUPS + MOE_EPG * gsel
    ise = (lane >= lo) & (lane < lo + MOE_EPG)
    el = jnp.where(ise, lg, NEG)
    emax = jnp.max(el, axis=-1, keepdims=True)
    ep = jnp.where(ise, jnp.exp(el - emax), 0.0)
    prob = ep / jnp.sum(ep, axis=-1, keepdims=True)
    pm = jnp.where(ise, prob, -1.0)
    p1 = jnp.max(pm, axis=-1, keepdims=True)
    i1 = jnp.min(jnp.where(pm == p1, lane, big), axis=-1, keepdims=True)
    pm2 = jnp.where(lane == i1, -1.0, pm)
    p2 = jnp.max(pm2, axis=-1, keepdims=True)
    i2 = jnp.min(jnp.where(pm2 == p2, lane, big), axis=-1, keepdims=True)
    den = p1 + p2
    w1 = gw * p1 / den
    w2 = gw * p2 / den
    e1 = (i1 - MOE_GROUPS).astype(F32)
    e2 = (i2 - MOE_GROUPS).astype(F32)
    return jnp.where(lane == 0, e1, jnp.where(lane == 1, e2, jnp.where(lane == 2, w1, jnp.where(lane == 3, w2, 0.0))))


def _moe_router(st, wgr, bgr):
    tm = st.tm

    def body(x_ref, sc_ref, sh_ref, w_ref, b_ref, h_ref, r_ref):
        h = _modulate(x_ref[...], sc_ref[...], sh_ref[...])
        h_ref[...] = h
        r_ref[...] = _route(_d_f32(h, w_ref[...]) + b_ref[...])

    ins = [_rows(st.x, tm), st.mod(4), st.mod(3), _full(wgr), _full(bgr)]
    outs = [_rows_out(st.M, D, tm), _rows_out(st.M, LANES, tm)]
    return _call(body, (st.M // tm,), ins, outs, name="moe_router")


def _moe_counts(rinfo, R):
    Mtot = rinfo.shape[0]
    nt = Mtot // R

    def body(r_ref, o_ref):
        j = pl.program_id(0)
        t = pl.program_id(1)

        @pl.when((j == 0) & (t == 0))
        def _():
            o_ref[...] = jnp.zeros_like(o_ref)

        xt = r_ref[...].T
        row = jnp.where(j == 0, xt[0:1, :], xt[1:2, :])
        sub = lax.broadcasted_iota(I32, (LANES, R), 0).astype(F32)
        oh = jnp.where(sub == row, 1.0, 0.0)
        o_ref[...] += jnp.sum(oh, axis=1, keepdims=True)

    ins = [(rinfo, pl.BlockSpec((R, LANES), lambda j, t: (t, 0)))]
    outs = [((LANES, LANES), F32, pl.BlockSpec((LANES, LANES), lambda j, t: (0, 0)))]
    return _call(body, (2, nt), ins, outs, name="moe_counts")[0]


def _moe_dest(rinfo, pstart, R):
    Mtot = rinfo.shape[0]
    nt = Mtot // R
    upper = jnp.asarray(np.triu(np.ones((R, R), np.float32), 1), BF16)

    def body(r_ref, p_ref, u_ref, o_ref, carry):
        j = pl.program_id(0)
        t = pl.program_id(1)

        @pl.when((j == 0) & (t == 0))
        def _():
            carry[...] = jnp.zeros_like(carry)

        xt = r_ref[...].T
        row = jnp.where(j == 0, xt[0:1, :], xt[1:2, :])
        sub = lax.broadcasted_iota(I32, (LANES, R), 0).astype(F32)
        oh = jnp.where(sub == row, 1.0, 0.0)
        cum = _d(oh, u_ref[...])
        base = carry[:, 0:1] + p_ref[:, 0:1]
        dest = jnp.sum(oh * (cum + base), axis=0, keepdims=True)
        o_ref[...] = dest.astype(I32)
        carry[...] += jnp.sum(oh, axis=1, keepdims=True)

    ins = [(rinfo, pl.BlockSpec((R, LANES), lambda j, t: (t, 0))), _full(pstart), _full(upper)]
    outs = [((2 * nt, 1, R), I32, pl.BlockSpec((None, 1, R), lambda j, t: (j * nt + t, 0, 0)))]
    return _call(body, (2, nt), ins, outs, scratch=[pltpu.VMEM((LANES, LANES), F32)], name="moe_dest")[0]


def _row_copy_body(scatter, Mtot, R, idx_ref, src_ref, *rest):
    dst_ref, sem = rest[-2], rest[-1]
    base = pl.program_id(0) * R

    def copy(s):
        tok = jnp.where(s >= Mtot, s - Mtot, s)
        if scatter:
            return pltpu.make_async_copy(src_ref.at[pl.ds(tok, 1)], dst_ref.at[pl.ds(idx_ref[s], 1)], sem)
        return pltpu.make_async_copy(src_ref.at[pl.ds(idx_ref[s], 1)], dst_ref.at[pl.ds(s, 1)], sem)

    def issue(r, c):
        copy(base + r).start()
        return c

    def drain(r, c):
        copy(base + r).wait()
        return c

    lax.fori_loop(0, R, issue, 0)
    lax.fori_loop(0, R, drain, 0)


def _moe_dispatch(h_all, dest, buf0, R):
    Mtot = h_all.shape[0]
    any_spec = pl.BlockSpec(memory_space=pl.ANY)
    body = functools.partial(_row_copy_body, True, Mtot, R)
    return _call(body, (2 * Mtot // R,), [(h_all, any_spec), (buf0, any_spec)],
                 [(buf0.shape, F32, any_spec)], scratch=[pltpu.SemaphoreType.DMA(())],
                 prefetch=[dest], aliases={2: 0}, name="moe_dispatch")[0]


def _moe_collect(ybuf, dest, Mtot, R):
    any_spec = pl.BlockSpec(memory_space=pl.ANY)
    body = functools.partial(_row_copy_body, False, Mtot, R)
    return _call(body, (2 * Mtot // R,), [(ybuf, any_spec)], [((2 * Mtot, D), F32, any_spec)],
                 scratch=[pltpu.SemaphoreType.DMA(())], prefetch=[dest], name="moe_collect")[0]


def _moe_ffn(buf, blk_expert, nvalid, w1, w3, w2, layer):
    nblk = buf.shape[0] // MOE_BLK
    FF = w1.shape[-1]

    def body(be_ref, nv_ref, x_ref, w1_ref, w3_ref, w2_ref, y_ref):
        valid = pl.program_id(0) < nv_ref[0]

        @pl.when(valid)
        def _():
            x = x_ref[...].astype(BF16)
            a = _d(x, w1_ref[...])
            b = _d(x, w3_ref[...])
            y_ref[...] = _d(_silu(a) * b, w2_ref[...])

        @pl.when(jnp.logical_not(valid))
        def _():
            y_ref[...] = jnp.zeros_like(y_ref)

    def blk(i, be, nv):
        return jnp.minimum(i, nv[0] - 1)

    ins = [(buf, pl.BlockSpec((MOE_BLK, D), lambda i, be, nv: (blk(i, be, nv), 0))),
           (w1, pl.BlockSpec((None, None, D, FF), lambda i, be, nv: (layer, be[blk(i, be, nv)], 0, 0))),
           (w3, pl.BlockSpec((None, None, D, FF), lambda i, be, nv: (layer, be[blk(i, be, nv)], 0, 0))),
           (w2, pl.BlockSpec((None, None, FF, D), lambda i, be, nv: (layer, be[blk(i, be, nv)], 0, 0)))]
    outs = [(buf.shape, F32, pl.BlockSpec((MOE_BLK, D), lambda i, be, nv: (i, 0)))]
    return _call(body, (nblk,), ins, outs, prefetch=[blk_expert, nvalid], name="moe_ffn")[0]


def _moe_combine(st, yslot, rinfo_all, off, Mtot, alpha, ln_g, ln_b):
    tm = _pick_tile(st.M, off, Mtot, cap=st.tm)
    b0, b1, br = off // tm, (Mtot + off) // tm, off // tm

    def body(y0_ref, y1_ref, r_ref, x_ref, gate_ref, g_ref, b_ref, o_ref):
        r = r_ref[...]
        y = r[:, 2:3] * y0_ref[...] + r[:, 3:4] * y1_ref[...]
        o_ref[...] = _res_ln(alpha, y, x_ref[...], gate_ref[...], g_ref[...], b_ref[...])

    ins = [(yslot, pl.BlockSpec((tm, D), lambda i: (b0 + i, 0))),
           (yslot, pl.BlockSpec((tm, D), lambda i: (b1 + i, 0))),
           (rinfo_all, pl.BlockSpec((tm, LANES), lambda i: (br + i, 0))),
           _rows(st.x, tm), st.mod(5, tm), _full(ln_g), _full(ln_b)]
    return _call(body, (st.M // tm,), ins, [_rows_out(st.M, D, tm)], name="moe_combine")[0]


def _moe_layer(streams, layer, alpha, p):
    wgr = jnp.zeros((D, LANES), F32).at[:, :MOE_GROUPS].set(p["moe_w_group"][layer])
    wgr = wgr.at[:, MOE_GROUPS:MOE_GROUPS + MOE_EXPERTS].set(p["moe_w_router"][layer])
    bgr = jnp.zeros((1, LANES), F32).at[0, :MOE_GROUPS].set(p["moe_b_group"][layer])
    bgr = bgr.at[0, MOE_GROUPS:MOE_GROUPS + MOE_EXPERTS].set(p["moe_b_router"][layer])
    hs, rs = zip(*[_moe_router(st, wgr, bgr) for st in streams])
    h_all = jnp.concatenate(hs, axis=0)
    rinfo = jnp.concatenate(rs, axis=0)
    Mtot = h_all.shape[0]
    R = _pick_tile(Mtot)
    counts = _moe_counts(rinfo, R)[:MOE_EXPERTS, 0].astype(I32)
    padded = (counts + MOE_BLK - 1) // MOE_BLK * MOE_BLK
    pad_end = jnp.cumsum(padded)
    pstart = jnp.zeros((LANES,), F32).at[:MOE_EXPERTS].set((pad_end - padded).astype(F32))
    pstart = jnp.broadcast_to(pstart[:, None], (LANES, LANES))
    nblk = -(-2 * Mtot // MOE_BLK) + MOE_EXPERTS
    blk_expert = jnp.minimum(jnp.searchsorted(pad_end, jnp.arange(nblk, dtype=I32) * MOE_BLK, side="right"),
                             MOE_EXPERTS - 1).astype(I32)
    nvalid = (pad_end[-1:] // MOE_BLK).astype(I32)
    dest = _moe_dest(rinfo, pstart, R).reshape(-1)
    buf = _moe_dispatch(h_all, dest, jnp.zeros((nblk * MOE_BLK, D), F32), R)
    ybuf = _moe_ffn(buf, blk_expert, nvalid, p["moe_w1"], p["moe_w3"], p["moe_w2"], layer)
    yslot = _moe_collect(ybuf, dest, Mtot, R)
    ln_g = p["ln_g"][layer, 1].reshape(1, D)
    ln_b = p["ln_b"][layer, 1].reshape(1, D)
    off = 0
    for st in streams:
        st.x = _moe_combine(st, yslot, rinfo, off, Mtot, alpha, ln_g, ln_b)
        off += st.M


def _scan_body(nbg, nv, tb, w_ref, kkn_ref, b_ref, k_ref, r_ref, vt_ref, s0_ref, hexp_ref, hsum_ref, bd_ref,
               o_ref, sf_ref, s_scr):
    t = pl.program_id(1)

    @pl.when(t == 0)
    def _():
        s_scr[...] = s0_ref[...]

    def sub(sb, carry):
        base = pl.multiple_of(sb * SUBLANES, SUBLANES)
        rows = [[ref[bb, pl.ds(base, SUBLANES), :] for bb in range(nbg)]
                for ref in (w_ref, kkn_ref, b_ref, k_ref, r_ref)]
        vt = vt_ref[:, sb].reshape(nbg * nv, LANES).astype(BF16)
        oacc = jnp.zeros((nbg * nv, LANES), F32)
        for j in range(SUBLANES):
            S = [s_scr[bb] for bb in range(nbg)]
            P = jnp.concatenate([S[bb] * rows[1][bb][j:j + 1] for bb in range(nbg)], axis=0).astype(BF16)
            sa = jnp.concatenate([_d(P[:, c0:c0 + MXU_TILE], bd_ref[...]) for c0 in range(0, D, MXU_TILE)], axis=1)
            vb = _d(vt, hexp_ref[j])
            P2 = []
            for bb in range(nbg):
                sl = slice(bb * nv, (bb + 1) * nv)
                Sn = S[bb] * rows[0][bb][j:j + 1] + sa[sl] * rows[2][bb][j:j + 1] + vb[sl] * rows[3][bb][j:j + 1]
                s_scr[bb] = Sn
                P2.append(Sn * rows[4][bb][j:j + 1])
            oacc = oacc + _d(jnp.concatenate(P2, axis=0), hsum_ref[j])
        o_ref[:, sb] = oacc.reshape(nbg, nv, LANES)
        return carry

    lax.fori_loop(0, tb // SUBLANES, sub, 0)

    @pl.when(t == pl.num_programs(1) - 1)
    def _():
        sf_ref[...] = s_scr[...]


def _delta_scan(w, kkn, b, k, r, v, S0, B, T, hsz):
    nh = D // hsz
    nv = hsz
    nbg = 4 if B % 4 == 0 else (2 if B % 2 == 0 else 1)
    tb = min(64, T)
    hexp, hsum, bd = _scan_consts(hsz)
    vt = v.reshape(B, T // SUBLANES, SUBLANES, nh, nv).transpose(0, 1, 4, 2, 3)
    vt = jnp.pad(vt, ((0, 0),) * 4 + ((0, 16 - nh),)).reshape(B, T // SUBLANES, nv, LANES)
    seq = lambda a: (a.reshape(B, T, D), pl.BlockSpec((nbg, tb, D), lambda g, t: (g, t, 0)))
    ins = [seq(w), seq(kkn), seq(b), seq(k), seq(r),
           (vt, pl.BlockSpec((nbg, tb // SUBLANES, nv, LANES), lambda g, t: (g, t, 0, 0))),
           (S0, pl.BlockSpec((nbg, nv, D), lambda g, t: (g, 0, 0))),
           _full(hexp), _full(hsum), _full(bd)]
    outs = [((B, T // SUBLANES, nv, LANES), F32,
             pl.BlockSpec((nbg, tb // SUBLANES, nv, LANES), lambda g, t: (g, t, 0, 0))),
            ((B, nv, D), F32, pl.BlockSpec((nbg, nv, D), lambda g, t: (g, 0, 0)))]
    body = functools.partial(_scan_body, nbg, nv, tb)
    op, sf = _call(body, (B // nbg, T // tb), ins, outs, scratch=[pltpu.VMEM((nbg, nv, D), F32)], name="delta_scan")
    o = op.reshape(B, T // SUBLANES, nv, SUBLANES, 16)[..., :nh].transpose(0, 1, 3, 4, 2).reshape(B * T, D)
    return o, sf


def _shifted_rows(h, first, period, shift=1):
    row = lax.broadcasted_iota(I32, h.shape, 0)
    return jnp.where(row % period < shift, first, pltpu.roll(h, shift, axis=0))


def _rwkv_prep(st, shift_prev, p, tm):
    long_seq = st.T % tm == 0
    tpb = st.T // tm if long_seq else 1
    hs, he = _seg_consts(RWKV_HSZ)
    row = lambda a: _full(a.reshape(1, D))
    wts = [_full(p["rwkv_mu"]), _full(p["rwkv_w_rkv"].astype(BF16)),
           _full(p["rwkv_w1"].astype(BF16)), _full(p["rwkv_w2"].astype(BF16)),
           _full(p["rwkv_a1"].astype(BF16)), _full(p["rwkv_a2"].astype(BF16)),
           _full(p["rwkv_g1"].astype(BF16)), _full(p["rwkv_g2"].astype(BF16)),
           row(p["rwkv_w0"]), row(p["rwkv_a0"]), row(p["rwkv_k_k"]), row(p["rwkv_k_a"]), _full(hs), _full(he)]
    if long_seq:
        nsub = tm // SUBLANES
        first_ins = [(st.x, pl.BlockSpec((SUBLANES, D), lambda i: (jnp.maximum(i * nsub - 1, 0), 0))),
                     (shift_prev.reshape(st.B, 1, D), pl.BlockSpec((None, 1, D), lambda i: (i // tpb, 0, 0)))]
    else:
        first_ins = [_rows(jnp.repeat(shift_prev, st.T, axis=0), tm)]
    nf = len(first_ins)

    def body(x_ref, sc_ref, sh_ref, *refs):
        first_refs, refs = refs[:nf], refs[nf:]
        (mu_ref, wrkv_ref, w1_ref, w2_ref, a1_ref, a2_ref, g1_ref, g2_ref, w0_ref, a0_ref, kk_ref, ka_ref,
         hs_ref, he_ref) = refs[:14]
        h_ref, r_ref, w_ref, k_ref, v_ref, kkn_ref, b_ref, g_ref = refs[14:]
        sc, sh = sc_ref[...], sh_ref[...]
        h = _modulate(x_ref[...], sc, sh)
        if long_seq:
            hh = _modulate(first_refs[0][...], sc, sh)[SUBLANES - 1:SUBLANES]
            first = jnp.where(pl.program_id(0) % tpb == 0, first_refs[1][...], hh)
            hprev = _shifted_rows(h, first, tm)
        else:
            hprev = _shifted_rows(h, first_refs[0][...], st.T)
        xx = hprev - h
        mu = mu_ref[...]
        xr, xw, xk, xv, xa, xg = [h + xx * mu[i:i + 1] for i in range(6)]
        r = _d(xr, wrkv_ref[0])
        k = _d(xk, wrkv_ref[1])
        v = _d(xv, wrkv_ref[2])
        logw = -_softplus(-(w0_ref[...] + _d(jnp.tanh(_d(xw, w1_ref[...])), w2_ref[...]))) - 0.5
        a = _sigmoid(a0_ref[...] + _d(_d(xa, a1_ref[...]), a2_ref[...]))
        g = _d(_sigmoid(_d(xg, g1_ref[...])), g2_ref[...])
        kk = k * kk_ref[...]
        inv = lax.rsqrt(_d_x3(kk * kk, hs_ref[...]) + 1e-6)
        kk = kk * _d_x3(inv, he_ref[...])
        h_ref[...] = h
        r_ref[...] = r
        w_ref[...] = jnp.exp(-jnp.exp(logw))
        k_ref[...] = k * (1.0 + (a - 1.0) * ka_ref[...])
        v_ref[...] = v
        kkn_ref[...] = -kk
        b_ref[...] = kk * a
        g_ref[...] = g

    ins = [_rows(st.x, tm), st.mod(1, tm), st.mod(0, tm)] + first_ins + wts
    outs = [_rows_out(st.M, D, tm) for _ in range(8)]
    return _call(body, (st.M // tm,), ins, outs, name="rwkv_prep")


def _rwkv_out(st, o, r, kmod, v, g, p, alpha, ln_g, ln_b, tm):
    hs, he = _seg_consts(RWKV_HSZ)
    inv_n = 1.0 / RWKV_HSZ

    def pro(o, r, k, v, g, lw, lb, rk, hs, he):
        mean = _d_x3(_d_x3(o, hs) * inv_n, he)
        c = o - mean
        rstd = lax.rsqrt(_d_x3(c * c, hs) * inv_n + RWKV_GN_EPS)
        on = c * _d_x3(rstd, he) * lw + lb
        bonus = _d_x3(_d_x3(r * k * rk, hs), he) * v
        return (on + bonus) * g

    def epi(acc, x, gate, g_, b_):
        return (_res_ln(alpha, acc, x, gate, g_, b_),)

    row = lambda a: _full(a.reshape(1, D))
    pro_ins = [_rows(a, tm) for a in (r, kmod, v, g)] + [row(p["rwkv_ln_w"]), row(p["rwkv_ln_b"]),
                                                       row(p["rwkv_r_k"]), _full(hs), _full(he)]
    epi_ins = [_rows(st.x, tm), st.mod(2, tm), _full(ln_g), _full(ln_b)]
    return _mm(o, _full(p["rwkv_w_o"].astype(BF16)), tm=tm, pro=pro, pro_ins=pro_ins, epi=epi, epi_ins=epi_ins,
               outs=[_rows_out(st.M, D, tm)], name="rwkv_out")[0]


def _rwkv_layer(st, shift_prev, wkv0, p, alpha, ln_g, ln_b):
    B, T = st.B, st.T
    tm = min(256, st.M)
    nh = D // RWKV_HSZ
    h, r, w, kmod, v, kkn, b, g = _rwkv_prep(st, shift_prev, p, tm)
    S0 = wkv0.transpose(0, 2, 1, 3).reshape(B, RWKV_HSZ, D)
    o, sf = _delta_scan(w, kkn, b, kmod, r, v, S0, B, T, RWKV_HSZ)
    st.x = _rwkv_out(st, o, r, kmod, v, g, p, alpha, ln_g, ln_b, tm)
    wkv = sf.reshape(B, RWKV_HSZ, nh, RWKV_HSZ).transpose(0, 2, 1, 3)
    return wkv, h.reshape(B, T, D)[:, -1]


def _pad_cols(w, n):
    return jnp.pad(w, ((0, 0), (0, n - w.shape[1])))


def _gdn_proj(st, p, tm):
    C = 3 * D
    w = _pad_cols(p["gdn_w_in"], C + D + LANES).astype(BF16)

    def epi(acc):
        return acc[:, :C], acc[:, C:C + D], acc[:, C + D:]

    outs = [_rows_out(st.M, C, tm), _rows_out(st.M, D, tm), _rows_out(st.M, LANES, tm)]
    return _mm(st.x, _full(w), tm=tm, pro=_modulate, pro_ins=[st.mod(1, tm), st.mod(0, tm)], epi=epi, outs=outs,
               name="gdn_proj")


def _gdn_conv(st, pre, ba, conv_buf, p, tm):
    C = 3 * D
    H = GDN_HEADS
    long_seq = st.T % tm == 0
    tpb = st.T // tm if long_seq else 1
    hs, he = _seg_consts(GDN_HSZ)
    hsn = _seg_np(GDN_HSZ)
    he_b = jnp.asarray(hsn.T, BF16)
    he_a = jnp.asarray(np.roll(hsn.T, H, axis=0), BF16)
    alog = jnp.zeros((1, LANES), F32).at[0, H:2 * H].set(p["gdn_A_log"])
    dtb = jnp.zeros((1, LANES), F32).at[0, H:2 * H].set(p["gdn_dt_bias"])
    if long_seq:
        nsub = tm // SUBLANES
        init8 = jnp.pad(conv_buf, ((0, 0), (SUBLANES - (GDN_CONV - 1), 0), (0, 0)))
        first_ins = [(pre, pl.BlockSpec((SUBLANES, C), lambda i: (jnp.maximum(i * nsub - 1, 0), 0))),
                     (init8, pl.BlockSpec((None, SUBLANES, C), lambda i: (i // tpb, 0, 0)))]
    else:
        padded = jnp.pad(conv_buf, ((0, 0), (0, st.T), (0, 0)))
        first_ins = [_rows(padded[:, GDN_CONV - 1 - j:GDN_CONV - 1 - j + st.T].reshape(st.M, C), tm)
                     for j in range(1, GDN_CONV)]
    nf = len(first_ins)

    def body(pre_ref, ba_ref, *refs):
        first_refs, refs = refs[:nf], refs[nf:]
        cw_ref, alog_ref, dtb_ref, hs_ref, he_ref, heb_ref, hea_ref = refs[:7]
        w_ref, kkn_ref, k_ref, q_ref, v_ref = refs[7:]
        x = pre_ref[...]
        if long_seq:
            halo = jnp.where(pl.program_id(0) % tpb == 0, first_refs[1][...], first_refs[0][...])
            big = jnp.concatenate([halo, x], axis=0)
            sh = [pltpu.roll(big, j, axis=0)[SUBLANES:] for j in range(1, GDN_CONV)]
        else:
            sh = [_shifted_rows(x, first_refs[j - 1][...], st.T, j) for j in range(1, GDN_CONV)]
        cw = cw_ref[...]
        conv = sh[2] * cw[0:1]
        conv = conv + sh[1] * cw[1:2]
        conv = conv + sh[0] * cw[2:3]
        conv = conv + x * cw[3:4]
        c = _silu(conv)
        q, k, v = c[:, :D], c[:, D:2 * D], c[:, 2 * D:]
        qn = q * _d_x3(lax.rsqrt(_d_x3(q * q, hs_ref[...]) + 1e-6), he_ref[...]) * (GDN_HSZ ** -0.5)
        kn = k * _d_x3(lax.rsqrt(_d_x3(k * k, hs_ref[...]) + 1e-6), he_ref[...])
        ba = ba_ref[...]
        beta = _d_x3(_sigmoid(ba), heb_ref[...])
        a = _d_x3(jnp.exp(-jnp.exp(alog_ref[...]) * _softplus(ba + dtb_ref[...])), hea_ref[...])
        w_ref[...] = a
        kkn_ref[...] = -(a * beta) * kn
        k_ref[...] = kn
        q_ref[...] = qn
        v_ref[...] = beta * v

    ins = [_rows(pre, tm), _rows(ba, tm)] + first_ins + [_full(p["gdn_conv_w"]), _full(alog), _full(dtb), _full(hs),
                                                         _full(he), _full(he_b), _full(he_a)]
    outs = [_rows_out(st.M, D, tm) for _ in range(5)]
    return _call(body, (st.M // tm,), ins, outs, name="gdn_conv")


def _gdn_out(st, o, z, p, alpha, ln_g, ln_b, tm):
    hs, he = _seg_consts(GDN_HSZ)
    nw = jnp.tile(p["gdn_norm_w"], GDN_HEADS).reshape(1, D)

    def pro(o, z, nw, hs, he):
        rstd = lax.rsqrt(_d_x3(o * o, hs) * (1.0 / GDN_HSZ) + 1e-6)
        return o * _d_x3(rstd, he) * nw * _silu(z)

    def epi(acc, x, gate, g_, b_):
        return (_res_ln(alpha, acc, x, gate, g_, b_),)

    return _mm(o, _full(p["gdn_w_o"].astype(BF16)), tm=tm, pro=pro, pro_ins=[_rows(z, tm), _full(nw), _full(hs), _full(he)],
               epi=epi, epi_ins=[_rows(st.x, tm), st.mod(2, tm), _full(ln_g), _full(ln_b)],
               outs=[_rows_out(st.M, D, tm)], name="gdn_out")[0]


def _gdn_layer(st, conv_buf, S0, p, alpha, ln_g, ln_b):
    B, T = st.B, st.T
    tm = min(256, st.M)
    pre, z, ba = _gdn_proj(st, p, tm)
    w, kkn, kn, qn, vb = _gdn_conv(st, pre, ba, conv_buf, p, tm)
    S0t = S0.transpose(0, 3, 1, 2).reshape(B, GDN_HSZ, D)
    o, sf = _delta_scan(w, kkn, kn, kn, qn, vb, S0t, B, T, GDN_HSZ)
    st.x = _gdn_out(st, o, z, p, alpha, ln_g, ln_b, tm)
    S = sf.reshape(B, GDN_HSZ, GDN_HEADS, GDN_HSZ).transpose(0, 2, 3, 1)
    xpad = jnp.concatenate([conv_buf, pre.reshape(B, T, 3 * D)[:, -(GDN_CONV - 1):]], axis=1)
    return S, xpad[:, -(GDN_CONV - 1):]


def _flash_body(cfg, *refs):
    tq, hq, hk = cfg["tq"], cfg["hq"], cfg["hk"]
    fox, bias, aug, window = cfg["fox"], cfg["bias"], cfg["aug"], cfg["window"]
    tk = tq
    G = hq // hk
    Kc = HD + aug
    R = G * tq
    refs = list(refs)
    q_ref, k_ref, v_ref = refs[:3]
    pos = 3
    if aug:
        mb_ref, e_ref = refs[pos:pos + 2]
        pos += 2
    if bias:
        tz_ref = refs[pos]
        pos += 1
    if fox:
        cq_ref, ck_ref = refs[pos:pos + 2]
        pos += 2
    o_ref, kb, vb = refs[pos:pos + 3]
    g = pl.program_id(1)
    qi = pl.program_id(2)

    @pl.when(qi == 0)
    def _():
        vb[...] = v_ref[...].astype(BF16)
        if aug:
            k = k_ref[...]
            kb[...] = jnp.concatenate(
                [jnp.concatenate([k[:, kv * HD:(kv + 1) * HD].astype(BF16), e_ref[...]], axis=1) for kv in range(hk)],
                axis=1)
        else:
            kb[...] = k_ref[...].astype(BF16)

    scale = HD ** -0.5
    q = q_ref[...]
    row_t = lax.broadcasted_iota(I32, (R, tk), 0) % tq
    col_s = lax.broadcasted_iota(I32, (R, tk), 1)
    qs, cqs = [], []
    for kv in range(hk):
        x = jnp.concatenate([q[:, (kv * G + gg) * HD:(kv * G + gg + 1) * HD] for gg in range(G)], axis=0) * scale
        if aug:
            x = jnp.concatenate([x.astype(BF16), jnp.concatenate([mb_ref[kv]] * G, axis=0)], axis=1)
        qs.append(x.astype(BF16))
        if fox:
            lane = lax.broadcasted_iota(I32, (tq, LANES), 1)
            cqs.append(jnp.sum(jnp.where(lane == g * hk + kv, cq_ref[...], 0.0), axis=-1, keepdims=True))

    def logits(kv, c, rel, valid):
        off = pl.multiple_of(c * tk, tk)
        s = _d_nt(qs[kv], kb[pl.ds(off, tk), kv * Kc:(kv + 1) * Kc])
        if fox:
            sub = lax.broadcasted_iota(I32, (FOX_HEADS, tk), 0)
            ck = jnp.sum(jnp.where(sub == g * hk + kv, ck_ref[:, pl.ds(off, tk)], 0.0), axis=0, keepdims=True)
            s = s + cqs[kv] - ck
        if bias and rel in (0, 1):
            s = s + jnp.concatenate([tz_ref[kv * G + gg, rel] for gg in range(G)], axis=0)
        if rel == 0:
            s = jnp.where(row_t >= col_s, s, NEG)
        if rel == 3:
            s = jnp.where(col_s > row_t, s, NEG)
        if valid is not None:
            s = jnp.where(valid, s, NEG)
        return s

    def pv(kv, c, p):
        off = pl.multiple_of(c * tk, tk)
        return _d(p, vb[pl.ds(off, tk), kv * HD:(kv + 1) * HD])

    def update(state, c, rel, valid=None):
        new = []
        for kv in range(hk):
            m, l, acc = state[3 * kv:3 * kv + 3]
            s = logits(kv, c, rel, valid)
            m2 = jnp.maximum(m, jnp.max(s, axis=-1, keepdims=True))
            a = jnp.exp(m - m2)
            p = jnp.exp(s - m2)
            new += [m2, a * l + jnp.sum(p, axis=-1, keepdims=True), a * acc + pv(kv, c, p)]
        return tuple(new)

    state = []
    for kv in range(hk):
        s = logits(kv, qi, 0, None)
        m = jnp.max(s, axis=-1, keepdims=True)
        p = jnp.exp(s - m)
        state += [m, jnp.sum(p, axis=-1, keepdims=True), pv(kv, qi, p)]
    state = tuple(state)
    if window is not None:
        nch = window // tk
        for dc in range(1, nch + 1):
            c = qi - dc
            rel = 1 if dc == 1 else (3 if dc == nch else 2)
            state = update(state, jnp.maximum(c, 0), rel, c >= 0)
    else:
        lo_far = qi
        if bias:
            state = update(state, jnp.maximum(qi - 1, 0), 1, qi >= 1)
            lo_far = qi - 1
        state = lax.fori_loop(0, jnp.maximum(lo_far, 0), lambda c, st: update(st, c, 2), state)
    outs = []
    for kv in range(hk):
        m, l, acc = state[3 * kv:3 * kv + 3]
        o = acc / l
        outs += [o[gg * tq:(gg + 1) * tq] for gg in range(G)]
    o_ref[...] = jnp.concatenate(outs, axis=1)


def _flash(cfg, B, T, ngroups, q_in, k_in, v_in, extra_ins, M):
    tq, hq, hk = cfg["tq"], cfg["hq"], cfg["hk"]
    Kc = HD + cfg["aug"]
    ins = [q_in, k_in, v_in] + list(extra_ins)
    nq = T // tq
    outs = [((M, ngroups * hq * HD), F32, pl.BlockSpec((tq, hq * HD), lambda b, g, i: (b * nq + i, g)))]
    scratch = [pltpu.VMEM((T, hk * Kc), BF16), pltpu.VMEM((T, hk * HD), BF16)]
    return _call(functools.partial(_flash_body, cfg), (B, ngroups, nq), ins, outs, scratch=scratch,
                 name="flash_" + cfg["name"])[0]


def _log_sigmoid(x):
    return -_softplus(-x)


def _fox_proj(st, p, tm):
    hw = FOX_HEADS * HD
    w = _pad_cols(p["fox_w_in"], 3 * hw + LANES).astype(BF16)
    bf = jnp.zeros((1, LANES), F32).at[0, :FOX_HEADS].set(p["fox_b_f"])

    def epi(acc, bf):
        return acc[:, :hw], acc[:, hw:3 * hw], _log_sigmoid(acc[:, 3 * hw:] + bf)

    outs = [_rows_out(st.M, hw, tm), _rows_out(st.M, 2 * hw, tm), _rows_out(st.M, LANES, tm)]
    return _mm(st.x, _full(w), tm=tm, pro=_modulate, pro_ins=[st.mod(1, tm), st.mod(0, tm)], epi=epi,
               epi_ins=[_full(bf)], outs=outs, name="fox_proj")


def _lower_tri(n):
    return jnp.asarray(np.tril(np.ones((n, n), np.float32)), BF16)


def _cumsum_rows(x, B, T):
    ch = _pick_tile(T)
    tri = _lower_tri(ch)

    def body(x_ref, tri_ref, o_ref):
        carry = jnp.zeros((1, LANES), F32)
        for c in range(T // ch):
            cc = _d_3x(tri_ref[...], x_ref[c * ch:(c + 1) * ch, :]) + carry
            o_ref[c * ch:(c + 1) * ch, :] = cc
            carry = cc[ch - 1:ch, :]

    return _call(body, (B,), [_rows(x, T), _full(tri)], [_rows_out(B * T, LANES, T)], name="cumsum_rows")[0]


def _out_proj(st, o, w_o, alpha, ln_g, ln_b, tm, name):
    def epi(acc, x, gate, g_, b_):
        return (_res_ln(alpha, acc, x, gate, g_, b_),)

    return _mm(o, _full(w_o.astype(BF16)), tm=tm, epi=epi,
               epi_ins=[_rows(st.x, tm), st.mod(2, tm), _full(ln_g), _full(ln_b)],
               outs=[_rows_out(st.M, D, tm)], name=name)[0]


def _fox_prompt(st, p, alpha, ln_g, ln_b):
    B, T, M = st.B, st.T, st.M
    tm = min(256, M)
    q, kv, logf = _fox_proj(st, p, tm)
    cum = _cumsum_rows(logf, B, T)
    ckT = cum.reshape(B, T, LANES)[:, :, :FOX_HEADS].transpose(0, 2, 1)
    tq = min(256, T)
    nq = T // tq
    cfg = dict(name="fox", tq=tq, hq=2, hk=2, fox=True, bias=False, aug=0, window=None)
    npair = FOX_HEADS // 2
    q_in = (q, pl.BlockSpec((tq, 2 * HD), lambda b, g, i: (b * nq + i, g)))
    k_in = (kv, pl.BlockSpec((T, 2 * HD), lambda b, g, i: (b, g)))
    v_in = (kv, pl.BlockSpec((T, 2 * HD), lambda b, g, i: (b, npair + g)))
    extra = [(cum, pl.BlockSpec((tq, LANES), lambda b, g, i: (b * nq + i, 0))),
             (ckT, pl.BlockSpec((None, FOX_HEADS, T), lambda b, g, i: (b, 0, 0)))]
    o = _flash(cfg, B, T, npair, q_in, k_in, v_in, extra, M)
    st.x = _out_proj(st, o, p["fox_w_o"], alpha, ln_g, ln_b, tm, "fox_out")
    return kv.reshape(B, T, 2, FOX_HEADS, HD), logf.reshape(B, T, LANES)[:, :, :FOX_HEADS]


def _page_ins(cache, page_shape, npages, first_of_step):
    nd = len(page_shape)
    return [(cache, pl.BlockSpec((None,) + tuple(page_shape),
                                 lambda b, s, pt, j=j: (pt[b, first_of_step(s) + j],) + (0,) * nd))
            for j in range(npages)]


def _fox_cum_sample(logf_new, cache_logf, page_table, Tn):
    B, npg = page_table.shape
    PAGE = cache_logf.shape[1]
    H = cache_logf.shape[2]
    tri = _lower_tri(PAGE)

    def body(pt_ref, *refs):
        pages, new_ref, tri_ref, o_ref = refs[:npg], refs[npg], refs[npg + 1], refs[npg + 2]
        carry = jnp.zeros((1, H), F32)
        for j in range(npg):
            cc = _d_3x(tri_ref[...], pages[j][...]) + carry
            o_ref[j * PAGE:(j + 1) * PAGE, :] = cc
            carry = cc[PAGE - 1:PAGE, :]
        xn = jnp.concatenate([new_ref[...][:, :H], jnp.zeros((PAGE - Tn, H), F32)], axis=0)
        o_ref[npg * PAGE:(npg + 1) * PAGE, :] = _d_3x(tri_ref[...], xn) + carry

    ins = _page_ins(cache_logf, (PAGE, H), npg, lambda s: 0)
    ins += [(logf_new, pl.BlockSpec((Tn, LANES), lambda b, s, pt: (b, 0))),
            (tri, pl.BlockSpec(tri.shape, lambda b, s, pt: (0, 0)))]
    Lp = (npg + 1) * PAGE
    outs = [((B, Lp, H), F32, pl.BlockSpec((None, Lp, H), lambda b, s, pt: (b, 0, 0)))]
    return _call(body, (B, 1), ins, outs, prefetch=[page_table], name="fox_cum_sample")[0]


def _rep_mat(n_rows, n_src, per):
    r = np.arange(n_rows)
    src = r // per if per else r % n_src
    return jnp.asarray((src[:, None] == np.arange(n_src)[None, :]).astype(np.float32), BF16)


def _fox_decode(q, kv_new, cache_kv, page_table, cq, ckT, Tn):
    B, npg = page_table.shape
    PAGE = cache_kv.shape[1]
    hw = FOX_HEADS * HD
    R = FOX_HEADS * Tn
    pps = PAGES_PER_STEP if npg % PAGES_PER_STEP == 0 else 1
    nsteps = npg // pps
    cache = cache_kv.reshape(cache_kv.shape[0], PAGE, 2 * hw)
    rep_t = _rep_mat(R, Tn, 0)
    rep_h = _rep_mat(R, FOX_HEADS, Tn)
    scale = HD ** -0.5

    def body(pt_ref, *refs):
        pages = refs[:pps]
        q_ref, new_ref, cq_ref, ck_ref, rt_ref, rh_ref, o_ref, qbd, m_s, l_s, acc = refs[pps:]
        s_id = pl.program_id(1)
        row_h = lax.broadcasted_iota(I32, (R, hw), 0) // Tn
        lane_h = lax.broadcasted_iota(I32, (R, hw), 1) // HD

        @pl.when(s_id == 0)
        def _():
            qrep = _d(rt_ref[...], q_ref[...])
            qbd[...] = jnp.where(row_h == lane_h, qrep * scale, 0.0).astype(BF16)
            m_s[...] = jnp.full(m_s.shape, NEG, F32)
            l_s[...] = jnp.zeros(l_s.shape, F32)
            acc[...] = jnp.zeros(acc.shape, F32)

        cqv = cq_ref[...]

        def chunk(kp, vp, off, mask):
            s = _d_nt(qbd[...], kp) + cqv - _d_3x(rh_ref[...], ck_ref[:, pl.ds(off, PAGE)])
            if mask is not None:
                s = jnp.where(mask, s, NEG)
            m = m_s[:, 0:1]
            m2 = jnp.maximum(m, jnp.max(s, axis=-1, keepdims=True))
            a = jnp.exp(m - m2)
            pr = jnp.exp(s - m2)
            l_s[...] = jnp.broadcast_to(a * l_s[:, 0:1] + jnp.sum(pr, axis=-1, keepdims=True), l_s.shape)
            m_s[...] = jnp.broadcast_to(m2, m_s.shape)
            acc[...] = a * acc[...] + _d(pr, vp)

        for j in range(pps):
            pg = pages[j][...]
            chunk(pg[:, :hw], pg[:, hw:], pl.multiple_of((s_id * pps + j) * PAGE, PAGE), None)

        @pl.when(s_id == nsteps - 1)
        def _():
            new = jnp.concatenate([new_ref[...], jnp.zeros((PAGE - Tn, 2 * hw), F32)], axis=0)
            t_row = lax.broadcasted_iota(I32, (R, PAGE), 0) % Tn
            col = lax.broadcasted_iota(I32, (R, PAGE), 1)
            chunk(new[:, :hw], new[:, hw:], npg * PAGE, col <= t_row)
            of = jnp.where(row_h == lane_h, acc[...] / l_s[:, 0:1], 0.0)
            out = of[0:Tn]
            for h in range(1, FOX_HEADS):
                out = out + of[h * Tn:(h + 1) * Tn]
            o_ref[...] = out

    Lp = ckT.shape[2]
    ins = _page_ins(cache, (PAGE, 2 * hw), pps, lambda s: s * pps)
    ins += [(q, pl.BlockSpec((Tn, hw), lambda b, s, pt: (b, 0))),
            (kv_new, pl.BlockSpec((Tn, 2 * hw), lambda b, s, pt: (b, 0))),
            (cq, pl.BlockSpec((None, R, 1), lambda b, s, pt: (b, 0, 0))),
            (ckT, pl.BlockSpec((None, FOX_HEADS, Lp), lambda b, s, pt: (b, 0, 0))),
            (rep_t, pl.BlockSpec(rep_t.shape, lambda b, s, pt: (0, 0))),
            (rep_h, pl.BlockSpec(rep_h.shape, lambda b, s, pt: (0, 0)))]
    outs = [((B * Tn, hw), F32, pl.BlockSpec((Tn, hw), lambda b, s, pt: (b, 0)))]
    scratch = [pltpu.VMEM((R, hw), BF16), pltpu.VMEM((R, LANES), F32), pltpu.VMEM((R, LANES), F32),
               pltpu.VMEM((R, hw), F32)]
    return _call(body, (B, nsteps), ins, outs, scratch=scratch, prefetch=[page_table], name="fox_decode")[0]


def _fox_sample(st, cache_kv, cache_logf, page_table, p, alpha, ln_g, ln_b):
    B, Tn, M = st.B, st.T, st.M
    tm = min(256, M)
    npg = page_table.shape[1]
    PAGE = cache_kv.shape[1]
    q, kv, logf = _fox_proj(st, p, tm)
    cum = _fox_cum_sample(logf, cache_logf, page_table, Tn)
    cq = cum[:, npg * PAGE:npg * PAGE + Tn].transpose(0, 2, 1).reshape(B, FOX_HEADS * Tn, 1)
    ckT = cum.transpose(0, 2, 1)
    o = _fox_decode(q, kv, cache_kv, page_table, cq, ckT, Tn)
    st.x = _out_proj(st, o, p["fox_w_o"], alpha, ln_g, ln_b, tm, "fox_out")
    return kv.reshape(B, Tn, 2, FOX_HEADS, HD), logf.reshape(B, Tn, LANES)[:, :, :FOX_HEADS]


KVW = NSA_KVH * HD
HALF = CMP_BLK // 2


def _t5_bucket(dist):
    exact = REL_BUCKETS // 2
    d = jnp.maximum(dist, 0)
    far = exact + (jnp.log(jnp.maximum(d, 1).astype(F32) / exact) / math.log(REL_MAX_DIST / exact)
                   * (REL_BUCKETS - exact)).astype(I32)
    return jnp.where(d < exact, d, jnp.minimum(far, REL_BUCKETS - 1))


def _rel_bias(table, dist):
    return jnp.moveaxis(table[_t5_bucket(dist)], -1, 0)


def _nsa_proj(st, p, tm):
    qw = NSA_HEADS * HD
    w = _pad_cols(p["nsa_w_in"], qw + 6 * KVW + LANES).astype(BF16)

    def epi(acc):
        return (acc[:, :qw], acc[:, qw:qw + 2 * KVW], acc[:, qw + 2 * KVW:qw + 4 * KVW],
                acc[:, qw + 4 * KVW:qw + 6 * KVW], acc[:, qw + 6 * KVW:])

    outs = [_rows_out(st.M, qw, tm)] + [_rows_out(st.M, 2 * KVW, tm)] * 3 + [_rows_out(st.M, LANES, tm)]
    return _mm(st.x, _full(w), tm=tm, pro=_modulate, pro_ins=[st.mod(1, tm), st.mod(0, tm)], epi=epi, outs=outs,
               name="nsa_proj")


def _cmp_weights(p):
    eye = jnp.eye(NSA_KVH, dtype=F32)
    wk = jnp.einsum("ab,vlde->vladbe", eye, p["nsa_cmp_w1"]).reshape(2, CMP_BLK, KVW, KVW)
    wc = wk.reshape(2, 2, HALF, KVW, KVW).transpose(1, 2, 0, 3, 4)
    w2c = jnp.einsum("ab,vde->vadbe", eye, p["nsa_cmp_w2"]).reshape(2, KVW, KVW)
    b1 = jnp.tile(p["nsa_cmp_b1"][:, None, :], (1, NSA_KVH, 1)).reshape(1, 2 * KVW)
    return wc.astype(BF16), w2c.astype(BF16), b1


def _compress_body(nx, *refs):
    x_refs = refs[:nx]
    wc_ref, w2_ref, b1_ref, o_ref, ua, ub = refs[nx:]
    rows = ua.shape[1]
    nl = 2 * KVW // LANES
    acc = [[jnp.zeros((rows, KVW), F32) for _ in range(2)] for _ in range(2)]
    for l in range(HALF):
        for kv in range(2):
            lo = l * 2 * KVW + kv * KVW
            piece = [r[:, lo:lo + KVW] for r in x_refs]
            piece = (jnp.concatenate(piece, axis=0) if nx > 1 else piece[0]).astype(BF16)
            for half in range(2):
                acc[half][kv] = acc[half][kv] + _d(piece, wc_ref[half, l, kv])
    for scr, a in ((ua, acc[0]), (ub, acc[1])):
        full = jnp.concatenate(a, axis=1)
        for c in range(nl):
            scr[c] = full[:, c * LANES:(c + 1) * LANES]
    hid = jnp.concatenate([ua[c, pl.ds(0, rows // 2, stride=2), :] + ub[c, pl.ds(1, rows // 2, stride=2), :]
                           for c in range(nl)], axis=1)
    hid = _gelu_tanh(hid + b1_ref[...])
    o_ref[...] = jnp.concatenate([_d(hid[:, :KVW], w2_ref[0]), _d(hid[:, KVW:], w2_ref[1])], axis=1)


def _compress_dense(rows_kv, cw):
    wc, w2c, b1 = cw
    M = rows_kv.shape[0]
    x = rows_kv.reshape(M // HALF, HALF * 2 * KVW)
    nh = M // HALF
    th = _pick_tile(nh, cap=128)
    ins = [_rows(x, th), _full(wc), _full(w2c), _full(b1)]
    outs = [_rows_out(nh // 2, 2 * KVW, th // 2)]
    scratch = [pltpu.VMEM((2 * KVW // LANES, th, LANES), F32)] * 2
    return _call(functools.partial(_compress_body, 1), (nh // th,), ins, outs, scratch=scratch, name="nsa_compress")[0]


def _pair_mat(nc, ns):
    n = np.arange(nc)
    return jnp.asarray((n[:, None] // (SEL_BLK // CMP_BLK) == np.arange(ns)[None, :]).astype(np.float32), BF16)


def _top_blocks(score, n_sel):
    lane = lax.broadcasted_iota(I32, score.shape, 1)
    big = jnp.int32(1 << 20)
    sel = jnp.zeros(score.shape, jnp.bool_)
    work = score
    for _ in range(n_sel):
        m = jnp.max(work, axis=-1, keepdims=True)
        idx = jnp.min(jnp.where(work == m, lane, big), axis=-1, keepdims=True)
        hit = lane == idx
        sel = sel | hit
        work = jnp.where(hit, -3e38, work)
    return jnp.where(sel, 0.0, SEL_NEG)


def _masked_softmax(s, mask):
    s = jnp.where(mask, s, NEG)
    m = jnp.max(s, axis=-1, keepdims=True)
    p = jnp.where(mask, jnp.exp(s - m), 0.0)
    l = jnp.sum(p, axis=-1, keepdims=True)
    return p / jnp.where(l > 0.0, l, 1.0)


def _block_scores(imp, tpos, ns):
    blk = lax.broadcasted_iota(I32, imp.shape, 1)
    cur = tpos // SEL_BLK
    forced = (blk == 0) | (blk == cur) | (blk == cur - 1)
    score = jnp.where(forced, FORCE_SCORE, imp)
    return jnp.where(blk * SEL_BLK > tpos, -1.0, score)


def _nsa_cmp_prompt(q, kcvc, bias_c, B, T, tq):
    nc, ns = T // CMP_BLK, -(-T // SEL_BLK)
    n_sel = min(N_SEL, ns)
    nq = T // tq
    G = NSA_G
    pair = _pair_mat(nc, ns)
    scale = HD ** -0.5

    def body(q_ref, kc_ref, vc_ref, b_ref, pair_ref, o_ref, mb_ref):
        qi = pl.program_id(1)
        q = q_ref[...]
        R = G * tq
        tpos = qi * tq + lax.broadcasted_iota(I32, (R, 1), 0) % tq
        cmp_end = lax.broadcasted_iota(I32, (R, nc), 1) * CMP_BLK + (CMP_BLK - 1)
        mask = cmp_end <= tpos
        outs = []
        for kv in range(NSA_KVH):
            qs = jnp.concatenate([q[:, (kv * G + gg) * HD:(kv * G + gg + 1) * HD] for gg in range(G)], axis=0) * scale
            s = _d_nt(qs, kc_ref[:, kv * HD:(kv + 1) * HD])
            s = s + jnp.concatenate([b_ref[kv * G + gg] for gg in range(G)], axis=0)
            pc = _masked_softmax(s, mask)
            oc = _d(pc, vc_ref[:, kv * HD:(kv + 1) * HD])
            outs += [oc[gg * tq:(gg + 1) * tq] for gg in range(G)]
            imp = pc[0:tq]
            for gg in range(1, G):
                imp = imp + pc[gg * tq:(gg + 1) * tq]
            score = _block_scores(_d_x3(imp, pair_ref[...]), tpos[0:tq], ns)
            mb_ref[kv] = _top_blocks(score, n_sel).astype(mb_ref.dtype)
        o_ref[...] = jnp.concatenate(outs, axis=1)

    ins = [(q, pl.BlockSpec((tq, NSA_HEADS * HD), lambda b, i: (b * nq + i, 0))),
           (kcvc, pl.BlockSpec((nc, KVW), lambda b, i: (b, 0))),
           (kcvc, pl.BlockSpec((nc, KVW), lambda b, i: (b, 1))),
           (bias_c, pl.BlockSpec((NSA_HEADS, tq, nc), lambda b, i: (0, i, 0))),
           _full(pair)]
    outs = [((B * T, NSA_HEADS * HD), F32, pl.BlockSpec((tq, NSA_HEADS * HD), lambda b, i: (b * nq + i, 0))),
            ((B, NSA_KVH, T, ns), BF16, pl.BlockSpec((None, NSA_KVH, tq, ns), lambda b, i: (b, 0, i, 0)))]
    return _call(body, (B, nq), ins, outs, name="nsa_cmp_select")


def _gate_mats():
    hsn = _seg_np(HD)
    return [jnp.asarray(np.roll(hsn.T, br * NSA_HEADS, axis=0), BF16) for br in range(3)]


def _nsa_out(st, o_c, o_s, o_w, gates, p, alpha, ln_g, ln_b, tm):
    def pro(oc, os_, ow, gl, e0, e1, e2):
        sg = _sigmoid(gl)
        return _d_x3(sg, e0) * oc + _d_x3(sg, e1) * os_ + _d_x3(sg, e2) * ow

    def epi(acc, x, gate, g_, b_):
        return (_res_ln(alpha, acc, x, gate, g_, b_),)

    pro_ins = [_rows(o_s, tm), _rows(o_w, tm), _rows(gates, tm)] + [_full(e) for e in _gate_mats()]
    return _mm(o_c, _full(p["nsa_w_o"].astype(BF16)), tm=tm, pro=pro, pro_ins=pro_ins, epi=epi,
               epi_ins=[_rows(st.x, tm), st.mod(2, tm), _full(ln_g), _full(ln_b)],
               outs=[_rows_out(st.M, D, tm)], name="nsa_out")[0]


def _nsa_prompt(st, p, alpha, ln_g, ln_b):
    B, T, M = st.B, st.T, st.M
    tm = min(256, M)
    tq = ATT_T
    nq = T // tq
    table = p["rel_bias"]
    q, cmp_rows, slc_rows, win_rows, gates = _nsa_proj(st, p, tm)
    kcvc = _compress_dense(cmp_rows, _cmp_weights(p))
    nc, ns = T // CMP_BLK, -(-T // SEL_BLK)
    tpos = jnp.arange(T, dtype=I32)
    bias_c = _rel_bias(table, tpos[:, None] - (jnp.arange(nc, dtype=I32) * CMP_BLK + CMP_BLK - 1)[None, :])
    o_c, mb = _nsa_cmp_prompt(q, kcvc, bias_c, B, T, tq)
    r = jnp.arange(tq, dtype=I32)
    far = table[REL_BUCKETS - 1][:, None, None]
    tz = jnp.stack([_rel_bias(table, r[:, None] - r[None, :]) - far,
                    _rel_bias(table, tq + r[:, None] - r[None, :]) - far], axis=1)
    e_blk = jnp.asarray((np.arange(T)[:, None] // SEL_BLK == np.arange(ns)[None, :]).astype(np.float32), BF16)
    q_in = (q, pl.BlockSpec((tq, NSA_HEADS * HD), lambda b, g, i: (b * nq + i, 0)))
    kv_in = lambda a, c: (a, pl.BlockSpec((T, KVW), lambda b, g, i: (b, c)))
    tz_in = (tz, pl.BlockSpec(tz.shape, lambda b, g, i: (0, 0, 0, 0)))
    cfg = dict(name="nsa_slc", tq=tq, hq=NSA_HEADS, hk=NSA_KVH, fox=False, bias=True, aug=ns, window=None)
    extra = [(mb, pl.BlockSpec((None, NSA_KVH, tq, ns), lambda b, g, i: (b, 0, i, 0))),
             (e_blk, pl.BlockSpec(e_blk.shape, lambda b, g, i: (0, 0))), tz_in]
    o_s = _flash(cfg, B, T, 1, q_in, kv_in(slc_rows, 0), kv_in(slc_rows, 1), extra, M)
    cfg = dict(name="nsa_win", tq=tq, hq=NSA_HEADS, hk=NSA_KVH, fox=False, bias=True, aug=0, window=WINDOW)
    o_w = _flash(cfg, B, T, 1, q_in, kv_in(win_rows, 0), kv_in(win_rows, 1), [tz_in], M)
    st.x = _nsa_out(st, o_c, o_s, o_w, gates, p, alpha, ln_g, ln_b, tm)
    shp = (B, T, 2, NSA_KVH, HD)
    keep = min(WINDOW, T)
    return cmp_rows.reshape(shp), slc_rows.reshape(shp), win_rows.reshape(shp)[:, T - keep:]


def _compress_paged(cache_cmp, page_table, cw):
    wc, w2c, b1 = cw
    B, npg = page_table.shape
    PAGE = cache_cmp.shape[1]
    hp = PAGE // HALF
    cache = cache_cmp.reshape(cache_cmp.shape[0], hp, HALF * 2 * KVW)
    nb = 2 if B % 2 == 0 else 1
    ins = [(cache, pl.BlockSpec((None, hp, HALF * 2 * KVW), lambda g, pt, bb=bb, j=j: (pt[g * nb + bb, j], 0, 0)))
           for bb in range(nb) for j in range(npg)]
    const = lambda a: (a, pl.BlockSpec(a.shape, lambda g, pt: (0,) * a.ndim))
    ins += [const(wc), const(w2c), const(b1)]
    rows = nb * npg * hp
    outs = [((B * npg * hp // 2, 2 * KVW), F32, pl.BlockSpec((rows // 2, 2 * KVW), lambda g, pt: (g, 0)))]
    scratch = [pltpu.VMEM((2 * KVW // LANES, rows, LANES), F32)] * 2
    body = lambda pt_ref, *refs: _compress_body(nb * npg, *refs)
    return _call(body, (B // nb,), ins, outs, scratch=scratch, prefetch=[page_table], name="nsa_compress_paged")[0]


def _nsa_decode(q, slc_new, win_new, kcvc, cache_slc, cache_win, page_table, gcol, consts, Tn, offset):
    B, npg = page_table.shape
    PAGE = cache_slc.shape[1]
    Wb = cache_win.shape[1]
    R = NSA_HEADS * Tn
    L = offset + Tn
    nc, ns = L // CMP_BLK, -(-L // SEL_BLK)
    n_sel = min(N_SEL, ns)
    nck = npg + 1
    qw = NSA_HEADS * HD
    cache = cache_slc.reshape(cache_slc.shape[0], PAGE, 2 * KVW)
    win = cache_win.reshape(B, Wb, 2 * KVW)
    scale = HD ** -0.5
    names = ["rep_t", "fold", "unfold", "bias_c", "mask_c", "pair", "e_blk", "bias_s", "bias_w"]
    cvals = [consts[n] for n in names]

    def body(pt_ref, *refs):
        pages = refs[:npg]
        (q_ref, sn_ref, wn_ref, kc_ref, vc_ref, win_ref, g_ref, rt_ref, fold_ref, unfold_ref, bc_ref, mc_ref, pair_ref,
         e_ref, bs_ref, bw_ref, o_ref, wout_ref) = refs[npg:]
        row_h = lax.broadcasted_iota(I32, (R, qw), 0) // Tn
        lane_h = lax.broadcasted_iota(I32, (R, qw), 1) // HD
        own = row_h == lane_h
        qrep = _d(rt_ref[...], q_ref[...])
        qbd = (_d(jnp.where(own, qrep, 0.0), fold_ref[...]) * scale).astype(BF16)
        pad = lambda x: jnp.concatenate([x, jnp.zeros((PAGE - Tn, x.shape[1]), x.dtype)], axis=0)

        pc = _masked_softmax(_d_nt(qbd, kc_ref[...]) + bc_ref[...], mc_ref[...] > 0.0)
        o_c = _d(pc, vc_ref[...])
        imp = []
        for kv in range(NSA_KVH):
            a = pc[kv * NSA_G * Tn:(kv * NSA_G + 1) * Tn]
            for gg in range(1, NSA_G):
                a = a + pc[(kv * NSA_G + gg) * Tn:(kv * NSA_G + gg + 1) * Tn]
            imp.append(a)
        imp = jnp.concatenate(imp, axis=0)
        tpos = offset + lax.broadcasted_iota(I32, (NSA_KVH * Tn, 1), 0) % Tn
        mb = _top_blocks(_block_scores(_d_x3(imp, pair_ref[...]), tpos, ns), n_sel)
        mb = jnp.concatenate([mb[kv * Tn:(kv + 1) * Tn] for kv in range(NSA_KVH) for _ in range(NSA_G)], axis=0)

        sn = pad(sn_ref[...])
        ks = [pages[j][:, :KVW] for j in range(npg)] + [sn[:, :KVW]]
        vs = [pages[j][:, KVW:] for j in range(npg)] + [sn[:, KVW:]]
        s = jnp.concatenate([_d_nt(qbd, ks[j]) + _d(mb, e_ref[:, j * PAGE:(j + 1) * PAGE]) for j in range(nck)], axis=1)
        s = s + bs_ref[...]
        p = jnp.exp(s - jnp.max(s, axis=-1, keepdims=True))
        o_s = _d(p[:, 0:PAGE], vs[0])
        for j in range(1, nck):
            o_s = o_s + _d(p[:, j * PAGE:(j + 1) * PAGE], vs[j])
        o_s = o_s / jnp.sum(p, axis=-1, keepdims=True)

        w = win_ref[...]
        wn = pad(wn_ref[...])
        s = jnp.concatenate([_d_nt(qbd, w[:, :KVW]), _d_nt(qbd, wn[:, :KVW])], axis=1) + bw_ref[...]
        p = jnp.exp(s - jnp.max(s, axis=-1, keepdims=True))
        o_w = (_d(p[:, :Wb], w[:, KVW:]) + _d(p[:, Wb:], wn[:, KVW:])) / jnp.sum(p, axis=-1, keepdims=True)
        wout_ref[...] = jnp.concatenate([w[Tn:], wn_ref[...]], axis=0)

        sg = _sigmoid(g_ref[...])
        o = sg[:, 0:1] * o_c + sg[:, 1:2] * o_s + sg[:, 2:3] * o_w
        of = jnp.where(own, _d_x3(o, unfold_ref[...]), 0.0)
        out = of[0:Tn]
        for h in range(1, NSA_HEADS):
            out = out + of[h * Tn:(h + 1) * Tn]
        o_ref[...] = out

    c2 = lambda b, pt: (0, 0)
    ins = [(cache, pl.BlockSpec((None, PAGE, 2 * KVW), lambda b, pt, j=j: (pt[b, j], 0, 0))) for j in range(npg)]
    ins += [(q, pl.BlockSpec((Tn, qw), lambda b, pt: (b, 0))),
            (slc_new, pl.BlockSpec((Tn, 2 * KVW), lambda b, pt: (b, 0))),
            (win_new, pl.BlockSpec((Tn, 2 * KVW), lambda b, pt: (b, 0))),
            (kcvc, pl.BlockSpec((nc, KVW), lambda b, pt: (b, 0))),
            (kcvc, pl.BlockSpec((nc, KVW), lambda b, pt: (b, 1))),
            (win, pl.BlockSpec((None, Wb, 2 * KVW), lambda b, pt: (b, 0, 0))),
            (gcol, pl.BlockSpec((None, R, 3), lambda b, pt: (b, 0, 0)))]
    ins += [(a, pl.BlockSpec(a.shape, c2)) for a in cvals]
    outs = [((B * Tn, qw), F32, pl.BlockSpec((Tn, qw), lambda b, pt: (b, 0))),
            ((B, Wb, 2 * KVW), F32, pl.BlockSpec((None, Wb, 2 * KVW), lambda b, pt: (b, 0, 0)))]
    return _call(body, (B,), ins, outs, prefetch=[page_table], name="nsa_decode")


def _nsa_decode_consts(table, Tn, offset, npg, PAGE, Wb):
    R = NSA_HEADS * Tn
    L = offset + Tn
    nc, ns = L // CMP_BLK, -(-L // SEL_BLK)
    Lp = (npg + 1) * PAGE
    tpos = offset + jnp.arange(Tn, dtype=I32)
    rows = lambda x: x.reshape(R, x.shape[-1])
    cmp_end = jnp.arange(nc, dtype=I32) * CMP_BLK + CMP_BLK - 1
    dist_c = tpos[:, None] - cmp_end[None, :]
    spos = jnp.arange(Lp, dtype=I32)
    dist_s = tpos[:, None] - spos[None, :]
    ok_s = (dist_s >= 0) & (spos[None, :] < L)
    col = jnp.arange(Wb + PAGE, dtype=I32)
    wpos = offset - Wb + col
    dist_w = tpos[:, None] - wpos[None, :]
    ok_w = (dist_w >= 0) & (dist_w < WINDOW) & (wpos[None, :] >= 0) & (col[None, :] < Wb + Tn)
    tile = lambda m: jnp.tile(m[None], (NSA_HEADS, 1, 1))
    fold = np.zeros((NSA_HEADS, HD, NSA_KVH, HD), np.float32)
    for h in range(NSA_HEADS):
        fold[h, :, h // NSA_G, :] = np.eye(HD)
    fold = fold.reshape(NSA_HEADS * HD, KVW)
    return dict(
        rep_t=_rep_mat(R, Tn, 0), fold=jnp.asarray(fold, BF16), unfold=jnp.asarray(fold.T, BF16),
        bias_c=rows(_rel_bias(table, dist_c)), mask_c=rows(tile((dist_c >= 0).astype(F32))), pair=_pair_mat(nc, ns),
        e_blk=jnp.asarray((np.arange(ns)[:, None] == np.arange(Lp)[None, :] // SEL_BLK).astype(np.float32), BF16),
        bias_s=rows(_rel_bias(table, dist_s) + tile(jnp.where(ok_s, 0.0, NEG))),
        bias_w=rows(_rel_bias(table, dist_w) + tile(jnp.where(ok_w, 0.0, NEG))))


def _nsa_sample(st, cache_cmp, cache_slc, cache_win, page_table, p, alpha, ln_g, ln_b):
    B, Tn, M = st.B, st.T, st.M
    tm = min(256, M)
    npg = page_table.shape[1]
    PAGE = cache_slc.shape[1]
    Wb = cache_win.shape[1]
    offset = npg * PAGE
    assert offset % CMP_BLK == 0 and Tn < CMP_BLK and Wb == WINDOW and Tn % SUBLANES == 0
    q, cmp_rows, slc_rows, win_rows, gates = _nsa_proj(st, p, tm)
    kcvc = _compress_paged(cache_cmp, page_table, _cmp_weights(p))
    gcol = gates.reshape(B, Tn, LANES)[:, :, :3 * NSA_HEADS].reshape(B, Tn, 3, NSA_HEADS)
    gcol = gcol.transpose(0, 3, 1, 2).reshape(B, NSA_HEADS * Tn, 3)
    consts = _nsa_decode_consts(p["rel_bias"], Tn, offset, npg, PAGE, Wb)
    o, wout = _nsa_decode(q, slc_rows, win_rows, kcvc, cache_slc, cache_win, page_table, gcol, consts, Tn, offset)
    st.x = _out_proj(st, o, p["nsa_w_o"], alpha, ln_g, ln_b, tm, "nsa_out_s")
    shp = (B, Tn, 2, NSA_KVH, HD)
    return cmp_rows.reshape(shp), slc_rows.reshape(shp), wout.reshape(B, Wb, 2, NSA_KVH, HD)


def kernel(x_prompt, x_sample, state_rwkv_wkv, state_rwkv_shift, cache_nsa_cmp, cache_nsa_slc, cache_nsa_win, cache_fox_kv, cache_fox_logf, state_gdn_S, state_gdn_conv, page_table, c_prompt, c_sample, w_mod, b_mod, ln_g, ln_b, moe_w_group, moe_b_group, moe_w_router, moe_b_router, moe_w1, moe_w3, moe_w2, rwkv_mu, rwkv_w_rkv, rwkv_w0, rwkv_w1, rwkv_w2, rwkv_a0, rwkv_a1, rwkv_a2, rwkv_g1, rwkv_g2, rwkv_k_k, rwkv_k_a, rwkv_r_k, rwkv_ln_w, rwkv_ln_b, rwkv_w_o, nsa_w_in, nsa_cmp_w1, nsa_cmp_b1, nsa_cmp_w2, nsa_w_o, rel_bias, fox_w_in, fox_b_f, fox_w_o, gdn_w_in, gdn_conv_w, gdn_A_log, gdn_dt_bias, gdn_norm_w, gdn_w_o):
    p = dict(locals())
    Bp, T, _ = x_prompt.shape
    Bs, Tn, _ = x_sample.shape
    depth = w_mod.shape[0]
    alpha = (2 * depth) ** 0.25
    sp = _Stream(x_prompt.reshape(Bp * T, D), Bp, T, min(512, T))
    ss = _Stream(x_sample.reshape(Bs * Tn, D), Bs, Tn, min(256, Bs * Tn))
    nc = Bp + Bs
    c_all = jnp.pad(jnp.concatenate([c_prompt, c_sample], axis=0), ((0, -nc % SUBLANES), (0, 0)))
    out = {}
    for layer in range(depth):
        m6 = _ada(c_all, w_mod, b_mod, layer)
        sp.set_mods(m6[:Bp])
        ss.set_mods(m6[Bp:nc])
        g0 = ln_g[layer, 0].reshape(1, D)
        b0 = ln_b[layer, 0].reshape(1, D)
        kind = layer % 4
        if kind == 0:
            nh = D // RWKV_HSZ
            out["wkv_p"], out["shift_p"] = _rwkv_layer(sp, jnp.zeros((Bp, D), F32),
                                                       jnp.zeros((Bp, nh, RWKV_HSZ, RWKV_HSZ), F32), p, alpha, g0, b0)
            out["wkv_s"], out["shift_s"] = _rwkv_layer(ss, state_rwkv_shift, state_rwkv_wkv, p, alpha, g0, b0)
        elif kind == 1:
            out["cmp_p"], out["slc_p"], out["win_p"] = _nsa_prompt(sp, p, alpha, g0, b0)
            out["cmp_s"], out["slc_s"], out["win_s"] = _nsa_sample(ss, cache_nsa_cmp, cache_nsa_slc, cache_nsa_win,
                                                                   page_table, p, alpha, g0, b0)
        elif kind == 2:
            out["kv_p"], out["logf_p"] = _fox_prompt(sp, p, alpha, g0, b0)
            out["kv_s"], out["logf_s"] = _fox_sample(ss, cache_fox_kv, cache_fox_logf, page_table, p, alpha, g0, b0)
        else:
            out["S_p"], out["conv_p"] = _gdn_layer(sp, jnp.zeros((Bp, GDN_CONV - 1, 3 * D), F32),
                                                   jnp.zeros((Bp, GDN_HEADS, GDN_HSZ, GDN_HSZ), F32), p, alpha, g0, b0)
            out["S_s"], out["conv_s"] = _gdn_layer(ss, state_gdn_conv, state_gdn_S, p, alpha, g0, b0)
        _moe_layer([sp, ss], layer, alpha, p)
    return (sp.x.reshape(Bp, T, D), ss.x.reshape(Bs, Tn, D), out["wkv_p"], out["wkv_s"], out["shift_p"], out["shift_s"],
            out["cmp_p"], out["cmp_s"], out["slc_p"], out["slc_s"], out["win_p"], out["win_s"],
            out["kv_p"], out["kv_s"], out["logf_p"], out["logf_s"], out["S_p"], out["S_s"], out["conv_p"], out["conv_s"])
```

```python
import functools
import math

import numpy as np
import jax
import jax.numpy as jnp
from jax import lax
from jax.experimental import pallas as pl
from jax.experimental.pallas import tpu as pltpu

F32 = jnp.float32
BF16 = jnp.bfloat16
I32 = jnp.int32
NEG = -1e30
LN_EPS = 1e-5
D = 1024
LANES = 128
SUBLANES = 8
MXU_TILE = 256
VMEM_LIMIT_MB = 56

RWKV_HSZ = 64
RWKV_GN_EPS = 64e-5
NSA_HEADS, NSA_KVH, HD = 16, 4, 64
NSA_G = NSA_HEADS // NSA_KVH
CMP_BLK, SEL_BLK, N_SEL, WINDOW = 32, 64, 16, 512
FORCE_SCORE = 1e4
REL_BUCKETS, REL_MAX_DIST = 32, 128
FOX_HEADS = 16
GDN_HEADS, GDN_HSZ, GDN_CONV = 8, 128, 4
MOE_GROUPS, MOE_EPG, MOE_BLK = 4, 8, 256
MOE_EXPERTS = MOE_GROUPS * MOE_EPG
ATT_T = 128
SEL_NEG = -65536.0
PAGES_PER_STEP = 4


def _d(a, b):
    return jnp.dot(a.astype(BF16), b.astype(BF16), preferred_element_type=F32)


def _d_nt(a, b):
    return lax.dot_general(a.astype(BF16), b.astype(BF16), (((1,), (1,)), ((), ())),
                           preferred_element_type=F32)


def _split3(x):
    h = x.astype(BF16)
    r1 = x - h.astype(F32)
    m = r1.astype(BF16)
    l = (r1 - m.astype(F32)).astype(BF16)
    return h, m, l


def _d_x3(x, sel):
    h, m, l = _split3(x)
    return _d(h, sel) + _d(m, sel) + _d(l, sel)


def _d_3x(sel, x):
    h, m, l = _split3(x)
    return _d(sel, h) + _d(sel, m) + _d(sel, l)


def _d_f32(x, w):
    xh, xm, xl = _split3(x)
    wh, wm, wl = _split3(w)
    return (_d(xh, wh) + _d(xh, wm) + _d(xm, wh)) + (_d(xh, wl) + _d(xl, wh) + _d(xm, wm))


def _sigmoid(x):
    return 1.0 / (1.0 + jnp.exp(-x))


def _softplus(x):
    return jnp.maximum(x, 0.0) + jnp.log(1.0 + jnp.exp(-jnp.abs(x)))


def _silu(x):
    return x * _sigmoid(x)


def _gelu_tanh(x):
    return 0.5 * x * (1.0 + jnp.tanh(math.sqrt(2.0 / math.pi) * (x + 0.044715 * (x * x * x))))


def _layer_norm(z, g, b):
    mu = jnp.mean(z, axis=-1, keepdims=True)
    zc = z - mu
    var = jnp.mean(zc * zc, axis=-1, keepdims=True)
    return zc * lax.rsqrt(var + LN_EPS) * g + b


def _modulate(x, sc, sh):
    return x * (1.0 + sc) + sh


def _res_ln(alpha, y, xres, gate, g, b):
    return _layer_norm(alpha * xres + (1.0 + gate) * y, g, b)


@functools.lru_cache(maxsize=None)
def _seg_np(hsz):
    head = np.arange(D) // hsz
    hs = (head[:, None] == np.arange(LANES)[None, :]).astype(np.float32)
    return hs


def _seg_consts(hsz):
    hs = _seg_np(hsz)
    return jnp.asarray(hs, BF16), jnp.asarray(hs.T, BF16)


def _scan_consts(hsz):
    nh = D // hsz
    head = np.arange(D) // hsz
    slot_j = np.arange(LANES) // 16
    slot_h = np.arange(LANES) % 16
    hexp = np.zeros((SUBLANES, LANES, D), np.float32)
    for j in range(SUBLANES):
        hexp[j] = ((slot_j[:, None] == j) & (slot_h[:, None] == head[None, :]) & (slot_h[:, None] < nh))
    hsum = np.transpose(hexp, (0, 2, 1))
    blk = np.arange(MXU_TILE) // hsz
    bd = (blk[:, None] == blk[None, :]).astype(np.float32)
    return jnp.asarray(hexp, BF16), jnp.asarray(hsum, BF16), jnp.asarray(bd, BF16)


def _call(body, grid, ins, outs, scratch=(), name=None, prefetch=None, aliases=None):
    arrays = [a for a, _ in ins]
    in_specs = [s for _, s in ins]
    out_shape = [jax.ShapeDtypeStruct(s, d) for s, d, _ in outs]
    out_specs = [s for _, _, s in outs]
    params = pltpu.CompilerParams(dimension_semantics=("arbitrary",) * len(grid),
                                  vmem_limit_bytes=VMEM_LIMIT_MB << 20)
    kw = {}
    if aliases:
        kw["input_output_aliases"] = aliases
    if prefetch is None:
        fn = pl.pallas_call(body, grid=grid, in_specs=in_specs, out_specs=out_specs, out_shape=out_shape,
                            scratch_shapes=list(scratch), compiler_params=params, name=name, **kw)
        res = fn(*arrays)
    else:
        gs = pltpu.PrefetchScalarGridSpec(num_scalar_prefetch=len(prefetch), grid=grid, in_specs=in_specs,
                                          out_specs=out_specs, scratch_shapes=list(scratch))
        fn = pl.pallas_call(body, grid_spec=gs, out_shape=out_shape, compiler_params=params, name=name, **kw)
        res = fn(*prefetch, *arrays)
    return list(res)


def _full(a):
    nd = a.ndim
    return (a, pl.BlockSpec(a.shape, lambda *_: (0,) * nd))


def _rows(a, tm):
    return (a, pl.BlockSpec((tm, a.shape[1]), lambda i, *_: (i, 0)))


def _rows_out(M, C, tm, dtype=F32):
    return ((M, C), dtype, pl.BlockSpec((tm, C), lambda i, *_: (i, 0)))


class _Stream:
    def __init__(self, x, B, T, tm):
        self.x, self.B, self.T, self.tm = x, B, T, tm
        self.M = B * T
        self.m6 = None

    def set_mods(self, m6):
        self.m6 = m6
        self.rep = jnp.repeat(m6, self.T, axis=0) if self.T < self.tm else None

    def mod(self, c, tm=None):
        tm = tm or self.tm
        if self.T % tm == 0:
            tpb = self.T // tm
            a = self.m6[:, c * D:(c + 1) * D].reshape(self.B, 1, D)
            return (a, pl.BlockSpec((None, 1, D), lambda i, *_: (i // tpb, 0, 0)))
        assert tm % self.T == 0 and self.M % tm == 0
        a = self.rep[:, c * D:(c + 1) * D].reshape(self.M // tm, tm, D)
        return (a, pl.BlockSpec((None, tm, D), lambda i, *_: (i, 0, 0)))


def _mm(x, w_in, *, tm, pro=None, pro_ins=(), epi=None, epi_ins=(), outs=None, name="mm"):
    M, K = x.shape
    n_pro, n_epi = len(pro_ins), len(epi_ins)
    n_out = len(outs)

    def body(*refs):
        x_ref = refs[0]
        pro_refs = refs[1:1 + n_pro]
        w_ref = refs[1 + n_pro]
        epi_refs = refs[2 + n_pro:2 + n_pro + n_epi]
        out_refs = refs[2 + n_pro + n_epi:2 + n_pro + n_epi + n_out]
        a = x_ref[...]
        if pro is not None:
            a = pro(a, *[r[...] for r in pro_refs])
        acc = _d(a, w_ref[...])
        res = epi(acc, *[r[...] for r in epi_refs]) if epi is not None else (acc,)
        for o, r in zip(out_refs, res):
            o[...] = r.astype(o.dtype)

    ins = [_rows(x, tm)] + list(pro_ins) + [w_in] + list(epi_ins)
    return _call(body, (M // tm,), ins, outs, name=name)


def _ada(c_all, w_mod, b_mod, layer):
    Mp = c_all.shape[0]
    N = w_mod.shape[2]
    tn = 1536

    def body(c_ref, w_ref, b_ref, o_ref):
        o_ref[...] = _d(_silu(c_ref[...]), w_ref[...]) + b_ref[...]

    ins = [(c_all, pl.BlockSpec((Mp, D), lambda j: (0, 0))),
           (w_mod, pl.BlockSpec((None, D, tn), lambda j: (layer, 0, j))),
           (b_mod.reshape(b_mod.shape[0], 1, N), pl.BlockSpec((None, 1, tn), lambda j: (layer, 0, j)))]
    outs = [((Mp, N), F32, pl.BlockSpec((Mp, tn), lambda j: (0, j)))]
    return _call(body, (N // tn,), ins, outs, name="ada_mod")[0]


def _pick_tile(*sizes, cap=512):
    t = cap
    while t > SUBLANES and any(s % t for s in sizes):
        t //= 2
    assert all(s % t == 0 for s in sizes), sizes
    return t


def _route(lg):
    lane = lax.broadcasted_iota(I32, lg.shape, 1)
    big = jnp.int32(1 << 20)
    isg = lane < MOE_GROUPS
    gl = jnp.where(isg, lg, NEG)
    gmax = jnp.max(gl, axis=-1, keepdims=True)
    gsel = jnp.min(jnp.where(gl == gmax, lane, big), axis=-1, keepdims=True)
    gsum = jnp.sum(jnp.where(isg, jnp.exp(gl - gmax), 0.0), axis=-1, keepdims=True)
    gw = 1.0 / gsum
    lo = MOE_GROUPS + MOE_EPG * gsel
    ise = (lane >= lo) & (lane < lo + MOE_EPG)
    el = jnp.where(ise, lg, NEG)
    emax = jnp.max(el, axis=-1, keepdims=True)
    ep = jnp.where(ise, jnp.exp(el - emax), 0.0)
    prob = ep / jnp.sum(ep, axis=-1, keepdims=True)
    pm = jnp.where(ise, prob, -1.0)
    p1 = jnp.max(pm, axis=-1, keepdims=True)
    i1 = jnp.min(jnp.where(pm == p1, lane, big), axis=-1, keepdims=True)
    pm2 = jnp.where(lane == i1, -1.0, pm)
    p2 = jnp.max(pm2, axis=-1, keepdims=True)
    i2 = jnp.min(jnp.where(pm2 == p2, lane, big), axis=-1, keepdims=True)
    den = p1 + p2
    w1 = gw * p1 / den
    w2 = gw * p2 / den
    e1 = (i1 - MOE_GROUPS).astype(F32)
    e2 = (i2 - MOE_GROUPS).astype(F32)
    return jnp.where(lane == 0, e1, jnp.where(lane == 1, e2, jnp.where(lane == 2, w1, jnp.where(lane == 3, w2, 0.0))))


NSEG = D // LANES


def _to_tiles(ref, x):
    for s in range(NSEG):
        ref[:, s, :] = x[:, s * LANES:(s + 1) * LANES]


def _from_tiles(ref):
    return jnp.concatenate([ref[:, s, :] for s in range(NSEG)], axis=1)


def _moe_router(st, wgr, bgr):
    tm = st.tm

    def body(x_ref, sc_ref, sh_ref, w_ref, b_ref, h_ref, r_ref):
        h = _modulate(x_ref[...], sc_ref[...], sh_ref[...])
        _to_tiles(h_ref, h)
        r_ref[...] = _route(_d_f32(h, w_ref[...]) + b_ref[...])

    ins = [_rows(st.x, tm), st.mod(4), st.mod(3), _full(wgr), _full(bgr)]
    outs = [((st.M, NSEG, LANES), F32, pl.BlockSpec((tm, NSEG, LANES), lambda i: (i, 0, 0))),
            _rows_out(st.M, LANES, tm)]
    return _call(body, (st.M // tm,), ins, outs, name="moe_router")


def _moe_counts(rinfo, R):
    Mtot = rinfo.shape[0]
    nt = Mtot // R

    def body(r_ref, o_ref):
        j = pl.program_id(0)
        t = pl.program_id(1)

        @pl.when((j == 0) & (t == 0))
        def _():
            o_ref[...] = jnp.zeros_like(o_ref)

        xt = r_ref[...].T
        row = jnp.where(j == 0, xt[0:1, :], xt[1:2, :])
        sub = lax.broadcasted_iota(I32, (LANES, R), 0).astype(F32)
        oh = jnp.where(sub == row, 1.0, 0.0)
        o_ref[...] += jnp.sum(oh, axis=1, keepdims=True)

    ins = [(rinfo, pl.BlockSpec((R, LANES), lambda j, t: (t, 0)))]
    outs = [((LANES, LANES), F32, pl.BlockSpec((LANES, LANES), lambda j, t: (0, 0)))]
    return _call(body, (2, nt), ins, outs, name="moe_counts")[0]


def _moe_dest(rinfo, pstart, R):
    Mtot = rinfo.shape[0]
    nt = Mtot // R
    upper = jnp.asarray(np.triu(np.ones((R, R), np.float32), 1), BF16)

    def body(r_ref, p_ref, u_ref, o_ref, carry):
        j = pl.program_id(0)
        t = pl.program_id(1)

        @pl.when((j == 0) & (t == 0))
        def _():
            carry[...] = jnp.zeros_like(carry)

        xt = r_ref[...].T
        row = jnp.where(j == 0, xt[0:1, :], xt[1:2, :])
        sub = lax.broadcasted_iota(I32, (LANES, R), 0).astype(F32)
        oh = jnp.where(sub == row, 1.0, 0.0)
        cum = _d(oh, u_ref[...])
        base = carry[:, 0:1] + p_ref[:, 0:1]
        dest = jnp.sum(oh * (cum + base), axis=0, keepdims=True)
        o_ref[...] = dest.astype(I32)
        carry[...] += jnp.sum(oh, axis=1, keepdims=True)

    ins = [(rinfo, pl.BlockSpec((R, LANES), lambda j, t: (t, 0))), _full(pstart), _full(upper)]
    outs = [((2 * nt, 1, R), I32, pl.BlockSpec((None, 1, R), lambda j, t: (j * nt + t, 0, 0)))]
    return _call(body, (2, nt), ins, outs, scratch=[pltpu.VMEM((LANES, LANES), F32)], name="moe_dest")[0]


def _moe_ffn(h3, slots, blk_expert, nvalid, w1, w3, w2, layer, Mtot):
    nblk = slots.shape[0] // MOE_BLK
    FF = w1.shape[-1]
    any_spec = pl.BlockSpec(memory_space=pl.ANY)

    def body(be_ref, nv_ref, slot_ref, h_ref, w1_ref, w3_ref, w2_ref, y_ref, xbuf, ybuf, sem_in, sem_out):
        i = pl.program_id(0)
        nv = nv_ref[0]
        buf = i % 2

        def gather(blk, b, r):
            s = slot_ref[blk * MOE_BLK + r]
            tok = jnp.where(s >= Mtot, s - Mtot, s)
            return pltpu.make_async_copy(h_ref.at[tok], xbuf.at[b, r], sem_in.at[b])

        def scatter(blk, b, r):
            return pltpu.make_async_copy(ybuf.at[b, r], y_ref.at[slot_ref[blk * MOE_BLK + r]], sem_out.at[b])

        def each_row(fn, blk, b, start, only_real):
            def step(r, c):
                def go():
                    cp = fn(blk, b, r)
                    cp.start() if start else cp.wait()
                if only_real:
                    pl.when(slot_ref[blk * MOE_BLK + r] < 2 * Mtot)(go)
                else:
                    go()
                return c
            lax.fori_loop(0, MOE_BLK, step, 0, unroll=8)

        @pl.when(i < nv)
        def _():
            @pl.when(i == 0)
            def _():
                each_row(gather, 0, 0, True, False)

            @pl.when(i + 1 < nv)
            def _():
                each_row(gather, i + 1, 1 - buf, True, False)

            each_row(gather, i, buf, False, False)
            x = _from_tiles(xbuf.at[buf]).astype(BF16)
            a = _d(x, w1_ref[...])
            b = _d(x, w3_ref[...])
            y = _d(_silu(a) * b, w2_ref[...])

            @pl.when(i >= 2)
            def _():
                each_row(scatter, i - 2, buf, False, True)

            _to_tiles(ybuf.at[buf], y)
            each_row(scatter, i, buf, True, True)

            @pl.when(i == nv - 1)
            def _():
                @pl.when(i >= 1)
                def _():
                    each_row(scatter, i - 1, 1 - buf, False, True)

                each_row(scatter, i, buf, False, True)

    def blk(i, be, nv, sl):
        return be[jnp.minimum(i, nv[0] - 1)]

    ins = [(h3, any_spec),
           (w1, pl.BlockSpec((None, None, D, FF), lambda i, be, nv, sl: (layer, blk(i, be, nv, sl), 0, 0))),
           (w3, pl.BlockSpec((None, None, D, FF), lambda i, be, nv, sl: (layer, blk(i, be, nv, sl), 0, 0))),
           (w2, pl.BlockSpec((None, None, FF, D), lambda i, be, nv, sl: (layer, blk(i, be, nv, sl), 0, 0)))]
    outs = [((2 * Mtot, NSEG, LANES), F32, any_spec)]
    scratch = [pltpu.VMEM((2, MOE_BLK, NSEG, LANES), F32), pltpu.VMEM((2, MOE_BLK, NSEG, LANES), F32),
               pltpu.SemaphoreType.DMA((2,)), pltpu.SemaphoreType.DMA((2,))]
    return _call(body, (nblk,), ins, outs, scratch=scratch, prefetch=[blk_expert, nvalid, slots], name="moe_ffn")[0]


def _moe_combine(st, yslot, rinfo_all, off, Mtot, alpha, ln_g, ln_b):
    tm = _pick_tile(st.M, off, Mtot, cap=st.tm)
    b0, b1, br = off // tm, (Mtot + off) // tm, off // tm

    def body(y0_ref, y1_ref, r_ref, x_ref, gate_ref, g_ref, b_ref, o_ref):
        r = r_ref[...]
        y = r[:, 2:3] * _from_tiles(y0_ref) + r[:, 3:4] * _from_tiles(y1_ref)
        o_ref[...] = _res_ln(alpha, y, x_ref[...], gate_ref[...], g_ref[...], b_ref[...])

    ins = [(yslot, pl.BlockSpec((tm, NSEG, LANES), lambda i: (b0 + i, 0, 0))),
           (yslot, pl.BlockSpec((tm, NSEG, LANES), lambda i: (b1 + i, 0, 0))),
           (rinfo_all, pl.BlockSpec((tm, LANES), lambda i: (br + i, 0))),
           _rows(st.x, tm), st.mod(5, tm), _full(ln_g), _full(ln_b)]
    return _call(body, (st.M // tm,), ins, [_rows_out(st.M, D, tm)], name="moe_combine")[0]


def _moe_layer(streams, layer, alpha, p):
    wgr = jnp.zeros((D, LANES), F32).at[:, :MOE_GROUPS].set(p["moe_w_group"][layer])
    wgr = wgr.at[:, MOE_GROUPS:MOE_GROUPS + MOE_EXPERTS].set(p["moe_w_router"][layer])
    bgr = jnp.zeros((1, LANES), F32).at[0, :MOE_GROUPS].set(p["moe_b_group"][layer])
    bgr = bgr.at[0, MOE_GROUPS:MOE_GROUPS + MOE_EXPERTS].set(p["moe_b_router"][layer])
    hs, rs = zip(*[_moe_router(st, wgr, bgr) for st in streams])
    h3 = jnp.concatenate(list(hs) + [jnp.zeros((SUBLANES, NSEG, LANES), F32)], axis=0)
    rinfo = jnp.concatenate(rs, axis=0)
    Mtot = rinfo.shape[0]
    R = _pick_tile(Mtot)
    counts = _moe_counts(rinfo, R)[:MOE_EXPERTS, 0].astype(I32)
    padded = (counts + MOE_BLK - 1) // MOE_BLK * MOE_BLK
    pad_end = jnp.cumsum(padded)
    pstart = jnp.zeros((LANES,), F32).at[:MOE_EXPERTS].set((pad_end - padded).astype(F32))
    pstart = jnp.broadcast_to(pstart[:, None], (LANES, LANES))
    nblk = -(-2 * Mtot // MOE_BLK) + MOE_EXPERTS
    blk_expert = jnp.minimum(jnp.searchsorted(pad_end, jnp.arange(nblk, dtype=I32) * MOE_BLK, side="right"),
                             MOE_EXPERTS - 1).astype(I32)
    nvalid = (pad_end[-1:] // MOE_BLK).astype(I32)
    dest = _moe_dest(rinfo, pstart, R).reshape(-1)
    slots = jnp.full((nblk * MOE_BLK,), 2 * Mtot, I32).at[dest].set(jnp.arange(2 * Mtot, dtype=I32))
    yslot = _moe_ffn(h3, slots, blk_expert, nvalid, p["moe_w1"], p["moe_w3"], p["moe_w2"], layer, Mtot)
    ln_g = p["ln_g"][layer, 1].reshape(1, D)
    ln_b = p["ln_b"][layer, 1].reshape(1, D)
    off = 0
    for st in streams:
        st.x = _moe_combine(st, yslot, rinfo, off, Mtot, alpha, ln_g, ln_b)
        off += st.M


def _scan_body(nbg, nv, tb, w_ref, kkn_ref, b_ref, k_ref, r_ref, vt_ref, s0_ref, hexp_ref, hsum_ref, bd_ref,
               o_ref, sf_ref, s_scr):
    t = pl.program_id(1)

    @pl.when(t == 0)
    def _():
        s_scr[...] = s0_ref[...]

    def sub(sb, carry):
        base = pl.multiple_of(sb * SUBLANES, SUBLANES)
        rows = [[ref[bb, pl.ds(base, SUBLANES), :] for bb in range(nbg)]
                for ref in (w_ref, kkn_ref, b_ref, k_ref, r_ref)]
        vt = vt_ref[:, sb].reshape(nbg * nv, LANES).astype(BF16)
        oacc = jnp.zeros((nbg * nv, LANES), F32)
        for j in range(SUBLANES):
            S = [s_scr[bb] for bb in range(nbg)]
            P = jnp.concatenate([S[bb] * rows[1][bb][j:j + 1] for bb in range(nbg)], axis=0).astype(BF16)
            sa = jnp.concatenate([_d(P[:, c0:c0 + MXU_TILE], bd_ref[...]) for c0 in range(0, D, MXU_TILE)], axis=1)
            vb = _d(vt, hexp_ref[j])
            P2 = []
            for bb in range(nbg):
                sl = slice(bb * nv, (bb + 1) * nv)
                Sn = S[bb] * rows[0][bb][j:j + 1] + sa[sl] * rows[2][bb][j:j + 1] + vb[sl] * rows[3][bb][j:j + 1]
                s_scr[bb] = Sn
                P2.append(Sn * rows[4][bb][j:j + 1])
            oacc = oacc + _d(jnp.concatenate(P2, axis=0), hsum_ref[j])
        o_ref[:, sb] = oacc.reshape(nbg, nv, LANES)
        return carry

    lax.fori_loop(0, tb // SUBLANES, sub, 0)

    @pl.when(t == pl.num_programs(1) - 1)
    def _():
        sf_ref[...] = s_scr[...]


def _delta_scan(w, kkn, b, k, r, v, S0, B, T, hsz):
    nh = D // hsz
    nv = hsz
    nbg = 4 if B % 4 == 0 else (2 if B % 2 == 0 else 1)
    tb = min(64, T)
    hexp, hsum, bd = _scan_consts(hsz)
    vt = v.reshape(B, T // SUBLANES, SUBLANES, nh, nv).transpose(0, 1, 4, 2, 3)
    vt = jnp.pad(vt, ((0, 0),) * 4 + ((0, 16 - nh),)).reshape(B, T // SUBLANES, nv, LANES)
    seq = lambda a: (a.reshape(B, T, D), pl.BlockSpec((nbg, tb, D), lambda g, t: (g, t, 0)))
    ins = [seq(w), seq(kkn), seq(b), seq(k), seq(r),
           (vt, pl.BlockSpec((nbg, tb // SUBLANES, nv, LANES), lambda g, t: (g, t, 0, 0))),
           (S0, pl.BlockSpec((nbg, nv, D), lambda g, t: (g, 0, 0))),
           _full(hexp), _full(hsum), _full(bd)]
    outs = [((B, T // SUBLANES, nv, LANES), F32,
             pl.BlockSpec((nbg, tb // SUBLANES, nv, LANES), lambda g, t: (g, t, 0, 0))),
            ((B, nv, D), F32, pl.BlockSpec((nbg, nv, D), lambda g, t: (g, 0, 0)))]
    body = functools.partial(_scan_body, nbg, nv, tb)
    op, sf = _call(body, (B // nbg, T // tb), ins, outs, scratch=[pltpu.VMEM((nbg, nv, D), F32)], name="delta_scan")
    o = op.reshape(B, T // SUBLANES, nv, SUBLANES, 16)[..., :nh].transpose(0, 1, 3, 4, 2).reshape(B * T, D)
    return o, sf


def _shifted_rows(h, first, period, shift=1):
    row = lax.broadcasted_iota(I32, h.shape, 0)
    return jnp.where(row % period < shift, first, pltpu.roll(h, shift, axis=0))


def _rwkv_prep(st, shift_prev, p, tm):
    long_seq = st.T % tm == 0
    tpb = st.T // tm if long_seq else 1
    hs, he = _seg_consts(RWKV_HSZ)
    row = lambda a: _full(a.reshape(1, D))
    wts = [_full(p["rwkv_mu"]), _full(p["rwkv_w_rkv"].astype(BF16)),
           _full(p["rwkv_w1"].astype(BF16)), _full(p["rwkv_w2"].astype(BF16)),
           _full(p["rwkv_a1"].astype(BF16)), _full(p["rwkv_a2"].astype(BF16)),
           _full(p["rwkv_g1"].astype(BF16)), _full(p["rwkv_g2"].astype(BF16)),
           row(p["rwkv_w0"]), row(p["rwkv_a0"]), row(p["rwkv_k_k"]), row(p["rwkv_k_a"]), _full(hs), _full(he)]
    if long_seq:
        nsub = tm // SUBLANES
        first_ins = [(st.x, pl.BlockSpec((SUBLANES, D), lambda i: (jnp.maximum(i * nsub - 1, 0), 0))),
                     (shift_prev.reshape(st.B, 1, D), pl.BlockSpec((None, 1, D), lambda i: (i // tpb, 0, 0)))]
    else:
        first_ins = [_rows(jnp.repeat(shift_prev, st.T, axis=0), tm)]
    nf = len(first_ins)

    def body(x_ref, sc_ref, sh_ref, *refs):
        first_refs, refs = refs[:nf], refs[nf:]
        (mu_ref, wrkv_ref, w1_ref, w2_ref, a1_ref, a2_ref, g1_ref, g2_ref, w0_ref, a0_ref, kk_ref, ka_ref,
         hs_ref, he_ref) = refs[:14]
        h_ref, r_ref, w_ref, k_ref, v_ref, kkn_ref, b_ref, g_ref = refs[14:]
        sc, sh = sc_ref[...], sh_ref[...]
        h = _modulate(x_ref[...], sc, sh)
        if long_seq:
            hh = _modulate(first_refs[0][...], sc, sh)[SUBLANES - 1:SUBLANES]
            first = jnp.where(pl.program_id(0) % tpb == 0, first_refs[1][...], hh)
            hprev = _shifted_rows(h, first, tm)
        else:
            hprev = _shifted_rows(h, first_refs[0][...], st.T)
        xx = hprev - h
        mu = mu_ref[...]
        xr, xw, xk, xv, xa, xg = [h + xx * mu[i:i + 1] for i in range(6)]
        r = _d(xr, wrkv_ref[0])
        k = _d(xk, wrkv_ref[1])
        v = _d(xv, wrkv_ref[2])
        logw = -_softplus(-(w0_ref[...] + _d(jnp.tanh(_d(xw, w1_ref[...])), w2_ref[...]))) - 0.5
        a = _sigmoid(a0_ref[...] + _d(_d(xa, a1_ref[...]), a2_ref[...]))
        g = _d(_sigmoid(_d(xg, g1_ref[...])), g2_ref[...])
        kk = k * kk_ref[...]
        inv = lax.rsqrt(_d_x3(kk * kk, hs_ref[...]) + 1e-6)
        kk = kk * _d_x3(inv, he_ref[...])
        h_ref[...] = h
        r_ref[...] = r
        w_ref[...] = jnp.exp(-jnp.exp(logw))
        k_ref[...] = k * (1.0 + (a - 1.0) * ka_ref[...])
        v_ref[...] = v
        kkn_ref[...] = -kk
        b_ref[...] = kk * a
        g_ref[...] = g

    ins = [_rows(st.x, tm), st.mod(1, tm), st.mod(0, tm)] + first_ins + wts
    outs = [_rows_out(st.M, D, tm) for _ in range(8)]
    return _call(body, (st.M // tm,), ins, outs, name="rwkv_prep")


def _rwkv_out(st, o, r, kmod, v, g, p, alpha, ln_g, ln_b, tm):
    hs, he = _seg_consts(RWKV_HSZ)
    inv_n = 1.0 / RWKV_HSZ

    def pro(o, r, k, v, g, lw, lb, rk, hs, he):
        mean = _d_x3(_d_x3(o, hs) * inv_n, he)
        c = o - mean
        rstd = lax.rsqrt(_d_x3(c * c, hs) * inv_n + RWKV_GN_EPS)
        on = c * _d_x3(rstd, he) * lw + lb
        bonus = _d_x3(_d_x3(r * k * rk, hs), he) * v
        return (on + bonus) * g

    def epi(acc, x, gate, g_, b_):
        return (_res_ln(alpha, acc, x, gate, g_, b_),)

    row = lambda a: _full(a.reshape(1, D))
    pro_ins = [_rows(a, tm) for a in (r, kmod, v, g)] + [row(p["rwkv_ln_w"]), row(p["rwkv_ln_b"]),
                                                       row(p["rwkv_r_k"]), _full(hs), _full(he)]
    epi_ins = [_rows(st.x, tm), st.mod(2, tm), _full(ln_g), _full(ln_b)]
    return _mm(o, _full(p["rwkv_w_o"].astype(BF16)), tm=tm, pro=pro, pro_ins=pro_ins, epi=epi, epi_ins=epi_ins,
               outs=[_rows_out(st.M, D, tm)], name="rwkv_out")[0]


def _rwkv_layer(st, shift_prev, wkv0, p, alpha, ln_g, ln_b):
    B, T = st.B, st.T
    tm = min(256, st.M)
    nh = D // RWKV_HSZ
    h, r, w, kmod, v, kkn, b, g = _rwkv_prep(st, shift_prev, p, tm)
    S0 = wkv0.transpose(0, 2, 1, 3).reshape(B, RWKV_HSZ, D)
    o, sf = _delta_scan(w, kkn, b, kmod, r, v, S0, B, T, RWKV_HSZ)
    st.x = _rwkv_out(st, o, r, kmod, v, g, p, alpha, ln_g, ln_b, tm)
    wkv = sf.reshape(B, RWKV_HSZ, nh, RWKV_HSZ).transpose(0, 2, 1, 3)
    return wkv, h.reshape(B, T, D)[:, -1]


def _pad_cols(w, n):
    return jnp.pad(w, ((0, 0), (0, n - w.shape[1])))


def _gdn_proj(st, p, tm):
    C = 3 * D
    w = _pad_cols(p["gdn_w_in"], C + D + LANES).astype(BF16)

    def epi(acc):
        return acc[:, :C], acc[:, C:C + D], acc[:, C + D:]

    outs = [_rows_out(st.M, C, tm), _rows_out(st.M, D, tm), _rows_out(st.M, LANES, tm)]
    return _mm(st.x, _full(w), tm=tm, pro=_modulate, pro_ins=[st.mod(1, tm), st.mod(0, tm)], epi=epi, outs=outs,
               name="gdn_proj")


def _gdn_conv(st, pre, ba, conv_buf, p, tm):
    C = 3 * D
    H = GDN_HEADS
    long_seq = st.T % tm == 0
    tpb = st.T // tm if long_seq else 1
    hs, he = _seg_consts(GDN_HSZ)
    hsn = _seg_np(GDN_HSZ)
    he_b = jnp.asarray(hsn.T, BF16)
    he_a = jnp.asarray(np.roll(hsn.T, H, axis=0), BF16)
    alog = jnp.zeros((1, LANES), F32).at[0, H:2 * H].set(p["gdn_A_log"])
    dtb = jnp.zeros((1, LANES), F32).at[0, H:2 * H].set(p["gdn_dt_bias"])
    if long_seq:
        nsub = tm // SUBLANES
        init8 = jnp.pad(conv_buf, ((0, 0), (SUBLANES - (GDN_CONV - 1), 0), (0, 0)))
        first_ins = [(pre, pl.BlockSpec((SUBLANES, C), lambda i: (jnp.maximum(i * nsub - 1, 0), 0))),
                     (init8, pl.BlockSpec((None, SUBLANES, C), lambda i: (i // tpb, 0, 0)))]
    else:
        padded = jnp.pad(conv_buf, ((0, 0), (0, st.T), (0, 0)))
        first_ins = [_rows(padded[:, GDN_CONV - 1 - j:GDN_CONV - 1 - j + st.T].reshape(st.M, C), tm)
                     for j in range(1, GDN_CONV)]
    nf = len(first_ins)

    def body(pre_ref, ba_ref, *refs):
        first_refs, refs = refs[:nf], refs[nf:]
        cw_ref, alog_ref, dtb_ref, hs_ref, he_ref, heb_ref, hea_ref = refs[:7]
        w_ref, kkn_ref, k_ref, q_ref, v_ref = refs[7:]
        x = pre_ref[...]
        if long_seq:
            halo = jnp.where(pl.program_id(0) % tpb == 0, first_refs[1][...], first_refs[0][...])
            big = jnp.concatenate([halo, x], axis=0)
            sh = [pltpu.roll(big, j, axis=0)[SUBLANES:] for j in range(1, GDN_CONV)]
        else:
            sh = [_shifted_rows(x, first_refs[j - 1][...], st.T, j) for j in range(1, GDN_CONV)]
        cw = cw_ref[...]
        conv = sh[2] * cw[0:1]
        conv = conv + sh[1] * cw[1:2]
        conv = conv + sh[0] * cw[2:3]
        conv = conv + x * cw[3:4]
        c = _silu(conv)
        q, k, v = c[:, :D], c[:, D:2 * D], c[:, 2 * D:]
        qn = q * _d_x3(lax.rsqrt(_d_x3(q * q, hs_ref[...]) + 1e-6), he_ref[...]) * (GDN_HSZ ** -0.5)
        kn = k * _d_x3(lax.rsqrt(_d_x3(k * k, hs_ref[...]) + 1e-6), he_ref[...])
        ba = ba_ref[...]
        beta = _d_x3(_sigmoid(ba), heb_ref[...])
        a = _d_x3(jnp.exp(-jnp.exp(alog_ref[...]) * _softplus(ba + dtb_ref[...])), hea_ref[...])
        w_ref[...] = a
        kkn_ref[...] = -(a * beta) * kn
        k_ref[...] = kn
        q_ref[...] = qn
        v_ref[...] = beta * v

    ins = [_rows(pre, tm), _rows(ba, tm)] + first_ins + [_full(p["gdn_conv_w"]), _full(alog), _full(dtb), _full(hs),
                                                         _full(he), _full(he_b), _full(he_a)]
    outs = [_rows_out(st.M, D, tm) for _ in range(5)]
    return _call(body, (st.M // tm,), ins, outs, name="gdn_conv")


def _gdn_out(st, o, z, p, alpha, ln_g, ln_b, tm):
    hs, he = _seg_consts(GDN_HSZ)
    nw = jnp.tile(p["gdn_norm_w"], GDN_HEADS).reshape(1, D)

    def pro(o, z, nw, hs, he):
        rstd = lax.rsqrt(_d_x3(o * o, hs) * (1.0 / GDN_HSZ) + 1e-6)
        return o * _d_x3(rstd, he) * nw * _silu(z)

    def epi(acc, x, gate, g_, b_):
        return (_res_ln(alpha, acc, x, gate, g_, b_),)

    return _mm(o, _full(p["gdn_w_o"].astype(BF16)), tm=tm, pro=pro, pro_ins=[_rows(z, tm), _full(nw), _full(hs), _full(he)],
               epi=epi, epi_ins=[_rows(st.x, tm), st.mod(2, tm), _full(ln_g), _full(ln_b)],
               outs=[_rows_out(st.M, D, tm)], name="gdn_out")[0]


def _gdn_layer(st, conv_buf, S0, p, alpha, ln_g, ln_b):
    B, T = st.B, st.T
    tm = min(256, st.M)
    pre, z, ba = _gdn_proj(st, p, tm)
    w, kkn, kn, qn, vb = _gdn_conv(st, pre, ba, conv_buf, p, tm)
    S0t = S0.transpose(0, 3, 1, 2).reshape(B, GDN_HSZ, D)
    o, sf = _delta_scan(w, kkn, kn, kn, qn, vb, S0t, B, T, GDN_HSZ)
    st.x = _gdn_out(st, o, z, p, alpha, ln_g, ln_b, tm)
    S = sf.reshape(B, GDN_HSZ, GDN_HEADS, GDN_HSZ).transpose(0, 2, 3, 1)
    xpad = jnp.concatenate([conv_buf, pre.reshape(B, T, 3 * D)[:, -(GDN_CONV - 1):]], axis=1)
    return S, xpad[:, -(GDN_CONV - 1):]


def _flash_body(cfg, *refs):
    tq, hq, hk = cfg["tq"], cfg["hq"], cfg["hk"]
    fox, bias, aug, window = cfg["fox"], cfg["bias"], cfg["aug"], cfg["window"]
    tk = tq
    G = hq // hk
    Kc = HD + aug
    R = G * tq
    refs = list(refs)
    q_ref, k_ref, v_ref = refs[:3]
    pos = 3
    if aug:
        mb_ref, e_ref = refs[pos:pos + 2]
        pos += 2
    if bias:
        tz_ref = refs[pos]
        pos += 1
    if fox:
        cq_ref, ck_ref = refs[pos:pos + 2]
        pos += 2
    o_ref, kb, vb, s_scr = refs[pos:pos + 4]
    g = pl.program_id(1)
    qi = pl.program_id(2)

    @pl.when(qi == 0)
    def _():
        vb[...] = v_ref[...].astype(BF16)
        if aug:
            k = k_ref[...]
            kb[...] = jnp.concatenate(
                [jnp.concatenate([k[:, kv * HD:(kv + 1) * HD].astype(BF16), e_ref[...]], axis=1) for kv in range(hk)],
                axis=1)
        else:
            kb[...] = k_ref[...].astype(BF16)

    scale = HD ** -0.5
    q = q_ref[...]
    row_t = lax.broadcasted_iota(I32, (R, tk), 0) % tq
    col_s = lax.broadcasted_iota(I32, (R, tk), 1)
    qs, cqs = [], []
    for kv in range(hk):
        x = jnp.concatenate([q[:, (kv * G + gg) * HD:(kv * G + gg + 1) * HD] for gg in range(G)], axis=0) * scale
        if aug:
            x = jnp.concatenate([x.astype(BF16), jnp.concatenate([mb_ref[kv]] * G, axis=0)], axis=1)
        qs.append(x.astype(BF16))
        if fox:
            lane = lax.broadcasted_iota(I32, (tq, LANES), 1)
            cqs.append(jnp.sum(jnp.where(lane == g * hk + kv, cq_ref[...], 0.0), axis=-1, keepdims=True))

    def logits(kv, c, rel, valid):
        off = pl.multiple_of(c * tk, tk)
        s = _d_nt(qs[kv], kb[pl.ds(off, tk), kv * Kc:(kv + 1) * Kc])
        if fox:
            sub = lax.broadcasted_iota(I32, (FOX_HEADS, tk), 0)
            ck = jnp.sum(jnp.where(sub == g * hk + kv, ck_ref[:, pl.ds(off, tk)], 0.0), axis=0, keepdims=True)
            s = s + cqs[kv] - ck
        if bias and rel in (0, 1):
            s = s + jnp.concatenate([tz_ref[kv * G + gg, rel] for gg in range(G)], axis=0)
        if rel == 0:
            s = jnp.where(row_t >= col_s, s, NEG)
        if rel == 3:
            s = jnp.where(col_s > row_t, s, NEG)
        if valid is not None:
            s = jnp.where(valid, s, NEG)
        return s

    nt = tk // LANES

    def lane_tiles(x):
        return [x[:, j * LANES:(j + 1) * LANES] for j in range(nt)]

    if window is not None:
        nch = window // tk
        static = [(qi, 0, None)] + [(jnp.maximum(qi - dc, 0), 1 if dc == 1 else (3 if dc == nch else 2), qi - dc >= 0)
                                    for dc in range(1, nch + 1)]
        n_far = 0
    elif bias:
        static = [(qi, 0, None), (jnp.maximum(qi - 1, 0), 1, qi >= 1)]
        n_far = jnp.maximum(qi - 1, 0)
    else:
        static = [(qi, 0, None)]
        n_far = qi
    n_static = len(static)
    unroll = cfg["unroll"]

    def far_loop(fn, carry):
        ng = n_far // unroll

        def group(gi, cr):
            for u in range(unroll):
                cr = fn(gi * unroll + u, cr)
            return cr

        carry = lax.fori_loop(0, ng, group, carry)
        return lax.fori_loop(ng * unroll, n_far, fn, carry)

    outs = []
    for kv in range(hk):
        def score(c, rel, valid, slot, m128):
            s = logits(kv, c, rel, valid)
            s_scr[slot] = s
            for t in lane_tiles(s):
                m128 = jnp.maximum(m128, t)
            return m128

        m128 = jnp.full((R, LANES), NEG, F32)
        for slot, (c, rel, valid) in enumerate(static):
            m128 = score(c, rel, valid, slot, m128)
        if window is None:
            m128 = far_loop(lambda c, m: score(c, 2, None, n_static + c, m), m128)
        mrep = jnp.broadcast_to(jnp.max(m128, axis=-1, keepdims=True), (R, LANES))

        def accumulate(c, slot, carry):
            l128, acc = carry
            p = [jnp.exp(t - mrep) for t in lane_tiles(s_scr[slot])]
            for t in p:
                l128 = l128 + t
            off = pl.multiple_of(c * tk, tk)
            pm = jnp.concatenate(p, axis=1) if nt > 1 else p[0]
            return l128, acc + _d(pm, vb[pl.ds(off, tk), kv * HD:(kv + 1) * HD])

        carry = (jnp.zeros((R, LANES), F32), jnp.zeros((R, HD), F32))
        for slot, (c, rel, valid) in enumerate(static):
            carry = accumulate(c, slot, carry)
        if window is None:
            carry = far_loop(lambda c, cr: accumulate(c, n_static + c, cr), carry)
        l128, acc = carry
        o = acc / jnp.sum(l128, axis=-1, keepdims=True)
        outs += [o[gg * tq:(gg + 1) * tq] for gg in range(G)]
    o_ref[...] = jnp.concatenate(outs, axis=1)


def _flash(cfg, B, T, ngroups, q_in, k_in, v_in, extra_ins, M):
    tq, hq, hk = cfg["tq"], cfg["hq"], cfg["hk"]
    Kc = HD + cfg["aug"]
    ins = [q_in, k_in, v_in] + list(extra_ins)
    nq = T // tq
    R = (hq // hk) * tq
    if cfg["window"] is not None:
        nslots = cfg["window"] // tq + 1
    else:
        nslots = nq + (1 if cfg["bias"] else 0)
    outs = [((M, ngroups * hq * HD), F32, pl.BlockSpec((tq, hq * HD), lambda b, g, i: (b * nq + i, g)))]
    scratch = [pltpu.VMEM((T, hk * Kc), BF16), pltpu.VMEM((T, hk * HD), BF16), pltpu.VMEM((nslots, R, tq), F32)]
    return _call(functools.partial(_flash_body, cfg), (B, ngroups, nq), ins, outs, scratch=scratch,
                 name="flash_" + cfg["name"])[0]


def _log_sigmoid(x):
    return -_softplus(-x)


def _fox_proj(st, p, tm):
    hw = FOX_HEADS * HD
    w = _pad_cols(p["fox_w_in"], 3 * hw + LANES).astype(BF16)
    bf = jnp.zeros((1, LANES), F32).at[0, :FOX_HEADS].set(p["fox_b_f"])

    def epi(acc, bf):
        return acc[:, :hw], acc[:, hw:3 * hw], _log_sigmoid(acc[:, 3 * hw:] + bf)

    outs = [_rows_out(st.M, hw, tm), _rows_out(st.M, 2 * hw, tm), _rows_out(st.M, LANES, tm)]
    return _mm(st.x, _full(w), tm=tm, pro=_modulate, pro_ins=[st.mod(1, tm), st.mod(0, tm)], epi=epi,
               epi_ins=[_full(bf)], outs=outs, name="fox_proj")


def _lower_tri(n):
    return jnp.asarray(np.tril(np.ones((n, n), np.float32)), BF16)


def _cumsum_rows(x, B, T):
    ch = _pick_tile(T)
    tri = _lower_tri(ch)

    def body(x_ref, tri_ref, o_ref):
        carry = jnp.zeros((1, LANES), F32)
        for c in range(T // ch):
            cc = _d_3x(tri_ref[...], x_ref[c * ch:(c + 1) * ch, :]) + carry
            o_ref[c * ch:(c + 1) * ch, :] = cc
            carry = cc[ch - 1:ch, :]

    return _call(body, (B,), [_rows(x, T), _full(tri)], [_rows_out(B * T, LANES, T)], name="cumsum_rows")[0]


def _out_proj(st, o, w_o, alpha, ln_g, ln_b, tm, name):
    def epi(acc, x, gate, g_, b_):
        return (_res_ln(alpha, acc, x, gate, g_, b_),)

    return _mm(o, _full(w_o.astype(BF16)), tm=tm, epi=epi,
               epi_ins=[_rows(st.x, tm), st.mod(2, tm), _full(ln_g), _full(ln_b)],
               outs=[_rows_out(st.M, D, tm)], name=name)[0]


def _fox_prompt(st, p, alpha, ln_g, ln_b):
    B, T, M = st.B, st.T, st.M
    tm = min(256, M)
    q, kv, logf = _fox_proj(st, p, tm)
    cum = _cumsum_rows(logf, B, T)
    ckT = cum.reshape(B, T, LANES)[:, :, :FOX_HEADS].transpose(0, 2, 1)
    tq = min(512, T)
    nq = T // tq
    cfg = dict(name="fox", tq=tq, hq=2, hk=2, fox=True, bias=False, aug=0, window=None, unroll=2)
    npair = FOX_HEADS // 2
    q_in = (q, pl.BlockSpec((tq, 2 * HD), lambda b, g, i: (b * nq + i, g)))
    k_in = (kv, pl.BlockSpec((T, 2 * HD), lambda b, g, i: (b, g)))
    v_in = (kv, pl.BlockSpec((T, 2 * HD), lambda b, g, i: (b, npair + g)))
    extra = [(cum, pl.BlockSpec((tq, LANES), lambda b, g, i: (b * nq + i, 0))),
             (ckT, pl.BlockSpec((None, FOX_HEADS, T), lambda b, g, i: (b, 0, 0)))]
    o = _flash(cfg, B, T, npair, q_in, k_in, v_in, extra, M)
    st.x = _out_proj(st, o, p["fox_w_o"], alpha, ln_g, ln_b, tm, "fox_out")
    return kv.reshape(B, T, 2, FOX_HEADS, HD), logf.reshape(B, T, LANES)[:, :, :FOX_HEADS]


def _page_ins(cache, page_shape, npages, first_of_step):
    nd = len(page_shape)
    return [(cache, pl.BlockSpec((None,) + tuple(page_shape),
                                 lambda b, s, pt, j=j: (pt[b, first_of_step(s) + j],) + (0,) * nd))
            for j in range(npages)]


def _fox_cum_sample(logf_new, cache_logf, page_table, Tn):
    B, npg = page_table.shape
    PAGE = cache_logf.shape[1]
    H = cache_logf.shape[2]
    tri = _lower_tri(PAGE)

    def body(pt_ref, *refs):
        pages, new_ref, tri_ref, o_ref = refs[:npg], refs[npg], refs[npg + 1], refs[npg + 2]
        carry = jnp.zeros((1, H), F32)
        for j in range(npg):
            cc = _d_3x(tri_ref[...], pages[j][...]) + carry
            o_ref[j * PAGE:(j + 1) * PAGE, :] = cc
            carry = cc[PAGE - 1:PAGE, :]
        xn = jnp.concatenate([new_ref[...][:, :H], jnp.zeros((PAGE - Tn, H), F32)], axis=0)
        o_ref[npg * PAGE:(npg + 1) * PAGE, :] = _d_3x(tri_ref[...], xn) + carry

    ins = _page_ins(cache_logf, (PAGE, H), npg, lambda s: 0)
    ins += [(logf_new, pl.BlockSpec((Tn, LANES), lambda b, s, pt: (b, 0))),
            (tri, pl.BlockSpec(tri.shape, lambda b, s, pt: (0, 0)))]
    Lp = (npg + 1) * PAGE
    outs = [((B, Lp, H), F32, pl.BlockSpec((None, Lp, H), lambda b, s, pt: (b, 0, 0)))]
    return _call(body, (B, 1), ins, outs, prefetch=[page_table], name="fox_cum_sample")[0]


def _rep_mat(n_rows, n_src, per):
    r = np.arange(n_rows)
    src = r // per if per else r % n_src
    return jnp.asarray((src[:, None] == np.arange(n_src)[None, :]).astype(np.float32), BF16)


def _fox_decode(q, kv_new, cache_kv, page_table, cq, ckT, Tn):
    B, npg = page_table.shape
    PAGE = cache_kv.shape[1]
    hw = FOX_HEADS * HD
    R = FOX_HEADS * Tn
    pps = PAGES_PER_STEP if npg % PAGES_PER_STEP == 0 else 1
    nsteps = npg // pps
    cache = cache_kv.reshape(cache_kv.shape[0], PAGE, 2 * hw)
    rep_t = _rep_mat(R, Tn, 0)
    rep_h = _rep_mat(R, FOX_HEADS, Tn)
    scale = HD ** -0.5

    def body(pt_ref, *refs):
        pages = refs[:pps]
        q_ref, new_ref, cq_ref, ck_ref, rt_ref, rh_ref, o_ref, qbd, m_s, l_s, acc = refs[pps:]
        s_id = pl.program_id(1)
        row_h = lax.broadcasted_iota(I32, (R, hw), 0) // Tn
        lane_h = lax.broadcasted_iota(I32, (R, hw), 1) // HD

        @pl.when(s_id == 0)
        def _():
            qrep = _d(rt_ref[...], q_ref[...])
            qbd[...] = jnp.where(row_h == lane_h, qrep * scale, 0.0).astype(BF16)
            m_s[...] = jnp.full(m_s.shape, NEG, F32)
            l_s[...] = jnp.zeros(l_s.shape, F32)
            acc[...] = jnp.zeros(acc.shape, F32)

        cqv = cq_ref[...]

        def chunk(kp, vp, off, mask):
            s = _d_nt(qbd[...], kp) + cqv - _d_3x(rh_ref[...], ck_ref[:, pl.ds(off, PAGE)])
            if mask is not None:
                s = jnp.where(mask, s, NEG)
            m = m_s[:, 0:1]
            m2 = jnp.maximum(m, jnp.max(s, axis=-1, keepdims=True))
            a = jnp.exp(m - m2)
            pr = jnp.exp(s - m2)
            l_s[...] = jnp.broadcast_to(a * l_s[:, 0:1] + jnp.sum(pr, axis=-1, keepdims=True), l_s.shape)
            m_s[...] = jnp.broadcast_to(m2, m_s.shape)
            acc[...] = a * acc[...] + _d(pr, vp)

        for j in range(pps):
            pg = pages[j][...]
            chunk(pg[:, :hw], pg[:, hw:], pl.multiple_of((s_id * pps + j) * PAGE, PAGE), None)

        @pl.when(s_id == nsteps - 1)
        def _():
            new = jnp.concatenate([new_ref[...], jnp.zeros((PAGE - Tn, 2 * hw), F32)], axis=0)
            t_row = lax.broadcasted_iota(I32, (R, PAGE), 0) % Tn
            col = lax.broadcasted_iota(I32, (R, PAGE), 1)
            chunk(new[:, :hw], new[:, hw:], npg * PAGE, col <= t_row)
            of = jnp.where(row_h == lane_h, acc[...] / l_s[:, 0:1], 0.0)
            out = of[0:Tn]
            for h in range(1, FOX_HEADS):
                out = out + of[h * Tn:(h + 1) * Tn]
            o_ref[...] = out

    Lp = ckT.shape[2]
    ins = _page_ins(cache, (PAGE, 2 * hw), pps, lambda s: s * pps)
    ins += [(q, pl.BlockSpec((Tn, hw), lambda b, s, pt: (b, 0))),
            (kv_new, pl.BlockSpec((Tn, 2 * hw), lambda b, s, pt: (b, 0))),
            (cq, pl.BlockSpec((None, R, 1), lambda b, s, pt: (b, 0, 0))),
            (ckT, pl.BlockSpec((None, FOX_HEADS, Lp), lambda b, s, pt: (b, 0, 0))),
            (rep_t, pl.BlockSpec(rep_t.shape, lambda b, s, pt: (0, 0))),
            (rep_h, pl.BlockSpec(rep_h.shape, lambda b, s, pt: (0, 0)))]
    outs = [((B * Tn, hw), F32, pl.BlockSpec((Tn, hw), lambda b, s, pt: (b, 0)))]
    scratch = [pltpu.VMEM((R, hw), BF16), pltpu.VMEM((R, LANES), F32), pltpu.VMEM((R, LANES), F32),
               pltpu.VMEM((R, hw), F32)]
    return _call(body, (B, nsteps), ins, outs, scratch=scratch, prefetch=[page_table], name="fox_decode")[0]


def _fox_sample(st, cache_kv, cache_logf, page_table, p, alpha, ln_g, ln_b):
    B, Tn, M = st.B, st.T, st.M
    tm = min(256, M)
    npg = page_table.shape[1]
    PAGE = cache_kv.shape[1]
    q, kv, logf = _fox_proj(st, p, tm)
    cum = _fox_cum_sample(logf, cache_logf, page_table, Tn)
    cq = cum[:, npg * PAGE:npg * PAGE + Tn].transpose(0, 2, 1).reshape(B, FOX_HEADS * Tn, 1)
    ckT = cum.transpose(0, 2, 1)
    o = _fox_decode(q, kv, cache_kv, page_table, cq, ckT, Tn)
    st.x = _out_proj(st, o, p["fox_w_o"], alpha, ln_g, ln_b, tm, "fox_out")
    return kv.reshape(B, Tn, 2, FOX_HEADS, HD), logf.reshape(B, Tn, LANES)[:, :, :FOX_HEADS]


KVW = NSA_KVH * HD
HALF = CMP_BLK // 2


def _t5_bucket(dist):
    exact = REL_BUCKETS // 2
    d = jnp.maximum(dist, 0)
    far = exact + (jnp.log(jnp.maximum(d, 1).astype(F32) / exact) / math.log(REL_MAX_DIST / exact)
                   * (REL_BUCKETS - exact)).astype(I32)
    return jnp.where(d < exact, d, jnp.minimum(far, REL_BUCKETS - 1))


def _rel_bias(table, dist):
    return jnp.moveaxis(table[_t5_bucket(dist)], -1, 0)


def _nsa_proj(st, p, tm):
    qw = NSA_HEADS * HD
    w = _pad_cols(p["nsa_w_in"], qw + 6 * KVW + LANES).astype(BF16)

    def epi(acc):
        return (acc[:, :qw], acc[:, qw:qw + 2 * KVW], acc[:, qw + 2 * KVW:qw + 4 * KVW],
                acc[:, qw + 4 * KVW:qw + 6 * KVW], acc[:, qw + 6 * KVW:])

    outs = [_rows_out(st.M, qw, tm)] + [_rows_out(st.M, 2 * KVW, tm)] * 3 + [_rows_out(st.M, LANES, tm)]
    return _mm(st.x, _full(w), tm=tm, pro=_modulate, pro_ins=[st.mod(1, tm), st.mod(0, tm)], epi=epi, outs=outs,
               name="nsa_proj")


def _cmp_weights(p):
    eye = jnp.eye(NSA_KVH, dtype=F32)
    wk = jnp.einsum("ab,vlde->vladbe", eye, p["nsa_cmp_w1"]).reshape(2, CMP_BLK, KVW, KVW)
    wc = wk.reshape(2, 2, HALF, KVW, KVW).transpose(1, 2, 0, 3, 4)
    w2c = jnp.einsum("ab,vde->vadbe", eye, p["nsa_cmp_w2"]).reshape(2, KVW, KVW)
    b1 = jnp.tile(p["nsa_cmp_b1"][:, None, :], (1, NSA_KVH, 1)).reshape(1, 2 * KVW)
    return wc.astype(BF16), w2c.astype(BF16), b1


def _compress_body(nx, *refs):
    x_refs = refs[:nx]
    wc_ref, w2_ref, b1_ref, o_ref, ua, ub = refs[nx:]
    rows = ua.shape[1]
    nl = 2 * KVW // LANES
    acc = [[jnp.zeros((rows, KVW), F32) for _ in range(2)] for _ in range(2)]
    for l in range(HALF):
        for kv in range(2):
            lo = l * 2 * KVW + kv * KVW
            piece = [r[:, lo:lo + KVW] for r in x_refs]
            piece = (jnp.concatenate(piece, axis=0) if nx > 1 else piece[0]).astype(BF16)
            for half in range(2):
                acc[half][kv] = acc[half][kv] + _d(piece, wc_ref[half, l, kv])
    for scr, a in ((ua, acc[0]), (ub, acc[1])):
        full = jnp.concatenate(a, axis=1)
        for c in range(nl):
            scr[c] = full[:, c * LANES:(c + 1) * LANES]
    hid = jnp.concatenate([ua[c, pl.ds(0, rows // 2, stride=2), :] + ub[c, pl.ds(1, rows // 2, stride=2), :]
                           for c in range(nl)], axis=1)
    hid = _gelu_tanh(hid + b1_ref[...])
    o_ref[...] = jnp.concatenate([_d(hid[:, :KVW], w2_ref[0]), _d(hid[:, KVW:], w2_ref[1])], axis=1)


def _compress_dense(rows_kv, cw):
    wc, w2c, b1 = cw
    M = rows_kv.shape[0]
    x = rows_kv.reshape(M // HALF, HALF * 2 * KVW)
    nh = M // HALF
    th = _pick_tile(nh, cap=128)
    ins = [_rows(x, th), _full(wc), _full(w2c), _full(b1)]
    outs = [_rows_out(nh // 2, 2 * KVW, th // 2)]
    scratch = [pltpu.VMEM((2 * KVW // LANES, th, LANES), F32)] * 2
    return _call(functools.partial(_compress_body, 1), (nh // th,), ins, outs, scratch=scratch, name="nsa_compress")[0]


def _pair_mat(nc, ns):
    n = np.arange(nc)
    return jnp.asarray((n[:, None] // (SEL_BLK // CMP_BLK) == np.arange(ns)[None, :]).astype(np.float32), BF16)


def _top_blocks(score, n_sel):
    lane = lax.broadcasted_iota(I32, score.shape, 1)
    big = jnp.int32(1 << 20)
    sel = jnp.zeros(score.shape, jnp.bool_)
    work = score
    for _ in range(n_sel):
        m = jnp.max(work, axis=-1, keepdims=True)
        idx = jnp.min(jnp.where(work == m, lane, big), axis=-1, keepdims=True)
        hit = lane == idx
        sel = sel | hit
        work = jnp.where(hit, -3e38, work)
    return jnp.where(sel, 0.0, SEL_NEG)


def _masked_softmax(s, mask):
    s = jnp.where(mask, s, NEG)
    m = jnp.max(s, axis=-1, keepdims=True)
    p = jnp.where(mask, jnp.exp(s - m), 0.0)
    l = jnp.sum(p, axis=-1, keepdims=True)
    return p / jnp.where(l > 0.0, l, 1.0)


def _block_scores(imp, tpos, ns):
    blk = lax.broadcasted_iota(I32, imp.shape, 1)
    cur = tpos // SEL_BLK
    forced = (blk == 0) | (blk == cur) | (blk == cur - 1)
    score = jnp.where(forced, FORCE_SCORE, imp)
    return jnp.where(blk * SEL_BLK > tpos, -1.0, score)


def _nsa_cmp_prompt(q, kcvc, bias_c, B, T, tq):
    nc, ns = T // CMP_BLK, -(-T // SEL_BLK)
    n_sel = min(N_SEL, ns)
    nq = T // tq
    G = NSA_G
    pair = _pair_mat(nc, ns)
    scale = HD ** -0.5

    def body(q_ref, kc_ref, vc_ref, b_ref, pair_ref, o_ref, mb_ref):
        qi = pl.program_id(1)
        q = q_ref[...]
        R = G * tq
        tpos = qi * tq + lax.broadcasted_iota(I32, (R, 1), 0) % tq
        cmp_end = lax.broadcasted_iota(I32, (R, nc), 1) * CMP_BLK + (CMP_BLK - 1)
        mask = cmp_end <= tpos
        outs = []
        for kv in range(NSA_KVH):
            qs = jnp.concatenate([q[:, (kv * G + gg) * HD:(kv * G + gg + 1) * HD] for gg in range(G)], axis=0) * scale
            s = _d_nt(qs, kc_ref[:, kv * HD:(kv + 1) * HD])
            s = s + jnp.concatenate([b_ref[kv * G + gg] for gg in range(G)], axis=0)
            pc = _masked_softmax(s, mask)
            oc = _d(pc, vc_ref[:, kv * HD:(kv + 1) * HD])
            outs += [oc[gg * tq:(gg + 1) * tq] for gg in range(G)]
            imp = pc[0:tq]
            for gg in range(1, G):
                imp = imp + pc[gg * tq:(gg + 1) * tq]
            score = _block_scores(_d_x3(imp, pair_ref[...]), tpos[0:tq], ns)
            mb_ref[kv] = _top_blocks(score, n_sel).astype(mb_ref.dtype)
        o_ref[...] = jnp.concatenate(outs, axis=1)

    ins = [(q, pl.BlockSpec((tq, NSA_HEADS * HD), lambda b, i: (b * nq + i, 0))),
           (kcvc, pl.BlockSpec((nc, KVW), lambda b, i: (b, 0))),
           (kcvc, pl.BlockSpec((nc, KVW), lambda b, i: (b, 1))),
           (bias_c, pl.BlockSpec((NSA_HEADS, tq, nc), lambda b, i: (0, i, 0))),
           _full(pair)]
    outs = [((B * T, NSA_HEADS * HD), F32, pl.BlockSpec((tq, NSA_HEADS * HD), lambda b, i: (b * nq + i, 0))),
            ((B, NSA_KVH, T, ns), BF16, pl.BlockSpec((None, NSA_KVH, tq, ns), lambda b, i: (b, 0, i, 0)))]
    return _call(body, (B, nq), ins, outs, name="nsa_cmp_select")


def _gate_mats():
    hsn = _seg_np(HD)
    return [jnp.asarray(np.roll(hsn.T, br * NSA_HEADS, axis=0), BF16) for br in range(3)]


def _nsa_out(st, o_c, o_s, o_w, gates, p, alpha, ln_g, ln_b, tm):
    def pro(oc, os_, ow, gl, e0, e1, e2):
        sg = _sigmoid(gl)
        return _d_x3(sg, e0) * oc + _d_x3(sg, e1) * os_ + _d_x3(sg, e2) * ow

    def epi(acc, x, gate, g_, b_):
        return (_res_ln(alpha, acc, x, gate, g_, b_),)

    pro_ins = [_rows(o_s, tm), _rows(o_w, tm), _rows(gates, tm)] + [_full(e) for e in _gate_mats()]
    return _mm(o_c, _full(p["nsa_w_o"].astype(BF16)), tm=tm, pro=pro, pro_ins=pro_ins, epi=epi,
               epi_ins=[_rows(st.x, tm), st.mod(2, tm), _full(ln_g), _full(ln_b)],
               outs=[_rows_out(st.M, D, tm)], name="nsa_out")[0]


def _nsa_prompt(st, p, alpha, ln_g, ln_b):
    B, T, M = st.B, st.T, st.M
    tm = min(256, M)
    tq = ATT_T
    nq = T // tq
    table = p["rel_bias"]
    q, cmp_rows, slc_rows, win_rows, gates = _nsa_proj(st, p, tm)
    kcvc = _compress_dense(cmp_rows, _cmp_weights(p))
    nc, ns = T // CMP_BLK, -(-T // SEL_BLK)
    lo = CMP_BLK * (nc - 1) + CMP_BLK - 1
    fd = _rel_bias(table, jnp.arange(-lo, T, dtype=I32))
    bias_c = jnp.stack([fd[:, lo - (CMP_BLK * n + CMP_BLK - 1):][:, :T] for n in range(nc)], axis=1)
    bias_c = bias_c.transpose(0, 2, 1)
    o_c, mb = _nsa_cmp_prompt(q, kcvc, bias_c, B, T, tq)
    r = jnp.arange(tq, dtype=I32)
    far = table[REL_BUCKETS - 1][:, None, None]
    tz = jnp.stack([_rel_bias(table, r[:, None] - r[None, :]) - far,
                    _rel_bias(table, tq + r[:, None] - r[None, :]) - far], axis=1)
    e_blk = jnp.asarray((np.arange(T)[:, None] // SEL_BLK == np.arange(ns)[None, :]).astype(np.float32), BF16)
    q_in = (q, pl.BlockSpec((tq, NSA_HEADS * HD), lambda b, g, i: (b * nq + i, 0)))
    kv_in = lambda a, c: (a, pl.BlockSpec((T, KVW), lambda b, g, i: (b, c)))
    tz_in = (tz, pl.BlockSpec(tz.shape, lambda b, g, i: (0, 0, 0, 0)))
    cfg = dict(name="nsa_slc", tq=tq, hq=NSA_HEADS, hk=NSA_KVH, fox=False, bias=True, aug=ns, window=None, unroll=4)
    extra = [(mb, pl.BlockSpec((None, NSA_KVH, tq, ns), lambda b, g, i: (b, 0, i, 0))),
             (e_blk, pl.BlockSpec(e_blk.shape, lambda b, g, i: (0, 0))), tz_in]
    o_s = _flash(cfg, B, T, 1, q_in, kv_in(slc_rows, 0), kv_in(slc_rows, 1), extra, M)
    cfg = dict(name="nsa_win", tq=tq, hq=NSA_HEADS, hk=NSA_KVH, fox=False, bias=True, aug=0, window=WINDOW, unroll=1)
    o_w = _flash(cfg, B, T, 1, q_in, kv_in(win_rows, 0), kv_in(win_rows, 1), [tz_in], M)
    st.x = _nsa_out(st, o_c, o_s, o_w, gates, p, alpha, ln_g, ln_b, tm)
    shp = (B, T, 2, NSA_KVH, HD)
    keep = min(WINDOW, T)
    return cmp_rows.reshape(shp), slc_rows.reshape(shp), win_rows.reshape(shp)[:, T - keep:]


def _compress_paged(cache_cmp, page_table, cw):
    wc, w2c, b1 = cw
    B, npg = page_table.shape
    PAGE = cache_cmp.shape[1]
    hp = PAGE // HALF
    cache = cache_cmp.reshape(cache_cmp.shape[0], hp, HALF * 2 * KVW)
    nb = 2 if B % 2 == 0 else 1
    ins = [(cache, pl.BlockSpec((None, hp, HALF * 2 * KVW), lambda g, pt, bb=bb, j=j: (pt[g * nb + bb, j], 0, 0)))
           for bb in range(nb) for j in range(npg)]
    const = lambda a: (a, pl.BlockSpec(a.shape, lambda g, pt: (0,) * a.ndim))
    ins += [const(wc), const(w2c), const(b1)]
    rows = nb * npg * hp
    outs = [((B * npg * hp // 2, 2 * KVW), F32, pl.BlockSpec((rows // 2, 2 * KVW), lambda g, pt: (g, 0)))]
    scratch = [pltpu.VMEM((2 * KVW // LANES, rows, LANES), F32)] * 2
    body = lambda pt_ref, *refs: _compress_body(nb * npg, *refs)
    return _call(body, (B // nb,), ins, outs, scratch=scratch, prefetch=[page_table], name="nsa_compress_paged")[0]


def _nsa_decode(q, slc_new, win_new, kcvc, cache_slc, cache_win, page_table, gcol, consts, Tn, offset):
    B, npg = page_table.shape
    PAGE = cache_slc.shape[1]
    Wb = cache_win.shape[1]
    R = NSA_HEADS * Tn
    L = offset + Tn
    nc, ns = L // CMP_BLK, -(-L // SEL_BLK)
    n_sel = min(N_SEL, ns)
    nck = npg + 1
    qw = NSA_HEADS * HD
    cache = cache_slc.reshape(cache_slc.shape[0], PAGE, 2 * KVW)
    win = cache_win.reshape(B, Wb, 2 * KVW)
    scale = HD ** -0.5
    names = ["rep_t", "fold", "unfold", "bias_c", "mask_c", "pair", "e_blk", "bias_s", "bias_w"]
    cvals = [consts[n] for n in names]

    def body(pt_ref, *refs):
        pages = refs[:npg]
        (q_ref, sn_ref, wn_ref, kc_ref, vc_ref, win_ref, g_ref, rt_ref, fold_ref, unfold_ref, bc_ref, mc_ref, pair_ref,
         e_ref, bs_ref, bw_ref, o_ref, wout_ref) = refs[npg:]
        row_h = lax.broadcasted_iota(I32, (R, qw), 0) // Tn
        lane_h = lax.broadcasted_iota(I32, (R, qw), 1) // HD
        own = row_h == lane_h
        qrep = _d(rt_ref[...], q_ref[...])
        qbd = (_d(jnp.where(own, qrep, 0.0), fold_ref[...]) * scale).astype(BF16)
        pad = lambda x: jnp.concatenate([x, jnp.zeros((PAGE - Tn, x.shape[1]), x.dtype)], axis=0)

        pc = _masked_softmax(_d_nt(qbd, kc_ref[...]) + bc_ref[...], mc_ref[...] > 0.0)
        o_c = _d(pc, vc_ref[...])
        imp = []
        for kv in range(NSA_KVH):
            a = pc[kv * NSA_G * Tn:(kv * NSA_G + 1) * Tn]
            for gg in range(1, NSA_G):
                a = a + pc[(kv * NSA_G + gg) * Tn:(kv * NSA_G + gg + 1) * Tn]
            imp.append(a)
        imp = jnp.concatenate(imp, axis=0)
        tpos = offset + lax.broadcasted_iota(I32, (NSA_KVH * Tn, 1), 0) % Tn
        mb = _top_blocks(_block_scores(_d_x3(imp, pair_ref[...]), tpos, ns), n_sel)
        mb = jnp.concatenate([mb[kv * Tn:(kv + 1) * Tn] for kv in range(NSA_KVH) for _ in range(NSA_G)], axis=0)

        sn = pad(sn_ref[...])
        ks = [pages[j][:, :KVW] for j in range(npg)] + [sn[:, :KVW]]
        vs = [pages[j][:, KVW:] for j in range(npg)] + [sn[:, KVW:]]
        s = jnp.concatenate([_d_nt(qbd, ks[j]) + _d(mb, e_ref[:, j * PAGE:(j + 1) * PAGE]) for j in range(nck)], axis=1)
        s = s + bs_ref[...]
        p = jnp.exp(s - jnp.max(s, axis=-1, keepdims=True))
        o_s = _d(p[:, 0:PAGE], vs[0])
        for j in range(1, nck):
            o_s = o_s + _d(p[:, j * PAGE:(j + 1) * PAGE], vs[j])
        o_s = o_s / jnp.sum(p, axis=-1, keepdims=True)

        w = win_ref[...]
        wn = pad(wn_ref[...])
        s = jnp.concatenate([_d_nt(qbd, w[:, :KVW]), _d_nt(qbd, wn[:, :KVW])], axis=1) + bw_ref[...]
        p = jnp.exp(s - jnp.max(s, axis=-1, keepdims=True))
        o_w = (_d(p[:, :Wb], w[:, KVW:]) + _d(p[:, Wb:], wn[:, KVW:])) / jnp.sum(p, axis=-1, keepdims=True)
        wout_ref[...] = jnp.concatenate([w[Tn:], wn_ref[...]], axis=0)

        sg = _sigmoid(g_ref[...])
        o = sg[:, 0:1] * o_c + sg[:, 1:2] * o_s + sg[:, 2:3] * o_w
        of = jnp.where(own, _d_x3(o, unfold_ref[...]), 0.0)
        out = of[0:Tn]
        for h in range(1, NSA_HEADS):
            out = out + of[h * Tn:(h + 1) * Tn]
        o_ref[...] = out

    c2 = lambda b, pt: (0, 0)
    ins = [(cache, pl.BlockSpec((None, PAGE, 2 * KVW), lambda b, pt, j=j: (pt[b, j], 0, 0))) for j in range(npg)]
    ins += [(q, pl.BlockSpec((Tn, qw), lambda b, pt: (b, 0))),
            (slc_new, pl.BlockSpec((Tn, 2 * KVW), lambda b, pt: (b, 0))),
            (win_new, pl.BlockSpec((Tn, 2 * KVW), lambda b, pt: (b, 0))),
            (kcvc, pl.BlockSpec((nc, KVW), lambda b, pt: (b, 0))),
            (kcvc, pl.BlockSpec((nc, KVW), lambda b, pt: (b, 1))),
            (win, pl.BlockSpec((None, Wb, 2 * KVW), lambda b, pt: (b, 0, 0))),
            (gcol, pl.BlockSpec((None, R, 3), lambda b, pt: (b, 0, 0)))]
    ins += [(a, pl.BlockSpec(a.shape, c2)) for a in cvals]
    outs = [((B * Tn, qw), F32, pl.BlockSpec((Tn, qw), lambda b, pt: (b, 0))),
            ((B, Wb, 2 * KVW), F32, pl.BlockSpec((None, Wb, 2 * KVW), lambda b, pt: (b, 0, 0)))]
    return _call(body, (B,), ins, outs, prefetch=[page_table], name="nsa_decode")


def _nsa_decode_consts(table, Tn, offset, npg, PAGE, Wb):
    R = NSA_HEADS * Tn
    L = offset + Tn
    nc, ns = L // CMP_BLK, -(-L // SEL_BLK)
    Lp = (npg + 1) * PAGE
    tpos = offset + jnp.arange(Tn, dtype=I32)
    rows = lambda x: x.reshape(R, x.shape[-1])
    cmp_end = jnp.arange(nc, dtype=I32) * CMP_BLK + CMP_BLK - 1
    dist_c = tpos[:, None] - cmp_end[None, :]
    spos = jnp.arange(Lp, dtype=I32)
    dist_s = tpos[:, None] - spos[None, :]
    ok_s = (dist_s >= 0) & (spos[None, :] < L)
    col = jnp.arange(Wb + PAGE, dtype=I32)
    wpos = offset - Wb + col
    dist_w = tpos[:, None] - wpos[None, :]
    ok_w = (dist_w >= 0) & (dist_w < WINDOW) & (wpos[None, :] >= 0) & (col[None, :] < Wb + Tn)
    tile = lambda m: jnp.tile(m[None], (NSA_HEADS, 1, 1))
    fold = np.zeros((NSA_HEADS, HD, NSA_KVH, HD), np.float32)
    for h in range(NSA_HEADS):
        fold[h, :, h // NSA_G, :] = np.eye(HD)
    fold = fold.reshape(NSA_HEADS * HD, KVW)
    return dict(
        rep_t=_rep_mat(R, Tn, 0), fold=jnp.asarray(fold, BF16), unfold=jnp.asarray(fold.T, BF16),
        bias_c=rows(_rel_bias(table, dist_c)), mask_c=rows(tile((dist_c >= 0).astype(F32))), pair=_pair_mat(nc, ns),
        e_blk=jnp.asarray((np.arange(ns)[:, None] == np.arange(Lp)[None, :] // SEL_BLK).astype(np.float32), BF16),
        bias_s=rows(_rel_bias(table, dist_s) + tile(jnp.where(ok_s, 0.0, NEG))),
        bias_w=rows(_rel_bias(table, dist_w) + tile(jnp.where(ok_w, 0.0, NEG))))


def _nsa_sample(st, cache_cmp, cache_slc, cache_win, page_table, p, alpha, ln_g, ln_b):
    B, Tn, M = st.B, st.T, st.M
    tm = min(256, M)
    npg = page_table.shape[1]
    PAGE = cache_slc.shape[1]
    Wb = cache_win.shape[1]
    offset = npg * PAGE
    assert offset % CMP_BLK == 0 and Tn < CMP_BLK and Wb == WINDOW and Tn % SUBLANES == 0
    q, cmp_rows, slc_rows, win_rows, gates = _nsa_proj(st, p, tm)
    kcvc = _compress_paged(cache_cmp, page_table, _cmp_weights(p))
    gcol = gates.reshape(B, Tn, LANES)[:, :, :3 * NSA_HEADS].reshape(B, Tn, 3, NSA_HEADS)
    gcol = gcol.transpose(0, 3, 1, 2).reshape(B, NSA_HEADS * Tn, 3)
    consts = _nsa_decode_consts(p["rel_bias"], Tn, offset, npg, PAGE, Wb)
    o, wout = _nsa_decode(q, slc_rows, win_rows, kcvc, cache_slc, cache_win, page_table, gcol, consts, Tn, offset)
    st.x = _out_proj(st, o, p["nsa_w_o"], alpha, ln_g, ln_b, tm, "nsa_out_s")
    shp = (B, Tn, 2, NSA_KVH, HD)
    return cmp_rows.reshape(shp), slc_rows.reshape(shp), wout.reshape(B, Wb, 2, NSA_KVH, HD)


def kernel(x_prompt, x_sample, state_rwkv_wkv, state_rwkv_shift, cache_nsa_cmp, cache_nsa_slc, cache_nsa_win, cache_fox_kv, cache_fox_logf, state_gdn_S, state_gdn_conv, page_table, c_prompt, c_sample, w_mod, b_mod, ln_g, ln_b, moe_w_group, moe_b_group, moe_w_router, moe_b_router, moe_w1, moe_w3, moe_w2, rwkv_mu, rwkv_w_rkv, rwkv_w0, rwkv_w1, rwkv_w2, rwkv_a0, rwkv_a1, rwkv_a2, rwkv_g1, rwkv_g2, rwkv_k_k, rwkv_k_a, rwkv_r_k, rwkv_ln_w, rwkv_ln_b, rwkv_w_o, nsa_w_in, nsa_cmp_w1, nsa_cmp_b1, nsa_cmp_w2, nsa_w_o, rel_bias, fox_w_in, fox_b_f, fox_w_o, gdn_w_in, gdn_conv_w, gdn_A_log, gdn_dt_bias, gdn_norm_w, gdn_w_o):
    p = dict(locals())
    Bp, T, _ = x_prompt.shape
    Bs, Tn, _ = x_sample.shape
    depth = w_mod.shape[0]
    alpha = (2 * depth) ** 0.25
    sp = _Stream(x_prompt.reshape(Bp * T, D), Bp, T, min(512, T))
    ss = _Stream(x_sample.reshape(Bs * Tn, D), Bs, Tn, min(256, Bs * Tn))
    nc = Bp + Bs
    c_all = jnp.pad(jnp.concatenate([c_prompt, c_sample], axis=0), ((0, -nc % SUBLANES), (0, 0)))
    out = {}
    for layer in range(depth):
        m6 = _ada(c_all, w_mod, b_mod, layer)
        sp.set_mods(m6[:Bp])
        ss.set_mods(m6[Bp:nc])
        g0 = ln_g[layer, 0].reshape(1, D)
        b0 = ln_b[layer, 0].reshape(1, D)
        kind = layer % 4
        if kind == 0:
            nh = D // RWKV_HSZ
            out["wkv_p"], out["shift_p"] = _rwkv_layer(sp, jnp.zeros((Bp, D), F32),
                                                       jnp.zeros((Bp, nh, RWKV_HSZ, RWKV_HSZ), F32), p, alpha, g0, b0)
            out["wkv_s"], out["shift_s"] = _rwkv_layer(ss, state_rwkv_shift, state_rwkv_wkv, p, alpha, g0, b0)
        elif kind == 1:
            out["cmp_p"], out["slc_p"], out["win_p"] = _nsa_prompt(sp, p, alpha, g0, b0)
            out["cmp_s"], out["slc_s"], out["win_s"] = _nsa_sample(ss, cache_nsa_cmp, cache_nsa_slc, cache_nsa_win,
                                                                   page_table, p, alpha, g0, b0)
        elif kind == 2:
            out["kv_p"], out["logf_p"] = _fox_prompt(sp, p, alpha, g0, b0)
            out["kv_s"], out["logf_s"] = _fox_sample(ss, cache_fox_kv, cache_fox_logf, page_table, p, alpha, g0, b0)
        else:
            out["S_p"], out["conv_p"] = _gdn_layer(sp, jnp.zeros((Bp, GDN_CONV - 1, 3 * D), F32),
                                                   jnp.zeros((Bp, GDN_HEADS, GDN_HSZ, GDN_HSZ), F32), p, alpha, g0, b0)
            out["S_s"], out["conv_s"] = _gdn_layer(ss, state_gdn_conv, state_gdn_S, p, alpha, g0, b0)
        _moe_layer([sp, ss], layer, alpha, p)
    return (sp.x.reshape(Bp, T, D), ss.x.reshape(Bs, Tn, D), out["wkv_p"], out["wkv_s"], out["shift_p"], out["shift_s"],
            out["cmp_p"], out["cmp_s"], out["slc_p"], out["slc_s"], out["win_p"], out["win_s"],
            out["kv_p"], out["kv_s"], out["logf_p"], out["logf_s"], out["S_p"], out["S_s"], out["conv_p"], out["conv_s"])
```

```python
import functools
import math

import numpy as np
import jax
import jax.numpy as jnp
from jax import lax
from jax.experimental import pallas as pl
from jax.experimental.pallas import tpu as pltpu

F32 = jnp.float32
BF16 = jnp.bfloat16
I32 = jnp.int32
NEG = -1e30
LN_EPS = 1e-5
D = 1024
LANES = 128
SUBLANES = 8
MXU_TILE = 256
VMEM_LIMIT_MB = 56

RWKV_HSZ = 64
RWKV_GN_EPS = 64e-5
NSA_HEADS, NSA_KVH, HD = 16, 4, 64
NSA_G = NSA_HEADS // NSA_KVH
CMP_BLK, SEL_BLK, N_SEL, WINDOW = 32, 64, 16, 512
FORCE_SCORE = 1e4
REL_BUCKETS, REL_MAX_DIST = 32, 128
FOX_HEADS = 16
GDN_HEADS, GDN_HSZ, GDN_CONV = 8, 128, 4
MOE_GROUPS, MOE_EPG, MOE_BLK = 4, 8, 256
MOE_EXPERTS = MOE_GROUPS * MOE_EPG
ATT_T = 128
SEL_NEG = -65536.0
PAGES_PER_STEP = 4


def _d(a, b):
    return jnp.dot(a.astype(BF16), b.astype(BF16), preferred_element_type=F32)


def _d_nt(a, b):
    return lax.dot_general(a.astype(BF16), b.astype(BF16), (((1,), (1,)), ((), ())),
                           preferred_element_type=F32)


def _split3(x):
    h = x.astype(BF16)
    r1 = x - h.astype(F32)
    m = r1.astype(BF16)
    l = (r1 - m.astype(F32)).astype(BF16)
    return h, m, l


def _d_x3(x, sel):
    h, m, l = _split3(x)
    return _d(h, sel) + _d(m, sel) + _d(l, sel)


def _d_3x(sel, x):
    h, m, l = _split3(x)
    return _d(sel, h) + _d(sel, m) + _d(sel, l)


def _d_f32(x, w):
    xh, xm, xl = _split3(x)
    wh, wm, wl = _split3(w)
    return (_d(xh, wh) + _d(xh, wm) + _d(xm, wh)) + (_d(xh, wl) + _d(xl, wh) + _d(xm, wm))


def _sigmoid(x):
    return 1.0 / (1.0 + jnp.exp(-x))


def _softplus(x):
    return jnp.maximum(x, 0.0) + jnp.log(1.0 + jnp.exp(-jnp.abs(x)))


def _silu(x):
    return x * _sigmoid(x)


def _gelu_tanh(x):
    return 0.5 * x * (1.0 + jnp.tanh(math.sqrt(2.0 / math.pi) * (x + 0.044715 * (x * x * x))))


def _layer_norm(z, g, b):
    mu = jnp.mean(z, axis=-1, keepdims=True)
    zc = z - mu
    var = jnp.mean(zc * zc, axis=-1, keepdims=True)
    return zc * lax.rsqrt(var + LN_EPS) * g + b


def _modulate(x, sc, sh):
    return x * (1.0 + sc) + sh


def _res_ln(alpha, y, xres, gate, g, b):
    return _layer_norm(alpha * xres + (1.0 + gate) * y, g, b)


@functools.lru_cache(maxsize=None)
def _seg_np(hsz):
    head = np.arange(D) // hsz
    hs = (head[:, None] == np.arange(LANES)[None, :]).astype(np.float32)
    return hs


def _seg_consts(hsz):
    hs = _seg_np(hsz)
    return jnp.asarray(hs, BF16), jnp.asarray(hs.T, BF16)


def _scan_consts(hsz):
    nh = D // hsz
    head = np.arange(D) // hsz
    slot_j = np.arange(LANES) // 16
    slot_h = np.arange(LANES) % 16
    hexp = np.zeros((SUBLANES, LANES, D), np.float32)
    for j in range(SUBLANES):
        hexp[j] = ((slot_j[:, None] == j) & (slot_h[:, None] == head[None, :]) & (slot_h[:, None] < nh))
    hsum = np.transpose(hexp, (0, 2, 1))
    blk = np.arange(MXU_TILE) // hsz
    bd = (blk[:, None] == blk[None, :]).astype(np.float32)
    return jnp.asarray(hexp, BF16), jnp.asarray(hsum, BF16), jnp.asarray(bd, BF16)


def _call(body, grid, ins, outs, scratch=(), name=None, prefetch=None, aliases=None):
    arrays = [a for a, _ in ins]
    in_specs = [s for _, s in ins]
    out_shape = [jax.ShapeDtypeStruct(s, d) for s, d, _ in outs]
    out_specs = [s for _, _, s in outs]
    params = pltpu.CompilerParams(dimension_semantics=("arbitrary",) * len(grid),
                                  vmem_limit_bytes=VMEM_LIMIT_MB << 20)
    kw = {}
    if aliases:
        kw["input_output_aliases"] = aliases
    if prefetch is None:
        fn = pl.pallas_call(body, grid=grid, in_specs=in_specs, out_specs=out_specs, out_shape=out_shape,
                            scratch_shapes=list(scratch), compiler_params=params, name=name, **kw)
        res = fn(*arrays)
    else:
        gs = pltpu.PrefetchScalarGridSpec(num_scalar_prefetch=len(prefetch), grid=grid, in_specs=in_specs,
                                          out_specs=out_specs, scratch_shapes=list(scratch))
        fn = pl.pallas_call(body, grid_spec=gs, out_shape=out_shape, compiler_params=params, name=name, **kw)
        res = fn(*prefetch, *arrays)
    return list(res)


def _full(a):
    nd = a.ndim
    return (a, pl.BlockSpec(a.shape, lambda *_: (0,) * nd))


def _rows(a, tm):
    return (a, pl.BlockSpec((tm, a.shape[1]), lambda i, *_: (i, 0)))


def _rows_out(M, C, tm, dtype=F32):
    return ((M, C), dtype, pl.BlockSpec((tm, C), lambda i, *_: (i, 0)))


class _Stream:
    def __init__(self, x, B, T, tm):
        self.x, self.B, self.T, self.tm = x, B, T, tm
        self.M = B * T
        self.m6 = None

    def set_mods(self, m6):
        self.m6 = m6
        self.rep = jnp.repeat(m6, self.T, axis=0) if self.T < self.tm else None

    def mod(self, c, tm=None):
        tm = tm or self.tm
        if self.T % tm == 0:
            tpb = self.T // tm
            a = self.m6[:, c * D:(c + 1) * D].reshape(self.B, 1, D)
            return (a, pl.BlockSpec((None, 1, D), lambda i, *_: (i // tpb, 0, 0)))
        assert tm % self.T == 0 and self.M % tm == 0
        a = self.rep[:, c * D:(c + 1) * D].reshape(self.M // tm, tm, D)
        return (a, pl.BlockSpec((None, tm, D), lambda i, *_: (i, 0, 0)))


def _mm(x, w_in, *, tm, pro=None, pro_ins=(), epi=None, epi_ins=(), outs=None, name="mm"):
    M, K = x.shape
    n_pro, n_epi = len(pro_ins), len(epi_ins)
    n_out = len(outs)

    def body(*refs):
        x_ref = refs[0]
        pro_refs = refs[1:1 + n_pro]
        w_ref = refs[1 + n_pro]
        epi_refs = refs[2 + n_pro:2 + n_pro + n_epi]
        out_refs = refs[2 + n_pro + n_epi:2 + n_pro + n_epi + n_out]
        a = x_ref[...]
        if pro is not None:
            a = pro(a, *[r[...] for r in pro_refs])
        acc = _d(a, w_ref[...])
        res = epi(acc, *[r[...] for r in epi_refs]) if epi is not None else (acc,)
        for o, r in zip(out_refs, res):
            o[...] = r.astype(o.dtype)

    ins = [_rows(x, tm)] + list(pro_ins) + [w_in] + list(epi_ins)
    return _call(body, (M // tm,), ins, outs, name=name)


def _ada(c_all, w_mod, b_mod, layer):
    Mp = c_all.shape[0]
    N = w_mod.shape[2]
    tn = 1536

    def body(c_ref, w_ref, b_ref, o_ref):
        o_ref[...] = _d(_silu(c_ref[...]), w_ref[...]) + b_ref[...]

    ins = [(c_all, pl.BlockSpec((Mp, D), lambda j: (0, 0))),
           (w_mod, pl.BlockSpec((None, D, tn), lambda j: (layer, 0, j))),
           (b_mod.reshape(b_mod.shape[0], 1, N), pl.BlockSpec((None, 1, tn), lambda j: (layer, 0, j)))]
    outs = [((Mp, N), F32, pl.BlockSpec((Mp, tn), lambda j: (0, j)))]
    return _call(body, (N // tn,), ins, outs, name="ada_mod")[0]


def _pick_tile(*sizes, cap=512):
    t = cap
    while t > SUBLANES and any(s % t for s in sizes):
        t //= 2
    assert all(s % t == 0 for s in sizes), sizes
    return t


def _route(lg):
    lane = lax.broadcasted_iota(I32, lg.shape, 1)
    big = jnp.int32(1 << 20)
    isg = lane < MOE_GROUPS
    gl = jnp.where(isg, lg, NEG)
    gmax = jnp.max(gl, axis=-1, keepdims=True)
    gsel = jnp.min(jnp.where(gl == gmax, lane, big), axis=-1, keepdims=True)
    gsum = jnp.sum(jnp.where(isg, jnp.exp(gl - gmax), 0.0), axis=-1, keepdims=True)
    gw = 1.0 / gsum
    lo = MOE_GROUPS + MOE_EPG * gsel
    ise = (lane >= lo) & (lane < lo + MOE_EPG)
    el = jnp.where(ise, lg, NEG)
    emax = jnp.max(el, axis=-1, keepdims=True)
    ep = jnp.where(ise, jnp.exp(el - emax), 0.0)
    prob = ep / jnp.sum(ep, axis=-1, keepdims=True)
    pm = jnp.where(ise, prob, -1.0)
    p1 = jnp.max(pm, axis=-1, keepdims=True)
    i1 = jnp.min(jnp.where(pm == p1, lane, big), axis=-1, keepdims=True)
    pm2 = jnp.where(lane == i1, -1.0, pm)
    p2 = jnp.max(pm2, axis=-1, keepdims=True)
    i2 = jnp.min(jnp.where(pm2 == p2, lane, big), axis=-1, keepdims=True)
    den = p1 + p2
    w1 = gw * p1 / den
    w2 = gw * p2 / den
    e1 = (i1 - MOE_GROUPS).astype(F32)
    e2 = (i2 - MOE_GROUPS).astype(F32)
    return jnp.where(lane == 0, e1, jnp.where(lane == 1, e2, jnp.where(lane == 2, w1, jnp.where(lane == 3, w2, 0.0))))


NSEG = D // LANES


def _to_tiles(ref, x):
    for s in range(NSEG):
        ref[:, s, :] = x[:, s * LANES:(s + 1) * LANES]


def _from_tiles(ref):
    return jnp.concatenate([ref[:, s, :] for s in range(NSEG)], axis=1)


def _moe_router(st, wgr, bgr):
    tm = st.tm

    def body(x_ref, sc_ref, sh_ref, w_ref, b_ref, h_ref, r_ref):
        h = _modulate(x_ref[...], sc_ref[...], sh_ref[...])
        _to_tiles(h_ref, h)
        r_ref[...] = _route(_d_f32(h, w_ref[...]) + b_ref[...])

    ins = [_rows(st.x, tm), st.mod(4), st.mod(3), _full(wgr), _full(bgr)]
    outs = [((st.M, NSEG, LANES), F32, pl.BlockSpec((tm, NSEG, LANES), lambda i: (i, 0, 0))),
            _rows_out(st.M, LANES, tm)]
    return _call(body, (st.M // tm,), ins, outs, name="moe_router")


def _moe_counts(rinfo, R):
    Mtot = rinfo.shape[0]
    nt = Mtot // R

    def body(r_ref, o_ref):
        j = pl.program_id(0)
        t = pl.program_id(1)

        @pl.when((j == 0) & (t == 0))
        def _():
            o_ref[...] = jnp.zeros_like(o_ref)

        xt = r_ref[...].T
        row = jnp.where(j == 0, xt[0:1, :], xt[1:2, :])
        sub = lax.broadcasted_iota(I32, (LANES, R), 0).astype(F32)
        oh = jnp.where(sub == row, 1.0, 0.0)
        o_ref[...] += jnp.sum(oh, axis=1, keepdims=True)

    ins = [(rinfo, pl.BlockSpec((R, LANES), lambda j, t: (t, 0)))]
    outs = [((LANES, LANES), F32, pl.BlockSpec((LANES, LANES), lambda j, t: (0, 0)))]
    return _call(body, (2, nt), ins, outs, name="moe_counts")[0]


def _moe_dest(rinfo, pstart, R):
    Mtot = rinfo.shape[0]
    nt = Mtot // R
    upper = jnp.asarray(np.triu(np.ones((R, R), np.float32), 1), BF16)

    def body(r_ref, p_ref, u_ref, o_ref, carry):
        j = pl.program_id(0)
        t = pl.program_id(1)

        @pl.when((j == 0) & (t == 0))
        def _():
            carry[...] = jnp.zeros_like(carry)

        xt = r_ref[...].T
        row = jnp.where(j == 0, xt[0:1, :], xt[1:2, :])
        sub = lax.broadcasted_iota(I32, (LANES, R), 0).astype(F32)
        oh = jnp.where(sub == row, 1.0, 0.0)
        cum = _d(oh, u_ref[...])
        base = carry[:, 0:1] + p_ref[:, 0:1]
        dest = jnp.sum(oh * (cum + base), axis=0, keepdims=True)
        o_ref[...] = dest.astype(I32)
        carry[...] += jnp.sum(oh, axis=1, keepdims=True)

    ins = [(rinfo, pl.BlockSpec((R, LANES), lambda j, t: (t, 0))), _full(pstart), _full(upper)]
    outs = [((2 * nt, 1, R), I32, pl.BlockSpec((None, 1, R), lambda j, t: (j * nt + t, 0, 0)))]
    return _call(body, (2, nt), ins, outs, scratch=[pltpu.VMEM((LANES, LANES), F32)], name="moe_dest")[0]


def _moe_ffn(h_parts, slots, blk_expert, nvalid, w1, w3, w2, layer, Mtot):
    nblk = slots.shape[0] // MOE_BLK
    FF = w1.shape[-1]
    any_spec = pl.BlockSpec(memory_space=pl.ANY)
    nparts = len(h_parts)
    starts = np.cumsum([0] + [h.shape[0] for h in h_parts])
    GRP = SUBLANES

    def body(be_ref, nv_ref, slot_ref, *refs):
        h_refs = refs[:nparts]
        w1_ref, w3_ref, w2_ref, y_ref, xbuf, ybuf, sem_in, sem_out = refs[nparts:]
        i = pl.program_id(0)
        nv = nv_ref[0]
        buf = i % 2

        def each_row(kind, blk, b, start):
            def group(r8, c):
                for u in range(GRP):
                    r = r8 * GRP + u
                    s = slot_ref[blk * MOE_BLK + r]

                    @pl.when(s < 2 * Mtot)
                    def _():
                        if kind == "gather":
                            tok = jnp.where(s >= Mtot, s - Mtot, s)
                            if start:
                                for k in range(nparts):
                                    @pl.when((tok >= int(starts[k])) & (tok < int(starts[k + 1])))
                                    def _():
                                        pltpu.make_async_copy(h_refs[k].at[tok - int(starts[k])], xbuf.at[b, r],
                                                              sem_in.at[b]).start(priority=u % 2)
                            else:
                                pltpu.make_async_copy(h_refs[0].at[0], xbuf.at[b, r], sem_in.at[b]).wait()
                        else:
                            cp = pltpu.make_async_copy(ybuf.at[b, r], y_ref.at[s], sem_out.at[b])
                            if start:
                                cp.start(priority=u % 2)
                            else:
                                cp.wait()
                return c
            lax.fori_loop(0, MOE_BLK // GRP, group, 0)

        @pl.when(i < nv)
        def _():
            @pl.when(i == 0)
            def _():
                xbuf[...] = jnp.zeros_like(xbuf)
                each_row("gather", 0, 0, True)

            @pl.when(i + 1 < nv)
            def _():
                each_row("gather", i + 1, 1 - buf, True)

            each_row("gather", i, buf, False)
            x = _from_tiles(xbuf.at[buf]).astype(BF16)
            a = _d(x, w1_ref[...])
            b = _d(x, w3_ref[...])
            y = _d(_silu(a) * b, w2_ref[...])

            @pl.when(i >= 2)
            def _():
                each_row("scatter", i - 2, buf, False)

            _to_tiles(ybuf.at[buf], y)
            each_row("scatter", i, buf, True)

            @pl.when(i == nv - 1)
            def _():
                @pl.when(i >= 1)
                def _():
                    each_row("scatter", i - 1, 1 - buf, False)

                each_row("scatter", i, buf, False)

    def blk(i, be, nv, sl):
        return be[jnp.minimum(i, nv[0] - 1)]

    ins = [(h, any_spec) for h in h_parts] + [
           (w1, pl.BlockSpec((None, None, D, FF), lambda i, be, nv, sl: (layer, blk(i, be, nv, sl), 0, 0))),
           (w3, pl.BlockSpec((None, None, D, FF), lambda i, be, nv, sl: (layer, blk(i, be, nv, sl), 0, 0))),
           (w2, pl.BlockSpec((None, None, FF, D), lambda i, be, nv, sl: (layer, blk(i, be, nv, sl), 0, 0)))]
    outs = [((2 * Mtot, NSEG, LANES), F32, any_spec)]
    scratch = [pltpu.VMEM((2, MOE_BLK, NSEG, LANES), F32), pltpu.VMEM((2, MOE_BLK, NSEG, LANES), F32),
               pltpu.SemaphoreType.DMA((2,)), pltpu.SemaphoreType.DMA((2,))]
    return _call(body, (nblk,), ins, outs, scratch=scratch, prefetch=[blk_expert, nvalid, slots], name="moe_ffn")[0]


def _moe_combine(st, yslot, rinfo_all, off, Mtot, alpha, ln_g, ln_b):
    tm = _pick_tile(st.M, off, Mtot, cap=st.tm)
    b0, b1, br = off // tm, (Mtot + off) // tm, off // tm

    def body(y0_ref, y1_ref, r_ref, x_ref, gate_ref, g_ref, b_ref, o_ref):
        r = r_ref[...]
        y = r[:, 2:3] * _from_tiles(y0_ref) + r[:, 3:4] * _from_tiles(y1_ref)
        o_ref[...] = _res_ln(alpha, y, x_ref[...], gate_ref[...], g_ref[...], b_ref[...])

    ins = [(yslot, pl.BlockSpec((tm, NSEG, LANES), lambda i: (b0 + i, 0, 0))),
           (yslot, pl.BlockSpec((tm, NSEG, LANES), lambda i: (b1 + i, 0, 0))),
           (rinfo_all, pl.BlockSpec((tm, LANES), lambda i: (br + i, 0))),
           _rows(st.x, tm), st.mod(5, tm), _full(ln_g), _full(ln_b)]
    return _call(body, (st.M // tm,), ins, [_rows_out(st.M, D, tm)], name="moe_combine")[0]


def _moe_layer(streams, layer, alpha, p):
    wgr = jnp.zeros((D, LANES), F32).at[:, :MOE_GROUPS].set(p["moe_w_group"][layer])
    wgr = wgr.at[:, MOE_GROUPS:MOE_GROUPS + MOE_EXPERTS].set(p["moe_w_router"][layer])
    bgr = jnp.zeros((1, LANES), F32).at[0, :MOE_GROUPS].set(p["moe_b_group"][layer])
    bgr = bgr.at[0, MOE_GROUPS:MOE_GROUPS + MOE_EXPERTS].set(p["moe_b_router"][layer])
    hs, rs = zip(*[_moe_router(st, wgr, bgr) for st in streams])
    rinfo = jnp.concatenate(rs, axis=0)
    Mtot = rinfo.shape[0]
    R = _pick_tile(Mtot)
    counts = _moe_counts(rinfo, R)[:MOE_EXPERTS, 0].astype(I32)
    padded = (counts + MOE_BLK - 1) // MOE_BLK * MOE_BLK
    pad_end = jnp.cumsum(padded)
    pstart = jnp.zeros((LANES,), F32).at[:MOE_EXPERTS].set((pad_end - padded).astype(F32))
    pstart = jnp.broadcast_to(pstart[:, None], (LANES, LANES))
    nblk = -(-2 * Mtot // MOE_BLK) + MOE_EXPERTS
    blk_first = jnp.arange(nblk, dtype=I32) * MOE_BLK
    blk_expert = jnp.minimum(jnp.sum((pad_end[None, :] <= blk_first[:, None]).astype(I32), axis=1), MOE_EXPERTS - 1)
    nvalid = (pad_end[-1:] // MOE_BLK).astype(I32)
    dest = _moe_dest(rinfo, pstart, R).reshape(-1)
    slots = jnp.full((nblk * MOE_BLK,), 2 * Mtot, I32).at[dest].set(jnp.arange(2 * Mtot, dtype=I32))
    yslot = _moe_ffn(list(hs), slots, blk_expert, nvalid, p["moe_w1"], p["moe_w3"], p["moe_w2"], layer, Mtot)
    ln_g = p["ln_g"][layer, 1].reshape(1, D)
    ln_b = p["ln_b"][layer, 1].reshape(1, D)
    off = 0
    for st in streams:
        st.x = _moe_combine(st, yslot, rinfo, off, Mtot, alpha, ln_g, ln_b)
        off += st.M


def _scan_body(nbg, nv, tb, w_ref, kkn_ref, b_ref, k_ref, r_ref, vt_ref, s0_ref, hexp_ref, hsum_ref, bd_ref,
               o_ref, sf_ref, s_scr):
    t = pl.program_id(1)

    @pl.when(t == 0)
    def _():
        s_scr[...] = s0_ref[...]

    def sub(sb, carry):
        base = pl.multiple_of(sb * SUBLANES, SUBLANES)
        rows = [[ref[bb, pl.ds(base, SUBLANES), :] for bb in range(nbg)]
                for ref in (w_ref, kkn_ref, b_ref, k_ref, r_ref)]
        vt = vt_ref[:, sb].reshape(nbg * nv, LANES).astype(BF16)
        oacc = jnp.zeros((nbg * nv, LANES), F32)
        for j in range(SUBLANES):
            S = [s_scr[bb] for bb in range(nbg)]
            P = jnp.concatenate([S[bb] * rows[1][bb][j:j + 1] for bb in range(nbg)], axis=0).astype(BF16)
            sa = jnp.concatenate([_d(P[:, c0:c0 + MXU_TILE], bd_ref[...]) for c0 in range(0, D, MXU_TILE)], axis=1)
            vb = _d(vt, hexp_ref[j])
            P2 = []
            for bb in range(nbg):
                sl = slice(bb * nv, (bb + 1) * nv)
                Sn = S[bb] * rows[0][bb][j:j + 1] + sa[sl] * rows[2][bb][j:j + 1] + vb[sl] * rows[3][bb][j:j + 1]
                s_scr[bb] = Sn
                P2.append(Sn * rows[4][bb][j:j + 1])
            oacc = oacc + _d(jnp.concatenate(P2, axis=0), hsum_ref[j])
        o_ref[:, sb] = oacc.reshape(nbg, nv, LANES)
        return carry

    lax.fori_loop(0, tb // SUBLANES, sub, 0)

    @pl.when(t == pl.num_programs(1) - 1)
    def _():
        sf_ref[...] = s_scr[...]


def _delta_scan(w, kkn, b, k, r, v, S0, B, T, hsz):
    nh = D // hsz
    nv = hsz
    nbg = 4 if B % 4 == 0 else (2 if B % 2 == 0 else 1)
    tb = min(64, T)
    hexp, hsum, bd = _scan_consts(hsz)
    vt = v.reshape(B, T // SUBLANES, SUBLANES, nh, nv).transpose(0, 1, 4, 2, 3)
    vt = jnp.pad(vt, ((0, 0),) * 4 + ((0, 16 - nh),)).reshape(B, T // SUBLANES, nv, LANES)
    seq = lambda a: (a.reshape(B, T, D), pl.BlockSpec((nbg, tb, D), lambda g, t: (g, t, 0)))
    ins = [seq(w), seq(kkn), seq(b), seq(k), seq(r),
           (vt, pl.BlockSpec((nbg, tb // SUBLANES, nv, LANES), lambda g, t: (g, t, 0, 0))),
           (S0, pl.BlockSpec((nbg, nv, D), lambda g, t: (g, 0, 0))),
           _full(hexp), _full(hsum), _full(bd)]
    outs = [((B, T // SUBLANES, nv, LANES), F32,
             pl.BlockSpec((nbg, tb // SUBLANES, nv, LANES), lambda g, t: (g, t, 0, 0))),
            ((B, nv, D), F32, pl.BlockSpec((nbg, nv, D), lambda g, t: (g, 0, 0)))]
    body = functools.partial(_scan_body, nbg, nv, tb)
    op, sf = _call(body, (B // nbg, T // tb), ins, outs, scratch=[pltpu.VMEM((nbg, nv, D), F32)], name="delta_scan")
    o = op.reshape(B, T // SUBLANES, nv, SUBLANES, 16)[..., :nh].transpose(0, 1, 3, 4, 2).reshape(B * T, D)
    return o, sf


def _shifted_rows(h, first, period, shift=1):
    row = lax.broadcasted_iota(I32, h.shape, 0)
    return jnp.where(row % period < shift, first, pltpu.roll(h, shift, axis=0))


def _rwkv_prep(st, shift_prev, p, tm):
    long_seq = st.T % tm == 0
    tpb = st.T // tm if long_seq else 1
    hs, he = _seg_consts(RWKV_HSZ)
    row = lambda a: _full(a.reshape(1, D))
    wts = [_full(p["rwkv_mu"]), _full(p["rwkv_w_rkv"].astype(BF16)),
           _full(p["rwkv_w1"].astype(BF16)), _full(p["rwkv_w2"].astype(BF16)),
           _full(p["rwkv_a1"].astype(BF16)), _full(p["rwkv_a2"].astype(BF16)),
           _full(p["rwkv_g1"].astype(BF16)), _full(p["rwkv_g2"].astype(BF16)),
           row(p["rwkv_w0"]), row(p["rwkv_a0"]), row(p["rwkv_k_k"]), row(p["rwkv_k_a"]), _full(hs), _full(he)]
    if long_seq:
        nsub = tm // SUBLANES
        first_ins = [(st.x, pl.BlockSpec((SUBLANES, D), lambda i: (jnp.maximum(i * nsub - 1, 0), 0))),
                     (shift_prev.reshape(st.B, 1, D), pl.BlockSpec((None, 1, D), lambda i: (i // tpb, 0, 0)))]
    else:
        first_ins = [_rows(jnp.repeat(shift_prev, st.T, axis=0), tm)]
    nf = len(first_ins)

    def body(x_ref, sc_ref, sh_ref, *refs):
        first_refs, refs = refs[:nf], refs[nf:]
        (mu_ref, wrkv_ref, w1_ref, w2_ref, a1_ref, a2_ref, g1_ref, g2_ref, w0_ref, a0_ref, kk_ref, ka_ref,
         hs_ref, he_ref) = refs[:14]
        h_ref, r_ref, w_ref, k_ref, v_ref, kkn_ref, b_ref, g_ref = refs[14:]
        sc, sh = sc_ref[...], sh_ref[...]
        h = _modulate(x_ref[...], sc, sh)
        if long_seq:
            hh = _modulate(first_refs[0][...], sc, sh)[SUBLANES - 1:SUBLANES]
            first = jnp.where(pl.program_id(0) % tpb == 0, first_refs[1][...], hh)
            hprev = _shifted_rows(h, first, tm)
        else:
            hprev = _shifted_rows(h, first_refs[0][...], st.T)
        xx = hprev - h
        mu = mu_ref[...]
        xr, xw, xk, xv, xa, xg = [h + xx * mu[i:i + 1] for i in range(6)]
        r = _d(xr, wrkv_ref[0])
        k = _d(xk, wrkv_ref[1])
        v = _d(xv, wrkv_ref[2])
        logw = -_softplus(-(w0_ref[...] + _d(jnp.tanh(_d(xw, w1_ref[...])), w2_ref[...]))) - 0.5
        a = _sigmoid(a0_ref[...] + _d(_d(xa, a1_ref[...]), a2_ref[...]))
        g = _d(_sigmoid(_d(xg, g1_ref[...])), g2_ref[...])
        kk = k * kk_ref[...]
        inv = lax.rsqrt(_d_x3(kk * kk, hs_ref[...]) + 1e-6)
        kk = kk * _d_x3(inv, he_ref[...])
        h_ref[...] = h
        r_ref[...] = r
        w_ref[...] = jnp.exp(-jnp.exp(logw))
        k_ref[...] = k * (1.0 + (a - 1.0) * ka_ref[...])
        v_ref[...] = v
        kkn_ref[...] = -kk
        b_ref[...] = kk * a
        g_ref[...] = g

    ins = [_rows(st.x, tm), st.mod(1, tm), st.mod(0, tm)] + first_ins + wts
    outs = [_rows_out(st.M, D, tm) for _ in range(8)]
    return _call(body, (st.M // tm,), ins, outs, name="rwkv_prep")


def _rwkv_out(st, o, r, kmod, v, g, p, alpha, ln_g, ln_b, tm):
    hs, he = _seg_consts(RWKV_HSZ)
    inv_n = 1.0 / RWKV_HSZ

    def pro(o, r, k, v, g, lw, lb, rk, hs, he):
        mean = _d_x3(_d_x3(o, hs) * inv_n, he)
        c = o - mean
        rstd = lax.rsqrt(_d_x3(c * c, hs) * inv_n + RWKV_GN_EPS)
        on = c * _d_x3(rstd, he) * lw + lb
        bonus = _d_x3(_d_x3(r * k * rk, hs), he) * v
        return (on + bonus) * g

    def epi(acc, x, gate, g_, b_):
        return (_res_ln(alpha, acc, x, gate, g_, b_),)

    row = lambda a: _full(a.reshape(1, D))
    pro_ins = [_rows(a, tm) for a in (r, kmod, v, g)] + [row(p["rwkv_ln_w"]), row(p["rwkv_ln_b"]),
                                                       row(p["rwkv_r_k"]), _full(hs), _full(he)]
    epi_ins = [_rows(st.x, tm), st.mod(2, tm), _full(ln_g), _full(ln_b)]
    return _mm(o, _full(p["rwkv_w_o"].astype(BF16)), tm=tm, pro=pro, pro_ins=pro_ins, epi=epi, epi_ins=epi_ins,
               outs=[_rows_out(st.M, D, tm)], name="rwkv_out")[0]


def _rwkv_layer(st, shift_prev, wkv0, p, alpha, ln_g, ln_b):
    B, T = st.B, st.T
    tm = min(256, st.M)
    nh = D // RWKV_HSZ
    h, r, w, kmod, v, kkn, b, g = _rwkv_prep(st, shift_prev, p, tm)
    S0 = wkv0.transpose(0, 2, 1, 3).reshape(B, RWKV_HSZ, D)
    o, sf = _delta_scan(w, kkn, b, kmod, r, v, S0, B, T, RWKV_HSZ)
    st.x = _rwkv_out(st, o, r, kmod, v, g, p, alpha, ln_g, ln_b, tm)
    wkv = sf.reshape(B, RWKV_HSZ, nh, RWKV_HSZ).transpose(0, 2, 1, 3)
    return wkv, h.reshape(B, T, D)[:, -1]


def _pad_cols(w, n):
    return jnp.pad(w, ((0, 0), (0, n - w.shape[1])))


def _gdn_proj(st, p, tm):
    C = 3 * D
    w = _pad_cols(p["gdn_w_in"], C + D + LANES).astype(BF16)

    def epi(acc):
        return acc[:, :C], acc[:, C:C + D], acc[:, C + D:]

    outs = [_rows_out(st.M, C, tm), _rows_out(st.M, D, tm), _rows_out(st.M, LANES, tm)]
    return _mm(st.x, _full(w), tm=tm, pro=_modulate, pro_ins=[st.mod(1, tm), st.mod(0, tm)], epi=epi, outs=outs,
               name="gdn_proj")


def _gdn_conv(st, pre, ba, conv_buf, p, tm, chunked):
    C = 3 * D
    H = GDN_HEADS
    long_seq = st.T % tm == 0
    tpb = st.T // tm if long_seq else 1
    hs, he = _seg_consts(GDN_HSZ)
    hsn = _seg_np(GDN_HSZ)
    he_b = jnp.asarray(hsn.T, BF16)
    he_a = jnp.asarray(np.roll(hsn.T, H, axis=0), BF16)
    alog = jnp.zeros((1, LANES), F32).at[0, H:2 * H].set(p["gdn_A_log"])
    dtb = jnp.zeros((1, LANES), F32).at[0, H:2 * H].set(p["gdn_dt_bias"])
    if long_seq:
        nsub = tm // SUBLANES
        init8 = jnp.pad(conv_buf, ((0, 0), (SUBLANES - (GDN_CONV - 1), 0), (0, 0)))
        first_ins = [(pre, pl.BlockSpec((SUBLANES, C), lambda i: (jnp.maximum(i * nsub - 1, 0), 0))),
                     (init8, pl.BlockSpec((None, SUBLANES, C), lambda i: (i // tpb, 0, 0)))]
    else:
        padded = jnp.pad(conv_buf, ((0, 0), (0, st.T), (0, 0)))
        first_ins = [_rows(padded[:, GDN_CONV - 1 - j:GDN_CONV - 1 - j + st.T].reshape(st.M, C), tm)
                     for j in range(1, GDN_CONV)]
    nf = len(first_ins)

    def body(pre_ref, ba_ref, *refs):
        first_refs, refs = refs[:nf], refs[nf:]
        cw_ref, alog_ref, dtb_ref, hs_ref, he_ref, heb_ref, hea_ref = refs[:7]
        w_ref, kkn_ref, k_ref, q_ref, v_ref = refs[7:]
        x = pre_ref[...]
        if long_seq:
            halo = jnp.where(pl.program_id(0) % tpb == 0, first_refs[1][...], first_refs[0][...])
            big = jnp.concatenate([halo, x], axis=0)
            sh = [pltpu.roll(big, j, axis=0)[SUBLANES:] for j in range(1, GDN_CONV)]
        else:
            sh = [_shifted_rows(x, first_refs[j - 1][...], st.T, j) for j in range(1, GDN_CONV)]
        cw = cw_ref[...]
        conv = sh[2] * cw[0:1]
        conv = conv + sh[1] * cw[1:2]
        conv = conv + sh[0] * cw[2:3]
        conv = conv + x * cw[3:4]
        c = _silu(conv)
        q, k, v = c[:, :D], c[:, D:2 * D], c[:, 2 * D:]
        qn = q * _d_x3(lax.rsqrt(_d_x3(q * q, hs_ref[...]) + 1e-6), he_ref[...]) * (GDN_HSZ ** -0.5)
        kn = k * _d_x3(lax.rsqrt(_d_x3(k * k, hs_ref[...]) + 1e-6), he_ref[...])
        ba = ba_ref[...]
        logdecay = -jnp.exp(alog_ref[...]) * _softplus(ba + dtb_ref[...])
        if chunked:
            w_ref[...] = _sigmoid(ba)
            kkn_ref[...] = logdecay
            k_ref[...] = kn
            q_ref[...] = qn
            v_ref[...] = v
        else:
            beta = _d_x3(_sigmoid(ba), heb_ref[...])
            a = _d_x3(jnp.exp(logdecay), hea_ref[...])
            w_ref[...] = a
            kkn_ref[...] = -(a * beta) * kn
            k_ref[...] = kn
            q_ref[...] = qn
            v_ref[...] = beta * v

    ins = [_rows(pre, tm), _rows(ba, tm)] + first_ins + [_full(p["gdn_conv_w"]), _full(alog), _full(dtb), _full(hs),
                                                         _full(he), _full(he_b), _full(he_a)]
    small = LANES if chunked else D
    outs = [_rows_out(st.M, small, tm), _rows_out(st.M, small, tm)] + [_rows_out(st.M, D, tm) for _ in range(3)]
    return _call(body, (st.M // tm,), ins, outs, name="gdn_conv")


GDN_CHUNK = 64


def _gdn_chunk_scan(q, k, v, beta, g, S0, B, T):
    C = GDN_CHUNK
    H, N = GDN_HEADS, GDN_HSZ
    nchunk = T // C
    tril = _lower_tri(C)
    triu = jnp.asarray(np.triu(np.ones((C, C), np.float32)), BF16)

    def body(q_ref, k_ref, v_ref, b_ref, g_ref, s0_ref, tril_ref, triu_ref, o_ref, sf_ref, s_scr):
        c = pl.program_id(1)

        @pl.when(c == 0)
        def _():
            s_scr[...] = s0_ref[...]

        gblk = g_ref[...]
        gc = _d_3x(tril_ref[...], gblk)
        gh, gm, gl = _split3(gblk)
        tn = lambda a: lax.dot_general(a, triu_ref[...], (((0,), (0,)), ((), ())), preferred_element_type=F32)
        gct = tn(gh) + tn(gm) + tn(gl)
        bblk = b_ref[...]
        ri = lax.broadcasted_iota(I32, (C, C), 0)
        ci = lax.broadcasted_iota(I32, (C, C), 1)
        lower, strict = ri >= ci, ri > ci
        eye = jnp.where(ri == ci, 1.0, 0.0)
        for h in range(H):
            sl = slice(h * N, (h + 1) * N)
            qh, kh, vh = q_ref[:, sl], k_ref[:, sl], v_ref[:, sl]
            bcol = bblk[:, h:h + 1]
            gcol = gc[:, H + h:H + h + 1]
            grow = gct[H + h:H + h + 1, :]
            gamma = jnp.where(lower, jnp.exp(jnp.minimum(gcol - grow, 0.0)), 0.0)
            kb = kh * bcol
            a_mat = jnp.where(strict, _d_nt(kb, kh) * gamma, 0.0)
            t_inv = eye - a_mat
            pw = a_mat
            for _ in range(int(math.log2(C)) - 1):
                pw = _d(pw, pw)
                t_inv = t_inv + _d(t_inv, pw)
            eg = jnp.exp(gcol)
            u = _d(t_inv, vh * bcol)
            w = _d(t_inv, kb * eg)
            qk = jnp.where(lower, _d_nt(qh, kh) * gamma, 0.0)
            S = s_scr[h]
            v_new = u - _d(w, S)
            o_ref[:, sl] = _d(qh * eg, S) + _d(qk, v_new)
            g_last = gcol[C - 1:C]
            kd = (kh * jnp.exp(g_last - gcol)).astype(BF16)
            s_scr[h] = S * jnp.exp(g_last) + lax.dot_general(kd, v_new.astype(BF16), (((0,), (0,)), ((), ())),
                                                             preferred_element_type=F32)

        @pl.when(c == nchunk - 1)
        def _():
            sf_ref[...] = s_scr[...]

    seq = lambda a, w: (a, pl.BlockSpec((C, w), lambda b, c: (b * nchunk + c, 0)))
    ins = [seq(q, D), seq(k, D), seq(v, D), seq(beta, LANES), seq(g, LANES),
           (S0, pl.BlockSpec((None, H, N, N), lambda b, c: (b, 0, 0, 0))),
           (tril, pl.BlockSpec(tril.shape, lambda b, c: (0, 0))), (triu, pl.BlockSpec(triu.shape, lambda b, c: (0, 0)))]
    outs = [((B * T, D), F32, pl.BlockSpec((C, D), lambda b, c: (b * nchunk + c, 0))),
            ((B, H, N, N), F32, pl.BlockSpec((None, H, N, N), lambda b, c: (b, 0, 0, 0)))]
    return _call(body, (B, nchunk), ins, outs, scratch=[pltpu.VMEM((H, N, N), F32)], name="gdn_chunk_scan")


def _gdn_out(st, o, z, p, alpha, ln_g, ln_b, tm):
    hs, he = _seg_consts(GDN_HSZ)
    nw = jnp.tile(p["gdn_norm_w"], GDN_HEADS).reshape(1, D)

    def pro(o, z, nw, hs, he):
        rstd = lax.rsqrt(_d_x3(o * o, hs) * (1.0 / GDN_HSZ) + 1e-6)
        return o * _d_x3(rstd, he) * nw * _silu(z)

    def epi(acc, x, gate, g_, b_):
        return (_res_ln(alpha, acc, x, gate, g_, b_),)

    return _mm(o, _full(p["gdn_w_o"].astype(BF16)), tm=tm, pro=pro, pro_ins=[_rows(z, tm), _full(nw), _full(hs), _full(he)],
               epi=epi, epi_ins=[_rows(st.x, tm), st.mod(2, tm), _full(ln_g), _full(ln_b)],
               outs=[_rows_out(st.M, D, tm)], name="gdn_out")[0]


def _gdn_layer(st, conv_buf, S0, p, alpha, ln_g, ln_b):
    B, T = st.B, st.T
    tm = min(256, st.M)
    pre, z, ba = _gdn_proj(st, p, tm)
    chunked = T % GDN_CHUNK == 0
    if chunked:
        beta, g, kn, qn, v = _gdn_conv(st, pre, ba, conv_buf, p, tm, True)
        o, S = _gdn_chunk_scan(qn, kn, v, beta, g, S0, B, T)
    else:
        w, kkn, kn, qn, vb = _gdn_conv(st, pre, ba, conv_buf, p, tm, False)
        S0t = S0.transpose(0, 3, 1, 2).reshape(B, GDN_HSZ, D)
        o, sf = _delta_scan(w, kkn, kn, kn, qn, vb, S0t, B, T, GDN_HSZ)
        S = sf.reshape(B, GDN_HSZ, GDN_HEADS, GDN_HSZ).transpose(0, 2, 3, 1)
    st.x = _gdn_out(st, o, z, p, alpha, ln_g, ln_b, tm)
    xpad = jnp.concatenate([conv_buf, pre.reshape(B, T, 3 * D)[:, -(GDN_CONV - 1):]], axis=1)
    return S, xpad[:, -(GDN_CONV - 1):]


def _flash_body(cfg, *refs):
    tq, hq, hk = cfg["tq"], cfg["hq"], cfg["hk"]
    fox, bias, aug, window = cfg["fox"], cfg["bias"], cfg["aug"], cfg["window"]
    tk = tq
    G = hq // hk
    Kc = HD + aug
    R = G * tq
    refs = list(refs)
    q_ref, k_ref, v_ref = refs[:3]
    pos = 3
    if aug:
        mb_ref, e_ref = refs[pos:pos + 2]
        pos += 2
    if bias:
        tz_ref = refs[pos]
        pos += 1
    if fox:
        cq_ref, ck_ref = refs[pos:pos + 2]
        pos += 2
    o_ref, kb, vb, s_scr = refs[pos:pos + 4]
    g = pl.program_id(1)
    qi = pl.program_id(2)

    @pl.when(qi == 0)
    def _():
        vb[...] = v_ref[...].astype(BF16)
        if aug:
            k = k_ref[...]
            kb[...] = jnp.concatenate(
                [jnp.concatenate([k[:, kv * HD:(kv + 1) * HD].astype(BF16), e_ref[...]], axis=1) for kv in range(hk)],
                axis=1)
        else:
            kb[...] = k_ref[...].astype(BF16)

    scale = HD ** -0.5
    q = q_ref[...]
    row_t = lax.broadcasted_iota(I32, (R, tk), 0) % tq
    col_s = lax.broadcasted_iota(I32, (R, tk), 1)
    qs, cqs = [], []
    for kv in range(hk):
        x = jnp.concatenate([q[:, (kv * G + gg) * HD:(kv * G + gg + 1) * HD] for gg in range(G)], axis=0) * scale
        if aug:
            x = jnp.concatenate([x.astype(BF16), jnp.concatenate([mb_ref[kv]] * G, axis=0)], axis=1)
        qs.append(x.astype(BF16))
        if fox:
            lane = lax.broadcasted_iota(I32, (tq, LANES), 1)
            cqs.append(jnp.sum(jnp.where(lane == g * hk + kv, cq_ref[...], 0.0), axis=-1, keepdims=True))

    def logits(kv, c, rel, valid):
        off = pl.multiple_of(c * tk, tk)
        s = _d_nt(qs[kv], kb[pl.ds(off, tk), kv * Kc:(kv + 1) * Kc])
        if fox:
            sub = lax.broadcasted_iota(I32, (FOX_HEADS, tk), 0)
            ck = jnp.sum(jnp.where(sub == g * hk + kv, ck_ref[:, pl.ds(off, tk)], 0.0), axis=0, keepdims=True)
            s = s + cqs[kv] - ck
        if bias and rel in (0, 1):
            s = s + jnp.concatenate([tz_ref[kv * G + gg, rel] for gg in range(G)], axis=0)
        if rel == 0:
            s = jnp.where(row_t >= col_s, s, NEG)
        if rel == 3:
            s = jnp.where(col_s > row_t, s, NEG)
        if valid is not None:
            s = jnp.where(valid, s, NEG)
        return s

    nt = tk // LANES

    def lane_tiles(x):
        return [x[:, j * LANES:(j + 1) * LANES] for j in range(nt)]

    if window is not None:
        nch = window // tk
        static = [(qi, 0, None)] + [(jnp.maximum(qi - dc, 0), 1 if dc == 1 else (3 if dc == nch else 2), qi - dc >= 0)
                                    for dc in range(1, nch + 1)]
        n_far = 0
    elif bias:
        static = [(qi, 0, None), (jnp.maximum(qi - 1, 0), 1, qi >= 1)]
        n_far = jnp.maximum(qi - 1, 0)
    else:
        static = [(qi, 0, None)]
        n_far = qi
    n_static = len(static)
    unroll = cfg["unroll"]

    def far_loop(fn, carry):
        ng = n_far // unroll

        def group(gi, cr):
            for u in range(unroll):
                cr = fn(gi * unroll + u, cr)
            return cr

        carry = lax.fori_loop(0, ng, group, carry)
        return lax.fori_loop(ng * unroll, n_far, fn, carry)

    outs = []
    for kv in range(hk):
        def score(c, rel, valid, slot, m128):
            s = logits(kv, c, rel, valid)
            s_scr[slot] = s
            for t in lane_tiles(s):
                m128 = jnp.maximum(m128, t)
            return m128

        m128 = jnp.full((R, LANES), NEG, F32)
        for slot, (c, rel, valid) in enumerate(static):
            m128 = score(c, rel, valid, slot, m128)
        if window is None:
            m128 = far_loop(lambda c, m: score(c, 2, None, n_static + c, m), m128)
        mrep = jnp.broadcast_to(jnp.max(m128, axis=-1, keepdims=True), (R, LANES))

        def accumulate(c, slot, carry):
            l128, acc = carry
            p = [jnp.exp(t - mrep) for t in lane_tiles(s_scr[slot])]
            for t in p:
                l128 = l128 + t
            off = pl.multiple_of(c * tk, tk)
            pm = jnp.concatenate(p, axis=1) if nt > 1 else p[0]
            return l128, acc + _d(pm, vb[pl.ds(off, tk), kv * HD:(kv + 1) * HD])

        carry = (jnp.zeros((R, LANES), F32), jnp.zeros((R, HD), F32))
        for slot, (c, rel, valid) in enumerate(static):
            carry = accumulate(c, slot, carry)
        if window is None:
            carry = far_loop(lambda c, cr: accumulate(c, n_static + c, cr), carry)
        l128, acc = carry
        o = acc / jnp.sum(l128, axis=-1, keepdims=True)
        outs += [o[gg * tq:(gg + 1) * tq] for gg in range(G)]
    o_ref[...] = jnp.concatenate(outs, axis=1)


def _flash(cfg, B, T, ngroups, q_in, k_in, v_in, extra_ins, M):
    tq, hq, hk = cfg["tq"], cfg["hq"], cfg["hk"]
    Kc = HD + cfg["aug"]
    ins = [q_in, k_in, v_in] + list(extra_ins)
    nq = T // tq
    R = (hq // hk) * tq
    if cfg["window"] is not None:
        nslots = cfg["window"] // tq + 1
    else:
        nslots = nq + (1 if cfg["bias"] else 0)
    outs = [((M, ngroups * hq * HD), F32, pl.BlockSpec((tq, hq * HD), lambda b, g, i: (b * nq + i, g)))]
    scratch = [pltpu.VMEM((T, hk * Kc), BF16), pltpu.VMEM((T, hk * HD), BF16), pltpu.VMEM((nslots, R, tq), F32)]
    return _call(functools.partial(_flash_body, cfg), (B, ngroups, nq), ins, outs, scratch=scratch,
                 name="flash_" + cfg["name"])[0]


def _log_sigmoid(x):
    return -_softplus(-x)


def _fox_proj(st, p, tm):
    hw = FOX_HEADS * HD
    w = _pad_cols(p["fox_w_in"], 3 * hw + LANES).astype(BF16)
    bf = jnp.zeros((1, LANES), F32).at[0, :FOX_HEADS].set(p["fox_b_f"])

    def epi(acc, bf):
        return acc[:, :hw], acc[:, hw:3 * hw], _log_sigmoid(acc[:, 3 * hw:] + bf)

    outs = [_rows_out(st.M, hw, tm), _rows_out(st.M, 2 * hw, tm), _rows_out(st.M, LANES, tm)]
    return _mm(st.x, _full(w), tm=tm, pro=_modulate, pro_ins=[st.mod(1, tm), st.mod(0, tm)], epi=epi,
               epi_ins=[_full(bf)], outs=outs, name="fox_proj")


def _lower_tri(n):
    return jnp.asarray(np.tril(np.ones((n, n), np.float32)), BF16)


def _cumsum_rows(x, B, T):
    ch = _pick_tile(T)
    tri = _lower_tri(ch)

    def body(x_ref, tri_ref, o_ref):
        carry = jnp.zeros((1, LANES), F32)
        for c in range(T // ch):
            cc = _d_3x(tri_ref[...], x_ref[c * ch:(c + 1) * ch, :]) + carry
            o_ref[c * ch:(c + 1) * ch, :] = cc
            carry = cc[ch - 1:ch, :]

    return _call(body, (B,), [_rows(x, T), _full(tri)], [_rows_out(B * T, LANES, T)], name="cumsum_rows")[0]


def _out_proj(st, o, w_o, alpha, ln_g, ln_b, tm, name):
    def epi(acc, x, gate, g_, b_):
        return (_res_ln(alpha, acc, x, gate, g_, b_),)

    return _mm(o, _full(w_o.astype(BF16)), tm=tm, epi=epi,
               epi_ins=[_rows(st.x, tm), st.mod(2, tm), _full(ln_g), _full(ln_b)],
               outs=[_rows_out(st.M, D, tm)], name=name)[0]


def _fox_prompt(st, p, alpha, ln_g, ln_b):
    B, T, M = st.B, st.T, st.M
    tm = min(256, M)
    q, kv, logf = _fox_proj(st, p, tm)
    cum = _cumsum_rows(logf, B, T)
    ckT = cum.reshape(B, T, LANES)[:, :, :FOX_HEADS].transpose(0, 2, 1)
    tq = min(512, T)
    nq = T // tq
    cfg = dict(name="fox", tq=tq, hq=2, hk=2, fox=True, bias=False, aug=0, window=None, unroll=2)
    npair = FOX_HEADS // 2
    q_in = (q, pl.BlockSpec((tq, 2 * HD), lambda b, g, i: (b * nq + i, g)))
    k_in = (kv, pl.BlockSpec((T, 2 * HD), lambda b, g, i: (b, g)))
    v_in = (kv, pl.BlockSpec((T, 2 * HD), lambda b, g, i: (b, npair + g)))
    extra = [(cum, pl.BlockSpec((tq, LANES), lambda b, g, i: (b * nq + i, 0))),
             (ckT, pl.BlockSpec((None, FOX_HEADS, T), lambda b, g, i: (b, 0, 0)))]
    o = _flash(cfg, B, T, npair, q_in, k_in, v_in, extra, M)
    st.x = _out_proj(st, o, p["fox_w_o"], alpha, ln_g, ln_b, tm, "fox_out")
    return kv.reshape(B, T, 2, FOX_HEADS, HD), logf.reshape(B, T, LANES)[:, :, :FOX_HEADS]


def _page_ins(cache, page_shape, npages, first_of_step):
    nd = len(page_shape)
    return [(cache, pl.BlockSpec((None,) + tuple(page_shape),
                                 lambda b, s, pt, j=j: (pt[b, first_of_step(s) + j],) + (0,) * nd))
            for j in range(npages)]


def _fox_cum_sample(logf_new, cache_logf, page_table, Tn):
    B, npg = page_table.shape
    PAGE = cache_logf.shape[1]
    H = cache_logf.shape[2]
    tri = _lower_tri(PAGE)

    def body(pt_ref, *refs):
        pages, new_ref, tri_ref, o_ref = refs[:npg], refs[npg], refs[npg + 1], refs[npg + 2]
        carry = jnp.zeros((1, H), F32)
        for j in range(npg):
            cc = _d_3x(tri_ref[...], pages[j][...]) + carry
            o_ref[j * PAGE:(j + 1) * PAGE, :] = cc
            carry = cc[PAGE - 1:PAGE, :]
        xn = jnp.concatenate([new_ref[...][:, :H], jnp.zeros((PAGE - Tn, H), F32)], axis=0)
        o_ref[npg * PAGE:(npg + 1) * PAGE, :] = _d_3x(tri_ref[...], xn) + carry

    ins = _page_ins(cache_logf, (PAGE, H), npg, lambda s: 0)
    ins += [(logf_new, pl.BlockSpec((Tn, LANES), lambda b, s, pt: (b, 0))),
            (tri, pl.BlockSpec(tri.shape, lambda b, s, pt: (0, 0)))]
    Lp = (npg + 1) * PAGE
    outs = [((B, Lp, H), F32, pl.BlockSpec((None, Lp, H), lambda b, s, pt: (b, 0, 0)))]
    return _call(body, (B, 1), ins, outs, prefetch=[page_table], name="fox_cum_sample")[0]


def _rep_mat(n_rows, n_src, per):
    r = np.arange(n_rows)
    src = r // per if per else r % n_src
    return jnp.asarray((src[:, None] == np.arange(n_src)[None, :]).astype(np.float32), BF16)


def _fox_decode(q, kv_new, cache_kv, page_table, cq, ckT, Tn):
    B, npg = page_table.shape
    PAGE = cache_kv.shape[1]
    hw = FOX_HEADS * HD
    R = FOX_HEADS * Tn
    pps = PAGES_PER_STEP if npg % PAGES_PER_STEP == 0 else 1
    nsteps = npg // pps
    cache = cache_kv.reshape(cache_kv.shape[0], PAGE, 2 * hw)
    rep_t = _rep_mat(R, Tn, 0)
    rep_h = _rep_mat(R, FOX_HEADS, Tn)
    scale = HD ** -0.5

    def body(pt_ref, *refs):
        pages = refs[:pps]
        q_ref, new_ref, cq_ref, ck_ref, rt_ref, rh_ref, o_ref, qbd, m_s, l_s, acc = refs[pps:]
        s_id = pl.program_id(1)
        row_h = lax.broadcasted_iota(I32, (R, hw), 0) // Tn
        lane_h = lax.broadcasted_iota(I32, (R, hw), 1) // HD

        @pl.when(s_id == 0)
        def _():
            qrep = _d(rt_ref[...], q_ref[...])
            qbd[...] = jnp.where(row_h == lane_h, qrep * scale, 0.0).astype(BF16)
            m_s[...] = jnp.full(m_s.shape, NEG, F32)
            l_s[...] = jnp.zeros(l_s.shape, F32)
            acc[...] = jnp.zeros(acc.shape, F32)

        cqv = cq_ref[...]

        def chunk(kp, vp, off, mask):
            s = _d_nt(qbd[...], kp) + cqv - _d_3x(rh_ref[...], ck_ref[:, pl.ds(off, kp.shape[0])])
            if mask is not None:
                s = jnp.where(mask, s, NEG)
            m = m_s[:, 0:1]
            m2 = jnp.maximum(m, jnp.max(s, axis=-1, keepdims=True))
            a = jnp.exp(m - m2)
            pr = jnp.exp(s - m2)
            l_s[...] = jnp.broadcast_to(a * l_s[:, 0:1] + jnp.sum(pr, axis=-1, keepdims=True), l_s.shape)
            m_s[...] = jnp.broadcast_to(m2, m_s.shape)
            acc[...] = a * acc[...] + _d(pr, vp)

        chunk(jnp.concatenate([pages[j][:, :hw].astype(BF16) for j in range(pps)], axis=0),
              jnp.concatenate([pages[j][:, hw:].astype(BF16) for j in range(pps)], axis=0),
              pl.multiple_of(s_id * (pps * PAGE), pps * PAGE), None)

        @pl.when(s_id == nsteps - 1)
        def _():
            new = jnp.concatenate([new_ref[...], jnp.zeros((PAGE - Tn, 2 * hw), F32)], axis=0)
            t_row = lax.broadcasted_iota(I32, (R, PAGE), 0) % Tn
            col = lax.broadcasted_iota(I32, (R, PAGE), 1)
            chunk(new[:, :hw], new[:, hw:], npg * PAGE, col <= t_row)
            of = jnp.where(row_h == lane_h, acc[...] / l_s[:, 0:1], 0.0)
            out = of[0:Tn]
            for h in range(1, FOX_HEADS):
                out = out + of[h * Tn:(h + 1) * Tn]
            o_ref[...] = out

    Lp = ckT.shape[2]
    ins = _page_ins(cache, (PAGE, 2 * hw), pps, lambda s: s * pps)
    ins += [(q, pl.BlockSpec((Tn, hw), lambda b, s, pt: (b, 0))),
            (kv_new, pl.BlockSpec((Tn, 2 * hw), lambda b, s, pt: (b, 0))),
            (cq, pl.BlockSpec((None, R, 1), lambda b, s, pt: (b, 0, 0))),
            (ckT, pl.BlockSpec((None, FOX_HEADS, Lp), lambda b, s, pt: (b, 0, 0))),
            (rep_t, pl.BlockSpec(rep_t.shape, lambda b, s, pt: (0, 0))),
            (rep_h, pl.BlockSpec(rep_h.shape, lambda b, s, pt: (0, 0)))]
    outs = [((B * Tn, hw), F32, pl.BlockSpec((Tn, hw), lambda b, s, pt: (b, 0)))]
    scratch = [pltpu.VMEM((R, hw), BF16), pltpu.VMEM((R, LANES), F32), pltpu.VMEM((R, LANES), F32),
               pltpu.VMEM((R, hw), F32)]
    return _call(body, (B, nsteps), ins, outs, scratch=scratch, prefetch=[page_table], name="fox_decode")[0]


def _fox_sample(st, cache_kv, cache_logf, page_table, p, alpha, ln_g, ln_b):
    B, Tn, M = st.B, st.T, st.M
    tm = min(256, M)
    npg = page_table.shape[1]
    PAGE = cache_kv.shape[1]
    q, kv, logf = _fox_proj(st, p, tm)
    cum = _fox_cum_sample(logf, cache_logf, page_table, Tn)
    cq = cum[:, npg * PAGE:npg * PAGE + Tn].transpose(0, 2, 1).reshape(B, FOX_HEADS * Tn, 1)
    ckT = cum.transpose(0, 2, 1)
    o = _fox_decode(q, kv, cache_kv, page_table, cq, ckT, Tn)
    st.x = _out_proj(st, o, p["fox_w_o"], alpha, ln_g, ln_b, tm, "fox_out")
    return kv.reshape(B, Tn, 2, FOX_HEADS, HD), logf.reshape(B, Tn, LANES)[:, :, :FOX_HEADS]


KVW = NSA_KVH * HD
HALF = CMP_BLK // 2


def _t5_bucket(dist):
    exact = REL_BUCKETS // 2
    d = jnp.maximum(dist, 0)
    far = exact + (jnp.log(jnp.maximum(d, 1).astype(F32) / exact) / math.log(REL_MAX_DIST / exact)
                   * (REL_BUCKETS - exact)).astype(I32)
    return jnp.where(d < exact, d, jnp.minimum(far, REL_BUCKETS - 1))


def _rel_bias(table, dist):
    return jnp.moveaxis(table[_t5_bucket(dist)], -1, 0)


def _nsa_proj(st, p, tm):
    qw = NSA_HEADS * HD
    w = _pad_cols(p["nsa_w_in"], qw + 6 * KVW + LANES).astype(BF16)

    def epi(acc):
        return (acc[:, :qw], acc[:, qw:qw + 2 * KVW], acc[:, qw + 2 * KVW:qw + 4 * KVW],
                acc[:, qw + 4 * KVW:qw + 6 * KVW], acc[:, qw + 6 * KVW:])

    outs = [_rows_out(st.M, qw, tm)] + [_rows_out(st.M, 2 * KVW, tm)] * 3 + [_rows_out(st.M, LANES, tm)]
    return _mm(st.x, _full(w), tm=tm, pro=_modulate, pro_ins=[st.mod(1, tm), st.mod(0, tm)], epi=epi, outs=outs,
               name="nsa_proj")


def _cmp_weights(p):
    eye = jnp.eye(NSA_KVH, dtype=F32)
    wk = jnp.einsum("ab,vlde->vladbe", eye, p["nsa_cmp_w1"]).reshape(2, CMP_BLK, KVW, KVW)
    wc = wk.reshape(2, 2, HALF, KVW, KVW).transpose(1, 2, 0, 3, 4)
    w2c = jnp.einsum("ab,vde->vadbe", eye, p["nsa_cmp_w2"]).reshape(2, KVW, KVW)
    b1 = jnp.tile(p["nsa_cmp_b1"][:, None, :], (1, NSA_KVH, 1)).reshape(1, 2 * KVW)
    return wc.astype(BF16), w2c.astype(BF16), b1


def _compress_body(nx, *refs):
    x_refs = refs[:nx]
    wc_ref, w2_ref, b1_ref, o_ref, ua, ub = refs[nx:]
    rows = ua.shape[1]
    nl = 2 * KVW // LANES
    acc = [[jnp.zeros((rows, KVW), F32) for _ in range(2)] for _ in range(2)]
    for l in range(HALF):
        for kv in range(2):
            lo = l * 2 * KVW + kv * KVW
            piece = [r[:, lo:lo + KVW] for r in x_refs]
            piece = (jnp.concatenate(piece, axis=0) if nx > 1 else piece[0]).astype(BF16)
            for half in range(2):
                acc[half][kv] = acc[half][kv] + _d(piece, wc_ref[half, l, kv])
    for scr, a in ((ua, acc[0]), (ub, acc[1])):
        full = jnp.concatenate(a, axis=1)
        for c in range(nl):
            scr[c] = full[:, c * LANES:(c + 1) * LANES]
    hid = jnp.concatenate([ua[c, pl.ds(0, rows // 2, stride=2), :] + ub[c, pl.ds(1, rows // 2, stride=2), :]
                           for c in range(nl)], axis=1)
    hid = _gelu_tanh(hid + b1_ref[...])
    o_ref[...] = jnp.concatenate([_d(hid[:, :KVW], w2_ref[0]), _d(hid[:, KVW:], w2_ref[1])], axis=1)


def _compress_dense(rows_kv, cw):
    wc, w2c, b1 = cw
    M = rows_kv.shape[0]
    x = rows_kv.reshape(M // HALF, HALF * 2 * KVW)
    nh = M // HALF
    th = _pick_tile(nh, cap=128)
    ins = [_rows(x, th), _full(wc), _full(w2c), _full(b1)]
    outs = [_rows_out(nh // 2, 2 * KVW, th // 2)]
    scratch = [pltpu.VMEM((2 * KVW // LANES, th, LANES), F32)] * 2
    return _call(functools.partial(_compress_body, 1), (nh // th,), ins, outs, scratch=scratch, name="nsa_compress")[0]


def _pair_mat(nc, ns):
    n = np.arange(nc)
    return jnp.asarray((n[:, None] // (SEL_BLK // CMP_BLK) == np.arange(ns)[None, :]).astype(np.float32), BF16)


def _top_blocks(score, n_sel):
    lane = lax.broadcasted_iota(I32, score.shape, 1)
    big = jnp.int32(1 << 20)
    sel = jnp.zeros(score.shape, jnp.bool_)
    work = score
    for _ in range(n_sel):
        m = jnp.max(work, axis=-1, keepdims=True)
        idx = jnp.min(jnp.where(work == m, lane, big), axis=-1, keepdims=True)
        hit = lane == idx
        sel = sel | hit
        work = jnp.where(hit, -3e38, work)
    return jnp.where(sel, 0.0, SEL_NEG)


def _masked_softmax(s, mask):
    s = jnp.where(mask, s, NEG)
    m = jnp.max(s, axis=-1, keepdims=True)
    p = jnp.where(mask, jnp.exp(s - m), 0.0)
    l = jnp.sum(p, axis=-1, keepdims=True)
    return p / jnp.where(l > 0.0, l, 1.0)


def _block_scores(imp, tpos, ns):
    blk = lax.broadcasted_iota(I32, imp.shape, 1)
    cur = tpos // SEL_BLK
    forced = (blk == 0) | (blk == cur) | (blk == cur - 1)
    score = jnp.where(forced, FORCE_SCORE, imp)
    return jnp.where(blk * SEL_BLK > tpos, -1.0, score)


CMP_NEAR_LO = -3
CMP_NEAR_N = 8


def _cmp_bias_pattern(table, tq):
    assert tq == LANES and CMP_BLK == 32 and REL_MAX_DIST == LANES
    r = jnp.arange(tq, dtype=I32)[:, None]
    m = CMP_NEAR_LO + jnp.arange(CMP_NEAR_N, dtype=I32)[None, :]
    near = _rel_bias(table, r - (CMP_BLK - 1) + CMP_BLK * m)
    far = jnp.broadcast_to(table[REL_BUCKETS - 1][:, None, None], (table.shape[1], tq, 1))
    return jnp.pad(jnp.concatenate([near, far], axis=2), ((0, 0), (0, 0), (0, LANES - CMP_NEAR_N - 1)))


def _nsa_cmp_prompt(q, kcvc, pat, B, T, tq):
    nc, ns = T // CMP_BLK, -(-T // SEL_BLK)
    n_sel = min(N_SEL, ns)
    nq = T // tq
    G = NSA_G
    pair = _pair_mat(nc, ns)
    scale = HD ** -0.5
    rb = tq // CMP_BLK

    def body(q_ref, kc_ref, vc_ref, b_ref, pair_ref, o_ref, mb_ref):
        qi = pl.program_id(1)
        q = q_ref[...]
        R = G * tq
        tpos = qi * tq + lax.broadcasted_iota(I32, (R, 1), 0) % tq
        cmp_end = lax.broadcasted_iota(I32, (R, nc), 1) * CMP_BLK + (CMP_BLK - 1)
        mask = cmp_end <= tpos
        j = lax.broadcasted_iota(I32, (LANES, nc), 0)
        m = rb * qi - lax.broadcasted_iota(I32, (LANES, nc), 1)
        sel = ((j < CMP_NEAR_N) & (m == j + CMP_NEAR_LO)) | ((j == CMP_NEAR_N) & (m >= CMP_NEAR_LO + CMP_NEAR_N))
        sel = jnp.where(sel, 1.0, 0.0).astype(BF16)
        outs = []
        for kv in range(NSA_KVH):
            qs = jnp.concatenate([q[:, (kv * G + gg) * HD:(kv * G + gg + 1) * HD] for gg in range(G)], axis=0) * scale
            s = _d_nt(qs, kc_ref[:, kv * HD:(kv + 1) * HD])
            s = s + _d_x3(jnp.concatenate([b_ref[kv * G + gg] for gg in range(G)], axis=0), sel)
            pc = _masked_softmax(s, mask)
            oc = _d(pc, vc_ref[:, kv * HD:(kv + 1) * HD])
            outs += [oc[gg * tq:(gg + 1) * tq] for gg in range(G)]
            imp = pc[0:tq]
            for gg in range(1, G):
                imp = imp + pc[gg * tq:(gg + 1) * tq]
            score = _block_scores(_d_x3(imp, pair_ref[...]), tpos[0:tq], ns)
            mb_ref[kv] = _top_blocks(score, n_sel).astype(mb_ref.dtype)
        o_ref[...] = jnp.concatenate(outs, axis=1)

    ins = [(q, pl.BlockSpec((tq, NSA_HEADS * HD), lambda b, i: (b * nq + i, 0))),
           (kcvc, pl.BlockSpec((nc, KVW), lambda b, i: (b, 0))),
           (kcvc, pl.BlockSpec((nc, KVW), lambda b, i: (b, 1))),
           (pat, pl.BlockSpec(pat.shape, lambda b, i: (0, 0, 0))),
           _full(pair)]
    outs = [((B * T, NSA_HEADS * HD), F32, pl.BlockSpec((tq, NSA_HEADS * HD), lambda b, i: (b * nq + i, 0))),
            ((B, NSA_KVH, T, ns), BF16, pl.BlockSpec((None, NSA_KVH, tq, ns), lambda b, i: (b, 0, i, 0)))]
    return _call(body, (B, nq), ins, outs, name="nsa_cmp_select")


def _gate_mats():
    hsn = _seg_np(HD)
    return [jnp.asarray(np.roll(hsn.T, br * NSA_HEADS, axis=0), BF16) for br in range(3)]


def _nsa_out(st, o_c, o_s, o_w, gates, p, alpha, ln_g, ln_b, tm):
    def pro(oc, os_, ow, gl, e0, e1, e2):
        sg = _sigmoid(gl)
        return _d_x3(sg, e0) * oc + _d_x3(sg, e1) * os_ + _d_x3(sg, e2) * ow

    def epi(acc, x, gate, g_, b_):
        return (_res_ln(alpha, acc, x, gate, g_, b_),)

    pro_ins = [_rows(o_s, tm), _rows(o_w, tm), _rows(gates, tm)] + [_full(e) for e in _gate_mats()]
    return _mm(o_c, _full(p["nsa_w_o"].astype(BF16)), tm=tm, pro=pro, pro_ins=pro_ins, epi=epi,
               epi_ins=[_rows(st.x, tm), st.mod(2, tm), _full(ln_g), _full(ln_b)],
               outs=[_rows_out(st.M, D, tm)], name="nsa_out")[0]


def _nsa_prompt(st, p, alpha, ln_g, ln_b):
    B, T, M = st.B, st.T, st.M
    tm = min(256, M)
    tq = ATT_T
    nq = T // tq
    table = p["rel_bias"]
    q, cmp_rows, slc_rows, win_rows, gates = _nsa_proj(st, p, tm)
    kcvc = _compress_dense(cmp_rows, _cmp_weights(p))
    nc, ns = T // CMP_BLK, -(-T // SEL_BLK)
    o_c, mb = _nsa_cmp_prompt(q, kcvc, _cmp_bias_pattern(table, tq), B, T, tq)
    r = jnp.arange(tq, dtype=I32)
    far = table[REL_BUCKETS - 1][:, None, None]
    tz = jnp.stack([_rel_bias(table, r[:, None] - r[None, :]) - far,
                    _rel_bias(table, tq + r[:, None] - r[None, :]) - far], axis=1)
    e_blk = jnp.asarray((np.arange(T)[:, None] // SEL_BLK == np.arange(ns)[None, :]).astype(np.float32), BF16)
    q_in = (q, pl.BlockSpec((tq, NSA_HEADS * HD), lambda b, g, i: (b * nq + i, 0)))
    kv_in = lambda a, c: (a, pl.BlockSpec((T, KVW), lambda b, g, i: (b, c)))
    tz_in = (tz, pl.BlockSpec(tz.shape, lambda b, g, i: (0, 0, 0, 0)))
    cfg = dict(name="nsa_slc", tq=tq, hq=NSA_HEADS, hk=NSA_KVH, fox=False, bias=True, aug=ns, window=None, unroll=4)
    extra = [(mb, pl.BlockSpec((None, NSA_KVH, tq, ns), lambda b, g, i: (b, 0, i, 0))),
             (e_blk, pl.BlockSpec(e_blk.shape, lambda b, g, i: (0, 0))), tz_in]
    o_s = _flash(cfg, B, T, 1, q_in, kv_in(slc_rows, 0), kv_in(slc_rows, 1), extra, M)
    cfg = dict(name="nsa_win", tq=tq, hq=NSA_HEADS, hk=NSA_KVH, fox=False, bias=True, aug=0, window=WINDOW, unroll=1)
    o_w = _flash(cfg, B, T, 1, q_in, kv_in(win_rows, 0), kv_in(win_rows, 1), [tz_in], M)
    st.x = _nsa_out(st, o_c, o_s, o_w, gates, p, alpha, ln_g, ln_b, tm)
    shp = (B, T, 2, NSA_KVH, HD)
    keep = min(WINDOW, T)
    return cmp_rows.reshape(shp), slc_rows.reshape(shp), win_rows.reshape(shp)[:, T - keep:]


def _compress_paged(cache_cmp, page_table, cw):
    wc, w2c, b1 = cw
    B, npg = page_table.shape
    PAGE = cache_cmp.shape[1]
    hp = PAGE // HALF
    cache = cache_cmp.reshape(cache_cmp.shape[0], hp, HALF * 2 * KVW)
    nb = 2 if B % 2 == 0 else 1
    ins = [(cache, pl.BlockSpec((None, hp, HALF * 2 * KVW), lambda g, pt, bb=bb, j=j: (pt[g * nb + bb, j], 0, 0)))
           for bb in range(nb) for j in range(npg)]
    const = lambda a: (a, pl.BlockSpec(a.shape, lambda g, pt: (0,) * a.ndim))
    ins += [const(wc), const(w2c), const(b1)]
    rows = nb * npg * hp
    outs = [((B * npg * hp // 2, 2 * KVW), F32, pl.BlockSpec((rows // 2, 2 * KVW), lambda g, pt: (g, 0)))]
    scratch = [pltpu.VMEM((2 * KVW // LANES, rows, LANES), F32)] * 2
    body = lambda pt_ref, *refs: _compress_body(nb * npg, *refs)
    return _call(body, (B // nb,), ins, outs, scratch=scratch, prefetch=[page_table], name="nsa_compress_paged")[0]


def _nsa_decode(q, slc_new, win_new, kcvc, cache_slc, cache_win, page_table, gcol, consts, Tn, offset):
    B, npg = page_table.shape
    PAGE = cache_slc.shape[1]
    Wb = cache_win.shape[1]
    R = NSA_HEADS * Tn
    L = offset + Tn
    nc, ns = L // CMP_BLK, -(-L // SEL_BLK)
    n_sel = min(N_SEL, ns)
    nck = npg + 1
    qw = NSA_HEADS * HD
    cache = cache_slc.reshape(cache_slc.shape[0], PAGE, 2 * KVW)
    win = cache_win.reshape(B, Wb, 2 * KVW)
    scale = HD ** -0.5
    names = ["rep_t", "fold", "unfold", "bias_c", "mask_c", "pair", "e_blk", "bias_s", "bias_w"]
    cvals = [consts[n] for n in names]

    def body(pt_ref, *refs):
        pages = refs[:npg]
        (q_ref, sn_ref, wn_ref, kc_ref, vc_ref, win_ref, g_ref, rt_ref, fold_ref, unfold_ref, bc_ref, mc_ref, pair_ref,
         e_ref, bs_ref, bw_ref, o_ref, wout_ref) = refs[npg:]
        row_h = lax.broadcasted_iota(I32, (R, qw), 0) // Tn
        lane_h = lax.broadcasted_iota(I32, (R, qw), 1) // HD
        own = row_h == lane_h
        qrep = _d(rt_ref[...], q_ref[...])
        qbd = (_d(jnp.where(own, qrep, 0.0), fold_ref[...]) * scale).astype(BF16)
        pad = lambda x: jnp.concatenate([x, jnp.zeros((PAGE - Tn, x.shape[1]), x.dtype)], axis=0)

        pc = _masked_softmax(_d_nt(qbd, kc_ref[...]) + bc_ref[...], mc_ref[...] > 0.0)
        o_c = _d(pc, vc_ref[...])
        imp = []
        for kv in range(NSA_KVH):
            a = pc[kv * NSA_G * Tn:(kv * NSA_G + 1) * Tn]
            for gg in range(1, NSA_G):
                a = a + pc[(kv * NSA_G + gg) * Tn:(kv * NSA_G + gg + 1) * Tn]
            imp.append(a)
        imp = jnp.concatenate(imp, axis=0)
        tpos = offset + lax.broadcasted_iota(I32, (NSA_KVH * Tn, 1), 0) % Tn
        mb = _top_blocks(_block_scores(_d_x3(imp, pair_ref[...]), tpos, ns), n_sel)
        mb = jnp.concatenate([mb[kv * Tn:(kv + 1) * Tn] for kv in range(NSA_KVH) for _ in range(NSA_G)], axis=0)

        sn = pad(sn_ref[...])
        ks = jnp.concatenate([pages[j][:, :KVW].astype(BF16) for j in range(npg)] + [sn[:, :KVW].astype(BF16)], axis=0)
        vs = jnp.concatenate([pages[j][:, KVW:].astype(BF16) for j in range(npg)] + [sn[:, KVW:].astype(BF16)], axis=0)
        s = _d_nt(qbd, ks) + _d(mb, e_ref[...]) + bs_ref[...]
        p = jnp.exp(s - jnp.max(s, axis=-1, keepdims=True))
        o_s = _d(p, vs) / jnp.sum(p, axis=-1, keepdims=True)

        w = win_ref[...]
        wn = pad(wn_ref[...])
        s = jnp.concatenate([_d_nt(qbd, w[:, :KVW]), _d_nt(qbd, wn[:, :KVW])], axis=1) + bw_ref[...]
        p = jnp.exp(s - jnp.max(s, axis=-1, keepdims=True))
        o_w = (_d(p[:, :Wb], w[:, KVW:]) + _d(p[:, Wb:], wn[:, KVW:])) / jnp.sum(p, axis=-1, keepdims=True)
        wout_ref[...] = jnp.concatenate([w[Tn:], wn_ref[...]], axis=0)

        sg = _sigmoid(g_ref[...])
        o = sg[:, 0:1] * o_c + sg[:, 1:2] * o_s + sg[:, 2:3] * o_w
        of = jnp.where(own, _d_x3(o, unfold_ref[...]), 0.0)
        out = of[0:Tn]
        for h in range(1, NSA_HEADS):
            out = out + of[h * Tn:(h + 1) * Tn]
        o_ref[...] = out

    c2 = lambda b, pt: (0, 0)
    ins = [(cache, pl.BlockSpec((None, PAGE, 2 * KVW), lambda b, pt, j=j: (pt[b, j], 0, 0))) for j in range(npg)]
    ins += [(q, pl.BlockSpec((Tn, qw), lambda b, pt: (b, 0))),
            (slc_new, pl.BlockSpec((Tn, 2 * KVW), lambda b, pt: (b, 0))),
            (win_new, pl.BlockSpec((Tn, 2 * KVW), lambda b, pt: (b, 0))),
            (kcvc, pl.BlockSpec((nc, KVW), lambda b, pt: (b, 0))),
            (kcvc, pl.BlockSpec((nc, KVW), lambda b, pt: (b, 1))),
            (win, pl.BlockSpec((None, Wb, 2 * KVW), lambda b, pt: (b, 0, 0))),
            (gcol, pl.BlockSpec((None, R, 3), lambda b, pt: (b, 0, 0)))]
    ins += [(a, pl.BlockSpec(a.shape, c2)) for a in cvals]
    outs = [((B * Tn, qw), F32, pl.BlockSpec((Tn, qw), lambda b, pt: (b, 0))),
            ((B, Wb, 2 * KVW), F32, pl.BlockSpec((None, Wb, 2 * KVW), lambda b, pt: (b, 0, 0)))]
    return _call(body, (B,), ins, outs, prefetch=[page_table], name="nsa_decode")


def _nsa_decode_consts(table, Tn, offset, npg, PAGE, Wb):
    R = NSA_HEADS * Tn
    L = offset + Tn
    nc, ns = L // CMP_BLK, -(-L // SEL_BLK)
    Lp = (npg + 1) * PAGE
    tpos = offset + jnp.arange(Tn, dtype=I32)
    rows = lambda x: x.reshape(R, x.shape[-1])
    cmp_end = jnp.arange(nc, dtype=I32) * CMP_BLK + CMP_BLK - 1
    dist_c = tpos[:, None] - cmp_end[None, :]
    spos = jnp.arange(Lp, dtype=I32)
    dist_s = tpos[:, None] - spos[None, :]
    ok_s = (dist_s >= 0) & (spos[None, :] < L)
    col = jnp.arange(Wb + PAGE, dtype=I32)
    wpos = offset - Wb + col
    dist_w = tpos[:, None] - wpos[None, :]
    ok_w = (dist_w >= 0) & (dist_w < WINDOW) & (wpos[None, :] >= 0) & (col[None, :] < Wb + Tn)
    tile = lambda m: jnp.tile(m[None], (NSA_HEADS, 1, 1))
    fold = np.zeros((NSA_HEADS, HD, NSA_KVH, HD), np.float32)
    for h in range(NSA_HEADS):
        fold[h, :, h // NSA_G, :] = np.eye(HD)
    fold = fold.reshape(NSA_HEADS * HD, KVW)
    return dict(
        rep_t=_rep_mat(R, Tn, 0), fold=jnp.asarray(fold, BF16), unfold=jnp.asarray(fold.T, BF16),
        bias_c=rows(_rel_bias(table, dist_c)), mask_c=rows(tile((dist_c >= 0).astype(F32))), pair=_pair_mat(nc, ns),
        e_blk=jnp.asarray((np.arange(ns)[:, None] == np.arange(Lp)[None, :] // SEL_BLK).astype(np.float32), BF16),
        bias_s=rows(_rel_bias(table, dist_s) + tile(jnp.where(ok_s, 0.0, NEG))),
        bias_w=rows(_rel_bias(table, dist_w) + tile(jnp.where(ok_w, 0.0, NEG))))


def _nsa_sample(st, cache_cmp, cache_slc, cache_win, page_table, p, alpha, ln_g, ln_b):
    B, Tn, M = st.B, st.T, st.M
    tm = min(256, M)
    npg = page_table.shape[1]
    PAGE = cache_slc.shape[1]
    Wb = cache_win.shape[1]
    offset = npg * PAGE
    assert offset % CMP_BLK == 0 and Tn < CMP_BLK and Wb == WINDOW and Tn % SUBLANES == 0
    q, cmp_rows, slc_rows, win_rows, gates = _nsa_proj(st, p, tm)
    kcvc = _compress_paged(cache_cmp, page_table, _cmp_weights(p))
    gcol = gates.reshape(B, Tn, LANES)[:, :, :3 * NSA_HEADS].reshape(B, Tn, 3, NSA_HEADS)
    gcol = gcol.transpose(0, 3, 1, 2).reshape(B, NSA_HEADS * Tn, 3)
    consts = _nsa_decode_consts(p["rel_bias"], Tn, offset, npg, PAGE, Wb)
    o, wout = _nsa_decode(q, slc_rows, win_rows, kcvc, cache_slc, cache_win, page_table, gcol, consts, Tn, offset)
    st.x = _out_proj(st, o, p["nsa_w_o"], alpha, ln_g, ln_b, tm, "nsa_out_s")
    shp = (B, Tn, 2, NSA_KVH, HD)
    return cmp_rows.reshape(shp), slc_rows.reshape(shp), wout.reshape(B, Wb, 2, NSA_KVH, HD)


def kernel(x_prompt, x_sample, state_rwkv_wkv, state_rwkv_shift, cache_nsa_cmp, cache_nsa_slc, cache_nsa_win, cache_fox_kv, cache_fox_logf, state_gdn_S, state_gdn_conv, page_table, c_prompt, c_sample, w_mod, b_mod, ln_g, ln_b, moe_w_group, moe_b_group, moe_w_router, moe_b_router, moe_w1, moe_w3, moe_w2, rwkv_mu, rwkv_w_rkv, rwkv_w0, rwkv_w1, rwkv_w2, rwkv_a0, rwkv_a1, rwkv_a2, rwkv_g1, rwkv_g2, rwkv_k_k, rwkv_k_a, rwkv_r_k, rwkv_ln_w, rwkv_ln_b, rwkv_w_o, nsa_w_in, nsa_cmp_w1, nsa_cmp_b1, nsa_cmp_w2, nsa_w_o, rel_bias, fox_w_in, fox_b_f, fox_w_o, gdn_w_in, gdn_conv_w, gdn_A_log, gdn_dt_bias, gdn_norm_w, gdn_w_o):
    p = dict(locals())
    Bp, T, _ = x_prompt.shape
    Bs, Tn, _ = x_sample.shape
    depth = w_mod.shape[0]
    alpha = (2 * depth) ** 0.25
    sp = _Stream(x_prompt.reshape(Bp * T, D), Bp, T, min(512, T))
    ss = _Stream(x_sample.reshape(Bs * Tn, D), Bs, Tn, min(256, Bs * Tn))
    nc = Bp + Bs
    c_all = jnp.pad(jnp.concatenate([c_prompt, c_sample], axis=0), ((0, -nc % SUBLANES), (0, 0)))
    out = {}
    for layer in range(depth):
        m6 = _ada(c_all, w_mod, b_mod, layer)
        sp.set_mods(m6[:Bp])
        ss.set_mods(m6[Bp:nc])
        g0 = ln_g[layer, 0].reshape(1, D)
        b0 = ln_b[layer, 0].reshape(1, D)
        kind = layer % 4
        if kind == 0:
            nh = D // RWKV_HSZ
            out["wkv_p"], out["shift_p"] = _rwkv_layer(sp, jnp.zeros((Bp, D), F32),
                                                       jnp.zeros((Bp, nh, RWKV_HSZ, RWKV_HSZ), F32), p, alpha, g0, b0)
            out["wkv_s"], out["shift_s"] = _rwkv_layer(ss, state_rwkv_shift, state_rwkv_wkv, p, alpha, g0, b0)
        elif kind == 1:
            out["cmp_p"], out["slc_p"], out["win_p"] = _nsa_prompt(sp, p, alpha, g0, b0)
            out["cmp_s"], out["slc_s"], out["win_s"] = _nsa_sample(ss, cache_nsa_cmp, cache_nsa_slc, cache_nsa_win,
                                                                   page_table, p, alpha, g0, b0)
        elif kind == 2:
            out["kv_p"], out["logf_p"] = _fox_prompt(sp, p, alpha, g0, b0)
            out["kv_s"], out["logf_s"] = _fox_sample(ss, cache_fox_kv, cache_fox_logf, page_table, p, alpha, g0, b0)
        else:
            out["S_p"], out["conv_p"] = _gdn_layer(sp, jnp.zeros((Bp, GDN_CONV - 1, 3 * D), F32),
                                                   jnp.zeros((Bp, GDN_HEADS, GDN_HSZ, GDN_HSZ), F32), p, alpha, g0, b0)
            out["S_s"], out["conv_s"] = _gdn_layer(ss, state_gdn_conv, state_gdn_S, p, alpha, g0, b0)
        _moe_layer([sp, ss], layer, alpha, p)
    return (sp.x.reshape(Bp, T, D), ss.x.reshape(Bs, Tn, D), out["wkv_p"], out["wkv_s"], out["shift_p"], out["shift_s"],
            out["cmp_p"], out["cmp_s"], out["slc_p"], out["slc_s"], out["win_p"], out["win_s"],
            out["kv_p"], out["kv_s"], out["logf_p"], out["logf_s"], out["S_p"], out["S_s"], out["conv_p"], out["conv_s"])
```

```python
import functools
import math

import numpy as np
import jax
import jax.numpy as jnp
from jax import lax
from jax.experimental import pallas as pl
from jax.experimental.pallas import tpu as pltpu

F32 = jnp.float32
BF16 = jnp.bfloat16
I32 = jnp.int32
NEG = -1e30
LN_EPS = 1e-5
D = 1024
LANES = 128
SUBLANES = 8
MXU_TILE = 256
VMEM_LIMIT_MB = 56

RWKV_HSZ = 64
RWKV_GN_EPS = 64e-5
NSA_HEADS, NSA_KVH, HD = 16, 4, 64
NSA_G = NSA_HEADS // NSA_KVH
CMP_BLK, SEL_BLK, N_SEL, WINDOW = 32, 64, 16, 512
FORCE_SCORE = 1e4
REL_BUCKETS, REL_MAX_DIST = 32, 128
FOX_HEADS = 16
GDN_HEADS, GDN_HSZ, GDN_CONV = 8, 128, 4
MOE_GROUPS, MOE_EPG, MOE_BLK = 4, 8, 256
MOE_EXPERTS = MOE_GROUPS * MOE_EPG
ATT_T = 128
SEL_NEG = -65536.0
PAGES_PER_STEP = 4


def _d(a, b):
    return jnp.dot(a.astype(BF16), b.astype(BF16), preferred_element_type=F32)


def _d_nt(a, b):
    return lax.dot_general(a.astype(BF16), b.astype(BF16), (((1,), (1,)), ((), ())),
                           preferred_element_type=F32)


def _split3(x):
    h = x.astype(BF16)
    r1 = x - h.astype(F32)
    m = r1.astype(BF16)
    l = (r1 - m.astype(F32)).astype(BF16)
    return h, m, l


def _d_x3(x, sel):
    h, m, l = _split3(x)
    return _d(h, sel) + _d(m, sel) + _d(l, sel)


def _d_3x(sel, x):
    h, m, l = _split3(x)
    return _d(sel, h) + _d(sel, m) + _d(sel, l)


def _d_f32(x, w):
    xh, xm, xl = _split3(x)
    wh, wm, wl = _split3(w)
    return (_d(xh, wh) + _d(xh, wm) + _d(xm, wh)) + (_d(xh, wl) + _d(xl, wh) + _d(xm, wm))


def _sigmoid(x):
    return 1.0 / (1.0 + jnp.exp(-x))


def _softplus(x):
    return jnp.maximum(x, 0.0) + jnp.log(1.0 + jnp.exp(-jnp.abs(x)))


def _silu(x):
    return x * _sigmoid(x)


def _gelu_tanh(x):
    return 0.5 * x * (1.0 + jnp.tanh(math.sqrt(2.0 / math.pi) * (x + 0.044715 * (x * x * x))))


def _layer_norm(z, g, b):
    mu = jnp.mean(z, axis=-1, keepdims=True)
    zc = z - mu
    var = jnp.mean(zc * zc, axis=-1, keepdims=True)
    return zc * lax.rsqrt(var + LN_EPS) * g + b


def _modulate(x, sc, sh):
    return x * (1.0 + sc) + sh


def _res_ln(alpha, y, xres, gate, g, b):
    return _layer_norm(alpha * xres + (1.0 + gate) * y, g, b)


@functools.lru_cache(maxsize=None)
def _seg_np(hsz):
    head = np.arange(D) // hsz
    hs = (head[:, None] == np.arange(LANES)[None, :]).astype(np.float32)
    return hs


def _seg_consts(hsz):
    hs = _seg_np(hsz)
    return jnp.asarray(hs, BF16), jnp.asarray(hs.T, BF16)


def _scan_consts(hsz):
    nh = D // hsz
    head = np.arange(D) // hsz
    slot_j = np.arange(LANES) // 16
    slot_h = np.arange(LANES) % 16
    hexp = np.zeros((SUBLANES, LANES, D), np.float32)
    for j in range(SUBLANES):
        hexp[j] = ((slot_j[:, None] == j) & (slot_h[:, None] == head[None, :]) & (slot_h[:, None] < nh))
    hsum = np.transpose(hexp, (0, 2, 1))
    blk = np.arange(MXU_TILE) // hsz
    bd = (blk[:, None] == blk[None, :]).astype(np.float32)
    return jnp.asarray(hexp, BF16), jnp.asarray(hsum, BF16), jnp.asarray(bd, BF16)


def _call(body, grid, ins, outs, scratch=(), name=None, prefetch=None, aliases=None):
    arrays = [a for a, _ in ins]
    in_specs = [s for _, s in ins]
    out_shape = [jax.ShapeDtypeStruct(s, d) for s, d, _ in outs]
    out_specs = [s for _, _, s in outs]
    params = pltpu.CompilerParams(dimension_semantics=("arbitrary",) * len(grid),
                                  vmem_limit_bytes=VMEM_LIMIT_MB << 20)
    kw = {}
    if aliases:
        kw["input_output_aliases"] = aliases
    if prefetch is None:
        fn = pl.pallas_call(body, grid=grid, in_specs=in_specs, out_specs=out_specs, out_shape=out_shape,
                            scratch_shapes=list(scratch), compiler_params=params, name=name, **kw)
        res = fn(*arrays)
    else:
        gs = pltpu.PrefetchScalarGridSpec(num_scalar_prefetch=len(prefetch), grid=grid, in_specs=in_specs,
                                          out_specs=out_specs, scratch_shapes=list(scratch))
        fn = pl.pallas_call(body, grid_spec=gs, out_shape=out_shape, compiler_params=params, name=name, **kw)
        res = fn(*prefetch, *arrays)
    return list(res)


def _full(a):
    nd = a.ndim
    return (a, pl.BlockSpec(a.shape, lambda *_: (0,) * nd))


def _rows(a, tm):
    return (a, pl.BlockSpec((tm, a.shape[1]), lambda i, *_: (i, 0)))


def _rows_out(M, C, tm, dtype=F32):
    return ((M, C), dtype, pl.BlockSpec((tm, C), lambda i, *_: (i, 0)))


class _Stream:
    def __init__(self, x, B, T, tm):
        self.x, self.B, self.T, self.tm = x, B, T, tm
        self.M = B * T
        self.m6 = None

    def set_mods(self, m6):
        self.m6 = m6
        self.rep = jnp.repeat(m6, self.T, axis=0) if self.T < self.tm else None

    def mod(self, c, tm=None):
        tm = tm or self.tm
        if self.T % tm == 0:
            tpb = self.T // tm
            a = self.m6[:, c * D:(c + 1) * D].reshape(self.B, 1, D)
            return (a, pl.BlockSpec((None, 1, D), lambda i, *_: (i // tpb, 0, 0)))
        assert tm % self.T == 0 and self.M % tm == 0
        a = self.rep[:, c * D:(c + 1) * D].reshape(self.M // tm, tm, D)
        return (a, pl.BlockSpec((None, tm, D), lambda i, *_: (i, 0, 0)))


def _mm(x, w_in, *, tm, pro=None, pro_ins=(), epi=None, epi_ins=(), outs=None, name="mm"):
    M, K = x.shape
    n_pro, n_epi = len(pro_ins), len(epi_ins)
    n_out = len(outs)

    def body(*refs):
        x_ref = refs[0]
        pro_refs = refs[1:1 + n_pro]
        w_ref = refs[1 + n_pro]
        epi_refs = refs[2 + n_pro:2 + n_pro + n_epi]
        out_refs = refs[2 + n_pro + n_epi:2 + n_pro + n_epi + n_out]
        a = x_ref[...]
        if pro is not None:
            a = pro(a, *[r[...] for r in pro_refs])
        acc = _d(a, w_ref[...])
        res = epi(acc, *[r[...] for r in epi_refs]) if epi is not None else (acc,)
        for o, r in zip(out_refs, res):
            o[...] = r.astype(o.dtype)

    ins = [_rows(x, tm)] + list(pro_ins) + [w_in] + list(epi_ins)
    return _call(body, (M // tm,), ins, outs, name=name)


def _ada(c_all, w_mod, b_mod, layer):
    Mp = c_all.shape[0]
    N = w_mod.shape[2]
    tn = 1536

    def body(c_ref, w_ref, b_ref, o_ref):
        o_ref[...] = _d(_silu(c_ref[...]), w_ref[...]) + b_ref[...]

    ins = [(c_all, pl.BlockSpec((Mp, D), lambda j: (0, 0))),
           (w_mod, pl.BlockSpec((None, D, tn), lambda j: (layer, 0, j))),
           (b_mod.reshape(b_mod.shape[0], 1, N), pl.BlockSpec((None, 1, tn), lambda j: (layer, 0, j)))]
    outs = [((Mp, N), F32, pl.BlockSpec((Mp, tn), lambda j: (0, j)))]
    return _call(body, (N // tn,), ins, outs, name="ada_mod")[0]


def _pick_tile(*sizes, cap=512):
    t = cap
    while t > SUBLANES and any(s % t for s in sizes):
        t //= 2
    assert all(s % t == 0 for s in sizes), sizes
    return t


def _route(lg):
    lane = lax.broadcasted_iota(I32, lg.shape, 1)
    big = jnp.int32(1 << 20)
    isg = lane < MOE_GROUPS
    gl = jnp.where(isg, lg, NEG)
    gmax = jnp.max(gl, axis=-1, keepdims=True)
    gsel = jnp.min(jnp.where(gl == gmax, lane, big), axis=-1, keepdims=True)
    gsum = jnp.sum(jnp.where(isg, jnp.exp(gl - gmax), 0.0), axis=-1, keepdims=True)
    gw = 1.0 / gsum
    lo = MOE_GROUPS + MOE_EPG * gsel
    ise = (lane >= lo) & (lane < lo + MOE_EPG)
    el = jnp.where(ise, lg, NEG)
    emax = jnp.max(el, axis=-1, keepdims=True)
    ep = jnp.where(ise, jnp.exp(el - emax), 0.0)
    prob = ep / jnp.sum(ep, axis=-1, keepdims=True)
    pm = jnp.where(ise, prob, -1.0)
    p1 = jnp.max(pm, axis=-1, keepdims=True)
    i1 = jnp.min(jnp.where(pm == p1, lane, big), axis=-1, keepdims=True)
    pm2 = jnp.where(lane == i1, -1.0, pm)
    p2 = jnp.max(pm2, axis=-1, keepdims=True)
    i2 = jnp.min(jnp.where(pm2 == p2, lane, big), axis=-1, keepdims=True)
    den = p1 + p2
    w1 = gw * p1 / den
    w2 = gw * p2 / den
    e1 = (i1 - MOE_GROUPS).astype(F32)
    e2 = (i2 - MOE_GROUPS).astype(F32)
    return jnp.where(lane == 0, e1, jnp.where(lane == 1, e2, jnp.where(lane == 2, w1, jnp.where(lane == 3, w2, 0.0))))


NSEG = D // LANES


def _to_tiles(ref, x):
    for s in range(NSEG):
        ref[:, s, :] = x[:, s * LANES:(s + 1) * LANES]


def _from_tiles(ref):
    return jnp.concatenate([ref[:, s, :] for s in range(NSEG)], axis=1)


def _moe_router(st, wgr, bgr, h_all, off):
    tm = _pick_tile(st.M, off, cap=st.tm)
    b0 = off // tm

    def body(x_ref, sc_ref, sh_ref, w_ref, b_ref, hin_ref, h_ref, r_ref):
        h = _modulate(x_ref[...], sc_ref[...], sh_ref[...])
        _to_tiles(h_ref, h)
        r_ref[...] = _route(_d_f32(h, w_ref[...]) + b_ref[...])

    ins = [_rows(st.x, tm), st.mod(4, tm), st.mod(3, tm), _full(wgr), _full(bgr),
           (h_all, pl.BlockSpec(memory_space=pl.ANY))]
    outs = [(h_all.shape, F32, pl.BlockSpec((tm, NSEG, LANES), lambda i: (b0 + i, 0, 0))),
            _rows_out(st.M, LANES, tm)]
    return _call(body, (st.M // tm,), ins, outs, aliases={5: 0}, name="moe_router")


def _moe_counts(rinfo, R):
    Mtot = rinfo.shape[0]
    nt = Mtot // R

    def body(r_ref, o_ref):
        j = pl.program_id(0)
        t = pl.program_id(1)

        @pl.when((j == 0) & (t == 0))
        def _():
            o_ref[...] = jnp.zeros_like(o_ref)

        xt = r_ref[...].T
        row = jnp.where(j == 0, xt[0:1, :], xt[1:2, :])
        sub = lax.broadcasted_iota(I32, (LANES, R), 0).astype(F32)
        oh = jnp.where(sub == row, 1.0, 0.0)
        o_ref[...] += jnp.sum(oh, axis=1, keepdims=True)

    ins = [(rinfo, pl.BlockSpec((R, LANES), lambda j, t: (t, 0)))]
    outs = [((LANES, LANES), F32, pl.BlockSpec((LANES, LANES), lambda j, t: (0, 0)))]
    return _call(body, (2, nt), ins, outs, name="moe_counts")[0]


def _moe_dest(rinfo, pstart, R):
    Mtot = rinfo.shape[0]
    nt = Mtot // R
    upper = jnp.asarray(np.triu(np.ones((R, R), np.float32), 1), BF16)

    def body(r_ref, p_ref, u_ref, o_ref, carry):
        j = pl.program_id(0)
        t = pl.program_id(1)

        @pl.when((j == 0) & (t == 0))
        def _():
            carry[...] = jnp.zeros_like(carry)

        xt = r_ref[...].T
        row = jnp.where(j == 0, xt[0:1, :], xt[1:2, :])
        sub = lax.broadcasted_iota(I32, (LANES, R), 0).astype(F32)
        oh = jnp.where(sub == row, 1.0, 0.0)
        cum = _d(oh, u_ref[...])
        base = carry[:, 0:1] + p_ref[:, 0:1]
        dest = jnp.sum(oh * (cum + base), axis=0, keepdims=True)
        o_ref[...] = dest.astype(I32)
        carry[...] += jnp.sum(oh, axis=1, keepdims=True)

    ins = [(rinfo, pl.BlockSpec((R, LANES), lambda j, t: (t, 0))), _full(pstart), _full(upper)]
    outs = [((2 * nt, 1, R), I32, pl.BlockSpec((None, 1, R), lambda j, t: (j * nt + t, 0, 0)))]
    return _call(body, (2, nt), ins, outs, scratch=[pltpu.VMEM((LANES, LANES), F32)], name="moe_dest")[0]


def _moe_ffn(h_all, slots, blk_expert, nvalid, w1, w3, w2, layer, Mtot):
    nblk = slots.shape[0] // MOE_BLK
    FF = w1.shape[-1]
    any_spec = pl.BlockSpec(memory_space=pl.ANY)
    GRP = SUBLANES

    def body(be_ref, nv_ref, slot_ref, h_ref, w1_ref, w3_ref, w2_ref, y_ref, xbuf, ybuf, sem_in, sem_out):
        i = pl.program_id(0)
        nv = nv_ref[0]
        buf = i % 2

        def row_copy(kind, blk, b, r):
            s = slot_ref[blk * MOE_BLK + r]
            if kind == "gather":
                tok = jnp.where(s >= 2 * Mtot, s - 2 * Mtot, jnp.where(s >= Mtot, s - Mtot, s))
                return pltpu.make_async_copy(h_ref.at[tok], xbuf.at[b, r], sem_in.at[b])
            return pltpu.make_async_copy(ybuf.at[b, r], y_ref.at[s], sem_out.at[b])

        def each_row(kind, blk, b, start):
            def group(r8, c):
                for u in range(GRP):
                    cp = row_copy(kind, blk, b, r8 * GRP + u)
                    cp.start() if start else cp.wait()
                return c
            lax.fori_loop(0, MOE_BLK // GRP, group, 0)

        @pl.when(i < nv)
        def _():
            @pl.when(i == 0)
            def _():
                each_row("gather", 0, 0, True)

            @pl.when(i + 1 < nv)
            def _():
                each_row("gather", i + 1, 1 - buf, True)

            each_row("gather", i, buf, False)
            x = _from_tiles(xbuf.at[buf]).astype(BF16)
            a = _d(x, w1_ref[...])
            b = _d(x, w3_ref[...])
            y = _d(_silu(a) * b, w2_ref[...])

            @pl.when(i >= 2)
            def _():
                each_row("scatter", i - 2, buf, False)

            _to_tiles(ybuf.at[buf], y)
            each_row("scatter", i, buf, True)

            @pl.when(i == nv - 1)
            def _():
                @pl.when(i >= 1)
                def _():
                    each_row("scatter", i - 1, 1 - buf, False)

                each_row("scatter", i, buf, False)

    def blk(i, be, nv, sl):
        return be[jnp.minimum(i, nv[0] - 1)]

    ins = [(h_all, any_spec),
           (w1, pl.BlockSpec((None, None, D, FF), lambda i, be, nv, sl: (layer, blk(i, be, nv, sl), 0, 0))),
           (w3, pl.BlockSpec((None, None, D, FF), lambda i, be, nv, sl: (layer, blk(i, be, nv, sl), 0, 0))),
           (w2, pl.BlockSpec((None, None, FF, D), lambda i, be, nv, sl: (layer, blk(i, be, nv, sl), 0, 0)))]
    outs = [((2 * Mtot + 2 * MOE_BLK, NSEG, LANES), F32, any_spec)]
    scratch = [pltpu.VMEM((2, MOE_BLK, NSEG, LANES), F32), pltpu.VMEM((2, MOE_BLK, NSEG, LANES), F32),
               pltpu.SemaphoreType.DMA((2,)), pltpu.SemaphoreType.DMA((2,))]
    return _call(body, (nblk,), ins, outs, scratch=scratch, prefetch=[blk_expert, nvalid, slots], name="moe_ffn")[0]


def _moe_combine(st, yslot, rinfo_all, off, Mtot, alpha, ln_g, ln_b):
    tm = _pick_tile(st.M, off, Mtot, cap=st.tm)
    b0, b1, br = off // tm, (Mtot + off) // tm, off // tm

    def body(y0_ref, y1_ref, r_ref, x_ref, gate_ref, g_ref, b_ref, o_ref):
        r = r_ref[...]
        y = r[:, 2:3] * _from_tiles(y0_ref) + r[:, 3:4] * _from_tiles(y1_ref)
        o_ref[...] = _res_ln(alpha, y, x_ref[...], gate_ref[...], g_ref[...], b_ref[...])

    ins = [(yslot, pl.BlockSpec((tm, NSEG, LANES), lambda i: (b0 + i, 0, 0))),
           (yslot, pl.BlockSpec((tm, NSEG, LANES), lambda i: (b1 + i, 0, 0))),
           (rinfo_all, pl.BlockSpec((tm, LANES), lambda i: (br + i, 0))),
           _rows(st.x, tm), st.mod(5, tm), _full(ln_g), _full(ln_b)]
    return _call(body, (st.M // tm,), ins, [_rows_out(st.M, D, tm)], name="moe_combine")[0]


def _moe_layer(streams, layer, alpha, p):
    wgr = jnp.zeros((D, LANES), F32).at[:, :MOE_GROUPS].set(p["moe_w_group"][layer])
    wgr = wgr.at[:, MOE_GROUPS:MOE_GROUPS + MOE_EXPERTS].set(p["moe_w_router"][layer])
    bgr = jnp.zeros((1, LANES), F32).at[0, :MOE_GROUPS].set(p["moe_b_group"][layer])
    bgr = bgr.at[0, MOE_GROUPS:MOE_GROUPS + MOE_EXPERTS].set(p["moe_b_router"][layer])
    Mtot = sum(st.M for st in streams)
    h_all = jnp.zeros((Mtot, NSEG, LANES), F32)
    rs, off = [], 0
    for st in streams:
        h_all, r = _moe_router(st, wgr, bgr, h_all, off)
        rs.append(r)
        off += st.M
    rinfo = jnp.concatenate(rs, axis=0)
    R = _pick_tile(Mtot)
    counts = _moe_counts(rinfo, R)[:MOE_EXPERTS, 0].astype(I32)
    padded = (counts + MOE_BLK - 1) // MOE_BLK * MOE_BLK
    pad_end = jnp.cumsum(padded)
    pstart = jnp.zeros((LANES,), F32).at[:MOE_EXPERTS].set((pad_end - padded).astype(F32))
    pstart = jnp.broadcast_to(pstart[:, None], (LANES, LANES))
    nblk = -(-2 * Mtot // MOE_BLK) + MOE_EXPERTS
    blk_first = jnp.arange(nblk, dtype=I32) * MOE_BLK
    blk_expert = jnp.minimum(jnp.sum((pad_end[None, :] <= blk_first[:, None]).astype(I32), axis=1), MOE_EXPERTS - 1)
    nvalid = (pad_end[-1:] // MOE_BLK).astype(I32)
    dest = _moe_dest(rinfo, pstart, R).reshape(-1)
    spare = 2 * Mtot + jnp.arange(nblk * MOE_BLK, dtype=I32) % (2 * MOE_BLK)
    slots = spare.at[dest].set(jnp.arange(2 * Mtot, dtype=I32))
    yslot = _moe_ffn(h_all, slots, blk_expert, nvalid, p["moe_w1"], p["moe_w3"], p["moe_w2"], layer, Mtot)
    ln_g = p["ln_g"][layer, 1].reshape(1, D)
    ln_b = p["ln_b"][layer, 1].reshape(1, D)
    off = 0
    for st in streams:
        st.x = _moe_combine(st, yslot, rinfo, off, Mtot, alpha, ln_g, ln_b)
        off += st.M


def _scan_body(nbg, nv, tb, w_ref, kkn_ref, b_ref, k_ref, r_ref, vt_ref, s0_ref, hexp_ref, hsum_ref, bd_ref,
               o_ref, sf_ref, s_scr):
    t = pl.program_id(1)

    @pl.when(t == 0)
    def _():
        s_scr[...] = s0_ref[...]

    def sub(sb, carry):
        base = pl.multiple_of(sb * SUBLANES, SUBLANES)
        rows = [[ref[bb, pl.ds(base, SUBLANES), :] for bb in range(nbg)]
                for ref in (w_ref, kkn_ref, b_ref, k_ref, r_ref)]
        vt = vt_ref[:, sb].reshape(nbg * nv, LANES).astype(BF16)
        oacc = jnp.zeros((nbg * nv, LANES), F32)
        for j in range(SUBLANES):
            S = [s_scr[bb] for bb in range(nbg)]
            P = jnp.concatenate([S[bb] * rows[1][bb][j:j + 1] for bb in range(nbg)], axis=0).astype(BF16)
            sa = jnp.concatenate([_d(P[:, c0:c0 + MXU_TILE], bd_ref[...]) for c0 in range(0, D, MXU_TILE)], axis=1)
            vb = _d(vt, hexp_ref[j])
            P2 = []
            for bb in range(nbg):
                sl = slice(bb * nv, (bb + 1) * nv)
                Sn = S[bb] * rows[0][bb][j:j + 1] + sa[sl] * rows[2][bb][j:j + 1] + vb[sl] * rows[3][bb][j:j + 1]
                s_scr[bb] = Sn
                P2.append(Sn * rows[4][bb][j:j + 1])
            oacc = oacc + _d(jnp.concatenate(P2, axis=0), hsum_ref[j])
        o_ref[:, sb] = oacc.reshape(nbg, nv, LANES)
        return carry

    lax.fori_loop(0, tb // SUBLANES, sub, 0)

    @pl.when(t == pl.num_programs(1) - 1)
    def _():
        sf_ref[...] = s_scr[...]


def _delta_scan(w, kkn, b, k, r, v, S0, B, T, hsz):
    nh = D // hsz
    nv = hsz
    nbg = 4 if B % 4 == 0 else (2 if B % 2 == 0 else 1)
    tb = min(64, T)
    hexp, hsum, bd = _scan_consts(hsz)
    vt = v.reshape(B, T // SUBLANES, SUBLANES, nh, nv).transpose(0, 1, 4, 2, 3)
    vt = jnp.pad(vt, ((0, 0),) * 4 + ((0, 16 - nh),)).reshape(B, T // SUBLANES, nv, LANES)
    seq = lambda a: (a.reshape(B, T, D), pl.BlockSpec((nbg, tb, D), lambda g, t: (g, t, 0)))
    ins = [seq(w), seq(kkn), seq(b), seq(k), seq(r),
           (vt, pl.BlockSpec((nbg, tb // SUBLANES, nv, LANES), lambda g, t: (g, t, 0, 0))),
           (S0, pl.BlockSpec((nbg, nv, D), lambda g, t: (g, 0, 0))),
           _full(hexp), _full(hsum), _full(bd)]
    outs = [((B, T // SUBLANES, nv, LANES), F32,
             pl.BlockSpec((nbg, tb // SUBLANES, nv, LANES), lambda g, t: (g, t, 0, 0))),
            ((B, nv, D), F32, pl.BlockSpec((nbg, nv, D), lambda g, t: (g, 0, 0)))]
    body = functools.partial(_scan_body, nbg, nv, tb)
    op, sf = _call(body, (B // nbg, T // tb), ins, outs, scratch=[pltpu.VMEM((nbg, nv, D), F32)], name="delta_scan")
    o = op.reshape(B, T // SUBLANES, nv, SUBLANES, 16)[..., :nh].transpose(0, 1, 3, 4, 2).reshape(B * T, D)
    return o, sf


def _shifted_rows(h, first, period, shift=1):
    row = lax.broadcasted_iota(I32, h.shape, 0)
    return jnp.where(row % period < shift, first, pltpu.roll(h, shift, axis=0))


def _rwkv_prep(st, shift_prev, p, tm):
    long_seq = st.T % tm == 0
    tpb = st.T // tm if long_seq else 1
    hs, he = _seg_consts(RWKV_HSZ)
    row = lambda a: _full(a.reshape(1, D))
    wts = [_full(p["rwkv_mu"]), _full(p["rwkv_w_rkv"].astype(BF16)),
           _full(p["rwkv_w1"].astype(BF16)), _full(p["rwkv_w2"].astype(BF16)),
           _full(p["rwkv_a1"].astype(BF16)), _full(p["rwkv_a2"].astype(BF16)),
           _full(p["rwkv_g1"].astype(BF16)), _full(p["rwkv_g2"].astype(BF16)),
           row(p["rwkv_w0"]), row(p["rwkv_a0"]), row(p["rwkv_k_k"]), row(p["rwkv_k_a"]), _full(hs), _full(he)]
    if long_seq:
        nsub = tm // SUBLANES
        first_ins = [(st.x, pl.BlockSpec((SUBLANES, D), lambda i: (jnp.maximum(i * nsub - 1, 0), 0))),
                     (shift_prev.reshape(st.B, 1, D), pl.BlockSpec((None, 1, D), lambda i: (i // tpb, 0, 0)))]
    else:
        first_ins = [_rows(jnp.repeat(shift_prev, st.T, axis=0), tm)]
    nf = len(first_ins)

    def body(x_ref, sc_ref, sh_ref, *refs):
        first_refs, refs = refs[:nf], refs[nf:]
        (mu_ref, wrkv_ref, w1_ref, w2_ref, a1_ref, a2_ref, g1_ref, g2_ref, w0_ref, a0_ref, kk_ref, ka_ref,
         hs_ref, he_ref) = refs[:14]
        h_ref, r_ref, w_ref, k_ref, v_ref, kkn_ref, b_ref, g_ref = refs[14:]
        sc, sh = sc_ref[...], sh_ref[...]
        h = _modulate(x_ref[...], sc, sh)
        if long_seq:
            hh = _modulate(first_refs[0][...], sc, sh)[SUBLANES - 1:SUBLANES]
            first = jnp.where(pl.program_id(0) % tpb == 0, first_refs[1][...], hh)
            hprev = _shifted_rows(h, first, tm)
        else:
            hprev = _shifted_rows(h, first_refs[0][...], st.T)
        xx = hprev - h
        mu = mu_ref[...]
        xr, xw, xk, xv, xa, xg = [h + xx * mu[i:i + 1] for i in range(6)]
        r = _d(xr, wrkv_ref[0])
        k = _d(xk, wrkv_ref[1])
        v = _d(xv, wrkv_ref[2])
        logw = -_softplus(-(w0_ref[...] + _d(jnp.tanh(_d(xw, w1_ref[...])), w2_ref[...]))) - 0.5
        a = _sigmoid(a0_ref[...] + _d(_d(xa, a1_ref[...]), a2_ref[...]))
        g = _d(_sigmoid(_d(xg, g1_ref[...])), g2_ref[...])
        kk = k * kk_ref[...]
        inv = lax.rsqrt(_d_x3(kk * kk, hs_ref[...]) + 1e-6)
        kk = kk * _d_x3(inv, he_ref[...])
        h_ref[...] = h
        r_ref[...] = r
        w_ref[...] = jnp.exp(-jnp.exp(logw))
        k_ref[...] = k * (1.0 + (a - 1.0) * ka_ref[...])
        v_ref[...] = v
        kkn_ref[...] = -kk
        b_ref[...] = kk * a
        g_ref[...] = g

    ins = [_rows(st.x, tm), st.mod(1, tm), st.mod(0, tm)] + first_ins + wts
    outs = [_rows_out(st.M, D, tm) for _ in range(8)]
    return _call(body, (st.M // tm,), ins, outs, name="rwkv_prep")


def _rwkv_out(st, o, r, kmod, v, g, p, alpha, ln_g, ln_b, tm):
    hs, he = _seg_consts(RWKV_HSZ)
    inv_n = 1.0 / RWKV_HSZ

    def pro(o, r, k, v, g, lw, lb, rk, hs, he):
        mean = _d_x3(_d_x3(o, hs) * inv_n, he)
        c = o - mean
        rstd = lax.rsqrt(_d_x3(c * c, hs) * inv_n + RWKV_GN_EPS)
        on = c * _d_x3(rstd, he) * lw + lb
        bonus = _d_x3(_d_x3(r * k * rk, hs), he) * v
        return (on + bonus) * g

    def epi(acc, x, gate, g_, b_):
        return (_res_ln(alpha, acc, x, gate, g_, b_),)

    row = lambda a: _full(a.reshape(1, D))
    pro_ins = [_rows(a, tm) for a in (r, kmod, v, g)] + [row(p["rwkv_ln_w"]), row(p["rwkv_ln_b"]),
                                                       row(p["rwkv_r_k"]), _full(hs), _full(he)]
    epi_ins = [_rows(st.x, tm), st.mod(2, tm), _full(ln_g), _full(ln_b)]
    return _mm(o, _full(p["rwkv_w_o"].astype(BF16)), tm=tm, pro=pro, pro_ins=pro_ins, epi=epi, epi_ins=epi_ins,
               outs=[_rows_out(st.M, D, tm)], name="rwkv_out")[0]


def _rwkv_layer(st, shift_prev, wkv0, p, alpha, ln_g, ln_b):
    B, T = st.B, st.T
    tm = min(256, st.M)
    nh = D // RWKV_HSZ
    h, r, w, kmod, v, kkn, b, g = _rwkv_prep(st, shift_prev, p, tm)
    S0 = wkv0.transpose(0, 2, 1, 3).reshape(B, RWKV_HSZ, D)
    o, sf = _delta_scan(w, kkn, b, kmod, r, v, S0, B, T, RWKV_HSZ)
    st.x = _rwkv_out(st, o, r, kmod, v, g, p, alpha, ln_g, ln_b, tm)
    wkv = sf.reshape(B, RWKV_HSZ, nh, RWKV_HSZ).transpose(0, 2, 1, 3)
    return wkv, h.reshape(B, T, D)[:, -1]


def _pad_cols(w, n):
    return jnp.pad(w, ((0, 0), (0, n - w.shape[1])))


def _gdn_proj(st, p, tm):
    C = 3 * D
    w = _pad_cols(p["gdn_w_in"], C + D + LANES).astype(BF16)

    def epi(acc):
        return acc[:, :C], acc[:, C:C + D], acc[:, C + D:]

    outs = [_rows_out(st.M, C, tm), _rows_out(st.M, D, tm), _rows_out(st.M, LANES, tm)]
    return _mm(st.x, _full(w), tm=tm, pro=_modulate, pro_ins=[st.mod(1, tm), st.mod(0, tm)], epi=epi, outs=outs,
               name="gdn_proj")


def _gdn_conv(st, pre, ba, conv_buf, p, tm, chunked):
    C = 3 * D
    H = GDN_HEADS
    long_seq = st.T % tm == 0
    tpb = st.T // tm if long_seq else 1
    hs, he = _seg_consts(GDN_HSZ)
    hsn = _seg_np(GDN_HSZ)
    he_b = jnp.asarray(hsn.T, BF16)
    he_a = jnp.asarray(np.roll(hsn.T, H, axis=0), BF16)
    alog = jnp.zeros((1, LANES), F32).at[0, H:2 * H].set(p["gdn_A_log"])
    dtb = jnp.zeros((1, LANES), F32).at[0, H:2 * H].set(p["gdn_dt_bias"])
    if long_seq:
        nsub = tm // SUBLANES
        init8 = jnp.pad(conv_buf, ((0, 0), (SUBLANES - (GDN_CONV - 1), 0), (0, 0)))
        first_ins = [(pre, pl.BlockSpec((SUBLANES, C), lambda i: (jnp.maximum(i * nsub - 1, 0), 0))),
                     (init8, pl.BlockSpec((None, SUBLANES, C), lambda i: (i // tpb, 0, 0)))]
    else:
        padded = jnp.pad(conv_buf, ((0, 0), (0, st.T), (0, 0)))
        first_ins = [_rows(padded[:, GDN_CONV - 1 - j:GDN_CONV - 1 - j + st.T].reshape(st.M, C), tm)
                     for j in range(1, GDN_CONV)]
    nf = len(first_ins)

    def body(pre_ref, ba_ref, *refs):
        first_refs, refs = refs[:nf], refs[nf:]
        cw_ref, alog_ref, dtb_ref, hs_ref, he_ref, heb_ref, hea_ref = refs[:7]
        w_ref, kkn_ref, k_ref, q_ref, v_ref = refs[7:]
        x = pre_ref[...]
        if long_seq:
            halo = jnp.where(pl.program_id(0) % tpb == 0, first_refs[1][...], first_refs[0][...])
            big = jnp.concatenate([halo, x], axis=0)
            sh = [pltpu.roll(big, j, axis=0)[SUBLANES:] for j in range(1, GDN_CONV)]
        else:
            sh = [_shifted_rows(x, first_refs[j - 1][...], st.T, j) for j in range(1, GDN_CONV)]
        cw = cw_ref[...]
        conv = sh[2] * cw[0:1]
        conv = conv + sh[1] * cw[1:2]
        conv = conv + sh[0] * cw[2:3]
        conv = conv + x * cw[3:4]
        c = _silu(conv)
        q, k, v = c[:, :D], c[:, D:2 * D], c[:, 2 * D:]
        qn = q * _d_x3(lax.rsqrt(_d_x3(q * q, hs_ref[...]) + 1e-6), he_ref[...]) * (GDN_HSZ ** -0.5)
        kn = k * _d_x3(lax.rsqrt(_d_x3(k * k, hs_ref[...]) + 1e-6), he_ref[...])
        ba = ba_ref[...]
        logdecay = -jnp.exp(alog_ref[...]) * _softplus(ba + dtb_ref[...])
        if chunked:
            w_ref[...] = _sigmoid(ba)
            kkn_ref[...] = logdecay
            k_ref[...] = kn
            q_ref[...] = qn
            v_ref[...] = v
        else:
            beta = _d_x3(_sigmoid(ba), heb_ref[...])
            a = _d_x3(jnp.exp(logdecay), hea_ref[...])
            w_ref[...] = a
            kkn_ref[...] = -(a * beta) * kn
            k_ref[...] = kn
            q_ref[...] = qn
            v_ref[...] = beta * v

    ins = [_rows(pre, tm), _rows(ba, tm)] + first_ins + [_full(p["gdn_conv_w"]), _full(alog), _full(dtb), _full(hs),
                                                         _full(he), _full(he_b), _full(he_a)]
    small = LANES if chunked else D
    outs = [_rows_out(st.M, small, tm), _rows_out(st.M, small, tm)] + [_rows_out(st.M, D, tm) for _ in range(3)]
    return _call(body, (st.M // tm,), ins, outs, name="gdn_conv")


GDN_CHUNK = 64


def _gdn_chunk_scan(q, k, v, beta, g, S0, B, T):
    C = GDN_CHUNK
    H, N = GDN_HEADS, GDN_HSZ
    nchunk = T // C
    tril = _lower_tri(C)
    triu = jnp.asarray(np.triu(np.ones((C, C), np.float32)), BF16)

    def body(q_ref, k_ref, v_ref, b_ref, g_ref, s0_ref, tril_ref, triu_ref, o_ref, sf_ref, s_scr):
        c = pl.program_id(1)

        @pl.when(c == 0)
        def _():
            s_scr[...] = s0_ref[...]

        gblk = g_ref[...]
        gc = _d_3x(tril_ref[...], gblk)
        gh, gm, gl = _split3(gblk)
        tn = lambda a: lax.dot_general(a, triu_ref[...], (((0,), (0,)), ((), ())), preferred_element_type=F32)
        gct = tn(gh) + tn(gm) + tn(gl)
        bblk = b_ref[...]
        ri = lax.broadcasted_iota(I32, (C, C), 0)
        ci = lax.broadcasted_iota(I32, (C, C), 1)
        lower, strict = ri >= ci, ri > ci
        eye = jnp.where(ri == ci, 1.0, 0.0)
        heads = range(H)
        sl = [slice(h * N, (h + 1) * N) for h in heads]
        bcol = [bblk[:, h:h + 1] for h in heads]
        gcol = [gc[:, H + h:H + h + 1] for h in heads]
        gamma = [jnp.where(lower, jnp.exp(jnp.minimum(gcol[h] - gct[H + h:H + h + 1, :], 0.0)), 0.0) for h in heads]
        kb = [k_ref[:, sl[h]] * bcol[h] for h in heads]
        pw = [jnp.where(strict, _d_nt(kb[h], k_ref[:, sl[h]]) * gamma[h], 0.0) for h in heads]
        t_inv = [eye - pw[h] for h in heads]
        for _ in range(int(math.log2(C)) - 1):
            pw = [_d(pw[h], pw[h]) for h in heads]
            t_inv = [t_inv[h] + _d(t_inv[h], pw[h]) for h in heads]
        eg = [jnp.exp(gcol[h]) for h in heads]
        u = [_d(t_inv[h], v_ref[:, sl[h]] * bcol[h]) for h in heads]
        w = [_d(t_inv[h], kb[h] * eg[h]) for h in heads]
        qk = [jnp.where(lower, _d_nt(q_ref[:, sl[h]], k_ref[:, sl[h]]) * gamma[h], 0.0) for h in heads]
        v_new = [u[h] - _d(w[h], s_scr[h]) for h in heads]
        for h in heads:
            o_ref[:, sl[h]] = _d(q_ref[:, sl[h]] * eg[h], s_scr[h]) + _d(qk[h], v_new[h])
        for h in heads:
            g_last = gcol[h][C - 1:C]
            kd = (k_ref[:, sl[h]] * jnp.exp(g_last - gcol[h])).astype(BF16)
            s_scr[h] = s_scr[h] * jnp.exp(g_last) + lax.dot_general(kd, v_new[h].astype(BF16), (((0,), (0,)), ((), ())),
                                                                   preferred_element_type=F32)

        @pl.when(c == nchunk - 1)
        def _():
            sf_ref[...] = s_scr[...]

    seq = lambda a, w: (a, pl.BlockSpec((C, w), lambda b, c: (b * nchunk + c, 0)))
    ins = [seq(q, D), seq(k, D), seq(v, D), seq(beta, LANES), seq(g, LANES),
           (S0, pl.BlockSpec((None, H, N, N), lambda b, c: (b, 0, 0, 0))),
           (tril, pl.BlockSpec(tril.shape, lambda b, c: (0, 0))), (triu, pl.BlockSpec(triu.shape, lambda b, c: (0, 0)))]
    outs = [((B * T, D), F32, pl.BlockSpec((C, D), lambda b, c: (b * nchunk + c, 0))),
            ((B, H, N, N), F32, pl.BlockSpec((None, H, N, N), lambda b, c: (b, 0, 0, 0)))]
    return _call(body, (B, nchunk), ins, outs, scratch=[pltpu.VMEM((H, N, N), F32)], name="gdn_chunk_scan")


def _gdn_out(st, o, z, p, alpha, ln_g, ln_b, tm):
    hs, he = _seg_consts(GDN_HSZ)
    nw = jnp.tile(p["gdn_norm_w"], GDN_HEADS).reshape(1, D)

    def pro(o, z, nw, hs, he):
        rstd = lax.rsqrt(_d_x3(o * o, hs) * (1.0 / GDN_HSZ) + 1e-6)
        return o * _d_x3(rstd, he) * nw * _silu(z)

    def epi(acc, x, gate, g_, b_):
        return (_res_ln(alpha, acc, x, gate, g_, b_),)

    return _mm(o, _full(p["gdn_w_o"].astype(BF16)), tm=tm, pro=pro, pro_ins=[_rows(z, tm), _full(nw), _full(hs), _full(he)],
               epi=epi, epi_ins=[_rows(st.x, tm), st.mod(2, tm), _full(ln_g), _full(ln_b)],
               outs=[_rows_out(st.M, D, tm)], name="gdn_out")[0]


def _gdn_layer(st, conv_buf, S0, p, alpha, ln_g, ln_b):
    B, T = st.B, st.T
    tm = min(256, st.M)
    pre, z, ba = _gdn_proj(st, p, tm)
    chunked = T % GDN_CHUNK == 0
    if chunked:
        beta, g, kn, qn, v = _gdn_conv(st, pre, ba, conv_buf, p, tm, True)
        o, S = _gdn_chunk_scan(qn, kn, v, beta, g, S0, B, T)
    else:
        w, kkn, kn, qn, vb = _gdn_conv(st, pre, ba, conv_buf, p, tm, False)
        S0t = S0.transpose(0, 3, 1, 2).reshape(B, GDN_HSZ, D)
        o, sf = _delta_scan(w, kkn, kn, kn, qn, vb, S0t, B, T, GDN_HSZ)
        S = sf.reshape(B, GDN_HSZ, GDN_HEADS, GDN_HSZ).transpose(0, 2, 3, 1)
    st.x = _gdn_out(st, o, z, p, alpha, ln_g, ln_b, tm)
    xpad = jnp.concatenate([conv_buf, pre.reshape(B, T, 3 * D)[:, -(GDN_CONV - 1):]], axis=1)
    return S, xpad[:, -(GDN_CONV - 1):]


def _flash_body(cfg, *refs):
    tq, hq, hk = cfg["tq"], cfg["hq"], cfg["hk"]
    fox, bias, aug, window = cfg["fox"], cfg["bias"], cfg["aug"], cfg["window"]
    tk = tq
    G = hq // hk
    Kc = HD + aug
    R = G * tq
    refs = list(refs)
    q_ref, k_ref, v_ref = refs[:3]
    pos = 3
    if aug:
        mb_ref, e_ref = refs[pos:pos + 2]
        pos += 2
    if bias:
        tz_ref = refs[pos]
        pos += 1
    if fox:
        cq_ref, ck_ref = refs[pos:pos + 2]
        pos += 2
    o_ref, kb, vb, s_scr = refs[pos:pos + 4]
    g = pl.program_id(1)
    qi = pl.program_id(2)

    @pl.when(qi == 0)
    def _():
        vb[...] = v_ref[...].astype(BF16)
        if aug:
            k = k_ref[...]
            kb[...] = jnp.concatenate(
                [jnp.concatenate([k[:, kv * HD:(kv + 1) * HD].astype(BF16), e_ref[...]], axis=1) for kv in range(hk)],
                axis=1)
        else:
            kb[...] = k_ref[...].astype(BF16)

    scale = HD ** -0.5
    q = q_ref[...]
    row_t = lax.broadcasted_iota(I32, (R, tk), 0) % tq
    col_s = lax.broadcasted_iota(I32, (R, tk), 1)
    qs, cqs = [], []
    for kv in range(hk):
        x = jnp.concatenate([q[:, (kv * G + gg) * HD:(kv * G + gg + 1) * HD] for gg in range(G)], axis=0) * scale
        if aug:
            x = jnp.concatenate([x.astype(BF16), jnp.concatenate([mb_ref[kv]] * G, axis=0)], axis=1)
        qs.append(x.astype(BF16))
        if fox:
            lane = lax.broadcasted_iota(I32, (tq, LANES), 1)
            cqs.append(jnp.sum(jnp.where(lane == g * hk + kv, cq_ref[...], 0.0), axis=-1, keepdims=True))

    def logits(kv, c, rel, valid):
        off = pl.multiple_of(c * tk, tk)
        s = _d_nt(qs[kv], kb[pl.ds(off, tk), kv * Kc:(kv + 1) * Kc])
        if fox:
            sub = lax.broadcasted_iota(I32, (FOX_HEADS, tk), 0)
            ck = jnp.sum(jnp.where(sub == g * hk + kv, ck_ref[:, pl.ds(off, tk)], 0.0), axis=0, keepdims=True)
            s = s + cqs[kv] - ck
        if bias and rel in (0, 1):
            s = s + jnp.concatenate([tz_ref[kv * G + gg, rel] for gg in range(G)], axis=0)
        if rel == 0:
            s = jnp.where(row_t >= col_s, s, NEG)
        if rel == 3:
            s = jnp.where(col_s > row_t, s, NEG)
        if valid is not None:
            s = jnp.where(valid, s, NEG)
        return s

    nt = tk // LANES

    def lane_tiles(x):
        return [x[:, j * LANES:(j + 1) * LANES] for j in range(nt)]

    if window is not None:
        nch = window // tk
        static = [(qi, 0, None)] + [(jnp.maximum(qi - dc, 0), 1 if dc == 1 else (3 if dc == nch else 2), qi - dc >= 0)
                                    for dc in range(1, nch + 1)]
        n_far = 0
    elif bias:
        static = [(qi, 0, None), (jnp.maximum(qi - 1, 0), 1, qi >= 1)]
        n_far = jnp.maximum(qi - 1, 0)
    else:
        static = [(qi, 0, None)]
        n_far = qi
    n_static = len(static)
    unroll = cfg["unroll"]

    def far_loop(fn, carry):
        ng = n_far // unroll

        def group(gi, cr):
            for u in range(unroll):
                cr = fn(gi * unroll + u, cr)
            return cr

        carry = lax.fori_loop(0, ng, group, carry)
        return lax.fori_loop(ng * unroll, n_far, fn, carry)

    outs = []
    for kv in range(hk):
        def score(c, rel, valid, slot, m128):
            s = logits(kv, c, rel, valid)
            s_scr[slot] = s
            for t in lane_tiles(s):
                m128 = jnp.maximum(m128, t)
            return m128

        m128 = jnp.full((R, LANES), NEG, F32)
        for slot, (c, rel, valid) in enumerate(static):
            m128 = score(c, rel, valid, slot, m128)
        if window is None:
            m128 = far_loop(lambda c, m: score(c, 2, None, n_static + c, m), m128)
        mrep = jnp.broadcast_to(jnp.max(m128, axis=-1, keepdims=True), (R, LANES))

        def accumulate(c, slot, carry):
            l128, acc = carry
            p = [jnp.exp(t - mrep) for t in lane_tiles(s_scr[slot])]
            for t in p:
                l128 = l128 + t
            off = pl.multiple_of(c * tk, tk)
            pm = jnp.concatenate(p, axis=1) if nt > 1 else p[0]
            return l128, acc + _d(pm, vb[pl.ds(off, tk), kv * HD:(kv + 1) * HD])

        carry = (jnp.zeros((R, LANES), F32), jnp.zeros((R, HD), F32))
        for slot, (c, rel, valid) in enumerate(static):
            carry = accumulate(c, slot, carry)
        if window is None:
            carry = far_loop(lambda c, cr: accumulate(c, n_static + c, cr), carry)
        l128, acc = carry
        o = acc / jnp.sum(l128, axis=-1, keepdims=True)
        outs += [o[gg * tq:(gg + 1) * tq] for gg in range(G)]
    o_ref[...] = jnp.concatenate(outs, axis=1)


def _flash(cfg, B, T, ngroups, q_in, k_in, v_in, extra_ins, M):
    tq, hq, hk = cfg["tq"], cfg["hq"], cfg["hk"]
    Kc = HD + cfg["aug"]
    ins = [q_in, k_in, v_in] + list(extra_ins)
    nq = T // tq
    R = (hq // hk) * tq
    if cfg["window"] is not None:
        nslots = cfg["window"] // tq + 1
    else:
        nslots = nq + (1 if cfg["bias"] else 0)
    outs = [((M, ngroups * hq * HD), F32, pl.BlockSpec((tq, hq * HD), lambda b, g, i: (b * nq + i, g)))]
    scratch = [pltpu.VMEM((T, hk * Kc), BF16), pltpu.VMEM((T, hk * HD), BF16), pltpu.VMEM((nslots, R, tq), F32)]
    return _call(functools.partial(_flash_body, cfg), (B, ngroups, nq), ins, outs, scratch=scratch,
                 name="flash_" + cfg["name"])[0]


def _log_sigmoid(x):
    return -_softplus(-x)


def _fox_proj(st, p, tm):
    hw = FOX_HEADS * HD
    w = _pad_cols(p["fox_w_in"], 3 * hw + LANES).astype(BF16)
    bf = jnp.zeros((1, LANES), F32).at[0, :FOX_HEADS].set(p["fox_b_f"])

    def epi(acc, bf):
        return acc[:, :hw], acc[:, hw:3 * hw], _log_sigmoid(acc[:, 3 * hw:] + bf)

    outs = [_rows_out(st.M, hw, tm), _rows_out(st.M, 2 * hw, tm), _rows_out(st.M, LANES, tm)]
    return _mm(st.x, _full(w), tm=tm, pro=_modulate, pro_ins=[st.mod(1, tm), st.mod(0, tm)], epi=epi,
               epi_ins=[_full(bf)], outs=outs, name="fox_proj")


def _lower_tri(n):
    return jnp.asarray(np.tril(np.ones((n, n), np.float32)), BF16)


def _cumsum_rows(x, B, T):
    ch = _pick_tile(T)
    tri = _lower_tri(ch)

    def body(x_ref, tri_ref, o_ref):
        carry = jnp.zeros((1, LANES), F32)
        for c in range(T // ch):
            cc = _d_3x(tri_ref[...], x_ref[c * ch:(c + 1) * ch, :]) + carry
            o_ref[c * ch:(c + 1) * ch, :] = cc
            carry = cc[ch - 1:ch, :]

    return _call(body, (B,), [_rows(x, T), _full(tri)], [_rows_out(B * T, LANES, T)], name="cumsum_rows")[0]


def _out_proj(st, o, w_o, alpha, ln_g, ln_b, tm, name):
    def epi(acc, x, gate, g_, b_):
        return (_res_ln(alpha, acc, x, gate, g_, b_),)

    return _mm(o, _full(w_o.astype(BF16)), tm=tm, epi=epi,
               epi_ins=[_rows(st.x, tm), st.mod(2, tm), _full(ln_g), _full(ln_b)],
               outs=[_rows_out(st.M, D, tm)], name=name)[0]


def _fox_prompt(st, p, alpha, ln_g, ln_b):
    B, T, M = st.B, st.T, st.M
    tm = min(256, M)
    q, kv, logf = _fox_proj(st, p, tm)
    cum = _cumsum_rows(logf, B, T)
    ckT = cum.reshape(B, T, LANES)[:, :, :FOX_HEADS].transpose(0, 2, 1)
    tq = min(512, T)
    nq = T // tq
    cfg = dict(name="fox", tq=tq, hq=2, hk=2, fox=True, bias=False, aug=0, window=None, unroll=2)
    npair = FOX_HEADS // 2
    q_in = (q, pl.BlockSpec((tq, 2 * HD), lambda b, g, i: (b * nq + i, g)))
    k_in = (kv, pl.BlockSpec((T, 2 * HD), lambda b, g, i: (b, g)))
    v_in = (kv, pl.BlockSpec((T, 2 * HD), lambda b, g, i: (b, npair + g)))
    extra = [(cum, pl.BlockSpec((tq, LANES), lambda b, g, i: (b * nq + i, 0))),
             (ckT, pl.BlockSpec((None, FOX_HEADS, T), lambda b, g, i: (b, 0, 0)))]
    o = _flash(cfg, B, T, npair, q_in, k_in, v_in, extra, M)
    st.x = _out_proj(st, o, p["fox_w_o"], alpha, ln_g, ln_b, tm, "fox_out")
    return kv.reshape(B, T, 2, FOX_HEADS, HD), logf.reshape(B, T, LANES)[:, :, :FOX_HEADS]


def _page_ins(cache, page_shape, npages, first_of_step):
    nd = len(page_shape)
    return [(cache, pl.BlockSpec((None,) + tuple(page_shape),
                                 lambda b, s, pt, j=j: (pt[b, first_of_step(s) + j],) + (0,) * nd))
            for j in range(npages)]


def _fox_cum_sample(logf_new, cache_logf, page_table, Tn):
    B, npg = page_table.shape
    PAGE = cache_logf.shape[1]
    H = cache_logf.shape[2]
    tri = _lower_tri(PAGE)

    def body(pt_ref, *refs):
        pages, new_ref, tri_ref, o_ref = refs[:npg], refs[npg], refs[npg + 1], refs[npg + 2]
        carry = jnp.zeros((1, H), F32)
        for j in range(npg):
            cc = _d_3x(tri_ref[...], pages[j][...]) + carry
            o_ref[j * PAGE:(j + 1) * PAGE, :] = cc
            carry = cc[PAGE - 1:PAGE, :]
        xn = jnp.concatenate([new_ref[...][:, :H], jnp.zeros((PAGE - Tn, H), F32)], axis=0)
        o_ref[npg * PAGE:(npg + 1) * PAGE, :] = _d_3x(tri_ref[...], xn) + carry

    ins = _page_ins(cache_logf, (PAGE, H), npg, lambda s: 0)
    ins += [(logf_new, pl.BlockSpec((Tn, LANES), lambda b, s, pt: (b, 0))),
            (tri, pl.BlockSpec(tri.shape, lambda b, s, pt: (0, 0)))]
    Lp = (npg + 1) * PAGE
    outs = [((B, Lp, H), F32, pl.BlockSpec((None, Lp, H), lambda b, s, pt: (b, 0, 0)))]
    return _call(body, (B, 1), ins, outs, prefetch=[page_table], name="fox_cum_sample")[0]


def _rep_mat(n_rows, n_src, per):
    r = np.arange(n_rows)
    src = r // per if per else r % n_src
    return jnp.asarray((src[:, None] == np.arange(n_src)[None, :]).astype(np.float32), BF16)


def _fox_decode(q, kv_new, cache_kv, page_table, cq, ckT, Tn):
    B, npg = page_table.shape
    PAGE, H = cache_kv.shape[1], FOX_HEADS
    hw = H * HD
    R = H * Tn
    pps = PAGES_PER_STEP if npg % PAGES_PER_STEP == 0 else 1
    nsteps = npg // pps
    cache_t = cache_kv.transpose(0, 2, 3, 4, 1).reshape(cache_kv.shape[0], 2, hw, PAGE)
    rep_t = _rep_mat(R, Tn, 0)
    rep_h = _rep_mat(R, H, Tn)
    scale = HD ** -0.5

    def body(pt_ref, *refs):
        pages = refs[:pps]
        q_ref, new_ref, cq_ref, ck_ref, rt_ref, rh_ref, o_ref, qbd, m_s, l_s, acc = refs[pps:]
        s_id = pl.program_id(1)
        own = lax.broadcasted_iota(I32, (R, hw), 0) // Tn == lax.broadcasted_iota(I32, (R, hw), 1) // HD

        @pl.when(s_id == 0)
        def _():
            qbd[...] = jnp.where(own, _d(rt_ref[...], q_ref[...]) * scale, 0.0).astype(BF16)
            m_s[...] = jnp.full(m_s.shape, NEG, F32)
            l_s[...] = jnp.zeros(l_s.shape, F32)
            acc[...] = jnp.zeros(acc.shape, F32)

        cqv = cq_ref[...]

        def update(s, pv):
            m = m_s[:, 0:1]
            m2 = jnp.maximum(m, jnp.max(s, axis=-1, keepdims=True))
            a = jnp.exp(m - m2)
            pr = jnp.exp(s - m2)
            l_s[...] = jnp.broadcast_to(a * l_s[:, 0:1] + jnp.sum(pr, axis=-1, keepdims=True), l_s.shape)
            m_s[...] = jnp.broadcast_to(m2, m_s.shape)
            acc[...] = a * acc[...] + pv(pr)

        kt = jnp.concatenate([pages[j][0].astype(BF16) for j in range(pps)], axis=1)
        vt = jnp.concatenate([pages[j][1].astype(BF16) for j in range(pps)], axis=1)
        off = pl.multiple_of(s_id * (pps * PAGE), pps * PAGE)
        update(_d(qbd[...], kt) + cqv - _d_3x(rh_ref[...], ck_ref[:, pl.ds(off, pps * PAGE)]),
               lambda pr: _d_nt(pr, vt))

        @pl.when(s_id == nsteps - 1)
        def _():
            new = jnp.concatenate([new_ref[...], jnp.zeros((PAGE - Tn, 2 * hw), F32)], axis=0)
            t_row = lax.broadcasted_iota(I32, (R, PAGE), 0) % Tn
            col = lax.broadcasted_iota(I32, (R, PAGE), 1)
            s_new = _d_nt(qbd[...], new[:, :hw]) + cqv - _d_3x(rh_ref[...], ck_ref[:, pl.ds(npg * PAGE, PAGE)])
            update(jnp.where(col <= t_row, s_new, NEG), lambda pr: _d(pr, new[:, hw:]))
            of = jnp.where(own, acc[...] / l_s[:, 0:1], 0.0)
            out = of[0:Tn]
            for h in range(1, H):
                out = out + of[h * Tn:(h + 1) * Tn]
            o_ref[...] = out

    Lp = ckT.shape[2]
    const = lambda a: (a, pl.BlockSpec(a.shape, lambda b, s, pt: (0,) * a.ndim))
    ins = _page_ins(cache_t, (2, hw, PAGE), pps, lambda s: s * pps)
    ins += [(q, pl.BlockSpec((Tn, hw), lambda b, s, pt: (b, 0))),
            (kv_new, pl.BlockSpec((Tn, 2 * hw), lambda b, s, pt: (b, 0))),
            (cq, pl.BlockSpec((None, R, 1), lambda b, s, pt: (b, 0, 0))),
            (ckT, pl.BlockSpec((None, H, Lp), lambda b, s, pt: (b, 0, 0))),
            const(rep_t), const(rep_h)]
    outs = [((B * Tn, hw), F32, pl.BlockSpec((Tn, hw), lambda b, s, pt: (b, 0)))]
    scratch = [pltpu.VMEM((R, hw), BF16), pltpu.VMEM((R, LANES), F32), pltpu.VMEM((R, LANES), F32),
               pltpu.VMEM((R, hw), F32)]
    return _call(body, (B, nsteps), ins, outs, scratch=scratch, prefetch=[page_table], name="fox_decode")[0]


def _fox_sample(st, cache_kv, cache_logf, page_table, p, alpha, ln_g, ln_b):
    B, Tn, M = st.B, st.T, st.M
    tm = min(256, M)
    npg = page_table.shape[1]
    PAGE = cache_kv.shape[1]
    q, kv, logf = _fox_proj(st, p, tm)
    cum = _fox_cum_sample(logf, cache_logf, page_table, Tn)
    cq = cum[:, npg * PAGE:npg * PAGE + Tn].transpose(0, 2, 1).reshape(B, FOX_HEADS * Tn, 1)
    ckT = cum.transpose(0, 2, 1)
    o = _fox_decode(q, kv, cache_kv, page_table, cq, ckT, Tn)
    st.x = _out_proj(st, o, p["fox_w_o"], alpha, ln_g, ln_b, tm, "fox_out")
    return kv.reshape(B, Tn, 2, FOX_HEADS, HD), logf.reshape(B, Tn, LANES)[:, :, :FOX_HEADS]


KVW = NSA_KVH * HD
HALF = CMP_BLK // 2


def _t5_bucket(dist):
    exact = REL_BUCKETS // 2
    d = jnp.maximum(dist, 0)
    far = exact + (jnp.log(jnp.maximum(d, 1).astype(F32) / exact) / math.log(REL_MAX_DIST / exact)
                   * (REL_BUCKETS - exact)).astype(I32)
    return jnp.where(d < exact, d, jnp.minimum(far, REL_BUCKETS - 1))


def _rel_bias(table, dist):
    return jnp.moveaxis(table[_t5_bucket(dist)], -1, 0)


def _nsa_proj(st, p, tm):
    qw = NSA_HEADS * HD
    w = _pad_cols(p["nsa_w_in"], qw + 6 * KVW + LANES).astype(BF16)

    def epi(acc):
        return (acc[:, :qw], acc[:, qw:qw + 2 * KVW], acc[:, qw + 2 * KVW:qw + 4 * KVW],
                acc[:, qw + 4 * KVW:qw + 6 * KVW], acc[:, qw + 6 * KVW:])

    outs = [_rows_out(st.M, qw, tm)] + [_rows_out(st.M, 2 * KVW, tm)] * 3 + [_rows_out(st.M, LANES, tm)]
    return _mm(st.x, _full(w), tm=tm, pro=_modulate, pro_ins=[st.mod(1, tm), st.mod(0, tm)], epi=epi, outs=outs,
               name="nsa_proj")


def _cmp_weights(p):
    eye = jnp.eye(NSA_KVH, dtype=F32)
    wk = jnp.einsum("ab,vlde->vladbe", eye, p["nsa_cmp_w1"]).reshape(2, CMP_BLK, KVW, KVW)
    wc = wk.reshape(2, 2, HALF, KVW, KVW).transpose(1, 2, 0, 3, 4)
    w2c = jnp.einsum("ab,vde->vadbe", eye, p["nsa_cmp_w2"]).reshape(2, KVW, KVW)
    b1 = jnp.tile(p["nsa_cmp_b1"][:, None, :], (1, NSA_KVH, 1)).reshape(1, 2 * KVW)
    return wc.astype(BF16), w2c.astype(BF16), b1


def _compress_body(nx, *refs):
    x_refs = refs[:nx]
    wc_ref, w2_ref, b1_ref, o_ref, ua, ub = refs[nx:]
    rows = ua.shape[1]
    nl = 2 * KVW // LANES
    acc = [[jnp.zeros((rows, KVW), F32) for _ in range(2)] for _ in range(2)]
    for l in range(HALF):
        for kv in range(2):
            lo = l * 2 * KVW + kv * KVW
            piece = [r[:, lo:lo + KVW] for r in x_refs]
            piece = (jnp.concatenate(piece, axis=0) if nx > 1 else piece[0]).astype(BF16)
            for half in range(2):
                acc[half][kv] = acc[half][kv] + _d(piece, wc_ref[half, l, kv])
    for scr, a in ((ua, acc[0]), (ub, acc[1])):
        full = jnp.concatenate(a, axis=1)
        for c in range(nl):
            scr[c] = full[:, c * LANES:(c + 1) * LANES]
    hid = jnp.concatenate([ua[c, pl.ds(0, rows // 2, stride=2), :] + ub[c, pl.ds(1, rows // 2, stride=2), :]
                           for c in range(nl)], axis=1)
    hid = _gelu_tanh(hid + b1_ref[...])
    o_ref[...] = jnp.concatenate([_d(hid[:, :KVW], w2_ref[0]), _d(hid[:, KVW:], w2_ref[1])], axis=1)


def _compress_dense(rows_kv, cw):
    wc, w2c, b1 = cw
    M = rows_kv.shape[0]
    x = rows_kv.reshape(M // HALF, HALF * 2 * KVW)
    nh = M // HALF
    th = _pick_tile(nh, cap=128)
    ins = [_rows(x, th), _full(wc), _full(w2c), _full(b1)]
    outs = [_rows_out(nh // 2, 2 * KVW, th // 2)]
    scratch = [pltpu.VMEM((2 * KVW // LANES, th, LANES), F32)] * 2
    return _call(functools.partial(_compress_body, 1), (nh // th,), ins, outs, scratch=scratch, name="nsa_compress")[0]


def _pair_mat(nc, ns):
    n = np.arange(nc)
    return jnp.asarray((n[:, None] // (SEL_BLK // CMP_BLK) == np.arange(ns)[None, :]).astype(np.float32), BF16)


def _top_blocks(score, n_sel):
    lane = lax.broadcasted_iota(I32, score.shape, 1)
    big = jnp.int32(1 << 20)
    sel = jnp.zeros(score.shape, jnp.bool_)
    work = score
    for _ in range(n_sel):
        m = jnp.max(work, axis=-1, keepdims=True)
        idx = jnp.min(jnp.where(work == m, lane, big), axis=-1, keepdims=True)
        hit = lane == idx
        sel = sel | hit
        work = jnp.where(hit, -3e38, work)
    return jnp.where(sel, 0.0, SEL_NEG)


def _masked_softmax(s, mask):
    s = jnp.where(mask, s, NEG)
    m = jnp.max(s, axis=-1, keepdims=True)
    p = jnp.where(mask, jnp.exp(s - m), 0.0)
    l = jnp.sum(p, axis=-1, keepdims=True)
    return p / jnp.where(l > 0.0, l, 1.0)


def _block_scores(imp, tpos, ns):
    blk = lax.broadcasted_iota(I32, imp.shape, 1)
    cur = tpos // SEL_BLK
    forced = (blk == 0) | (blk == cur) | (blk == cur - 1)
    score = jnp.where(forced, FORCE_SCORE, imp)
    return jnp.where(blk * SEL_BLK > tpos, -1.0, score)


CMP_NEAR_LO = -3
CMP_NEAR_N = 8


def _cmp_bias_pattern(table, tq):
    assert tq == LANES and CMP_BLK == 32 and REL_MAX_DIST == LANES
    r = jnp.arange(tq, dtype=I32)[:, None]
    m = CMP_NEAR_LO + jnp.arange(CMP_NEAR_N, dtype=I32)[None, :]
    near = _rel_bias(table, r - (CMP_BLK - 1) + CMP_BLK * m)
    far = jnp.broadcast_to(table[REL_BUCKETS - 1][:, None, None], (table.shape[1], tq, 1))
    return jnp.pad(jnp.concatenate([near, far], axis=2), ((0, 0), (0, 0), (0, LANES - CMP_NEAR_N - 1)))


def _nsa_cmp_prompt(q, kcvc, pat, B, T, tq):
    nc, ns = T // CMP_BLK, -(-T // SEL_BLK)
    n_sel = min(N_SEL, ns)
    nq = T // tq
    G = NSA_G
    pair = _pair_mat(nc, ns)
    scale = HD ** -0.5
    rb = tq // CMP_BLK

    def body(q_ref, kc_ref, vc_ref, b_ref, pair_ref, o_ref, mb_ref):
        qi = pl.program_id(1)
        q = q_ref[...]
        R = G * tq
        tpos = qi * tq + lax.broadcasted_iota(I32, (R, 1), 0) % tq
        cmp_end = lax.broadcasted_iota(I32, (R, nc), 1) * CMP_BLK + (CMP_BLK - 1)
        mask = cmp_end <= tpos
        j = lax.broadcasted_iota(I32, (LANES, nc), 0)
        m = rb * qi - lax.broadcasted_iota(I32, (LANES, nc), 1)
        sel = ((j < CMP_NEAR_N) & (m == j + CMP_NEAR_LO)) | ((j == CMP_NEAR_N) & (m >= CMP_NEAR_LO + CMP_NEAR_N))
        sel = jnp.where(sel, 1.0, 0.0).astype(BF16)
        outs = []
        for kv in range(NSA_KVH):
            qs = jnp.concatenate([q[:, (kv * G + gg) * HD:(kv * G + gg + 1) * HD] for gg in range(G)], axis=0) * scale
            s = _d_nt(qs, kc_ref[:, kv * HD:(kv + 1) * HD])
            s = s + _d_x3(jnp.concatenate([b_ref[kv * G + gg] for gg in range(G)], axis=0), sel)
            pc = _masked_softmax(s, mask)
            oc = _d(pc, vc_ref[:, kv * HD:(kv + 1) * HD])
            outs += [oc[gg * tq:(gg + 1) * tq] for gg in range(G)]
            imp = pc[0:tq]
            for gg in range(1, G):
                imp = imp + pc[gg * tq:(gg + 1) * tq]
            score = _block_scores(_d_x3(imp, pair_ref[...]), tpos[0:tq], ns)
            mb_ref[kv] = _top_blocks(score, n_sel).astype(mb_ref.dtype)
        o_ref[...] = jnp.concatenate(outs, axis=1)

    ins = [(q, pl.BlockSpec((tq, NSA_HEADS * HD), lambda b, i: (b * nq + i, 0))),
           (kcvc, pl.BlockSpec((nc, KVW), lambda b, i: (b, 0))),
           (kcvc, pl.BlockSpec((nc, KVW), lambda b, i: (b, 1))),
           (pat, pl.BlockSpec(pat.shape, lambda b, i: (0, 0, 0))),
           _full(pair)]
    outs = [((B * T, NSA_HEADS * HD), F32, pl.BlockSpec((tq, NSA_HEADS * HD), lambda b, i: (b * nq + i, 0))),
            ((B, NSA_KVH, T, ns), BF16, pl.BlockSpec((None, NSA_KVH, tq, ns), lambda b, i: (b, 0, i, 0)))]
    return _call(body, (B, nq), ins, outs, name="nsa_cmp_select")


def _gate_mats():
    hsn = _seg_np(HD)
    return [jnp.asarray(np.roll(hsn.T, br * NSA_HEADS, axis=0), BF16) for br in range(3)]


def _nsa_out(st, o_c, o_s, o_w, gates, p, alpha, ln_g, ln_b, tm):
    def pro(oc, os_, ow, gl, e0, e1, e2):
        sg = _sigmoid(gl)
        return _d_x3(sg, e0) * oc + _d_x3(sg, e1) * os_ + _d_x3(sg, e2) * ow

    def epi(acc, x, gate, g_, b_):
        return (_res_ln(alpha, acc, x, gate, g_, b_),)

    pro_ins = [_rows(o_s, tm), _rows(o_w, tm), _rows(gates, tm)] + [_full(e) for e in _gate_mats()]
    return _mm(o_c, _full(p["nsa_w_o"].astype(BF16)), tm=tm, pro=pro, pro_ins=pro_ins, epi=epi,
               epi_ins=[_rows(st.x, tm), st.mod(2, tm), _full(ln_g), _full(ln_b)],
               outs=[_rows_out(st.M, D, tm)], name="nsa_out")[0]


def _nsa_prompt(st, p, alpha, ln_g, ln_b):
    B, T, M = st.B, st.T, st.M
    tm = min(256, M)
    tq = ATT_T
    nq = T // tq
    table = p["rel_bias"]
    q, cmp_rows, slc_rows, win_rows, gates = _nsa_proj(st, p, tm)
    kcvc = _compress_dense(cmp_rows, _cmp_weights(p))
    nc, ns = T // CMP_BLK, -(-T // SEL_BLK)
    o_c, mb = _nsa_cmp_prompt(q, kcvc, _cmp_bias_pattern(table, tq), B, T, tq)
    r = jnp.arange(tq, dtype=I32)
    far = table[REL_BUCKETS - 1][:, None, None]
    tz = jnp.stack([_rel_bias(table, r[:, None] - r[None, :]) - far,
                    _rel_bias(table, tq + r[:, None] - r[None, :]) - far], axis=1)
    e_blk = jnp.asarray((np.arange(T)[:, None] // SEL_BLK == np.arange(ns)[None, :]).astype(np.float32), BF16)
    q_in = (q, pl.BlockSpec((tq, NSA_HEADS * HD), lambda b, g, i: (b * nq + i, 0)))
    kv_in = lambda a, c: (a, pl.BlockSpec((T, KVW), lambda b, g, i: (b, c)))
    tz_in = (tz, pl.BlockSpec(tz.shape, lambda b, g, i: (0, 0, 0, 0)))
    cfg = dict(name="nsa_slc", tq=tq, hq=NSA_HEADS, hk=NSA_KVH, fox=False, bias=True, aug=ns, window=None, unroll=4)
    extra = [(mb, pl.BlockSpec((None, NSA_KVH, tq, ns), lambda b, g, i: (b, 0, i, 0))),
             (e_blk, pl.BlockSpec(e_blk.shape, lambda b, g, i: (0, 0))), tz_in]
    o_s = _flash(cfg, B, T, 1, q_in, kv_in(slc_rows, 0), kv_in(slc_rows, 1), extra, M)
    cfg = dict(name="nsa_win", tq=tq, hq=NSA_HEADS, hk=NSA_KVH, fox=False, bias=True, aug=0, window=WINDOW, unroll=1)
    o_w = _flash(cfg, B, T, 1, q_in, kv_in(win_rows, 0), kv_in(win_rows, 1), [tz_in], M)
    st.x = _nsa_out(st, o_c, o_s, o_w, gates, p, alpha, ln_g, ln_b, tm)
    shp = (B, T, 2, NSA_KVH, HD)
    keep = min(WINDOW, T)
    return cmp_rows.reshape(shp), slc_rows.reshape(shp), win_rows.reshape(shp)[:, T - keep:]


def _compress_paged(cache_cmp, page_table, cw):
    wc, w2c, b1 = cw
    B, npg = page_table.shape
    PAGE = cache_cmp.shape[1]
    hp = PAGE // HALF
    cache = cache_cmp.reshape(cache_cmp.shape[0], hp, HALF * 2 * KVW)
    nb = 2 if B % 2 == 0 else 1
    ins = [(cache, pl.BlockSpec((None, hp, HALF * 2 * KVW), lambda g, pt, bb=bb, j=j: (pt[g * nb + bb, j], 0, 0)))
           for bb in range(nb) for j in range(npg)]
    const = lambda a: (a, pl.BlockSpec(a.shape, lambda g, pt: (0,) * a.ndim))
    ins += [const(wc), const(w2c), const(b1)]
    rows = nb * npg * hp
    outs = [((B * npg * hp // 2, 2 * KVW), F32, pl.BlockSpec((rows // 2, 2 * KVW), lambda g, pt: (g, 0)))]
    scratch = [pltpu.VMEM((2 * KVW // LANES, rows, LANES), F32)] * 2
    body = lambda pt_ref, *refs: _compress_body(nb * npg, *refs)
    return _call(body, (B // nb,), ins, outs, scratch=scratch, prefetch=[page_table], name="nsa_compress_paged")[0]


def _nsa_decode(q, slc_new, win_new, kcvc, cache_slc, cache_win, page_table, gcol, consts, Tn, offset):
    B, npg = page_table.shape
    PAGE = cache_slc.shape[1]
    Wb = cache_win.shape[1]
    R = NSA_HEADS * Tn
    L = offset + Tn
    nc, ns = L // CMP_BLK, -(-L // SEL_BLK)
    n_sel = min(N_SEL, ns)
    nck = npg + 1
    qw = NSA_HEADS * HD
    cache = cache_slc.transpose(0, 2, 3, 4, 1).reshape(cache_slc.shape[0], 2, KVW, PAGE)
    win = cache_win.transpose(0, 2, 3, 4, 1).reshape(B, 2, KVW, Wb)
    scale = HD ** -0.5
    names = ["rep_t", "fold", "unfold", "bias_c", "mask_c", "pair", "e_blk", "bias_s", "bias_w"]
    cvals = [consts[n] for n in names]

    def body(pt_ref, *refs):
        pages = refs[:npg]
        (q_ref, sn_ref, wn_ref, kc_ref, vc_ref, win_ref, g_ref, rt_ref, fold_ref, unfold_ref, bc_ref, mc_ref, pair_ref,
         e_ref, bs_ref, bw_ref, o_ref, wout_ref) = refs[npg:]
        row_h = lax.broadcasted_iota(I32, (R, qw), 0) // Tn
        lane_h = lax.broadcasted_iota(I32, (R, qw), 1) // HD
        own = row_h == lane_h
        qrep = _d(rt_ref[...], q_ref[...])
        qbd = (_d(jnp.where(own, qrep, 0.0), fold_ref[...]) * scale).astype(BF16)
        pad = lambda x: jnp.concatenate([x, jnp.zeros((PAGE - Tn, x.shape[1]), x.dtype)], axis=0)

        pc = _masked_softmax(_d_nt(qbd, kc_ref[...]) + bc_ref[...], mc_ref[...] > 0.0)
        o_c = _d(pc, vc_ref[...])
        imp = []
        for kv in range(NSA_KVH):
            a = pc[kv * NSA_G * Tn:(kv * NSA_G + 1) * Tn]
            for gg in range(1, NSA_G):
                a = a + pc[(kv * NSA_G + gg) * Tn:(kv * NSA_G + gg + 1) * Tn]
            imp.append(a)
        imp = jnp.concatenate(imp, axis=0)
        tpos = offset + lax.broadcasted_iota(I32, (NSA_KVH * Tn, 1), 0) % Tn
        mb = _top_blocks(_block_scores(_d_x3(imp, pair_ref[...]), tpos, ns), n_sel)
        mb = jnp.concatenate([mb[kv * Tn:(kv + 1) * Tn] for kv in range(NSA_KVH) for _ in range(NSA_G)], axis=0)

        sn = pad(sn_ref[...])
        past = npg * PAGE
        kt = jnp.concatenate([pages[j][0].astype(BF16) for j in range(npg)], axis=1)
        vt = jnp.concatenate([pages[j][1].astype(BF16) for j in range(npg)], axis=1)
        s = jnp.concatenate([_d(qbd, kt), _d_nt(qbd, sn[:, :KVW])], axis=1) + _d(mb, e_ref[...]) + bs_ref[...]
        p = jnp.exp(s - jnp.max(s, axis=-1, keepdims=True))
        o_s = (_d_nt(p[:, :past], vt) + _d(p[:, past:], sn[:, KVW:])) / jnp.sum(p, axis=-1, keepdims=True)

        wk, wv = win_ref[0], win_ref[1]
        wn = pad(wn_ref[...])
        s = jnp.concatenate([_d(qbd, wk), _d_nt(qbd, wn[:, :KVW])], axis=1) + bw_ref[...]
        p = jnp.exp(s - jnp.max(s, axis=-1, keepdims=True))
        o_w = (_d_nt(p[:, :Wb], wv) + _d(p[:, Wb:], wn[:, KVW:])) / jnp.sum(p, axis=-1, keepdims=True)
        lane = lax.broadcasted_iota(I32, (KVW, Wb), 1)
        for kv, old in enumerate((wk, wv)):
            nt = pltpu.roll(wn[:, kv * KVW:(kv + 1) * KVW].T, PAGE - Tn, axis=1)
            nt = jnp.concatenate([jnp.zeros((KVW, Wb - PAGE), F32), nt], axis=1)
            wout_ref[kv] = jnp.where(lane >= Wb - Tn, nt, pltpu.roll(old, Wb - Tn, axis=1))

        sg = _sigmoid(g_ref[...])
        o = sg[:, 0:1] * o_c + sg[:, 1:2] * o_s + sg[:, 2:3] * o_w
        of = jnp.where(own, _d_x3(o, unfold_ref[...]), 0.0)
        out = of[0:Tn]
        for h in range(1, NSA_HEADS):
            out = out + of[h * Tn:(h + 1) * Tn]
        o_ref[...] = out

    c2 = lambda b, pt: (0, 0)
    ins = [(cache, pl.BlockSpec((None, 2, KVW, PAGE), lambda b, pt, j=j: (pt[b, j], 0, 0, 0))) for j in range(npg)]
    ins += [(q, pl.BlockSpec((Tn, qw), lambda b, pt: (b, 0))),
            (slc_new, pl.BlockSpec((Tn, 2 * KVW), lambda b, pt: (b, 0))),
            (win_new, pl.BlockSpec((Tn, 2 * KVW), lambda b, pt: (b, 0))),
            (kcvc, pl.BlockSpec((nc, KVW), lambda b, pt: (b, 0))),
            (kcvc, pl.BlockSpec((nc, KVW), lambda b, pt: (b, 1))),
            (win, pl.BlockSpec((None, 2, KVW, Wb), lambda b, pt: (b, 0, 0, 0))),
            (gcol, pl.BlockSpec((None, R, 3), lambda b, pt: (b, 0, 0)))]
    ins += [(a, pl.BlockSpec(a.shape, c2)) for a in cvals]
    outs = [((B * Tn, qw), F32, pl.BlockSpec((Tn, qw), lambda b, pt: (b, 0))),
            ((B, 2, KVW, Wb), F32, pl.BlockSpec((None, 2, KVW, Wb), lambda b, pt: (b, 0, 0, 0)))]
    return _call(body, (B,), ins, outs, prefetch=[page_table], name="nsa_decode")


def _nsa_decode_consts(table, Tn, offset, npg, PAGE, Wb):
    R = NSA_HEADS * Tn
    L = offset + Tn
    nc, ns = L // CMP_BLK, -(-L // SEL_BLK)
    Lp = (npg + 1) * PAGE
    tpos = offset + jnp.arange(Tn, dtype=I32)
    rows = lambda x: x.reshape(R, x.shape[-1])
    cmp_end = jnp.arange(nc, dtype=I32) * CMP_BLK + CMP_BLK - 1
    dist_c = tpos[:, None] - cmp_end[None, :]
    spos = jnp.arange(Lp, dtype=I32)
    dist_s = tpos[:, None] - spos[None, :]
    ok_s = (dist_s >= 0) & (spos[None, :] < L)
    col = jnp.arange(Wb + PAGE, dtype=I32)
    wpos = offset - Wb + col
    dist_w = tpos[:, None] - wpos[None, :]
    ok_w = (dist_w >= 0) & (dist_w < WINDOW) & (wpos[None, :] >= 0) & (col[None, :] < Wb + Tn)
    tile = lambda m: jnp.tile(m[None], (NSA_HEADS, 1, 1))
    fold = np.zeros((NSA_HEADS, HD, NSA_KVH, HD), np.float32)
    for h in range(NSA_HEADS):
        fold[h, :, h // NSA_G, :] = np.eye(HD)
    fold = fold.reshape(NSA_HEADS * HD, KVW)
    return dict(
        rep_t=_rep_mat(R, Tn, 0), fold=jnp.asarray(fold, BF16), unfold=jnp.asarray(fold.T, BF16),
        bias_c=rows(_rel_bias(table, dist_c)), mask_c=rows(tile((dist_c >= 0).astype(F32))), pair=_pair_mat(nc, ns),
        e_blk=jnp.asarray((np.arange(ns)[:, None] == np.arange(Lp)[None, :] // SEL_BLK).astype(np.float32), BF16),
        bias_s=rows(_rel_bias(table, dist_s) + tile(jnp.where(ok_s, 0.0, NEG))),
        bias_w=rows(_rel_bias(table, dist_w) + tile(jnp.where(ok_w, 0.0, NEG))))


def _nsa_sample(st, cache_cmp, cache_slc, cache_win, page_table, p, alpha, ln_g, ln_b):
    B, Tn, M = st.B, st.T, st.M
    tm = min(256, M)
    npg = page_table.shape[1]
    PAGE = cache_slc.shape[1]
    Wb = cache_win.shape[1]
    offset = npg * PAGE
    assert offset % CMP_BLK == 0 and Tn < CMP_BLK and Wb == WINDOW and Tn % SUBLANES == 0
    q, cmp_rows, slc_rows, win_rows, gates = _nsa_proj(st, p, tm)
    kcvc = _compress_paged(cache_cmp, page_table, _cmp_weights(p))
    gcol = gates.reshape(B, Tn, LANES)[:, :, :3 * NSA_HEADS].reshape(B, Tn, 3, NSA_HEADS)
    gcol = gcol.transpose(0, 3, 1, 2).reshape(B, NSA_HEADS * Tn, 3)
    consts = _nsa_decode_consts(p["rel_bias"], Tn, offset, npg, PAGE, Wb)
    o, wout = _nsa_decode(q, slc_rows, win_rows, kcvc, cache_slc, cache_win, page_table, gcol, consts, Tn, offset)
    st.x = _out_proj(st, o, p["nsa_w_o"], alpha, ln_g, ln_b, tm, "nsa_out_s")
    shp = (B, Tn, 2, NSA_KVH, HD)
    win_keep = wout.reshape(B, 2, NSA_KVH, HD, Wb).transpose(0, 4, 1, 2, 3)
    return cmp_rows.reshape(shp), slc_rows.reshape(shp), win_keep


def kernel(x_prompt, x_sample, state_rwkv_wkv, state_rwkv_shift, cache_nsa_cmp, cache_nsa_slc, cache_nsa_win, cache_fox_kv, cache_fox_logf, state_gdn_S, state_gdn_conv, page_table, c_prompt, c_sample, w_mod, b_mod, ln_g, ln_b, moe_w_group, moe_b_group, moe_w_router, moe_b_router, moe_w1, moe_w3, moe_w2, rwkv_mu, rwkv_w_rkv, rwkv_w0, rwkv_w1, rwkv_w2, rwkv_a0, rwkv_a1, rwkv_a2, rwkv_g1, rwkv_g2, rwkv_k_k, rwkv_k_a, rwkv_r_k, rwkv_ln_w, rwkv_ln_b, rwkv_w_o, nsa_w_in, nsa_cmp_w1, nsa_cmp_b1, nsa_cmp_w2, nsa_w_o, rel_bias, fox_w_in, fox_b_f, fox_w_o, gdn_w_in, gdn_conv_w, gdn_A_log, gdn_dt_bias, gdn_norm_w, gdn_w_o):
    p = dict(locals())
    Bp, T, _ = x_prompt.shape
    Bs, Tn, _ = x_sample.shape
    depth = w_mod.shape[0]
    alpha = (2 * depth) ** 0.25
    sp = _Stream(x_prompt.reshape(Bp * T, D), Bp, T, min(512, T))
    ss = _Stream(x_sample.reshape(Bs * Tn, D), Bs, Tn, min(256, Bs * Tn))
    nc = Bp + Bs
    c_all = jnp.pad(jnp.concatenate([c_prompt, c_sample], axis=0), ((0, -nc % SUBLANES), (0, 0)))
    out = {}
    for layer in range(depth):
        m6 = _ada(c_all, w_mod, b_mod, layer)
        sp.set_mods(m6[:Bp])
        ss.set_mods(m6[Bp:nc])
        g0 = ln_g[layer, 0].reshape(1, D)
        b0 = ln_b[layer, 0].reshape(1, D)
        kind = layer % 4
        if kind == 0:
            nh = D // RWKV_HSZ
            out["wkv_p"], out["shift_p"] = _rwkv_layer(sp, jnp.zeros((Bp, D), F32),
                                                       jnp.zeros((Bp, nh, RWKV_HSZ, RWKV_HSZ), F32), p, alpha, g0, b0)
            out["wkv_s"], out["shift_s"] = _rwkv_layer(ss, state_rwkv_shift, state_rwkv_wkv, p, alpha, g0, b0)
        elif kind == 1:
            out["cmp_p"], out["slc_p"], out["win_p"] = _nsa_prompt(sp, p, alpha, g0, b0)
            out["cmp_s"], out["slc_s"], out["win_s"] = _nsa_sample(ss, cache_nsa_cmp, cache_nsa_slc, cache_nsa_win,
                                                                   page_table, p, alpha, g0, b0)
        elif kind == 2:
            out["kv_p"], out["logf_p"] = _fox_prompt(sp, p, alpha, g0, b0)
            out["kv_s"], out["logf_s"] = _fox_sample(ss, cache_fox_kv, cache_fox_logf, page_table, p, alpha, g0, b0)
        else:
            out["S_p"], out["conv_p"] = _gdn_layer(sp, jnp.zeros((Bp, GDN_CONV - 1, 3 * D), F32),
                                                   jnp.zeros((Bp, GDN_HEADS, GDN_HSZ, GDN_HSZ), F32), p, alpha, g0, b0)
            out["S_s"], out["conv_s"] = _gdn_layer(ss, state_gdn_conv, state_gdn_S, p, alpha, g0, b0)
        _moe_layer([sp, ss], layer, alpha, p)
    return (sp.x.reshape(Bp, T, D), ss.x.reshape(Bs, Tn, D), out["wkv_p"], out["wkv_s"], out["shift_p"], out["shift_s"],
            out["cmp_p"], out["cmp_s"], out["slc_p"], out["slc_s"], out["win_p"], out["win_s"],
            out["kv_p"], out["kv_s"], out["logf_p"], out["logf_s"], out["S_p"], out["S_s"], out["conv_p"], out["conv_s"])
```

```python
import functools
import math

import numpy as np
import jax
import jax.numpy as jnp
from jax import lax
from jax.experimental import pallas as pl
from jax.experimental.pallas import tpu as pltpu

F32 = jnp.float32
BF16 = jnp.bfloat16
I32 = jnp.int32
NEG = -1e30
LN_EPS = 1e-5
D = 1024
LANES = 128
SUBLANES = 8
MXU_TILE = 256
VMEM_LIMIT_MB = 56

RWKV_HSZ = 64
RWKV_GN_EPS = 64e-5
NSA_HEADS, NSA_KVH, HD = 16, 4, 64
NSA_G = NSA_HEADS // NSA_KVH
CMP_BLK, SEL_BLK, N_SEL, WINDOW = 32, 64, 16, 512
FORCE_SCORE = 1e4
REL_BUCKETS, REL_MAX_DIST = 32, 128
FOX_HEADS = 16
GDN_HEADS, GDN_HSZ, GDN_CONV = 8, 128, 4
MOE_GROUPS, MOE_EPG, MOE_BLK = 4, 8, 256
MOE_EXPERTS = MOE_GROUPS * MOE_EPG
ATT_T = 128
SEL_NEG = -65536.0
PAGES_PER_STEP = 4


def _d(a, b):
    return jnp.dot(a.astype(BF16), b.astype(BF16), preferred_element_type=F32)


def _d_nt(a, b):
    return lax.dot_general(a.astype(BF16), b.astype(BF16), (((1,), (1,)), ((), ())),
                           preferred_element_type=F32)


def _split3(x):
    h = x.astype(BF16)
    r1 = x - h.astype(F32)
    m = r1.astype(BF16)
    l = (r1 - m.astype(F32)).astype(BF16)
    return h, m, l


def _d_x3(x, sel):
    h, m, l = _split3(x)
    return _d(h, sel) + _d(m, sel) + _d(l, sel)


def _d_3x(sel, x):
    h, m, l = _split3(x)
    return _d(sel, h) + _d(sel, m) + _d(sel, l)


def _d_f32(x, w):
    xh, xm, xl = _split3(x)
    wh, wm, wl = _split3(w)
    return (_d(xh, wh) + _d(xh, wm) + _d(xm, wh)) + (_d(xh, wl) + _d(xl, wh) + _d(xm, wm))


def _sigmoid(x):
    return 1.0 / (1.0 + jnp.exp(-x))


def _softplus(x):
    return jnp.maximum(x, 0.0) + jnp.log(1.0 + jnp.exp(-jnp.abs(x)))


def _silu(x):
    return x * _sigmoid(x)


def _gelu_tanh(x):
    return 0.5 * x * (1.0 + jnp.tanh(math.sqrt(2.0 / math.pi) * (x + 0.044715 * (x * x * x))))


def _layer_norm(z, g, b):
    mu = jnp.mean(z, axis=-1, keepdims=True)
    zc = z - mu
    var = jnp.mean(zc * zc, axis=-1, keepdims=True)
    return zc * lax.rsqrt(var + LN_EPS) * g + b


def _modulate(x, sc, sh):
    return x * (1.0 + sc) + sh


def _res_ln(alpha, y, xres, gate, g, b):
    return _layer_norm(alpha * xres + (1.0 + gate) * y, g, b)


@functools.lru_cache(maxsize=None)
def _seg_np(hsz):
    head = np.arange(D) // hsz
    hs = (head[:, None] == np.arange(LANES)[None, :]).astype(np.float32)
    return hs


def _seg_consts(hsz):
    hs = _seg_np(hsz)
    return jnp.asarray(hs, BF16), jnp.asarray(hs.T, BF16)


def _scan_consts(hsz):
    nh = D // hsz
    head = np.arange(D) // hsz
    slot_j = np.arange(LANES) // 16
    slot_h = np.arange(LANES) % 16
    hexp = np.zeros((SUBLANES, LANES, D), np.float32)
    for j in range(SUBLANES):
        hexp[j] = ((slot_j[:, None] == j) & (slot_h[:, None] == head[None, :]) & (slot_h[:, None] < nh))
    hsum = np.transpose(hexp, (0, 2, 1))
    blk = np.arange(MXU_TILE) // hsz
    bd = (blk[:, None] == blk[None, :]).astype(np.float32)
    return jnp.asarray(hexp, BF16), jnp.asarray(hsum, BF16), jnp.asarray(bd, BF16)


def _call(body, grid, ins, outs, scratch=(), name=None, prefetch=None, aliases=None):
    arrays = [a for a, _ in ins]
    in_specs = [s for _, s in ins]
    out_shape = [jax.ShapeDtypeStruct(s, d) for s, d, _ in outs]
    out_specs = [s for _, _, s in outs]
    params = pltpu.CompilerParams(dimension_semantics=("arbitrary",) * len(grid),
                                  vmem_limit_bytes=VMEM_LIMIT_MB << 20)
    kw = {}
    if aliases:
        kw["input_output_aliases"] = aliases
    if prefetch is None:
        fn = pl.pallas_call(body, grid=grid, in_specs=in_specs, out_specs=out_specs, out_shape=out_shape,
                            scratch_shapes=list(scratch), compiler_params=params, name=name, **kw)
        res = fn(*arrays)
    else:
        gs = pltpu.PrefetchScalarGridSpec(num_scalar_prefetch=len(prefetch), grid=grid, in_specs=in_specs,
                                          out_specs=out_specs, scratch_shapes=list(scratch))
        fn = pl.pallas_call(body, grid_spec=gs, out_shape=out_shape, compiler_params=params, name=name, **kw)
        res = fn(*prefetch, *arrays)
    return list(res)


def _full(a):
    nd = a.ndim
    return (a, pl.BlockSpec(a.shape, lambda *_: (0,) * nd))


def _rows(a, tm):
    return (a, pl.BlockSpec((tm, a.shape[1]), lambda i, *_: (i, 0)))


def _rows_out(M, C, tm, dtype=F32):
    return ((M, C), dtype, pl.BlockSpec((tm, C), lambda i, *_: (i, 0)))


class _Stream:
    def __init__(self, x, B, T, tm):
        self.x, self.B, self.T, self.tm = x, B, T, tm
        self.M = B * T
        self.m6 = None

    def set_mods(self, m6):
        self.m6 = m6
        self.rep = jnp.repeat(m6, self.T, axis=0) if self.T < self.tm else None

    def mod(self, c, tm=None):
        tm = tm or self.tm
        if self.T % tm == 0:
            tpb = self.T // tm
            a = self.m6[:, c * D:(c + 1) * D].reshape(self.B, 1, D)
            return (a, pl.BlockSpec((None, 1, D), lambda i, *_: (i // tpb, 0, 0)))
        assert tm % self.T == 0 and self.M % tm == 0
        a = self.rep[:, c * D:(c + 1) * D].reshape(self.M // tm, tm, D)
        return (a, pl.BlockSpec((None, tm, D), lambda i, *_: (i, 0, 0)))


def _mm(x, w_in, *, tm, pro=None, pro_ins=(), epi=None, epi_ins=(), outs=None, name="mm"):
    M, K = x.shape
    n_pro, n_epi = len(pro_ins), len(epi_ins)
    n_out = len(outs)

    def body(*refs):
        x_ref = refs[0]
        pro_refs = refs[1:1 + n_pro]
        w_ref = refs[1 + n_pro]
        epi_refs = refs[2 + n_pro:2 + n_pro + n_epi]
        out_refs = refs[2 + n_pro + n_epi:2 + n_pro + n_epi + n_out]
        a = x_ref[...]
        if pro is not None:
            a = pro(a, *[r[...] for r in pro_refs])
        acc = _d(a, w_ref[...])
        res = epi(acc, *[r[...] for r in epi_refs]) if epi is not None else (acc,)
        for o, r in zip(out_refs, res):
            o[...] = r.astype(o.dtype)

    ins = [_rows(x, tm)] + list(pro_ins) + [w_in] + list(epi_ins)
    return _call(body, (M // tm,), ins, outs, name=name)


def _ada(c_all, w_mod, b_mod, layer):
    Mp = c_all.shape[0]
    N = w_mod.shape[2]
    tn = 1536

    def body(c_ref, w_ref, b_ref, o_ref):
        o_ref[...] = _d(_silu(c_ref[...]), w_ref[...]) + b_ref[...]

    ins = [(c_all, pl.BlockSpec((Mp, D), lambda j: (0, 0))),
           (w_mod, pl.BlockSpec((None, D, tn), lambda j: (layer, 0, j))),
           (b_mod.reshape(b_mod.shape[0], 1, N), pl.BlockSpec((None, 1, tn), lambda j: (layer, 0, j)))]
    outs = [((Mp, N), F32, pl.BlockSpec((Mp, tn), lambda j: (0, j)))]
    return _call(body, (N // tn,), ins, outs, name="ada_mod")[0]


def _pick_tile(*sizes, cap=512):
    t = cap
    while t > SUBLANES and any(s % t for s in sizes):
        t //= 2
    assert all(s % t == 0 for s in sizes), sizes
    return t


def _route(lg):
    lane = lax.broadcasted_iota(I32, lg.shape, 1)
    big = jnp.int32(1 << 20)
    isg = lane < MOE_GROUPS
    gl = jnp.where(isg, lg, NEG)
    gmax = jnp.max(gl, axis=-1, keepdims=True)
    gsel = jnp.min(jnp.where(gl == gmax, lane, big), axis=-1, keepdims=True)
    gsum = jnp.sum(jnp.where(isg, jnp.exp(gl - gmax), 0.0), axis=-1, keepdims=True)
    gw = 1.0 / gsum
    lo = MOE_GROUPS + MOE_EPG * gsel
    ise = (lane >= lo) & (lane < lo + MOE_EPG)
    el = jnp.where(ise, lg, NEG)
    emax = jnp.max(el, axis=-1, keepdims=True)
    ep = jnp.where(ise, jnp.exp(el - emax), 0.0)
    prob = ep / jnp.sum(ep, axis=-1, keepdims=True)
    pm = jnp.where(ise, prob, -1.0)
    p1 = jnp.max(pm, axis=-1, keepdims=True)
    i1 = jnp.min(jnp.where(pm == p1, lane, big), axis=-1, keepdims=True)
    pm2 = jnp.where(lane == i1, -1.0, pm)
    p2 = jnp.max(pm2, axis=-1, keepdims=True)
    i2 = jnp.min(jnp.where(pm2 == p2, lane, big), axis=-1, keepdims=True)
    den = p1 + p2
    w1 = gw * p1 / den
    w2 = gw * p2 / den
    e1 = (i1 - MOE_GROUPS).astype(F32)
    e2 = (i2 - MOE_GROUPS).astype(F32)
    return jnp.where(lane == 0, e1, jnp.where(lane == 1, e2, jnp.where(lane == 2, w1, jnp.where(lane == 3, w2, 0.0))))


NSEG = D // LANES


def _to_tiles(ref, x):
    for s in range(NSEG):
        ref[:, s, :] = x[:, s * LANES:(s + 1) * LANES]


def _from_tiles(ref):
    return jnp.concatenate([ref[:, s, :] for s in range(NSEG)], axis=1)


def _moe_router(st, wgr, bgr, h_all, off):
    tm = _pick_tile(st.M, off, cap=st.tm)
    b0 = off // tm

    def body(x_ref, sc_ref, sh_ref, w_ref, b_ref, hin_ref, h_ref, r_ref):
        h = _modulate(x_ref[...], sc_ref[...], sh_ref[...])
        _to_tiles(h_ref, h)
        r_ref[...] = _route(_d_f32(h, w_ref[...]) + b_ref[...])

    ins = [_rows(st.x, tm), st.mod(4, tm), st.mod(3, tm), _full(wgr), _full(bgr),
           (h_all, pl.BlockSpec(memory_space=pl.ANY))]
    outs = [(h_all.shape, F32, pl.BlockSpec((tm, NSEG, LANES), lambda i: (b0 + i, 0, 0))),
            _rows_out(st.M, LANES, tm)]
    return _call(body, (st.M // tm,), ins, outs, aliases={5: 0}, name="moe_router")


def _moe_counts(rinfo, R):
    Mtot = rinfo.shape[0]
    nt = Mtot // R

    def body(r_ref, o_ref):
        j = pl.program_id(0)
        t = pl.program_id(1)

        @pl.when((j == 0) & (t == 0))
        def _():
            o_ref[...] = jnp.zeros_like(o_ref)

        xt = r_ref[...].T
        row = jnp.where(j == 0, xt[0:1, :], xt[1:2, :])
        sub = lax.broadcasted_iota(I32, (LANES, R), 0).astype(F32)
        oh = jnp.where(sub == row, 1.0, 0.0)
        o_ref[...] += jnp.sum(oh, axis=1, keepdims=True)

    ins = [(rinfo, pl.BlockSpec((R, LANES), lambda j, t: (t, 0)))]
    outs = [((LANES, LANES), F32, pl.BlockSpec((LANES, LANES), lambda j, t: (0, 0)))]
    return _call(body, (2, nt), ins, outs, name="moe_counts")[0]


def _moe_dest(rinfo, pstart, R):
    Mtot = rinfo.shape[0]
    nt = Mtot // R
    upper = jnp.asarray(np.triu(np.ones((R, R), np.float32), 1), BF16)

    def body(r_ref, p_ref, u_ref, o_ref, carry):
        j = pl.program_id(0)
        t = pl.program_id(1)

        @pl.when((j == 0) & (t == 0))
        def _():
            carry[...] = jnp.zeros_like(carry)

        xt = r_ref[...].T
        row = jnp.where(j == 0, xt[0:1, :], xt[1:2, :])
        sub = lax.broadcasted_iota(I32, (LANES, R), 0).astype(F32)
        oh = jnp.where(sub == row, 1.0, 0.0)
        cum = _d(oh, u_ref[...])
        base = carry[:, 0:1] + p_ref[:, 0:1]
        dest = jnp.sum(oh * (cum + base), axis=0, keepdims=True)
        o_ref[...] = dest.astype(I32)
        carry[...] += jnp.sum(oh, axis=1, keepdims=True)

    ins = [(rinfo, pl.BlockSpec((R, LANES), lambda j, t: (t, 0))), _full(pstart), _full(upper)]
    outs = [((2 * nt, 1, R), I32, pl.BlockSpec((None, 1, R), lambda j, t: (j * nt + t, 0, 0)))]
    return _call(body, (2, nt), ins, outs, scratch=[pltpu.VMEM((LANES, LANES), F32)], name="moe_dest")[0]


def _moe_ffn(h_all, slots, blk_expert, nvalid, w1, w3, w2, layer, Mtot):
    nblk = slots.shape[0] // MOE_BLK
    FF = w1.shape[-1]
    any_spec = pl.BlockSpec(memory_space=pl.ANY)
    GRP = SUBLANES

    def body(be_ref, nv_ref, slot_ref, h_ref, w1_ref, w3_ref, w2_ref, y_ref, xbuf, ybuf, sem_in, sem_out):
        i = pl.program_id(0)
        nv = nv_ref[0]
        buf = i % 2

        def row_copy(kind, blk, b, r):
            s = slot_ref[blk * MOE_BLK + r]
            if kind == "gather":
                tok = jnp.where(s >= 2 * Mtot, s - 2 * Mtot, jnp.where(s >= Mtot, s - Mtot, s))
                return pltpu.make_async_copy(h_ref.at[tok], xbuf.at[b, r], sem_in.at[b])
            return pltpu.make_async_copy(ybuf.at[b, r], y_ref.at[s], sem_out.at[b])

        def each_row(kind, blk, b, start):
            for r in range(MOE_BLK):
                cp = row_copy(kind, blk, b, r)
                cp.start() if start else cp.wait()

        @pl.when(i < nv)
        def _():
            @pl.when(i == 0)
            def _():
                ybuf[...] = jnp.zeros_like(ybuf)
                for b in range(2):
                    cp = pltpu.make_async_copy(ybuf.at[b], y_ref.at[pl.ds(2 * Mtot + b * MOE_BLK, MOE_BLK)],
                                               sem_out.at[b])
                    cp.start()
                    cp.wait()
                each_row("gather", 0, 0, True)

            nxt = jnp.minimum(i + 1, nv - 1)
            each_row("gather", i, buf, False)
            x = _from_tiles(xbuf.at[buf]).astype(BF16)
            each_row("gather", nxt, 1 - buf, True)
            a = _d(x, w1_ref[...])
            b = _d(x, w3_ref[...])
            y = _d(_silu(a) * b, w2_ref[...])

            @pl.when(i >= 2)
            def _():
                each_row("scatter", i - 2, buf, False)

            _to_tiles(ybuf.at[buf], y)
            each_row("scatter", i, buf, True)

            @pl.when(i == nv - 1)
            def _():
                each_row("gather", nxt, 1 - buf, False)

                @pl.when(i >= 1)
                def _():
                    each_row("scatter", i - 1, 1 - buf, False)

                each_row("scatter", i, buf, False)

    def blk(i, be, nv, sl):
        return be[jnp.minimum(i, nv[0] - 1)]

    ins = [(h_all, any_spec),
           (w1, pl.BlockSpec((None, None, D, FF), lambda i, be, nv, sl: (layer, blk(i, be, nv, sl), 0, 0))),
           (w3, pl.BlockSpec((None, None, D, FF), lambda i, be, nv, sl: (layer, blk(i, be, nv, sl), 0, 0))),
           (w2, pl.BlockSpec((None, None, FF, D), lambda i, be, nv, sl: (layer, blk(i, be, nv, sl), 0, 0)))]
    outs = [((2 * Mtot + 2 * MOE_BLK, NSEG, LANES), F32, any_spec)]
    scratch = [pltpu.VMEM((2, MOE_BLK, NSEG, LANES), F32), pltpu.VMEM((2, MOE_BLK, NSEG, LANES), F32),
               pltpu.SemaphoreType.DMA((2,)), pltpu.SemaphoreType.DMA((2,))]
    return _call(body, (nblk,), ins, outs, scratch=scratch, prefetch=[blk_expert, nvalid, slots], name="moe_ffn")[0]


def _moe_combine(st, yslot, rinfo_all, off, Mtot, alpha, ln_g, ln_b):
    tm = _pick_tile(st.M, off, Mtot, cap=st.tm)
    b0, b1, br = off // tm, (Mtot + off) // tm, off // tm

    def body(y0_ref, y1_ref, r_ref, x_ref, gate_ref, g_ref, b_ref, o_ref):
        r = r_ref[...]
        y = r[:, 2:3] * _from_tiles(y0_ref) + r[:, 3:4] * _from_tiles(y1_ref)
        o_ref[...] = _res_ln(alpha, y, x_ref[...], gate_ref[...], g_ref[...], b_ref[...])

    ins = [(yslot, pl.BlockSpec((tm, NSEG, LANES), lambda i: (b0 + i, 0, 0))),
           (yslot, pl.BlockSpec((tm, NSEG, LANES), lambda i: (b1 + i, 0, 0))),
           (rinfo_all, pl.BlockSpec((tm, LANES), lambda i: (br + i, 0))),
           _rows(st.x, tm), st.mod(5, tm), _full(ln_g), _full(ln_b)]
    return _call(body, (st.M // tm,), ins, [_rows_out(st.M, D, tm)], name="moe_combine")[0]


def _moe_layer(streams, layer, alpha, p):
    wgr = jnp.zeros((D, LANES), F32).at[:, :MOE_GROUPS].set(p["moe_w_group"][layer])
    wgr = wgr.at[:, MOE_GROUPS:MOE_GROUPS + MOE_EXPERTS].set(p["moe_w_router"][layer])
    bgr = jnp.zeros((1, LANES), F32).at[0, :MOE_GROUPS].set(p["moe_b_group"][layer])
    bgr = bgr.at[0, MOE_GROUPS:MOE_GROUPS + MOE_EXPERTS].set(p["moe_b_router"][layer])
    Mtot = sum(st.M for st in streams)
    h_all = jnp.zeros((Mtot, NSEG, LANES), F32)
    rs, off = [], 0
    for st in streams:
        h_all, r = _moe_router(st, wgr, bgr, h_all, off)
        rs.append(r)
        off += st.M
    rinfo = jnp.concatenate(rs, axis=0)
    R = _pick_tile(Mtot)
    counts = _moe_counts(rinfo, R)[:MOE_EXPERTS, 0].astype(I32)
    padded = (counts + MOE_BLK - 1) // MOE_BLK * MOE_BLK
    pad_end = jnp.cumsum(padded)
    pstart = jnp.zeros((LANES,), F32).at[:MOE_EXPERTS].set((pad_end - padded).astype(F32))
    pstart = jnp.broadcast_to(pstart[:, None], (LANES, LANES))
    nblk = -(-2 * Mtot // MOE_BLK) + MOE_EXPERTS
    blk_first = jnp.arange(nblk, dtype=I32) * MOE_BLK
    blk_expert = jnp.minimum(jnp.sum((pad_end[None, :] <= blk_first[:, None]).astype(I32), axis=1), MOE_EXPERTS - 1)
    nvalid = (pad_end[-1:] // MOE_BLK).astype(I32)
    dest = _moe_dest(rinfo, pstart, R).reshape(-1)
    spare = 2 * Mtot + jnp.arange(nblk * MOE_BLK, dtype=I32) % (2 * MOE_BLK)
    slots = spare.at[dest].set(jnp.arange(2 * Mtot, dtype=I32))
    yslot = _moe_ffn(h_all, slots, blk_expert, nvalid, p["moe_w1"], p["moe_w3"], p["moe_w2"], layer, Mtot)
    ln_g = p["ln_g"][layer, 1].reshape(1, D)
    ln_b = p["ln_b"][layer, 1].reshape(1, D)
    off = 0
    for st in streams:
        st.x = _moe_combine(st, yslot, rinfo, off, Mtot, alpha, ln_g, ln_b)
        off += st.M


def _scan_body(nbg, nv, tb, w_ref, kkn_ref, b_ref, k_ref, r_ref, vt_ref, s0_ref, hexp_ref, hsum_ref, bd_ref,
               o_ref, sf_ref, s_scr):
    t = pl.program_id(1)

    @pl.when(t == 0)
    def _():
        s_scr[...] = s0_ref[...]

    def sub(sb, carry):
        base = pl.multiple_of(sb * SUBLANES, SUBLANES)
        rows = [[ref[bb, pl.ds(base, SUBLANES), :] for bb in range(nbg)]
                for ref in (w_ref, kkn_ref, b_ref, k_ref, r_ref)]
        vt = vt_ref[:, sb].reshape(nbg * nv, LANES).astype(BF16)
        oacc = jnp.zeros((nbg * nv, LANES), F32)
        for j in range(SUBLANES):
            S = [s_scr[bb] for bb in range(nbg)]
            P = jnp.concatenate([S[bb] * rows[1][bb][j:j + 1] for bb in range(nbg)], axis=0).astype(BF16)
            sa = jnp.concatenate([_d(P[:, c0:c0 + MXU_TILE], bd_ref[...]) for c0 in range(0, D, MXU_TILE)], axis=1)
            vb = _d(vt, hexp_ref[j])
            P2 = []
            for bb in range(nbg):
                sl = slice(bb * nv, (bb + 1) * nv)
                Sn = S[bb] * rows[0][bb][j:j + 1] + sa[sl] * rows[2][bb][j:j + 1] + vb[sl] * rows[3][bb][j:j + 1]
                s_scr[bb] = Sn
                P2.append(Sn * rows[4][bb][j:j + 1])
            oacc = oacc + _d(jnp.concatenate(P2, axis=0), hsum_ref[j])
        o_ref[:, sb] = oacc.reshape(nbg, nv, LANES)
        return carry

    lax.fori_loop(0, tb // SUBLANES, sub, 0)

    @pl.when(t == pl.num_programs(1) - 1)
    def _():
        sf_ref[...] = s_scr[...]


def _delta_scan(w, kkn, b, k, r, v, S0, B, T, hsz):
    nh = D // hsz
    nv = hsz
    nbg = 4 if B % 4 == 0 else (2 if B % 2 == 0 else 1)
    tb = min(64, T)
    hexp, hsum, bd = _scan_consts(hsz)
    vt = v.reshape(B, T // SUBLANES, SUBLANES, nh, nv).transpose(0, 1, 4, 2, 3)
    vt = jnp.pad(vt, ((0, 0),) * 4 + ((0, 16 - nh),)).reshape(B, T // SUBLANES, nv, LANES)
    seq = lambda a: (a.reshape(B, T, D), pl.BlockSpec((nbg, tb, D), lambda g, t: (g, t, 0)))
    ins = [seq(w), seq(kkn), seq(b), seq(k), seq(r),
           (vt, pl.BlockSpec((nbg, tb // SUBLANES, nv, LANES), lambda g, t: (g, t, 0, 0))),
           (S0, pl.BlockSpec((nbg, nv, D), lambda g, t: (g, 0, 0))),
           _full(hexp), _full(hsum), _full(bd)]
    outs = [((B, T // SUBLANES, nv, LANES), F32,
             pl.BlockSpec((nbg, tb // SUBLANES, nv, LANES), lambda g, t: (g, t, 0, 0))),
            ((B, nv, D), F32, pl.BlockSpec((nbg, nv, D), lambda g, t: (g, 0, 0)))]
    body = functools.partial(_scan_body, nbg, nv, tb)
    op, sf = _call(body, (B // nbg, T // tb), ins, outs, scratch=[pltpu.VMEM((nbg, nv, D), F32)], name="delta_scan")
    o = op.reshape(B, T // SUBLANES, nv, SUBLANES, 16)[..., :nh].transpose(0, 1, 3, 4, 2).reshape(B * T, D)
    return o, sf


def _shifted_rows(h, first, period, shift=1):
    row = lax.broadcasted_iota(I32, h.shape, 0)
    return jnp.where(row % period < shift, first, pltpu.roll(h, shift, axis=0))


def _rwkv_prep(st, shift_prev, p, tm):
    long_seq = st.T % tm == 0
    tpb = st.T // tm if long_seq else 1
    hs, he = _seg_consts(RWKV_HSZ)
    row = lambda a: _full(a.reshape(1, D))
    wts = [_full(p["rwkv_mu"]), _full(p["rwkv_w_rkv"].astype(BF16)),
           _full(p["rwkv_w1"].astype(BF16)), _full(p["rwkv_w2"].astype(BF16)),
           _full(p["rwkv_a1"].astype(BF16)), _full(p["rwkv_a2"].astype(BF16)),
           _full(p["rwkv_g1"].astype(BF16)), _full(p["rwkv_g2"].astype(BF16)),
           row(p["rwkv_w0"]), row(p["rwkv_a0"]), row(p["rwkv_k_k"]), row(p["rwkv_k_a"]), _full(hs), _full(he)]
    if long_seq:
        nsub = tm // SUBLANES
        first_ins = [(st.x, pl.BlockSpec((SUBLANES, D), lambda i: (jnp.maximum(i * nsub - 1, 0), 0))),
                     (shift_prev.reshape(st.B, 1, D), pl.BlockSpec((None, 1, D), lambda i: (i // tpb, 0, 0)))]
    else:
        first_ins = [_rows(jnp.repeat(shift_prev, st.T, axis=0), tm)]
    nf = len(first_ins)

    def body(x_ref, sc_ref, sh_ref, *refs):
        first_refs, refs = refs[:nf], refs[nf:]
        (mu_ref, wrkv_ref, w1_ref, w2_ref, a1_ref, a2_ref, g1_ref, g2_ref, w0_ref, a0_ref, kk_ref, ka_ref,
         hs_ref, he_ref) = refs[:14]
        h_ref, r_ref, w_ref, k_ref, v_ref, kkn_ref, b_ref, g_ref = refs[14:]
        sc, sh = sc_ref[...], sh_ref[...]
        h = _modulate(x_ref[...], sc, sh)
        if long_seq:
            hh = _modulate(first_refs[0][...], sc, sh)[SUBLANES - 1:SUBLANES]
            first = jnp.where(pl.program_id(0) % tpb == 0, first_refs[1][...], hh)
            hprev = _shifted_rows(h, first, tm)
        else:
            hprev = _shifted_rows(h, first_refs[0][...], st.T)
        xx = hprev - h
        mu = mu_ref[...]
        xr, xw, xk, xv, xa, xg = [h + xx * mu[i:i + 1] for i in range(6)]
        r = _d(xr, wrkv_ref[0])
        k = _d(xk, wrkv_ref[1])
        v = _d(xv, wrkv_ref[2])
        logw = -_softplus(-(w0_ref[...] + _d(jnp.tanh(_d(xw, w1_ref[...])), w2_ref[...]))) - 0.5
        a = _sigmoid(a0_ref[...] + _d(_d(xa, a1_ref[...]), a2_ref[...]))
        g = _d(_sigmoid(_d(xg, g1_ref[...])), g2_ref[...])
        kk = k * kk_ref[...]
        inv = lax.rsqrt(_d_x3(kk * kk, hs_ref[...]) + 1e-6)
        kk = kk * _d_x3(inv, he_ref[...])
        h_ref[...] = h
        r_ref[...] = r
        w_ref[...] = jnp.exp(-jnp.exp(logw))
        k_ref[...] = k * (1.0 + (a - 1.0) * ka_ref[...])
        v_ref[...] = v
        kkn_ref[...] = -kk
        b_ref[...] = kk * a
        g_ref[...] = g

    ins = [_rows(st.x, tm), st.mod(1, tm), st.mod(0, tm)] + first_ins + wts
    outs = [_rows_out(st.M, D, tm) for _ in range(8)]
    return _call(body, (st.M // tm,), ins, outs, name="rwkv_prep")


def _rwkv_out(st, o, r, kmod, v, g, p, alpha, ln_g, ln_b, tm):
    hs, he = _seg_consts(RWKV_HSZ)
    inv_n = 1.0 / RWKV_HSZ

    def pro(o, r, k, v, g, lw, lb, rk, hs, he):
        mean = _d_x3(_d_x3(o, hs) * inv_n, he)
        c = o - mean
        rstd = lax.rsqrt(_d_x3(c * c, hs) * inv_n + RWKV_GN_EPS)
        on = c * _d_x3(rstd, he) * lw + lb
        bonus = _d_x3(_d_x3(r * k * rk, hs), he) * v
        return (on + bonus) * g

    def epi(acc, x, gate, g_, b_):
        return (_res_ln(alpha, acc, x, gate, g_, b_),)

    row = lambda a: _full(a.reshape(1, D))
    pro_ins = [_rows(a, tm) for a in (r, kmod, v, g)] + [row(p["rwkv_ln_w"]), row(p["rwkv_ln_b"]),
                                                       row(p["rwkv_r_k"]), _full(hs), _full(he)]
    epi_ins = [_rows(st.x, tm), st.mod(2, tm), _full(ln_g), _full(ln_b)]
    return _mm(o, _full(p["rwkv_w_o"].astype(BF16)), tm=tm, pro=pro, pro_ins=pro_ins, epi=epi, epi_ins=epi_ins,
               outs=[_rows_out(st.M, D, tm)], name="rwkv_out")[0]


def _rwkv_layer(st, shift_prev, wkv0, p, alpha, ln_g, ln_b):
    B, T = st.B, st.T
    tm = min(256, st.M)
    nh = D // RWKV_HSZ
    h, r, w, kmod, v, kkn, b, g = _rwkv_prep(st, shift_prev, p, tm)
    S0 = wkv0.transpose(0, 2, 1, 3).reshape(B, RWKV_HSZ, D)
    o, sf = _delta_scan(w, kkn, b, kmod, r, v, S0, B, T, RWKV_HSZ)
    st.x = _rwkv_out(st, o, r, kmod, v, g, p, alpha, ln_g, ln_b, tm)
    wkv = sf.reshape(B, RWKV_HSZ, nh, RWKV_HSZ).transpose(0, 2, 1, 3)
    return wkv, h.reshape(B, T, D)[:, -1]


def _pad_cols(w, n):
    return jnp.pad(w, ((0, 0), (0, n - w.shape[1])))


def _gdn_proj(st, p, tm):
    C = 3 * D
    w = _pad_cols(p["gdn_w_in"], C + D + LANES).astype(BF16)

    def epi(acc):
        return acc[:, :C], acc[:, C:C + D], acc[:, C + D:]

    outs = [_rows_out(st.M, C, tm), _rows_out(st.M, D, tm), _rows_out(st.M, LANES, tm)]
    return _mm(st.x, _full(w), tm=tm, pro=_modulate, pro_ins=[st.mod(1, tm), st.mod(0, tm)], epi=epi, outs=outs,
               name="gdn_proj")


def _gdn_conv(st, pre, ba, conv_buf, p, tm, chunked):
    C = 3 * D
    H = GDN_HEADS
    long_seq = st.T % tm == 0
    tpb = st.T // tm if long_seq else 1
    hs, he = _seg_consts(GDN_HSZ)
    hsn = _seg_np(GDN_HSZ)
    he_b = jnp.asarray(hsn.T, BF16)
    he_a = jnp.asarray(np.roll(hsn.T, H, axis=0), BF16)
    alog = jnp.zeros((1, LANES), F32).at[0, H:2 * H].set(p["gdn_A_log"])
    dtb = jnp.zeros((1, LANES), F32).at[0, H:2 * H].set(p["gdn_dt_bias"])
    if long_seq:
        nsub = tm // SUBLANES
        init8 = jnp.pad(conv_buf, ((0, 0), (SUBLANES - (GDN_CONV - 1), 0), (0, 0)))
        first_ins = [(pre, pl.BlockSpec((SUBLANES, C), lambda i: (jnp.maximum(i * nsub - 1, 0), 0))),
                     (init8, pl.BlockSpec((None, SUBLANES, C), lambda i: (i // tpb, 0, 0)))]
    else:
        padded = jnp.pad(conv_buf, ((0, 0), (0, st.T), (0, 0)))
        first_ins = [_rows(padded[:, GDN_CONV - 1 - j:GDN_CONV - 1 - j + st.T].reshape(st.M, C), tm)
                     for j in range(1, GDN_CONV)]
    nf = len(first_ins)

    def body(pre_ref, ba_ref, *refs):
        first_refs, refs = refs[:nf], refs[nf:]
        cw_ref, alog_ref, dtb_ref, hs_ref, he_ref, heb_ref, hea_ref = refs[:7]
        w_ref, kkn_ref, k_ref, q_ref, v_ref = refs[7:]
        x = pre_ref[...]
        if long_seq:
            halo = jnp.where(pl.program_id(0) % tpb == 0, first_refs[1][...], first_refs[0][...])
            big = jnp.concatenate([halo, x], axis=0)
            sh = [pltpu.roll(big, j, axis=0)[SUBLANES:] for j in range(1, GDN_CONV)]
        else:
            sh = [_shifted_rows(x, first_refs[j - 1][...], st.T, j) for j in range(1, GDN_CONV)]
        cw = cw_ref[...]
        conv = sh[2] * cw[0:1]
        conv = conv + sh[1] * cw[1:2]
        conv = conv + sh[0] * cw[2:3]
        conv = conv + x * cw[3:4]
        c = _silu(conv)
        q, k, v = c[:, :D], c[:, D:2 * D], c[:, 2 * D:]
        qn = q * _d_x3(lax.rsqrt(_d_x3(q * q, hs_ref[...]) + 1e-6), he_ref[...]) * (GDN_HSZ ** -0.5)
        kn = k * _d_x3(lax.rsqrt(_d_x3(k * k, hs_ref[...]) + 1e-6), he_ref[...])
        ba = ba_ref[...]
        logdecay = -jnp.exp(alog_ref[...]) * _softplus(ba + dtb_ref[...])
        if chunked:
            w_ref[...] = _sigmoid(ba)
            kkn_ref[...] = logdecay
            k_ref[...] = kn
            q_ref[...] = qn
            v_ref[...] = v
        else:
            beta = _d_x3(_sigmoid(ba), heb_ref[...])
            a = _d_x3(jnp.exp(logdecay), hea_ref[...])
            w_ref[...] = a
            kkn_ref[...] = -(a * beta) * kn
            k_ref[...] = kn
            q_ref[...] = qn
            v_ref[...] = beta * v

    ins = [_rows(pre, tm), _rows(ba, tm)] + first_ins + [_full(p["gdn_conv_w"]), _full(alog), _full(dtb), _full(hs),
                                                         _full(he), _full(he_b), _full(he_a)]
    small = LANES if chunked else D
    outs = [_rows_out(st.M, small, tm), _rows_out(st.M, small, tm)] + [_rows_out(st.M, D, tm) for _ in range(3)]
    return _call(body, (st.M // tm,), ins, outs, name="gdn_conv")


GDN_CHUNK = 64


def _gdn_chunk_scan(q, k, v, beta, g, S0, B, T):
    C = GDN_CHUNK
    H, N = GDN_HEADS, GDN_HSZ
    nchunk = T // C
    tril = _lower_tri(C)
    triu = jnp.asarray(np.triu(np.ones((C, C), np.float32)), BF16)

    def body(q_ref, k_ref, v_ref, b_ref, g_ref, s0_ref, tril_ref, triu_ref, o_ref, sf_ref, s_scr):
        c = pl.program_id(1)

        @pl.when(c == 0)
        def _():
            s_scr[...] = s0_ref[...]

        gblk = g_ref[...]
        gc = _d_3x(tril_ref[...], gblk)
        gh, gm, gl = _split3(gblk)
        tn = lambda a: lax.dot_general(a, triu_ref[...], (((0,), (0,)), ((), ())), preferred_element_type=F32)
        gct = tn(gh) + tn(gm) + tn(gl)
        bblk = b_ref[...]
        ri = lax.broadcasted_iota(I32, (C, C), 0)
        ci = lax.broadcasted_iota(I32, (C, C), 1)
        lower, strict = ri >= ci, ri > ci
        eye = jnp.where(ri == ci, 1.0, 0.0)
        heads = range(H)
        sl = [slice(h * N, (h + 1) * N) for h in heads]
        bcol = [bblk[:, h:h + 1] for h in heads]
        gcol = [gc[:, H + h:H + h + 1] for h in heads]
        gamma = [jnp.where(lower, jnp.exp(jnp.minimum(gcol[h] - gct[H + h:H + h + 1, :], 0.0)), 0.0) for h in heads]
        kb = [k_ref[:, sl[h]] * bcol[h] for h in heads]
        pw = [jnp.where(strict, _d_nt(kb[h], k_ref[:, sl[h]]) * gamma[h], 0.0) for h in heads]
        t_inv = [eye - pw[h] for h in heads]
        for _ in range(int(math.log2(C)) - 1):
            pw = [_d(pw[h], pw[h]) for h in heads]
            t_inv = [t_inv[h] + _d(t_inv[h], pw[h]) for h in heads]
        eg = [jnp.exp(gcol[h]) for h in heads]
        u = [_d(t_inv[h], v_ref[:, sl[h]] * bcol[h]) for h in heads]
        w = [_d(t_inv[h], kb[h] * eg[h]) for h in heads]
        qk = [jnp.where(lower, _d_nt(q_ref[:, sl[h]], k_ref[:, sl[h]]) * gamma[h], 0.0) for h in heads]
        v_new = [u[h] - _d(w[h], s_scr[h]) for h in heads]
        for h in heads:
            o_ref[:, sl[h]] = _d(q_ref[:, sl[h]] * eg[h], s_scr[h]) + _d(qk[h], v_new[h])
        for h in heads:
            g_last = gcol[h][C - 1:C]
            kd = (k_ref[:, sl[h]] * jnp.exp(g_last - gcol[h])).astype(BF16)
            s_scr[h] = s_scr[h] * jnp.exp(g_last) + lax.dot_general(kd, v_new[h].astype(BF16), (((0,), (0,)), ((), ())),
                                                                   preferred_element_type=F32)

        @pl.when(c == nchunk - 1)
        def _():
            sf_ref[...] = s_scr[...]

    seq = lambda a, w: (a, pl.BlockSpec((C, w), lambda b, c: (b * nchunk + c, 0)))
    ins = [seq(q, D), seq(k, D), seq(v, D), seq(beta, LANES), seq(g, LANES),
           (S0, pl.BlockSpec((None, H, N, N), lambda b, c: (b, 0, 0, 0))),
           (tril, pl.BlockSpec(tril.shape, lambda b, c: (0, 0))), (triu, pl.BlockSpec(triu.shape, lambda b, c: (0, 0)))]
    outs = [((B * T, D), F32, pl.BlockSpec((C, D), lambda b, c: (b * nchunk + c, 0))),
            ((B, H, N, N), F32, pl.BlockSpec((None, H, N, N), lambda b, c: (b, 0, 0, 0)))]
    return _call(body, (B, nchunk), ins, outs, scratch=[pltpu.VMEM((H, N, N), F32)], name="gdn_chunk_scan")


def _gdn_out(st, o, z, p, alpha, ln_g, ln_b, tm):
    hs, he = _seg_consts(GDN_HSZ)
    nw = jnp.tile(p["gdn_norm_w"], GDN_HEADS).reshape(1, D)

    def pro(o, z, nw, hs, he):
        rstd = lax.rsqrt(_d_x3(o * o, hs) * (1.0 / GDN_HSZ) + 1e-6)
        return o * _d_x3(rstd, he) * nw * _silu(z)

    def epi(acc, x, gate, g_, b_):
        return (_res_ln(alpha, acc, x, gate, g_, b_),)

    return _mm(o, _full(p["gdn_w_o"].astype(BF16)), tm=tm, pro=pro, pro_ins=[_rows(z, tm), _full(nw), _full(hs), _full(he)],
               epi=epi, epi_ins=[_rows(st.x, tm), st.mod(2, tm), _full(ln_g), _full(ln_b)],
               outs=[_rows_out(st.M, D, tm)], name="gdn_out")[0]


def _gdn_layer(st, conv_buf, S0, p, alpha, ln_g, ln_b):
    B, T = st.B, st.T
    tm = min(256, st.M)
    pre, z, ba = _gdn_proj(st, p, tm)
    chunked = T % GDN_CHUNK == 0
    if chunked:
        beta, g, kn, qn, v = _gdn_conv(st, pre, ba, conv_buf, p, tm, True)
        o, S = _gdn_chunk_scan(qn, kn, v, beta, g, S0, B, T)
    else:
        w, kkn, kn, qn, vb = _gdn_conv(st, pre, ba, conv_buf, p, tm, False)
        S0t = S0.transpose(0, 3, 1, 2).reshape(B, GDN_HSZ, D)
        o, sf = _delta_scan(w, kkn, kn, kn, qn, vb, S0t, B, T, GDN_HSZ)
        S = sf.reshape(B, GDN_HSZ, GDN_HEADS, GDN_HSZ).transpose(0, 2, 3, 1)
    st.x = _gdn_out(st, o, z, p, alpha, ln_g, ln_b, tm)
    xpad = jnp.concatenate([conv_buf, pre.reshape(B, T, 3 * D)[:, -(GDN_CONV - 1):]], axis=1)
    return S, xpad[:, -(GDN_CONV - 1):]


def _flash_body(cfg, *refs):
    tq, hq, hk = cfg["tq"], cfg["hq"], cfg["hk"]
    fox, bias, aug, window = cfg["fox"], cfg["bias"], cfg["aug"], cfg["window"]
    tk = tq
    G = hq // hk
    Kc = HD + aug
    R = G * tq
    refs = list(refs)
    q_ref, k_ref, v_ref = refs[:3]
    pos = 3
    if aug:
        mb_ref, e_ref = refs[pos:pos + 2]
        pos += 2
    if bias:
        tz_ref = refs[pos]
        pos += 1
    if fox:
        cq_ref, ck_ref = refs[pos:pos + 2]
        pos += 2
    o_ref, kb, vb, s_scr = refs[pos:pos + 4]
    g = pl.program_id(1)
    qi = pl.program_id(2)

    @pl.when(qi == 0)
    def _():
        vb[...] = v_ref[...].astype(BF16)
        if aug:
            k = k_ref[...]
            kb[...] = jnp.concatenate(
                [jnp.concatenate([k[:, kv * HD:(kv + 1) * HD].astype(BF16), e_ref[...]], axis=1) for kv in range(hk)],
                axis=1)
        else:
            kb[...] = k_ref[...].astype(BF16)

    scale = HD ** -0.5
    q = q_ref[...]
    row_t = lax.broadcasted_iota(I32, (R, tk), 0) % tq
    col_s = lax.broadcasted_iota(I32, (R, tk), 1)
    qs, cqs = [], []
    for kv in range(hk):
        x = jnp.concatenate([q[:, (kv * G + gg) * HD:(kv * G + gg + 1) * HD] for gg in range(G)], axis=0) * scale
        if aug:
            x = jnp.concatenate([x.astype(BF16), jnp.concatenate([mb_ref[kv]] * G, axis=0)], axis=1)
        qs.append(x.astype(BF16))
        if fox:
            lane = lax.broadcasted_iota(I32, (tq, LANES), 1)
            cqs.append(jnp.sum(jnp.where(lane == g * hk + kv, cq_ref[...], 0.0), axis=-1, keepdims=True))

    def logits(kv, c, rel, valid):
        off = pl.multiple_of(c * tk, tk)
        s = _d_nt(qs[kv], kb[pl.ds(off, tk), kv * Kc:(kv + 1) * Kc])
        if fox:
            sub = lax.broadcasted_iota(I32, (FOX_HEADS, tk), 0)
            ck = jnp.sum(jnp.where(sub == g * hk + kv, ck_ref[:, pl.ds(off, tk)], 0.0), axis=0, keepdims=True)
            s = s + cqs[kv] - ck
        if bias and rel in (0, 1):
            s = s + jnp.concatenate([tz_ref[kv * G + gg, rel] for gg in range(G)], axis=0)
        if rel == 0:
            s = jnp.where(row_t >= col_s, s, NEG)
        if rel == 3:
            s = jnp.where(col_s > row_t, s, NEG)
        if valid is not None:
            s = jnp.where(valid, s, NEG)
        return s

    nt = tk // LANES

    def lane_tiles(x):
        return [x[:, j * LANES:(j + 1) * LANES] for j in range(nt)]

    if window is not None:
        nch = window // tk
        static = [(qi, 0, None)] + [(jnp.maximum(qi - dc, 0), 1 if dc == 1 else (3 if dc == nch else 2), qi - dc >= 0)
                                    for dc in range(1, nch + 1)]
        n_far = 0
    elif bias:
        static = [(qi, 0, None), (jnp.maximum(qi - 1, 0), 1, qi >= 1)]
        n_far = jnp.maximum(qi - 1, 0)
    else:
        static = [(qi, 0, None)]
        n_far = qi
    n_static = len(static)
    unroll = cfg["unroll"]

    def far_loop(fn, carry):
        ng = n_far // unroll

        def group(gi, cr):
            for u in range(unroll):
                cr = fn(gi * unroll + u, cr)
            return cr

        carry = lax.fori_loop(0, ng, group, carry)
        return lax.fori_loop(ng * unroll, n_far, fn, carry)

    outs = []
    for kv in range(hk):
        def score(c, rel, valid, slot, m128):
            s = logits(kv, c, rel, valid)
            s_scr[slot] = s
            for t in lane_tiles(s):
                m128 = jnp.maximum(m128, t)
            return m128

        m128 = jnp.full((R, LANES), NEG, F32)
        for slot, (c, rel, valid) in enumerate(static):
            m128 = score(c, rel, valid, slot, m128)
        if window is None:
            m128 = far_loop(lambda c, m: score(c, 2, None, n_static + c, m), m128)
        mrep = jnp.broadcast_to(jnp.max(m128, axis=-1, keepdims=True), (R, LANES))

        def accumulate(c, slot, carry):
            l128, acc = carry
            p = [jnp.exp(t - mrep) for t in lane_tiles(s_scr[slot])]
            for t in p:
                l128 = l128 + t
            off = pl.multiple_of(c * tk, tk)
            pm = jnp.concatenate(p, axis=1) if nt > 1 else p[0]
            return l128, acc + _d(pm, vb[pl.ds(off, tk), kv * HD:(kv + 1) * HD])

        carry = (jnp.zeros((R, LANES), F32), jnp.zeros((R, HD), F32))
        for slot, (c, rel, valid) in enumerate(static):
            carry = accumulate(c, slot, carry)
        if window is None:
            carry = far_loop(lambda c, cr: accumulate(c, n_static + c, cr), carry)
        l128, acc = carry
        o = acc / jnp.sum(l128, axis=-1, keepdims=True)
        outs += [o[gg * tq:(gg + 1) * tq] for gg in range(G)]
    o_ref[...] = jnp.concatenate(outs, axis=1)


def _flash(cfg, B, T, ngroups, q_in, k_in, v_in, extra_ins, M):
    tq, hq, hk = cfg["tq"], cfg["hq"], cfg["hk"]
    Kc = HD + cfg["aug"]
    ins = [q_in, k_in, v_in] + list(extra_ins)
    nq = T // tq
    R = (hq // hk) * tq
    if cfg["window"] is not None:
        nslots = cfg["window"] // tq + 1
    else:
        nslots = nq + (1 if cfg["bias"] else 0)
    outs = [((M, ngroups * hq * HD), F32, pl.BlockSpec((tq, hq * HD), lambda b, g, i: (b * nq + i, g)))]
    scratch = [pltpu.VMEM((T, hk * Kc), BF16), pltpu.VMEM((T, hk * HD), BF16), pltpu.VMEM((nslots, R, tq), F32)]
    return _call(functools.partial(_flash_body, cfg), (B, ngroups, nq), ins, outs, scratch=scratch,
                 name="flash_" + cfg["name"])[0]


def _log_sigmoid(x):
    return -_softplus(-x)


def _fox_proj(st, p, tm):
    hw = FOX_HEADS * HD
    w = _pad_cols(p["fox_w_in"], 3 * hw + LANES).astype(BF16)
    bf = jnp.zeros((1, LANES), F32).at[0, :FOX_HEADS].set(p["fox_b_f"])

    def epi(acc, bf):
        return acc[:, :hw], acc[:, hw:3 * hw], _log_sigmoid(acc[:, 3 * hw:] + bf)

    outs = [_rows_out(st.M, hw, tm), _rows_out(st.M, 2 * hw, tm), _rows_out(st.M, LANES, tm)]
    return _mm(st.x, _full(w), tm=tm, pro=_modulate, pro_ins=[st.mod(1, tm), st.mod(0, tm)], epi=epi,
               epi_ins=[_full(bf)], outs=outs, name="fox_proj")


def _lower_tri(n):
    return jnp.asarray(np.tril(np.ones((n, n), np.float32)), BF16)


def _cumsum_rows(x, B, T):
    ch = _pick_tile(T)
    tri = _lower_tri(ch)

    def body(x_ref, tri_ref, o_ref):
        carry = jnp.zeros((1, LANES), F32)
        for c in range(T // ch):
            cc = _d_3x(tri_ref[...], x_ref[c * ch:(c + 1) * ch, :]) + carry
            o_ref[c * ch:(c + 1) * ch, :] = cc
            carry = cc[ch - 1:ch, :]

    return _call(body, (B,), [_rows(x, T), _full(tri)], [_rows_out(B * T, LANES, T)], name="cumsum_rows")[0]


def _out_proj(st, o, w_o, alpha, ln_g, ln_b, tm, name):
    def epi(acc, x, gate, g_, b_):
        return (_res_ln(alpha, acc, x, gate, g_, b_),)

    return _mm(o, _full(w_o.astype(BF16)), tm=tm, epi=epi,
               epi_ins=[_rows(st.x, tm), st.mod(2, tm), _full(ln_g), _full(ln_b)],
               outs=[_rows_out(st.M, D, tm)], name=name)[0]


def _fox_prompt(st, p, alpha, ln_g, ln_b):
    B, T, M = st.B, st.T, st.M
    tm = min(256, M)
    q, kv, logf = _fox_proj(st, p, tm)
    cum = _cumsum_rows(logf, B, T)
    ckT = cum.reshape(B, T, LANES)[:, :, :FOX_HEADS].transpose(0, 2, 1)
    tq = min(512, T)
    nq = T // tq
    cfg = dict(name="fox", tq=tq, hq=2, hk=2, fox=True, bias=False, aug=0, window=None, unroll=2)
    npair = FOX_HEADS // 2
    q_in = (q, pl.BlockSpec((tq, 2 * HD), lambda b, g, i: (b * nq + i, g)))
    k_in = (kv, pl.BlockSpec((T, 2 * HD), lambda b, g, i: (b, g)))
    v_in = (kv, pl.BlockSpec((T, 2 * HD), lambda b, g, i: (b, npair + g)))
    extra = [(cum, pl.BlockSpec((tq, LANES), lambda b, g, i: (b * nq + i, 0))),
             (ckT, pl.BlockSpec((None, FOX_HEADS, T), lambda b, g, i: (b, 0, 0)))]
    o = _flash(cfg, B, T, npair, q_in, k_in, v_in, extra, M)
    st.x = _out_proj(st, o, p["fox_w_o"], alpha, ln_g, ln_b, tm, "fox_out")
    return kv.reshape(B, T, 2, FOX_HEADS, HD), logf.reshape(B, T, LANES)[:, :, :FOX_HEADS]


def _page_ins(cache, page_shape, npages, first_of_step):
    nd = len(page_shape)
    return [(cache, pl.BlockSpec((None,) + tuple(page_shape),
                                 lambda b, s, pt, j=j: (pt[b, first_of_step(s) + j],) + (0,) * nd))
            for j in range(npages)]


def _fox_cum_sample(logf_new, cache_logf, page_table, Tn):
    B, npg = page_table.shape
    PAGE = cache_logf.shape[1]
    H = cache_logf.shape[2]
    cache_t = cache_logf.transpose(0, 2, 1)
    triu = jnp.asarray(np.triu(np.ones((PAGE, PAGE), np.float32)), BF16)

    def body(pt_ref, *refs):
        pages, new_ref, tri_ref, o_ref = refs[:npg], refs[npg], refs[npg + 1], refs[npg + 2]
        carry = jnp.zeros((H, 1), F32)
        for j in range(npg):
            cc = _d_x3(pages[j][...], tri_ref[...]) + carry
            o_ref[:, j * PAGE:(j + 1) * PAGE] = cc
            carry = cc[:, PAGE - 1:PAGE]
        xn = jnp.concatenate([new_ref[...], jnp.zeros((PAGE - Tn, LANES), F32)], axis=0).T[:H, :]
        o_ref[:, npg * PAGE:(npg + 1) * PAGE] = _d_x3(xn, tri_ref[...]) + carry

    ins = _page_ins(cache_t, (H, PAGE), npg, lambda s: 0)
    ins += [(logf_new, pl.BlockSpec((Tn, LANES), lambda b, s, pt: (b, 0))),
            (triu, pl.BlockSpec(triu.shape, lambda b, s, pt: (0, 0)))]
    Lp = (npg + 1) * PAGE
    outs = [((B, H, Lp), F32, pl.BlockSpec((None, H, Lp), lambda b, s, pt: (b, 0, 0)))]
    return _call(body, (B, 1), ins, outs, prefetch=[page_table], name="fox_cum_sample")[0]


def _rep_mat(n_rows, n_src, per):
    r = np.arange(n_rows)
    src = r // per if per else r % n_src
    return jnp.asarray((src[:, None] == np.arange(n_src)[None, :]).astype(np.float32), BF16)


def _fox_decode(q, kv_new, cache_kv, page_table, cq, ckT, Tn):
    B, npg = page_table.shape
    PAGE, H = cache_kv.shape[1], FOX_HEADS
    hw = H * HD
    R = H * Tn
    pps = PAGES_PER_STEP if npg % PAGES_PER_STEP == 0 else 1
    nsteps = npg // pps
    cache_t = cache_kv.transpose(0, 2, 3, 4, 1).reshape(cache_kv.shape[0], 2, hw, PAGE)
    rep_t = _rep_mat(R, Tn, 0)
    rep_h = _rep_mat(R, H, Tn)
    scale = HD ** -0.5

    def body(pt_ref, *refs):
        pages = refs[:pps]
        q_ref, new_ref, cq_ref, ck_ref, rt_ref, rh_ref, o_ref, qbd, m_s, l_s, acc = refs[pps:]
        s_id = pl.program_id(1)
        own = lax.broadcasted_iota(I32, (R, hw), 0) // Tn == lax.broadcasted_iota(I32, (R, hw), 1) // HD

        @pl.when(s_id == 0)
        def _():
            qbd[...] = jnp.where(own, _d(rt_ref[...], q_ref[...]) * scale, 0.0).astype(BF16)
            m_s[...] = jnp.full(m_s.shape, NEG, F32)
            l_s[...] = jnp.zeros(l_s.shape, F32)
            acc[...] = jnp.zeros(acc.shape, F32)

        cqv = cq_ref[...]

        def update(s, pv):
            m = m_s[:, 0:1]
            m2 = jnp.maximum(m, jnp.max(s, axis=-1, keepdims=True))
            a = jnp.exp(m - m2)
            pr = jnp.exp(s - m2)
            l_s[...] = jnp.broadcast_to(a * l_s[:, 0:1] + jnp.sum(pr, axis=-1, keepdims=True), l_s.shape)
            m_s[...] = jnp.broadcast_to(m2, m_s.shape)
            acc[...] = a * acc[...] + pv(pr)

        kt = jnp.concatenate([pages[j][0].astype(BF16) for j in range(pps)], axis=1)
        vt = jnp.concatenate([pages[j][1].astype(BF16) for j in range(pps)], axis=1)
        off = pl.multiple_of(s_id * (pps * PAGE), pps * PAGE)
        update(_d(qbd[...], kt) + cqv - _d_3x(rh_ref[...], ck_ref[:, pl.ds(off, pps * PAGE)]),
               lambda pr: _d_nt(pr, vt))

        @pl.when(s_id == nsteps - 1)
        def _():
            new = jnp.concatenate([new_ref[...], jnp.zeros((PAGE - Tn, 2 * hw), F32)], axis=0)
            t_row = lax.broadcasted_iota(I32, (R, PAGE), 0) % Tn
            col = lax.broadcasted_iota(I32, (R, PAGE), 1)
            s_new = _d_nt(qbd[...], new[:, :hw]) + cqv - _d_3x(rh_ref[...], ck_ref[:, pl.ds(npg * PAGE, PAGE)])
            update(jnp.where(col <= t_row, s_new, NEG), lambda pr: _d(pr, new[:, hw:]))
            of = jnp.where(own, acc[...] / l_s[:, 0:1], 0.0)
            out = of[0:Tn]
            for h in range(1, H):
                out = out + of[h * Tn:(h + 1) * Tn]
            o_ref[...] = out

    Lp = ckT.shape[2]
    const = lambda a: (a, pl.BlockSpec(a.shape, lambda b, s, pt: (0,) * a.ndim))
    ins = _page_ins(cache_t, (2, hw, PAGE), pps, lambda s: s * pps)
    ins += [(q, pl.BlockSpec((Tn, hw), lambda b, s, pt: (b, 0))),
            (kv_new, pl.BlockSpec((Tn, 2 * hw), lambda b, s, pt: (b, 0))),
            (cq, pl.BlockSpec((None, R, 1), lambda b, s, pt: (b, 0, 0))),
            (ckT, pl.BlockSpec((None, H, Lp), lambda b, s, pt: (b, 0, 0))),
            const(rep_t), const(rep_h)]
    outs = [((B * Tn, hw), F32, pl.BlockSpec((Tn, hw), lambda b, s, pt: (b, 0)))]
    scratch = [pltpu.VMEM((R, hw), BF16), pltpu.VMEM((R, LANES), F32), pltpu.VMEM((R, LANES), F32),
               pltpu.VMEM((R, hw), F32)]
    return _call(body, (B, nsteps), ins, outs, scratch=scratch, prefetch=[page_table], name="fox_decode")[0]


def _fox_sample(st, cache_kv, cache_logf, page_table, p, alpha, ln_g, ln_b):
    B, Tn, M = st.B, st.T, st.M
    tm = min(256, M)
    npg = page_table.shape[1]
    PAGE = cache_kv.shape[1]
    q, kv, logf = _fox_proj(st, p, tm)
    ckT = _fox_cum_sample(logf, cache_logf, page_table, Tn)
    cq = ckT[:, :, npg * PAGE:npg * PAGE + Tn].reshape(B, FOX_HEADS * Tn, 1)
    o = _fox_decode(q, kv, cache_kv, page_table, cq, ckT, Tn)
    st.x = _out_proj(st, o, p["fox_w_o"], alpha, ln_g, ln_b, tm, "fox_out")
    return kv.reshape(B, Tn, 2, FOX_HEADS, HD), logf.reshape(B, Tn, LANES)[:, :, :FOX_HEADS]


KVW = NSA_KVH * HD
HALF = CMP_BLK // 2


def _t5_bucket(dist):
    exact = REL_BUCKETS // 2
    d = jnp.maximum(dist, 0)
    far = exact + (jnp.log(jnp.maximum(d, 1).astype(F32) / exact) / math.log(REL_MAX_DIST / exact)
                   * (REL_BUCKETS - exact)).astype(I32)
    return jnp.where(d < exact, d, jnp.minimum(far, REL_BUCKETS - 1))


def _rel_bias(table, dist):
    return jnp.moveaxis(table[_t5_bucket(dist)], -1, 0)


def _nsa_proj(st, p, tm):
    qw = NSA_HEADS * HD
    w = _pad_cols(p["nsa_w_in"], qw + 6 * KVW + LANES).astype(BF16)

    def epi(acc):
        return (acc[:, :qw], acc[:, qw:qw + 2 * KVW], acc[:, qw + 2 * KVW:qw + 4 * KVW],
                acc[:, qw + 4 * KVW:qw + 6 * KVW], acc[:, qw + 6 * KVW:])

    outs = [_rows_out(st.M, qw, tm)] + [_rows_out(st.M, 2 * KVW, tm)] * 3 + [_rows_out(st.M, LANES, tm)]
    return _mm(st.x, _full(w), tm=tm, pro=_modulate, pro_ins=[st.mod(1, tm), st.mod(0, tm)], epi=epi, outs=outs,
               name="nsa_proj")


def _cmp_weights(p):
    eye = jnp.eye(NSA_KVH, dtype=F32)
    wk = jnp.einsum("ab,vlde->vladbe", eye, p["nsa_cmp_w1"]).reshape(2, CMP_BLK, KVW, KVW)
    wc = wk.reshape(2, 2, HALF, KVW, KVW).transpose(1, 2, 0, 3, 4)
    w2c = jnp.einsum("ab,vde->vadbe", eye, p["nsa_cmp_w2"]).reshape(2, KVW, KVW)
    b1 = jnp.tile(p["nsa_cmp_b1"][:, None, :], (1, NSA_KVH, 1)).reshape(1, 2 * KVW)
    return wc.astype(BF16), w2c.astype(BF16), b1


def _compress_body(nx, *refs):
    x_refs = refs[:nx]
    wc_ref, w2_ref, b1_ref, o_ref, ua, ub = refs[nx:]
    rows = ua.shape[1]
    nl = 2 * KVW // LANES
    acc = [[jnp.zeros((rows, KVW), F32) for _ in range(2)] for _ in range(2)]
    for l in range(HALF):
        for kv in range(2):
            lo = l * 2 * KVW + kv * KVW
            piece = [r[:, lo:lo + KVW] for r in x_refs]
            piece = (jnp.concatenate(piece, axis=0) if nx > 1 else piece[0]).astype(BF16)
            for half in range(2):
                acc[half][kv] = acc[half][kv] + _d(piece, wc_ref[half, l, kv])
    for scr, a in ((ua, acc[0]), (ub, acc[1])):
        full = jnp.concatenate(a, axis=1)
        for c in range(nl):
            scr[c] = full[:, c * LANES:(c + 1) * LANES]
    hid = jnp.concatenate([ua[c, pl.ds(0, rows // 2, stride=2), :] + ub[c, pl.ds(1, rows // 2, stride=2), :]
                           for c in range(nl)], axis=1)
    hid = _gelu_tanh(hid + b1_ref[...])
    o_ref[...] = jnp.concatenate([_d(hid[:, :KVW], w2_ref[0]), _d(hid[:, KVW:], w2_ref[1])], axis=1)


def _compress_dense(rows_kv, cw):
    wc, w2c, b1 = cw
    M = rows_kv.shape[0]
    x = rows_kv.reshape(M // HALF, HALF * 2 * KVW)
    nh = M // HALF
    th = _pick_tile(nh, cap=128)
    ins = [_rows(x, th), _full(wc), _full(w2c), _full(b1)]
    outs = [_rows_out(nh // 2, 2 * KVW, th // 2)]
    scratch = [pltpu.VMEM((2 * KVW // LANES, th, LANES), F32)] * 2
    return _call(functools.partial(_compress_body, 1), (nh // th,), ins, outs, scratch=scratch, name="nsa_compress")[0]


def _pair_mat(nc, ns):
    n = np.arange(nc)
    return jnp.asarray((n[:, None] // (SEL_BLK // CMP_BLK) == np.arange(ns)[None, :]).astype(np.float32), BF16)


def _top_blocks(score, n_sel):
    lane = lax.broadcasted_iota(I32, score.shape, 1)
    big = jnp.int32(1 << 20)
    sel = jnp.zeros(score.shape, jnp.bool_)
    work = score
    for _ in range(n_sel):
        m = jnp.max(work, axis=-1, keepdims=True)
        idx = jnp.min(jnp.where(work == m, lane, big), axis=-1, keepdims=True)
        hit = lane == idx
        sel = sel | hit
        work = jnp.where(hit, -3e38, work)
    return jnp.where(sel, 0.0, SEL_NEG)


def _masked_softmax(s, mask):
    s = jnp.where(mask, s, NEG)
    m = jnp.max(s, axis=-1, keepdims=True)
    p = jnp.where(mask, jnp.exp(s - m), 0.0)
    l = jnp.sum(p, axis=-1, keepdims=True)
    return p / jnp.where(l > 0.0, l, 1.0)


def _block_scores(imp, tpos, ns):
    blk = lax.broadcasted_iota(I32, imp.shape, 1)
    cur = tpos // SEL_BLK
    forced = (blk == 0) | (blk == cur) | (blk == cur - 1)
    score = jnp.where(forced, FORCE_SCORE, imp)
    return jnp.where(blk * SEL_BLK > tpos, -1.0, score)


CMP_NEAR_LO = -3
CMP_NEAR_N = 8


def _cmp_bias_pattern(table, tq):
    assert tq == LANES and CMP_BLK == 32 and REL_MAX_DIST == LANES
    r = jnp.arange(tq, dtype=I32)[:, None]
    m = CMP_NEAR_LO + jnp.arange(CMP_NEAR_N, dtype=I32)[None, :]
    near = _rel_bias(table, r - (CMP_BLK - 1) + CMP_BLK * m)
    far = jnp.broadcast_to(table[REL_BUCKETS - 1][:, None, None], (table.shape[1], tq, 1))
    return jnp.pad(jnp.concatenate([near, far], axis=2), ((0, 0), (0, 0), (0, LANES - CMP_NEAR_N - 1)))


def _nsa_cmp_prompt(q, kcvc, pat, B, T, tq):
    nc, ns = T // CMP_BLK, -(-T // SEL_BLK)
    n_sel = min(N_SEL, ns)
    nq = T // tq
    G = NSA_G
    pair = _pair_mat(nc, ns)
    scale = HD ** -0.5
    rb = tq // CMP_BLK

    def body(q_ref, kc_ref, vc_ref, b_ref, pair_ref, o_ref, mb_ref):
        qi = pl.program_id(1)
        q = q_ref[...]
        R = G * tq
        tpos = qi * tq + lax.broadcasted_iota(I32, (R, 1), 0) % tq
        cmp_end = lax.broadcasted_iota(I32, (R, nc), 1) * CMP_BLK + (CMP_BLK - 1)
        mask = cmp_end <= tpos
        j = lax.broadcasted_iota(I32, (LANES, nc), 0)
        m = rb * qi - lax.broadcasted_iota(I32, (LANES, nc), 1)
        sel = ((j < CMP_NEAR_N) & (m == j + CMP_NEAR_LO)) | ((j == CMP_NEAR_N) & (m >= CMP_NEAR_LO + CMP_NEAR_N))
        sel = jnp.where(sel, 1.0, 0.0).astype(BF16)
        outs = []
        for kv in range(NSA_KVH):
            qs = jnp.concatenate([q[:, (kv * G + gg) * HD:(kv * G + gg + 1) * HD] for gg in range(G)], axis=0) * scale
            s = _d_nt(qs, kc_ref[:, kv * HD:(kv + 1) * HD])
            s = s + _d_x3(jnp.concatenate([b_ref[kv * G + gg] for gg in range(G)], axis=0), sel)
            pc = _masked_softmax(s, mask)
            oc = _d(pc, vc_ref[:, kv * HD:(kv + 1) * HD])
            outs += [oc[gg * tq:(gg + 1) * tq] for gg in range(G)]
            imp = pc[0:tq]
            for gg in range(1, G):
                imp = imp + pc[gg * tq:(gg + 1) * tq]
            score = _block_scores(_d_x3(imp, pair_ref[...]), tpos[0:tq], ns)
            mb_ref[kv] = _top_blocks(score, n_sel).astype(mb_ref.dtype)
        o_ref[...] = jnp.concatenate(outs, axis=1)

    ins = [(q, pl.BlockSpec((tq, NSA_HEADS * HD), lambda b, i: (b * nq + i, 0))),
           (kcvc, pl.BlockSpec((nc, KVW), lambda b, i: (b, 0))),
           (kcvc, pl.BlockSpec((nc, KVW), lambda b, i: (b, 1))),
           (pat, pl.BlockSpec(pat.shape, lambda b, i: (0, 0, 0))),
           _full(pair)]
    outs = [((B * T, NSA_HEADS * HD), F32, pl.BlockSpec((tq, NSA_HEADS * HD), lambda b, i: (b * nq + i, 0))),
            ((B, NSA_KVH, T, ns), BF16, pl.BlockSpec((None, NSA_KVH, tq, ns), lambda b, i: (b, 0, i, 0)))]
    return _call(body, (B, nq), ins, outs, name="nsa_cmp_select")


def _gate_mats():
    hsn = _seg_np(HD)
    return [jnp.asarray(np.roll(hsn.T, br * NSA_HEADS, axis=0), BF16) for br in range(3)]


def _nsa_out(st, o_c, o_s, o_w, gates, p, alpha, ln_g, ln_b, tm):
    def pro(oc, os_, ow, gl, e0, e1, e2):
        sg = _sigmoid(gl)
        return _d_x3(sg, e0) * oc + _d_x3(sg, e1) * os_ + _d_x3(sg, e2) * ow

    def epi(acc, x, gate, g_, b_):
        return (_res_ln(alpha, acc, x, gate, g_, b_),)

    pro_ins = [_rows(o_s, tm), _rows(o_w, tm), _rows(gates, tm)] + [_full(e) for e in _gate_mats()]
    return _mm(o_c, _full(p["nsa_w_o"].astype(BF16)), tm=tm, pro=pro, pro_ins=pro_ins, epi=epi,
               epi_ins=[_rows(st.x, tm), st.mod(2, tm), _full(ln_g), _full(ln_b)],
               outs=[_rows_out(st.M, D, tm)], name="nsa_out")[0]


def _nsa_prompt(st, p, alpha, ln_g, ln_b):
    B, T, M = st.B, st.T, st.M
    tm = min(256, M)
    tq = ATT_T
    nq = T // tq
    table = p["rel_bias"]
    q, cmp_rows, slc_rows, win_rows, gates = _nsa_proj(st, p, tm)
    kcvc = _compress_dense(cmp_rows, _cmp_weights(p))
    nc, ns = T // CMP_BLK, -(-T // SEL_BLK)
    o_c, mb = _nsa_cmp_prompt(q, kcvc, _cmp_bias_pattern(table, tq), B, T, tq)
    r = jnp.arange(tq, dtype=I32)
    far = table[REL_BUCKETS - 1][:, None, None]
    tz = jnp.stack([_rel_bias(table, r[:, None] - r[None, :]) - far,
                    _rel_bias(table, tq + r[:, None] - r[None, :]) - far], axis=1)
    e_blk = jnp.asarray((np.arange(T)[:, None] // SEL_BLK == np.arange(ns)[None, :]).astype(np.float32), BF16)
    q_in = (q, pl.BlockSpec((tq, NSA_HEADS * HD), lambda b, g, i: (b * nq + i, 0)))
    kv_in = lambda a, c: (a, pl.BlockSpec((T, KVW), lambda b, g, i: (b, c)))
    tz_in = (tz, pl.BlockSpec(tz.shape, lambda b, g, i: (0, 0, 0, 0)))
    cfg = dict(name="nsa_slc", tq=tq, hq=NSA_HEADS, hk=NSA_KVH, fox=False, bias=True, aug=ns, window=None, unroll=4)
    extra = [(mb, pl.BlockSpec((None, NSA_KVH, tq, ns), lambda b, g, i: (b, 0, i, 0))),
             (e_blk, pl.BlockSpec(e_blk.shape, lambda b, g, i: (0, 0))), tz_in]
    o_s = _flash(cfg, B, T, 1, q_in, kv_in(slc_rows, 0), kv_in(slc_rows, 1), extra, M)
    cfg = dict(name="nsa_win", tq=tq, hq=NSA_HEADS, hk=NSA_KVH, fox=False, bias=True, aug=0, window=WINDOW, unroll=1)
    o_w = _flash(cfg, B, T, 1, q_in, kv_in(win_rows, 0), kv_in(win_rows, 1), [tz_in], M)
    st.x = _nsa_out(st, o_c, o_s, o_w, gates, p, alpha, ln_g, ln_b, tm)
    shp = (B, T, 2, NSA_KVH, HD)
    keep = min(WINDOW, T)
    return cmp_rows.reshape(shp), slc_rows.reshape(shp), win_rows.reshape(shp)[:, T - keep:]


def _compress_paged(cache_cmp, page_table, cw):
    wc, w2c, b1 = cw
    B, npg = page_table.shape
    PAGE = cache_cmp.shape[1]
    bpp = PAGE // CMP_BLK
    cache = cache_cmp.transpose(0, 2, 3, 4, 1).reshape(cache_cmp.shape[0], 2, KVW, PAGE)
    nb = 2 if B % 2 == 0 else 1
    npages = nb * npg
    nblocks = npages * bpp
    npan = KVW // LANES

    def body(pt_ref, *refs):
        pages = refs[:npages]
        wc_ref, w2_ref, b1_ref, o_ref, xs = refs[npages:]
        for j in range(npages):
            for kv in range(2):
                x = pages[j][kv].T
                for c in range(npan):
                    xs[kv, c, j * PAGE:(j + 1) * PAGE, :] = x[:, c * LANES:(c + 1) * LANES]
        acc = [jnp.zeros((nblocks, KVW), F32) for _ in range(2)]
        for l in range(CMP_BLK):
            for kv in range(2):
                rows = jnp.concatenate([xs[kv, c, pl.ds(l, nblocks, stride=CMP_BLK), :] for c in range(npan)], axis=1)
                acc[kv] = acc[kv] + _d(rows, wc_ref[l // HALF, l % HALF, kv])
        hid = _gelu_tanh(jnp.concatenate(acc, axis=1) + b1_ref[...])
        o_ref[...] = jnp.concatenate([_d(hid[:, :KVW], w2_ref[0]), _d(hid[:, KVW:], w2_ref[1])], axis=1)

    ins = [(cache, pl.BlockSpec((None, 2, KVW, PAGE), lambda g, pt, bb=bb, j=j: (pt[g * nb + bb, j], 0, 0, 0)))
           for bb in range(nb) for j in range(npg)]
    const = lambda a: (a, pl.BlockSpec(a.shape, lambda g, pt: (0,) * a.ndim))
    ins += [const(wc), const(w2c), const(b1)]
    outs = [((B * npg * bpp, 2 * KVW), F32, pl.BlockSpec((nblocks, 2 * KVW), lambda g, pt: (g, 0)))]
    scratch = [pltpu.VMEM((2, npan, npages * PAGE, LANES), F32)]
    return _call(body, (B // nb,), ins, outs, scratch=scratch, prefetch=[page_table], name="nsa_compress_paged")[0]


def _nsa_decode(q, slc_new, win_new, kcvc, cache_slc, cache_win, page_table, gcol, consts, Tn, offset):
    B, npg = page_table.shape
    PAGE = cache_slc.shape[1]
    Wb = cache_win.shape[1]
    R = NSA_HEADS * Tn
    L = offset + Tn
    nc, ns = L // CMP_BLK, -(-L // SEL_BLK)
    n_sel = min(N_SEL, ns)
    nck = npg + 1
    qw = NSA_HEADS * HD
    cache = cache_slc.transpose(0, 2, 3, 4, 1).reshape(cache_slc.shape[0], 2, KVW, PAGE)
    win = cache_win.transpose(0, 2, 3, 4, 1).reshape(B, 2, KVW, Wb)
    scale = HD ** -0.5
    names = ["rep_t", "fold", "unfold", "bias_c", "mask_c", "pair", "e_blk", "bias_s", "bias_w"]
    cvals = [consts[n] for n in names]

    def body(pt_ref, *refs):
        pages = refs[:npg]
        (q_ref, sn_ref, wn_ref, kc_ref, vc_ref, win_ref, g_ref, rt_ref, fold_ref, unfold_ref, bc_ref, mc_ref, pair_ref,
         e_ref, bs_ref, bw_ref, o_ref, wout_ref) = refs[npg:]
        row_h = lax.broadcasted_iota(I32, (R, qw), 0) // Tn
        lane_h = lax.broadcasted_iota(I32, (R, qw), 1) // HD
        own = row_h == lane_h
        qrep = _d(rt_ref[...], q_ref[...])
        qbd = (_d(jnp.where(own, qrep, 0.0), fold_ref[...]) * scale).astype(BF16)
        pad = lambda x: jnp.concatenate([x, jnp.zeros((PAGE - Tn, x.shape[1]), x.dtype)], axis=0)

        pc = _masked_softmax(_d_nt(qbd, kc_ref[...]) + bc_ref[...], mc_ref[...] > 0.0)
        o_c = _d(pc, vc_ref[...])
        imp = []
        for kv in range(NSA_KVH):
            a = pc[kv * NSA_G * Tn:(kv * NSA_G + 1) * Tn]
            for gg in range(1, NSA_G):
                a = a + pc[(kv * NSA_G + gg) * Tn:(kv * NSA_G + gg + 1) * Tn]
            imp.append(a)
        imp = jnp.concatenate(imp, axis=0)
        tpos = offset + lax.broadcasted_iota(I32, (NSA_KVH * Tn, 1), 0) % Tn
        mb = _top_blocks(_block_scores(_d_x3(imp, pair_ref[...]), tpos, ns), n_sel)
        mb = jnp.concatenate([mb[kv * Tn:(kv + 1) * Tn] for kv in range(NSA_KVH) for _ in range(NSA_G)], axis=0)

        sn = pad(sn_ref[...])
        past = npg * PAGE
        kt = jnp.concatenate([pages[j][0].astype(BF16) for j in range(npg)], axis=1)
        vt = jnp.concatenate([pages[j][1].astype(BF16) for j in range(npg)], axis=1)
        s = jnp.concatenate([_d(qbd, kt), _d_nt(qbd, sn[:, :KVW])], axis=1) + _d(mb, e_ref[...]) + bs_ref[...]
        p = jnp.exp(s - jnp.max(s, axis=-1, keepdims=True))
        o_s = (_d_nt(p[:, :past], vt) + _d(p[:, past:], sn[:, KVW:])) / jnp.sum(p, axis=-1, keepdims=True)

        wk, wv = win_ref[0], win_ref[1]
        wn = pad(wn_ref[...])
        s = jnp.concatenate([_d(qbd, wk), _d_nt(qbd, wn[:, :KVW])], axis=1) + bw_ref[...]
        p = jnp.exp(s - jnp.max(s, axis=-1, keepdims=True))
        o_w = (_d_nt(p[:, :Wb], wv) + _d(p[:, Wb:], wn[:, KVW:])) / jnp.sum(p, axis=-1, keepdims=True)
        lane = lax.broadcasted_iota(I32, (KVW, Wb), 1)
        for kv, old in enumerate((wk, wv)):
            nt = pltpu.roll(wn[:, kv * KVW:(kv + 1) * KVW].T, PAGE - Tn, axis=1)
            nt = jnp.concatenate([jnp.zeros((KVW, Wb - PAGE), F32), nt], axis=1)
            wout_ref[kv] = jnp.where(lane >= Wb - Tn, nt, pltpu.roll(old, Wb - Tn, axis=1))

        sg = _sigmoid(g_ref[...])
        o = sg[:, 0:1] * o_c + sg[:, 1:2] * o_s + sg[:, 2:3] * o_w
        of = jnp.where(own, _d_x3(o, unfold_ref[...]), 0.0)
        out = of[0:Tn]
        for h in range(1, NSA_HEADS):
            out = out + of[h * Tn:(h + 1) * Tn]
        o_ref[...] = out

    c2 = lambda b, pt: (0, 0)
    ins = [(cache, pl.BlockSpec((None, 2, KVW, PAGE), lambda b, pt, j=j: (pt[b, j], 0, 0, 0))) for j in range(npg)]
    ins += [(q, pl.BlockSpec((Tn, qw), lambda b, pt: (b, 0))),
            (slc_new, pl.BlockSpec((Tn, 2 * KVW), lambda b, pt: (b, 0))),
            (win_new, pl.BlockSpec((Tn, 2 * KVW), lambda b, pt: (b, 0))),
            (kcvc, pl.BlockSpec((nc, KVW), lambda b, pt: (b, 0))),
            (kcvc, pl.BlockSpec((nc, KVW), lambda b, pt: (b, 1))),
            (win, pl.BlockSpec((None, 2, KVW, Wb), lambda b, pt: (b, 0, 0, 0))),
            (gcol, pl.BlockSpec((None, R, 3), lambda b, pt: (b, 0, 0)))]
    ins += [(a, pl.BlockSpec(a.shape, c2)) for a in cvals]
    outs = [((B * Tn, qw), F32, pl.BlockSpec((Tn, qw), lambda b, pt: (b, 0))),
            ((B, 2, KVW, Wb), F32, pl.BlockSpec((None, 2, KVW, Wb), lambda b, pt: (b, 0, 0, 0)))]
    return _call(body, (B,), ins, outs, prefetch=[page_table], name="nsa_decode")


def _nsa_decode_consts(table, Tn, offset, npg, PAGE, Wb):
    R = NSA_HEADS * Tn
    L = offset + Tn
    nc, ns = L // CMP_BLK, -(-L // SEL_BLK)
    Lp = (npg + 1) * PAGE
    tpos = offset + jnp.arange(Tn, dtype=I32)
    rows = lambda x: x.reshape(R, x.shape[-1])
    cmp_end = jnp.arange(nc, dtype=I32) * CMP_BLK + CMP_BLK - 1
    dist_c = tpos[:, None] - cmp_end[None, :]
    spos = jnp.arange(Lp, dtype=I32)
    dist_s = tpos[:, None] - spos[None, :]
    ok_s = (dist_s >= 0) & (spos[None, :] < L)
    col = jnp.arange(Wb + PAGE, dtype=I32)
    wpos = offset - Wb + col
    dist_w = tpos[:, None] - wpos[None, :]
    ok_w = (dist_w >= 0) & (dist_w < WINDOW) & (wpos[None, :] >= 0) & (col[None, :] < Wb + Tn)
    tile = lambda m: jnp.tile(m[None], (NSA_HEADS, 1, 1))
    fold = np.zeros((NSA_HEADS, HD, NSA_KVH, HD), np.float32)
    for h in range(NSA_HEADS):
        fold[h, :, h // NSA_G, :] = np.eye(HD)
    fold = fold.reshape(NSA_HEADS * HD, KVW)
    return dict(
        rep_t=_rep_mat(R, Tn, 0), fold=jnp.asarray(fold, BF16), unfold=jnp.asarray(fold.T, BF16),
        bias_c=rows(_rel_bias(table, dist_c)), mask_c=rows(tile((dist_c >= 0).astype(F32))), pair=_pair_mat(nc, ns),
        e_blk=jnp.asarray((np.arange(ns)[:, None] == np.arange(Lp)[None, :] // SEL_BLK).astype(np.float32), BF16),
        bias_s=rows(_rel_bias(table, dist_s) + tile(jnp.where(ok_s, 0.0, NEG))),
        bias_w=rows(_rel_bias(table, dist_w) + tile(jnp.where(ok_w, 0.0, NEG))))


def _nsa_sample(st, cache_cmp, cache_slc, cache_win, page_table, p, alpha, ln_g, ln_b):
    B, Tn, M = st.B, st.T, st.M
    tm = min(256, M)
    npg = page_table.shape[1]
    PAGE = cache_slc.shape[1]
    Wb = cache_win.shape[1]
    offset = npg * PAGE
    assert offset % CMP_BLK == 0 and Tn < CMP_BLK and Wb == WINDOW and Tn % SUBLANES == 0
    q, cmp_rows, slc_rows, win_rows, gates = _nsa_proj(st, p, tm)
    kcvc = _compress_paged(cache_cmp, page_table, _cmp_weights(p))
    gcol = gates.reshape(B, Tn, LANES)[:, :, :3 * NSA_HEADS].reshape(B, Tn, 3, NSA_HEADS)
    gcol = gcol.transpose(0, 3, 1, 2).reshape(B, NSA_HEADS * Tn, 3)
    consts = _nsa_decode_consts(p["rel_bias"], Tn, offset, npg, PAGE, Wb)
    o, wout = _nsa_decode(q, slc_rows, win_rows, kcvc, cache_slc, cache_win, page_table, gcol, consts, Tn, offset)
    st.x = _out_proj(st, o, p["nsa_w_o"], alpha, ln_g, ln_b, tm, "nsa_out_s")
    shp = (B, Tn, 2, NSA_KVH, HD)
    win_keep = wout.reshape(B, 2, NSA_KVH, HD, Wb).transpose(0, 4, 1, 2, 3)
    return cmp_rows.reshape(shp), slc_rows.reshape(shp), win_keep


def kernel(x_prompt, x_sample, state_rwkv_wkv, state_rwkv_shift, cache_nsa_cmp, cache_nsa_slc, cache_nsa_win, cache_fox_kv, cache_fox_logf, state_gdn_S, state_gdn_conv, page_table, c_prompt, c_sample, w_mod, b_mod, ln_g, ln_b, moe_w_group, moe_b_group, moe_w_router, moe_b_router, moe_w1, moe_w3, moe_w2, rwkv_mu, rwkv_w_rkv, rwkv_w0, rwkv_w1, rwkv_w2, rwkv_a0, rwkv_a1, rwkv_a2, rwkv_g1, rwkv_g2, rwkv_k_k, rwkv_k_a, rwkv_r_k, rwkv_ln_w, rwkv_ln_b, rwkv_w_o, nsa_w_in, nsa_cmp_w1, nsa_cmp_b1, nsa_cmp_w2, nsa_w_o, rel_bias, fox_w_in, fox_b_f, fox_w_o, gdn_w_in, gdn_conv_w, gdn_A_log, gdn_dt_bias, gdn_norm_w, gdn_w_o):
    p = dict(locals())
    Bp, T, _ = x_prompt.shape
    Bs, Tn, _ = x_sample.shape
    depth = w_mod.shape[0]
    alpha = (2 * depth) ** 0.25
    sp = _Stream(x_prompt.reshape(Bp * T, D), Bp, T, min(512, T))
    ss = _Stream(x_sample.reshape(Bs * Tn, D), Bs, Tn, min(256, Bs * Tn))
    nc = Bp + Bs
    c_all = jnp.pad(jnp.concatenate([c_prompt, c_sample], axis=0), ((0, -nc % SUBLANES), (0, 0)))
    out = {}
    for layer in range(depth):
        m6 = _ada(c_all, w_mod, b_mod, layer)
        sp.set_mods(m6[:Bp])
        ss.set_mods(m6[Bp:nc])
        g0 = ln_g[layer, 0].reshape(1, D)
        b0 = ln_b[layer, 0].reshape(1, D)
        kind = layer % 4
        if kind == 0:
            nh = D // RWKV_HSZ
            out["wkv_p"], out["shift_p"] = _rwkv_layer(sp, jnp.zeros((Bp, D), F32),
                                                       jnp.zeros((Bp, nh, RWKV_HSZ, RWKV_HSZ), F32), p, alpha, g0, b0)
            out["wkv_s"], out["shift_s"] = _rwkv_layer(ss, state_rwkv_shift, state_rwkv_wkv, p, alpha, g0, b0)
        elif kind == 1:
            out["cmp_p"], out["slc_p"], out["win_p"] = _nsa_prompt(sp, p, alpha, g0, b0)
            out["cmp_s"], out["slc_s"], out["win_s"] = _nsa_sample(ss, cache_nsa_cmp, cache_nsa_slc, cache_nsa_win,
                                                                   page_table, p, alpha, g0, b0)
        elif kind == 2:
            out["kv_p"], out["logf_p"] = _fox_prompt(sp, p, alpha, g0, b0)
            out["kv_s"], out["logf_s"] = _fox_sample(ss, cache_fox_kv, cache_fox_logf, page_table, p, alpha, g0, b0)
        else:
            out["S_p"], out["conv_p"] = _gdn_layer(sp, jnp.zeros((Bp, GDN_CONV - 1, 3 * D), F32),
                                                   jnp.zeros((Bp, GDN_HEADS, GDN_HSZ, GDN_HSZ), F32), p, alpha, g0, b0)
            out["S_s"], out["conv_s"] = _gdn_layer(ss, state_gdn_conv, state_gdn_S, p, alpha, g0, b0)
        _moe_layer([sp, ss], layer, alpha, p)
    return (sp.x.reshape(Bp, T, D), ss.x.reshape(Bs, Tn, D), out["wkv_p"], out["wkv_s"], out["shift_p"], out["shift_s"],
            out["cmp_p"], out["cmp_s"], out["slc_p"], out["slc_s"], out["win_p"], out["win_s"],
            out["kv_p"], out["kv_s"], out["logf_p"], out["logf_s"], out["S_p"], out["S_s"], out["conv_p"], out["conv_s"])
```

```python
import functools
import math

import numpy as np
import jax
import jax.numpy as jnp
from jax import lax
from jax.experimental import pallas as pl
from jax.experimental.pallas import tpu as pltpu

F32 = jnp.float32
BF16 = jnp.bfloat16
I32 = jnp.int32
NEG = -1e30
LN_EPS = 1e-5
D = 1024
LANES = 128
SUBLANES = 8
MXU_TILE = 256
VMEM_LIMIT_MB = 56

RWKV_HSZ = 64
RWKV_GN_EPS = 64e-5
NSA_HEADS, NSA_KVH, HD = 16, 4, 64
NSA_G = NSA_HEADS // NSA_KVH
CMP_BLK, SEL_BLK, N_SEL, WINDOW = 32, 64, 16, 512
FORCE_SCORE = 1e4
REL_BUCKETS, REL_MAX_DIST = 32, 128
FOX_HEADS = 16
GDN_HEADS, GDN_HSZ, GDN_CONV = 8, 128, 4
MOE_GROUPS, MOE_EPG, MOE_BLK = 4, 8, 256
MOE_EXPERTS = MOE_GROUPS * MOE_EPG
ATT_T = 128
SEL_NEG = -65536.0
PAGES_PER_STEP = 8


def _d(a, b):
    return jnp.dot(a.astype(BF16), b.astype(BF16), preferred_element_type=F32)


def _d_nt(a, b):
    return lax.dot_general(a.astype(BF16), b.astype(BF16), (((1,), (1,)), ((), ())),
                           preferred_element_type=F32)


def _split3(x):
    h = x.astype(BF16)
    r1 = x - h.astype(F32)
    m = r1.astype(BF16)
    l = (r1 - m.astype(F32)).astype(BF16)
    return h, m, l


def _d_x3(x, sel):
    h, m, l = _split3(x)
    return _d(h, sel) + _d(m, sel) + _d(l, sel)


def _d_3x(sel, x):
    h, m, l = _split3(x)
    return _d(sel, h) + _d(sel, m) + _d(sel, l)


def _d_f32(x, w):
    xh = x.astype(BF16)
    xl = (x - xh.astype(F32)).astype(BF16)
    wh = w.astype(BF16)
    wl = (w - wh.astype(F32)).astype(BF16)
    return _d(xh, wh) + (_d(xh, wl) + _d(xl, wh))


def _sigmoid(x):
    return 1.0 / (1.0 + jnp.exp(-x))


def _softplus(x):
    return jnp.maximum(x, 0.0) + jnp.log(1.0 + jnp.exp(-jnp.abs(x)))


def _silu(x):
    return x * _sigmoid(x)


def _gelu_tanh(x):
    return 0.5 * x * (1.0 + jnp.tanh(math.sqrt(2.0 / math.pi) * (x + 0.044715 * (x * x * x))))


def _layer_norm(z, g, b):
    mu = jnp.mean(z, axis=-1, keepdims=True)
    zc = z - mu
    var = jnp.mean(zc * zc, axis=-1, keepdims=True)
    return zc * lax.rsqrt(var + LN_EPS) * g + b


def _modulate(x, sc, sh):
    return x * (1.0 + sc) + sh


def _res_ln(alpha, y, xres, gate, g, b):
    return _layer_norm(alpha * xres + (1.0 + gate) * y, g, b)


@functools.lru_cache(maxsize=None)
def _seg_np(hsz):
    head = np.arange(D) // hsz
    hs = (head[:, None] == np.arange(LANES)[None, :]).astype(np.float32)
    return hs


def _seg_consts(hsz):
    hs = _seg_np(hsz)
    return jnp.asarray(hs, BF16), jnp.asarray(hs.T, BF16)


def _scan_consts(hsz):
    nh = D // hsz
    head = np.arange(D) // hsz
    slot_j = np.arange(LANES) // 16
    slot_h = np.arange(LANES) % 16
    hexp = np.zeros((SUBLANES, LANES, D), np.float32)
    for j in range(SUBLANES):
        hexp[j] = ((slot_j[:, None] == j) & (slot_h[:, None] == head[None, :]) & (slot_h[:, None] < nh))
    hsum = np.transpose(hexp, (0, 2, 1))
    blk = np.arange(MXU_TILE) // hsz
    bd = (blk[:, None] == blk[None, :]).astype(np.float32)
    return jnp.asarray(hexp, BF16), jnp.asarray(hsum, BF16), jnp.asarray(bd, BF16)


def _call(body, grid, ins, outs, scratch=(), name=None, prefetch=None, aliases=None):
    arrays = [a for a, _ in ins]
    in_specs = [s for _, s in ins]
    out_shape = [jax.ShapeDtypeStruct(s, d) for s, d, _ in outs]
    out_specs = [s for _, _, s in outs]
    params = pltpu.CompilerParams(dimension_semantics=("arbitrary",) * len(grid),
                                  vmem_limit_bytes=VMEM_LIMIT_MB << 20)
    kw = {}
    if aliases:
        kw["input_output_aliases"] = aliases
    if prefetch is None:
        fn = pl.pallas_call(body, grid=grid, in_specs=in_specs, out_specs=out_specs, out_shape=out_shape,
                            scratch_shapes=list(scratch), compiler_params=params, name=name, **kw)
        res = fn(*arrays)
    else:
        gs = pltpu.PrefetchScalarGridSpec(num_scalar_prefetch=len(prefetch), grid=grid, in_specs=in_specs,
                                          out_specs=out_specs, scratch_shapes=list(scratch))
        fn = pl.pallas_call(body, grid_spec=gs, out_shape=out_shape, compiler_params=params, name=name, **kw)
        res = fn(*prefetch, *arrays)
    return list(res)


def _full(a):
    nd = a.ndim
    return (a, pl.BlockSpec(a.shape, lambda *_: (0,) * nd))


def _rows(a, tm):
    return (a, pl.BlockSpec((tm, a.shape[1]), lambda i, *_: (i, 0)))


def _rows_out(M, C, tm, dtype=F32):
    return ((M, C), dtype, pl.BlockSpec((tm, C), lambda i, *_: (i, 0)))


class _Stream:
    def __init__(self, x, B, T, tm):
        self.x, self.B, self.T, self.tm = x, B, T, tm
        self.M = B * T
        self.m6 = None

    def set_mods(self, m6):
        self.m6 = m6
        self.rep = jnp.repeat(m6, self.T, axis=0) if self.T < self.tm else None

    def mod(self, c, tm=None):
        tm = tm or self.tm
        if self.T % tm == 0:
            tpb = self.T // tm
            a = self.m6[:, c * D:(c + 1) * D].reshape(self.B, 1, D)
            return (a, pl.BlockSpec((None, 1, D), lambda i, *_: (i // tpb, 0, 0)))
        assert tm % self.T == 0 and self.M % tm == 0
        a = self.rep[:, c * D:(c + 1) * D].reshape(self.M // tm, tm, D)
        return (a, pl.BlockSpec((None, tm, D), lambda i, *_: (i, 0, 0)))


def _mm(x, w_in, *, tm, pro=None, pro_ins=(), epi=None, epi_ins=(), outs=None, name="mm"):
    M, K = x.shape
    n_pro, n_epi = len(pro_ins), len(epi_ins)
    n_out = len(outs)

    def body(*refs):
        x_ref = refs[0]
        pro_refs = refs[1:1 + n_pro]
        w_ref = refs[1 + n_pro]
        epi_refs = refs[2 + n_pro:2 + n_pro + n_epi]
        out_refs = refs[2 + n_pro + n_epi:2 + n_pro + n_epi + n_out]
        a = x_ref[...]
        if pro is not None:
            a = pro(a, *[r[...] for r in pro_refs])
        acc = _d(a, w_ref[...])
        res = epi(acc, *[r[...] for r in epi_refs]) if epi is not None else (acc,)
        for o, r in zip(out_refs, res):
            o[...] = r.astype(o.dtype)

    ins = [_rows(x, tm)] + list(pro_ins) + [w_in] + list(epi_ins)
    return _call(body, (M // tm,), ins, outs, name=name)


def _ada(c_all, w_mod, b_mod, layer):
    Mp = c_all.shape[0]
    N = w_mod.shape[2]
    tn = 1536

    def body(c_ref, w_ref, b_ref, o_ref):
        o_ref[...] = _d(_silu(c_ref[...]), w_ref[...]) + b_ref[...]

    ins = [(c_all, pl.BlockSpec((Mp, D), lambda j: (0, 0))),
           (w_mod, pl.BlockSpec((None, D, tn), lambda j: (layer, 0, j))),
           (b_mod.reshape(b_mod.shape[0], 1, N), pl.BlockSpec((None, 1, tn), lambda j: (layer, 0, j)))]
    outs = [((Mp, N), F32, pl.BlockSpec((Mp, tn), lambda j: (0, j)))]
    return _call(body, (N // tn,), ins, outs, name="ada_mod")[0]


def _pick_tile(*sizes, cap=512):
    t = cap
    while t > SUBLANES and any(s % t for s in sizes):
        t //= 2
    assert all(s % t == 0 for s in sizes), sizes
    return t


def _route(lg):
    lane = lax.broadcasted_iota(I32, lg.shape, 1)
    big = jnp.int32(1 << 20)
    isg = lane < MOE_GROUPS
    gl = jnp.where(isg, lg, NEG)
    gmax = jnp.max(gl, axis=-1, keepdims=True)
    gsel = jnp.min(jnp.where(gl == gmax, lane, big), axis=-1, keepdims=True)
    gsum = jnp.sum(jnp.where(isg, jnp.exp(gl - gmax), 0.0), axis=-1, keepdims=True)
    gw = 1.0 / gsum
    lo = MOE_GROUPS + MOE_EPG * gsel
    ise = (lane >= lo) & (lane < lo + MOE_EPG)
    el = jnp.where(ise, lg, NEG)
    emax = jnp.max(el, axis=-1, keepdims=True)
    ep = jnp.where(ise, jnp.exp(el - emax), 0.0)
    prob = ep / jnp.sum(ep, axis=-1, keepdims=True)
    pm = jnp.where(ise, prob, -1.0)
    p1 = jnp.max(pm, axis=-1, keepdims=True)
    i1 = jnp.min(jnp.where(pm == p1, lane, big), axis=-1, keepdims=True)
    pm2 = jnp.where(lane == i1, -1.0, pm)
    p2 = jnp.max(pm2, axis=-1, keepdims=True)
    i2 = jnp.min(jnp.where(pm2 == p2, lane, big), axis=-1, keepdims=True)
    den = p1 + p2
    w1 = gw * p1 / den
    w2 = gw * p2 / den
    e1 = (i1 - MOE_GROUPS).astype(F32)
    e2 = (i2 - MOE_GROUPS).astype(F32)
    return jnp.where(lane == 0, e1, jnp.where(lane == 1, e2, jnp.where(lane == 2, w1, jnp.where(lane == 3, w2, 0.0))))


NSEG = D // LANES


def _to_tiles(ref, x):
    for s in range(NSEG):
        ref[:, s, :] = x[:, s * LANES:(s + 1) * LANES]


def _from_tiles(ref):
    return jnp.concatenate([ref[:, s, :] for s in range(NSEG)], axis=1)


def _moe_router(st, wgr, bgr, h_all, off):
    tm = _pick_tile(st.M, off, cap=st.tm)
    b0 = off // tm

    def body(x_ref, sc_ref, sh_ref, w_ref, b_ref, hin_ref, h_ref, r_ref):
        h = _modulate(x_ref[...], sc_ref[...], sh_ref[...])
        _to_tiles(h_ref, h)
        r_ref[...] = _route(_d_f32(h, w_ref[...]) + b_ref[...])

    ins = [_rows(st.x, tm), st.mod(4, tm), st.mod(3, tm), _full(wgr), _full(bgr),
           (h_all, pl.BlockSpec(memory_space=pl.ANY))]
    outs = [(h_all.shape, F32, pl.BlockSpec((tm, NSEG, LANES), lambda i: (b0 + i, 0, 0))),
            _rows_out(st.M, LANES, tm)]
    return _call(body, (st.M // tm,), ins, outs, aliases={5: 0}, name="moe_router")


def _moe_counts(rinfo, R):
    Mtot = rinfo.shape[0]
    nt = Mtot // R

    def body(r_ref, o_ref):
        j = pl.program_id(0)
        t = pl.program_id(1)

        @pl.when((j == 0) & (t == 0))
        def _():
            o_ref[...] = jnp.zeros_like(o_ref)

        xt = r_ref[...].T
        row = jnp.where(j == 0, xt[0:1, :], xt[1:2, :])
        sub = lax.broadcasted_iota(I32, (LANES, R), 0).astype(F32)
        oh = jnp.where(sub == row, 1.0, 0.0)
        o_ref[...] += jnp.sum(oh, axis=1, keepdims=True)

    ins = [(rinfo, pl.BlockSpec((R, LANES), lambda j, t: (t, 0)))]
    outs = [((LANES, LANES), F32, pl.BlockSpec((LANES, LANES), lambda j, t: (0, 0)))]
    return _call(body, (2, nt), ins, outs, name="moe_counts")[0]


def _moe_dest(rinfo, pstart, R):
    Mtot = rinfo.shape[0]
    nt = Mtot // R
    upper = jnp.asarray(np.triu(np.ones((R, R), np.float32), 1), BF16)

    def body(r_ref, p_ref, u_ref, o_ref, carry):
        j = pl.program_id(0)
        t = pl.program_id(1)

        @pl.when((j == 0) & (t == 0))
        def _():
            carry[...] = jnp.zeros_like(carry)

        xt = r_ref[...].T
        row = jnp.where(j == 0, xt[0:1, :], xt[1:2, :])
        sub = lax.broadcasted_iota(I32, (LANES, R), 0).astype(F32)
        oh = jnp.where(sub == row, 1.0, 0.0)
        cum = _d(oh, u_ref[...])
        base = carry[:, 0:1] + p_ref[:, 0:1]
        dest = jnp.sum(oh * (cum + base), axis=0, keepdims=True)
        o_ref[...] = dest.astype(I32)
        carry[...] += jnp.sum(oh, axis=1, keepdims=True)

    ins = [(rinfo, pl.BlockSpec((R, LANES), lambda j, t: (t, 0))), _full(pstart), _full(upper)]
    outs = [((2 * nt, 1, R), I32, pl.BlockSpec((None, 1, R), lambda j, t: (j * nt + t, 0, 0)))]
    return _call(body, (2, nt), ins, outs, scratch=[pltpu.VMEM((LANES, LANES), F32)], name="moe_dest")[0]


def _moe_ffn(h_all, slots, blk_expert, nvalid, w1, w3, w2, layer, Mtot):
    nblk = slots.shape[0] // MOE_BLK
    FF = w1.shape[-1]
    any_spec = pl.BlockSpec(memory_space=pl.ANY)
    GRP = SUBLANES

    def body(be_ref, nv_ref, slot_ref, h_ref, w1_ref, w3_ref, w2_ref, y_ref, xbuf, ybuf, sem_in, sem_out):
        i = pl.program_id(0)
        nv = nv_ref[0]
        buf = i % 2

        def row_copy(kind, blk, b, r):
            s = slot_ref[blk * MOE_BLK + r]
            if kind == "gather":
                tok = jnp.where(s >= 2 * Mtot, s - 2 * Mtot, jnp.where(s >= Mtot, s - Mtot, s))
                return pltpu.make_async_copy(h_ref.at[tok], xbuf.at[b, r], sem_in.at[b])
            return pltpu.make_async_copy(ybuf.at[b, r], y_ref.at[s], sem_out.at[b])

        def each_row(kind, blk, b, start):
            for r in range(MOE_BLK):
                cp = row_copy(kind, blk, b, r)
                cp.start() if start else cp.wait()

        @pl.when(i < nv)
        def _():
            @pl.when(i == 0)
            def _():
                ybuf[...] = jnp.zeros_like(ybuf)
                for b in range(2):
                    cp = pltpu.make_async_copy(ybuf.at[b], y_ref.at[pl.ds(2 * Mtot + b * MOE_BLK, MOE_BLK)],
                                               sem_out.at[b])
                    cp.start()
                    cp.wait()
                each_row("gather", 0, 0, True)

            nxt = jnp.minimum(i + 1, nv - 1)
            each_row("gather", i, buf, False)
            x = _from_tiles(xbuf.at[buf]).astype(BF16)
            each_row("gather", nxt, 1 - buf, True)
            a = _d(x, w1_ref[...])
            b = _d(x, w3_ref[...])
            y = _d(_silu(a) * b, w2_ref[...])

            @pl.when(i >= 2)
            def _():
                each_row("scatter", i - 2, buf, False)

            _to_tiles(ybuf.at[buf], y)
            each_row("scatter", i, buf, True)

            @pl.when(i == nv - 1)
            def _():
                each_row("gather", nxt, 1 - buf, False)

                @pl.when(i >= 1)
                def _():
                    each_row("scatter", i - 1, 1 - buf, False)

                each_row("scatter", i, buf, False)

    def blk(i, be, nv, sl):
        return be[jnp.minimum(i, nv[0] - 1)]

    ins = [(h_all, any_spec),
           (w1, pl.BlockSpec((None, None, D, FF), lambda i, be, nv, sl: (layer, blk(i, be, nv, sl), 0, 0))),
           (w3, pl.BlockSpec((None, None, D, FF), lambda i, be, nv, sl: (layer, blk(i, be, nv, sl), 0, 0))),
           (w2, pl.BlockSpec((None, None, FF, D), lambda i, be, nv, sl: (layer, blk(i, be, nv, sl), 0, 0)))]
    outs = [((2 * Mtot + 2 * MOE_BLK, NSEG, LANES), F32, any_spec)]
    scratch = [pltpu.VMEM((2, MOE_BLK, NSEG, LANES), F32), pltpu.VMEM((2, MOE_BLK, NSEG, LANES), F32),
               pltpu.SemaphoreType.DMA((2,)), pltpu.SemaphoreType.DMA((2,))]
    return _call(body, (nblk,), ins, outs, scratch=scratch, prefetch=[blk_expert, nvalid, slots], name="moe_ffn")[0]


def _moe_combine(st, yslot, rinfo_all, off, Mtot, alpha, ln_g, ln_b):
    tm = _pick_tile(st.M, off, Mtot, cap=st.tm)
    b0, b1, br = off // tm, (Mtot + off) // tm, off // tm

    def body(y0_ref, y1_ref, r_ref, x_ref, gate_ref, g_ref, b_ref, o_ref):
        r = r_ref[...]
        y = r[:, 2:3] * _from_tiles(y0_ref) + r[:, 3:4] * _from_tiles(y1_ref)
        o_ref[...] = _res_ln(alpha, y, x_ref[...], gate_ref[...], g_ref[...], b_ref[...])

    ins = [(yslot, pl.BlockSpec((tm, NSEG, LANES), lambda i: (b0 + i, 0, 0))),
           (yslot, pl.BlockSpec((tm, NSEG, LANES), lambda i: (b1 + i, 0, 0))),
           (rinfo_all, pl.BlockSpec((tm, LANES), lambda i: (br + i, 0))),
           _rows(st.x, tm), st.mod(5, tm), _full(ln_g), _full(ln_b)]
    return _call(body, (st.M // tm,), ins, [_rows_out(st.M, D, tm)], name="moe_combine")[0]


def _moe_layer(streams, layer, alpha, p):
    wgr = jnp.zeros((D, LANES), F32).at[:, :MOE_GROUPS].set(p["moe_w_group"][layer])
    wgr = wgr.at[:, MOE_GROUPS:MOE_GROUPS + MOE_EXPERTS].set(p["moe_w_router"][layer])
    bgr = jnp.zeros((1, LANES), F32).at[0, :MOE_GROUPS].set(p["moe_b_group"][layer])
    bgr = bgr.at[0, MOE_GROUPS:MOE_GROUPS + MOE_EXPERTS].set(p["moe_b_router"][layer])
    Mtot = sum(st.M for st in streams)
    h_all = jnp.zeros((Mtot, NSEG, LANES), F32)
    rs, off = [], 0
    for st in streams:
        h_all, r = _moe_router(st, wgr, bgr, h_all, off)
        rs.append(r)
        off += st.M
    rinfo = jnp.concatenate(rs, axis=0)
    R = _pick_tile(Mtot)
    counts = _moe_counts(rinfo, R)[:MOE_EXPERTS, 0].astype(I32)
    padded = (counts + MOE_BLK - 1) // MOE_BLK * MOE_BLK
    pad_end = jnp.cumsum(padded)
    pstart = jnp.zeros((LANES,), F32).at[:MOE_EXPERTS].set((pad_end - padded).astype(F32))
    pstart = jnp.broadcast_to(pstart[:, None], (LANES, LANES))
    nblk = -(-2 * Mtot // MOE_BLK) + MOE_EXPERTS
    blk_first = jnp.arange(nblk, dtype=I32) * MOE_BLK
    blk_expert = jnp.minimum(jnp.sum((pad_end[None, :] <= blk_first[:, None]).astype(I32), axis=1), MOE_EXPERTS - 1)
    nvalid = (pad_end[-1:] // MOE_BLK).astype(I32)
    dest = _moe_dest(rinfo, pstart, R).reshape(-1)
    spare = 2 * Mtot + jnp.arange(nblk * MOE_BLK, dtype=I32) % (2 * MOE_BLK)
    slots = spare.at[dest].set(jnp.arange(2 * Mtot, dtype=I32))
    yslot = _moe_ffn(h_all, slots, blk_expert, nvalid, p["moe_w1"], p["moe_w3"], p["moe_w2"], layer, Mtot)
    ln_g = p["ln_g"][layer, 1].reshape(1, D)
    ln_b = p["ln_b"][layer, 1].reshape(1, D)
    off = 0
    for st in streams:
        st.x = _moe_combine(st, yslot, rinfo, off, Mtot, alpha, ln_g, ln_b)
        off += st.M


def _scan_body(nbg, nv, tb, w_ref, kkn_ref, b_ref, k_ref, r_ref, vt_ref, s0_ref, hexp_ref, hsum_ref, bd_ref,
               o_ref, sf_ref, s_scr):
    t = pl.program_id(1)

    @pl.when(t == 0)
    def _():
        s_scr[...] = s0_ref[...]

    def sub(sb, carry):
        base = pl.multiple_of(sb * SUBLANES, SUBLANES)
        rows = [[ref[bb, pl.ds(base, SUBLANES), :] for bb in range(nbg)]
                for ref in (w_ref, kkn_ref, b_ref, k_ref, r_ref)]
        vt = vt_ref[:, sb].reshape(nbg * nv, LANES).astype(BF16)
        oacc = jnp.zeros((nbg * nv, LANES), F32)
        for j in range(SUBLANES):
            S = [s_scr[bb] for bb in range(nbg)]
            P = jnp.concatenate([S[bb] * rows[1][bb][j:j + 1] for bb in range(nbg)], axis=0).astype(BF16)
            sa = jnp.concatenate([_d(P[:, c0:c0 + MXU_TILE], bd_ref[...]) for c0 in range(0, D, MXU_TILE)], axis=1)
            vb = _d(vt, hexp_ref[j])
            P2 = []
            for bb in range(nbg):
                sl = slice(bb * nv, (bb + 1) * nv)
                Sn = S[bb] * rows[0][bb][j:j + 1] + sa[sl] * rows[2][bb][j:j + 1] + vb[sl] * rows[3][bb][j:j + 1]
                s_scr[bb] = Sn
                P2.append(Sn * rows[4][bb][j:j + 1])
            oacc = oacc + _d(jnp.concatenate(P2, axis=0), hsum_ref[j])
        o_ref[:, sb] = oacc.reshape(nbg, nv, LANES)
        return carry

    lax.fori_loop(0, tb // SUBLANES, sub, 0)

    @pl.when(t == pl.num_programs(1) - 1)
    def _():
        sf_ref[...] = s_scr[...]


def _delta_scan(w, kkn, b, k, r, v, S0, B, T, hsz):
    nh = D // hsz
    nv = hsz
    nbg = 4 if B % 4 == 0 else (2 if B % 2 == 0 else 1)
    tb = min(64, T)
    hexp, hsum, bd = _scan_consts(hsz)
    vt = v.reshape(B, T // SUBLANES, SUBLANES, nh, nv).transpose(0, 1, 4, 2, 3)
    vt = jnp.pad(vt, ((0, 0),) * 4 + ((0, 16 - nh),)).reshape(B, T // SUBLANES, nv, LANES)
    seq = lambda a: (a.reshape(B, T, D), pl.BlockSpec((nbg, tb, D), lambda g, t: (g, t, 0)))
    ins = [seq(w), seq(kkn), seq(b), seq(k), seq(r),
           (vt, pl.BlockSpec((nbg, tb // SUBLANES, nv, LANES), lambda g, t: (g, t, 0, 0))),
           (S0, pl.BlockSpec((nbg, nv, D), lambda g, t: (g, 0, 0))),
           _full(hexp), _full(hsum), _full(bd)]
    outs = [((B, T // SUBLANES, nv, LANES), F32,
             pl.BlockSpec((nbg, tb // SUBLANES, nv, LANES), lambda g, t: (g, t, 0, 0))),
            ((B, nv, D), F32, pl.BlockSpec((nbg, nv, D), lambda g, t: (g, 0, 0)))]
    body = functools.partial(_scan_body, nbg, nv, tb)
    op, sf = _call(body, (B // nbg, T // tb), ins, outs, scratch=[pltpu.VMEM((nbg, nv, D), F32)], name="delta_scan")
    o = op.reshape(B, T // SUBLANES, nv, SUBLANES, 16)[..., :nh].transpose(0, 1, 3, 4, 2).reshape(B * T, D)
    return o, sf


def _shifted_rows(h, first, period, shift=1):
    row = lax.broadcasted_iota(I32, h.shape, 0)
    return jnp.where(row % period < shift, first, pltpu.roll(h, shift, axis=0))


def _rwkv_prep(st, shift_prev, p, tm):
    long_seq = st.T % tm == 0
    tpb = st.T // tm if long_seq else 1
    hs, he = _seg_consts(RWKV_HSZ)
    row = lambda a: _full(a.reshape(1, D))
    wts = [_full(p["rwkv_mu"]), _full(p["rwkv_w_rkv"].astype(BF16)),
           _full(p["rwkv_w1"].astype(BF16)), _full(p["rwkv_w2"].astype(BF16)),
           _full(p["rwkv_a1"].astype(BF16)), _full(p["rwkv_a2"].astype(BF16)),
           _full(p["rwkv_g1"].astype(BF16)), _full(p["rwkv_g2"].astype(BF16)),
           row(p["rwkv_w0"]), row(p["rwkv_a0"]), row(p["rwkv_k_k"]), row(p["rwkv_k_a"]), _full(hs), _full(he)]
    if long_seq:
        nsub = tm // SUBLANES
        first_ins = [(st.x, pl.BlockSpec((SUBLANES, D), lambda i: (jnp.maximum(i * nsub - 1, 0), 0))),
                     (shift_prev.reshape(st.B, 1, D), pl.BlockSpec((None, 1, D), lambda i: (i // tpb, 0, 0)))]
    else:
        first_ins = [_rows(jnp.repeat(shift_prev, st.T, axis=0), tm)]
    nf = len(first_ins)

    def body(x_ref, sc_ref, sh_ref, *refs):
        first_refs, refs = refs[:nf], refs[nf:]
        (mu_ref, wrkv_ref, w1_ref, w2_ref, a1_ref, a2_ref, g1_ref, g2_ref, w0_ref, a0_ref, kk_ref, ka_ref,
         hs_ref, he_ref) = refs[:14]
        h_ref, r_ref, w_ref, k_ref, v_ref, kkn_ref, b_ref, g_ref = refs[14:]
        sc, sh = sc_ref[...], sh_ref[...]
        h = _modulate(x_ref[...], sc, sh)
        if long_seq:
            hh = _modulate(first_refs[0][...], sc, sh)[SUBLANES - 1:SUBLANES]
            first = jnp.where(pl.program_id(0) % tpb == 0, first_refs[1][...], hh)
            hprev = _shifted_rows(h, first, tm)
        else:
            hprev = _shifted_rows(h, first_refs[0][...], st.T)
        xx = hprev - h
        mu = mu_ref[...]
        xr, xw, xk, xv, xa, xg = [h + xx * mu[i:i + 1] for i in range(6)]
        r = _d(xr, wrkv_ref[0])
        k = _d(xk, wrkv_ref[1])
        v = _d(xv, wrkv_ref[2])
        logw = -_softplus(-(w0_ref[...] + _d(jnp.tanh(_d(xw, w1_ref[...])), w2_ref[...]))) - 0.5
        a = _sigmoid(a0_ref[...] + _d(_d(xa, a1_ref[...]), a2_ref[...]))
        g = _d(_sigmoid(_d(xg, g1_ref[...])), g2_ref[...])
        kk = k * kk_ref[...]
        inv = lax.rsqrt(_d_x3(kk * kk, hs_ref[...]) + 1e-6)
        kk = kk * _d_x3(inv, he_ref[...])
        h_ref[...] = h
        r_ref[...] = r
        w_ref[...] = jnp.exp(-jnp.exp(logw))
        k_ref[...] = k * (1.0 + (a - 1.0) * ka_ref[...])
        v_ref[...] = v
        kkn_ref[...] = -kk
        b_ref[...] = kk * a
        g_ref[...] = g

    ins = [_rows(st.x, tm), st.mod(1, tm), st.mod(0, tm)] + first_ins + wts
    outs = [_rows_out(st.M, D, tm) for _ in range(8)]
    return _call(body, (st.M // tm,), ins, outs, name="rwkv_prep")


def _rwkv_out(st, o, r, kmod, v, g, p, alpha, ln_g, ln_b, tm):
    hs, he = _seg_consts(RWKV_HSZ)
    inv_n = 1.0 / RWKV_HSZ

    def pro(o, r, k, v, g, lw, lb, rk, hs, he):
        mean = _d_x3(_d_x3(o, hs) * inv_n, he)
        c = o - mean
        rstd = lax.rsqrt(_d_x3(c * c, hs) * inv_n + RWKV_GN_EPS)
        on = c * _d_x3(rstd, he) * lw + lb
        bonus = _d_x3(_d_x3(r * k * rk, hs), he) * v
        return (on + bonus) * g

    def epi(acc, x, gate, g_, b_):
        return (_res_ln(alpha, acc, x, gate, g_, b_),)

    row = lambda a: _full(a.reshape(1, D))
    pro_ins = [_rows(a, tm) for a in (r, kmod, v, g)] + [row(p["rwkv_ln_w"]), row(p["rwkv_ln_b"]),
                                                       row(p["rwkv_r_k"]), _full(hs), _full(he)]
    epi_ins = [_rows(st.x, tm), st.mod(2, tm), _full(ln_g), _full(ln_b)]
    return _mm(o, _full(p["rwkv_w_o"].astype(BF16)), tm=tm, pro=pro, pro_ins=pro_ins, epi=epi, epi_ins=epi_ins,
               outs=[_rows_out(st.M, D, tm)], name="rwkv_out")[0]


def _rwkv_layer(st, shift_prev, wkv0, p, alpha, ln_g, ln_b):
    B, T = st.B, st.T
    tm = min(256, st.M)
    nh = D // RWKV_HSZ
    h, r, w, kmod, v, kkn, b, g = _rwkv_prep(st, shift_prev, p, tm)
    S0 = wkv0.transpose(0, 2, 1, 3).reshape(B, RWKV_HSZ, D)
    o, sf = _delta_scan(w, kkn, b, kmod, r, v, S0, B, T, RWKV_HSZ)
    st.x = _rwkv_out(st, o, r, kmod, v, g, p, alpha, ln_g, ln_b, tm)
    wkv = sf.reshape(B, RWKV_HSZ, nh, RWKV_HSZ).transpose(0, 2, 1, 3)
    return wkv, h.reshape(B, T, D)[:, -1]


def _pad_cols(w, n):
    return jnp.pad(w, ((0, 0), (0, n - w.shape[1])))


def _gdn_proj(st, p, tm):
    C = 3 * D
    w = _pad_cols(p["gdn_w_in"], C + D + LANES).astype(BF16)

    def epi(acc):
        return acc[:, :C], acc[:, C:C + D], acc[:, C + D:]

    outs = [_rows_out(st.M, C, tm), _rows_out(st.M, D, tm), _rows_out(st.M, LANES, tm)]
    return _mm(st.x, _full(w), tm=tm, pro=_modulate, pro_ins=[st.mod(1, tm), st.mod(0, tm)], epi=epi, outs=outs,
               name="gdn_proj")


def _gdn_conv(st, pre, ba, conv_buf, p, tm, chunked):
    C = 3 * D
    H = GDN_HEADS
    long_seq = st.T % tm == 0
    tpb = st.T // tm if long_seq else 1
    hs, he = _seg_consts(GDN_HSZ)
    hsn = _seg_np(GDN_HSZ)
    he_b = jnp.asarray(hsn.T, BF16)
    he_a = jnp.asarray(np.roll(hsn.T, H, axis=0), BF16)
    alog = jnp.zeros((1, LANES), F32).at[0, H:2 * H].set(p["gdn_A_log"])
    dtb = jnp.zeros((1, LANES), F32).at[0, H:2 * H].set(p["gdn_dt_bias"])
    if long_seq:
        nsub = tm // SUBLANES
        init8 = jnp.pad(conv_buf, ((0, 0), (SUBLANES - (GDN_CONV - 1), 0), (0, 0)))
        first_ins = [(pre, pl.BlockSpec((SUBLANES, C), lambda i: (jnp.maximum(i * nsub - 1, 0), 0))),
                     (init8, pl.BlockSpec((None, SUBLANES, C), lambda i: (i // tpb, 0, 0)))]
    else:
        padded = jnp.pad(conv_buf, ((0, 0), (0, st.T), (0, 0)))
        first_ins = [_rows(padded[:, GDN_CONV - 1 - j:GDN_CONV - 1 - j + st.T].reshape(st.M, C), tm)
                     for j in range(1, GDN_CONV)]
    nf = len(first_ins)

    def body(pre_ref, ba_ref, *refs):
        first_refs, refs = refs[:nf], refs[nf:]
        cw_ref, alog_ref, dtb_ref, hs_ref, he_ref, heb_ref, hea_ref = refs[:7]
        w_ref, kkn_ref, k_ref, q_ref, v_ref = refs[7:]
        x = pre_ref[...]
        if long_seq:
            halo = jnp.where(pl.program_id(0) % tpb == 0, first_refs[1][...], first_refs[0][...])
            big = jnp.concatenate([halo, x], axis=0)
            sh = [pltpu.roll(big, j, axis=0)[SUBLANES:] for j in range(1, GDN_CONV)]
        else:
            sh = [_shifted_rows(x, first_refs[j - 1][...], st.T, j) for j in range(1, GDN_CONV)]
        cw = cw_ref[...]
        conv = sh[2] * cw[0:1]
        conv = conv + sh[1] * cw[1:2]
        conv = conv + sh[0] * cw[2:3]
        conv = conv + x * cw[3:4]
        c = _silu(conv)
        q, k, v = c[:, :D], c[:, D:2 * D], c[:, 2 * D:]
        qn = q * _d_x3(lax.rsqrt(_d_x3(q * q, hs_ref[...]) + 1e-6), he_ref[...]) * (GDN_HSZ ** -0.5)
        kn = k * _d_x3(lax.rsqrt(_d_x3(k * k, hs_ref[...]) + 1e-6), he_ref[...])
        ba = ba_ref[...]
        logdecay = -jnp.exp(alog_ref[...]) * _softplus(ba + dtb_ref[...])
        if chunked:
            w_ref[...] = _sigmoid(ba)
            kkn_ref[...] = logdecay
            k_ref[...] = kn
            q_ref[...] = qn
            v_ref[...] = v
        else:
            beta = _d_x3(_sigmoid(ba), heb_ref[...])
            a = _d_x3(jnp.exp(logdecay), hea_ref[...])
            w_ref[...] = a
            kkn_ref[...] = -(a * beta) * kn
            k_ref[...] = kn
            q_ref[...] = qn
            v_ref[...] = beta * v

    ins = [_rows(pre, tm), _rows(ba, tm)] + first_ins + [_full(p["gdn_conv_w"]), _full(alog), _full(dtb), _full(hs),
                                                         _full(he), _full(he_b), _full(he_a)]
    small = LANES if chunked else D
    outs = [_rows_out(st.M, small, tm), _rows_out(st.M, small, tm)] + [_rows_out(st.M, D, tm) for _ in range(3)]
    return _call(body, (st.M // tm,), ins, outs, name="gdn_conv")


GDN_CHUNK = 64


def _gdn_chunk_scan(q, k, v, beta, g, S0, B, T):
    C = GDN_CHUNK
    H, N = GDN_HEADS, GDN_HSZ
    nchunk = T // C
    tril = _lower_tri(C)
    triu = jnp.asarray(np.triu(np.ones((C, C), np.float32)), BF16)

    def body(q_ref, k_ref, v_ref, b_ref, g_ref, s0_ref, tril_ref, triu_ref, o_ref, sf_ref, s_scr):
        c = pl.program_id(1)

        @pl.when(c == 0)
        def _():
            s_scr[...] = s0_ref[...]

        gblk = g_ref[...]
        gc = _d_3x(tril_ref[...], gblk)
        gh, gm, gl = _split3(gblk)
        tn = lambda a: lax.dot_general(a, triu_ref[...], (((0,), (0,)), ((), ())), preferred_element_type=F32)
        gct = tn(gh) + tn(gm) + tn(gl)
        bblk = b_ref[...]
        ri = lax.broadcasted_iota(I32, (C, C), 0)
        ci = lax.broadcasted_iota(I32, (C, C), 1)
        lower, strict = ri >= ci, ri > ci
        eye = jnp.where(ri == ci, 1.0, 0.0)
        heads = range(H)
        sl = [slice(h * N, (h + 1) * N) for h in heads]
        bcol = [bblk[:, h:h + 1] for h in heads]
        gcol = [gc[:, H + h:H + h + 1] for h in heads]
        gamma = [jnp.where(lower, jnp.exp(jnp.minimum(gcol[h] - gct[H + h:H + h + 1, :], 0.0)), 0.0) for h in heads]
        kb = [k_ref[:, sl[h]] * bcol[h] for h in heads]
        pw = [jnp.where(strict, _d_nt(kb[h], k_ref[:, sl[h]]) * gamma[h], 0.0) for h in heads]
        t_inv = [eye - pw[h] for h in heads]
        for _ in range(int(math.log2(C)) - 1):
            pw = [_d(pw[h], pw[h]) for h in heads]
            t_inv = [t_inv[h] + _d(t_inv[h], pw[h]) for h in heads]
        eg = [jnp.exp(gcol[h]) for h in heads]
        u = [_d(t_inv[h], v_ref[:, sl[h]] * bcol[h]) for h in heads]
        w = [_d(t_inv[h], kb[h] * eg[h]) for h in heads]
        qk = [jnp.where(lower, _d_nt(q_ref[:, sl[h]], k_ref[:, sl[h]]) * gamma[h], 0.0) for h in heads]
        v_new = [u[h] - _d(w[h], s_scr[h]) for h in heads]
        for h in heads:
            o_ref[:, sl[h]] = _d(q_ref[:, sl[h]] * eg[h], s_scr[h]) + _d(qk[h], v_new[h])
        for h in heads:
            g_last = gcol[h][C - 1:C]
            kd = (k_ref[:, sl[h]] * jnp.exp(g_last - gcol[h])).astype(BF16)
            s_scr[h] = s_scr[h] * jnp.exp(g_last) + lax.dot_general(kd, v_new[h].astype(BF16), (((0,), (0,)), ((), ())),
                                                                   preferred_element_type=F32)

        @pl.when(c == nchunk - 1)
        def _():
            sf_ref[...] = s_scr[...]

    seq = lambda a, w: (a, pl.BlockSpec((C, w), lambda b, c: (b * nchunk + c, 0)))
    ins = [seq(q, D), seq(k, D), seq(v, D), seq(beta, LANES), seq(g, LANES),
           (S0, pl.BlockSpec((None, H, N, N), lambda b, c: (b, 0, 0, 0))),
           (tril, pl.BlockSpec(tril.shape, lambda b, c: (0, 0))), (triu, pl.BlockSpec(triu.shape, lambda b, c: (0, 0)))]
    outs = [((B * T, D), F32, pl.BlockSpec((C, D), lambda b, c: (b * nchunk + c, 0))),
            ((B, H, N, N), F32, pl.BlockSpec((None, H, N, N), lambda b, c: (b, 0, 0, 0)))]
    return _call(body, (B, nchunk), ins, outs, scratch=[pltpu.VMEM((H, N, N), F32)], name="gdn_chunk_scan")


def _gdn_out(st, o, z, p, alpha, ln_g, ln_b, tm):
    hs, he = _seg_consts(GDN_HSZ)
    nw = jnp.tile(p["gdn_norm_w"], GDN_HEADS).reshape(1, D)

    def pro(o, z, nw, hs, he):
        rstd = lax.rsqrt(_d_x3(o * o, hs) * (1.0 / GDN_HSZ) + 1e-6)
        return o * _d_x3(rstd, he) * nw * _silu(z)

    def epi(acc, x, gate, g_, b_):
        return (_res_ln(alpha, acc, x, gate, g_, b_),)

    return _mm(o, _full(p["gdn_w_o"].astype(BF16)), tm=tm, pro=pro, pro_ins=[_rows(z, tm), _full(nw), _full(hs), _full(he)],
               epi=epi, epi_ins=[_rows(st.x, tm), st.mod(2, tm), _full(ln_g), _full(ln_b)],
               outs=[_rows_out(st.M, D, tm)], name="gdn_out")[0]


def _gdn_layer(st, conv_buf, S0, p, alpha, ln_g, ln_b):
    B, T = st.B, st.T
    tm = min(256, st.M)
    pre, z, ba = _gdn_proj(st, p, tm)
    chunked = T % GDN_CHUNK == 0
    if chunked:
        beta, g, kn, qn, v = _gdn_conv(st, pre, ba, conv_buf, p, tm, True)
        o, S = _gdn_chunk_scan(qn, kn, v, beta, g, S0, B, T)
    else:
        w, kkn, kn, qn, vb = _gdn_conv(st, pre, ba, conv_buf, p, tm, False)
        S0t = S0.transpose(0, 3, 1, 2).reshape(B, GDN_HSZ, D)
        o, sf = _delta_scan(w, kkn, kn, kn, qn, vb, S0t, B, T, GDN_HSZ)
        S = sf.reshape(B, GDN_HSZ, GDN_HEADS, GDN_HSZ).transpose(0, 2, 3, 1)
    st.x = _gdn_out(st, o, z, p, alpha, ln_g, ln_b, tm)
    xpad = jnp.concatenate([conv_buf, pre.reshape(B, T, 3 * D)[:, -(GDN_CONV - 1):]], axis=1)
    return S, xpad[:, -(GDN_CONV - 1):]


def _flash_body(cfg, *refs):
    tq, hq, hk = cfg["tq"], cfg["hq"], cfg["hk"]
    fox, bias, aug, window = cfg["fox"], cfg["bias"], cfg["aug"], cfg["window"]
    tk = tq
    G = hq // hk
    Kc = HD + aug
    R = G * tq
    refs = list(refs)
    q_ref, k_ref, v_ref = refs[:3]
    pos = 3
    if aug:
        mb_ref, e_ref = refs[pos:pos + 2]
        pos += 2
    if bias:
        tz_ref = refs[pos]
        pos += 1
    if fox:
        cq_ref, ck_ref = refs[pos:pos + 2]
        pos += 2
    o_ref, kb, vb, s_scr = refs[pos:pos + 4]
    g = pl.program_id(1)
    qi = pl.program_id(2)

    @pl.when(qi == 0)
    def _():
        vb[...] = v_ref[...].astype(BF16)
        if aug:
            k = k_ref[...]
            kb[...] = jnp.concatenate(
                [jnp.concatenate([k[:, kv * HD:(kv + 1) * HD].astype(BF16), e_ref[...]], axis=1) for kv in range(hk)],
                axis=1)
        else:
            kb[...] = k_ref[...].astype(BF16)

    scale = HD ** -0.5
    q = q_ref[...]
    row_t = lax.broadcasted_iota(I32, (R, tk), 0) % tq
    col_s = lax.broadcasted_iota(I32, (R, tk), 1)
    qs, cqs = [], []
    for kv in range(hk):
        x = jnp.concatenate([q[:, (kv * G + gg) * HD:(kv * G + gg + 1) * HD] for gg in range(G)], axis=0) * scale
        if aug:
            x = jnp.concatenate([x.astype(BF16), jnp.concatenate([mb_ref[kv]] * G, axis=0)], axis=1)
        qs.append(x.astype(BF16))
        if fox:
            lane = lax.broadcasted_iota(I32, (tq, LANES), 1)
            cqs.append(jnp.sum(jnp.where(lane == g * hk + kv, cq_ref[...], 0.0), axis=-1, keepdims=True))

    def logits(kv, c, rel, valid):
        off = pl.multiple_of(c * tk, tk)
        s = _d_nt(qs[kv], kb[pl.ds(off, tk), kv * Kc:(kv + 1) * Kc])
        if fox:
            sub = lax.broadcasted_iota(I32, (FOX_HEADS, tk), 0)
            ck = jnp.sum(jnp.where(sub == g * hk + kv, ck_ref[:, pl.ds(off, tk)], 0.0), axis=0, keepdims=True)
            s = s + cqs[kv] - ck
        if bias and rel in (0, 1):
            s = s + jnp.concatenate([tz_ref[kv * G + gg, rel] for gg in range(G)], axis=0)
        if rel == 0:
            s = jnp.where(row_t >= col_s, s, NEG)
        if rel == 3:
            s = jnp.where(col_s > row_t, s, NEG)
        if valid is not None:
            s = jnp.where(valid, s, NEG)
        return s

    nt = tk // LANES

    def lane_tiles(x):
        return [x[:, j * LANES:(j + 1) * LANES] for j in range(nt)]

    if window is not None:
        nch = window // tk
        static = [(qi, 0, None)] + [(jnp.maximum(qi - dc, 0), 1 if dc == 1 else (3 if dc == nch else 2), qi - dc >= 0)
                                    for dc in range(1, nch + 1)]
        n_far = 0
    elif bias:
        static = [(qi, 0, None), (jnp.maximum(qi - 1, 0), 1, qi >= 1)]
        n_far = jnp.maximum(qi - 1, 0)
    else:
        static = [(qi, 0, None)]
        n_far = qi
    n_static = len(static)
    unroll = cfg["unroll"]

    def far_loop(fn, carry):
        ng = n_far // unroll

        def group(gi, cr):
            for u in range(unroll):
                cr = fn(gi * unroll + u, cr)
            return cr

        carry = lax.fori_loop(0, ng, group, carry)
        return lax.fori_loop(ng * unroll, n_far, fn, carry)

    outs = []
    for kv in range(hk):
        def score(c, rel, valid, slot, m128):
            s = logits(kv, c, rel, valid)
            s_scr[slot] = s
            for t in lane_tiles(s):
                m128 = jnp.maximum(m128, t)
            return m128

        m128 = jnp.full((R, LANES), NEG, F32)
        for slot, (c, rel, valid) in enumerate(static):
            m128 = score(c, rel, valid, slot, m128)
        if window is None:
            m128 = far_loop(lambda c, m: score(c, 2, None, n_static + c, m), m128)
        mrep = jnp.broadcast_to(jnp.max(m128, axis=-1, keepdims=True), (R, LANES))

        def accumulate(c, slot, carry):
            l128, acc = carry
            p = [jnp.exp(t - mrep) for t in lane_tiles(s_scr[slot])]
            for t in p:
                l128 = l128 + t
            off = pl.multiple_of(c * tk, tk)
            pm = jnp.concatenate(p, axis=1) if nt > 1 else p[0]
            return l128, acc + _d(pm, vb[pl.ds(off, tk), kv * HD:(kv + 1) * HD])

        carry = (jnp.zeros((R, LANES), F32), jnp.zeros((R, HD), F32))
        for slot, (c, rel, valid) in enumerate(static):
            carry = accumulate(c, slot, carry)
        if window is None:
            carry = far_loop(lambda c, cr: accumulate(c, n_static + c, cr), carry)
        l128, acc = carry
        o = acc / jnp.sum(l128, axis=-1, keepdims=True)
        outs += [o[gg * tq:(gg + 1) * tq] for gg in range(G)]
    o_ref[...] = jnp.concatenate(outs, axis=1)


def _flash(cfg, B, T, ngroups, q_in, k_in, v_in, extra_ins, M):
    tq, hq, hk = cfg["tq"], cfg["hq"], cfg["hk"]
    Kc = HD + cfg["aug"]
    ins = [q_in, k_in, v_in] + list(extra_ins)
    nq = T // tq
    R = (hq // hk) * tq
    if cfg["window"] is not None:
        nslots = cfg["window"] // tq + 1
    else:
        nslots = nq + (1 if cfg["bias"] else 0)
    outs = [((M, ngroups * hq * HD), F32, pl.BlockSpec((tq, hq * HD), lambda b, g, i: (b * nq + i, g)))]
    scratch = [pltpu.VMEM((T, hk * Kc), BF16), pltpu.VMEM((T, hk * HD), BF16), pltpu.VMEM((nslots, R, tq), F32)]
    return _call(functools.partial(_flash_body, cfg), (B, ngroups, nq), ins, outs, scratch=scratch,
                 name="flash_" + cfg["name"])[0]


def _log_sigmoid(x):
    return -_softplus(-x)


def _fox_proj(st, p, tm):
    hw = FOX_HEADS * HD
    w = _pad_cols(p["fox_w_in"], 3 * hw + LANES).astype(BF16)
    bf = jnp.zeros((1, LANES), F32).at[0, :FOX_HEADS].set(p["fox_b_f"])

    def epi(acc, bf):
        return acc[:, :hw], acc[:, hw:3 * hw], _log_sigmoid(acc[:, 3 * hw:] + bf)

    outs = [_rows_out(st.M, hw, tm), _rows_out(st.M, 2 * hw, tm), _rows_out(st.M, LANES, tm)]
    return _mm(st.x, _full(w), tm=tm, pro=_modulate, pro_ins=[st.mod(1, tm), st.mod(0, tm)], epi=epi,
               epi_ins=[_full(bf)], outs=outs, name="fox_proj")


def _lower_tri(n):
    return jnp.asarray(np.tril(np.ones((n, n), np.float32)), BF16)


def _cumsum_rows(x, B, T):
    ch = _pick_tile(T)
    tri = _lower_tri(ch)

    def body(x_ref, tri_ref, o_ref):
        carry = jnp.zeros((1, LANES), F32)
        for c in range(T // ch):
            cc = _d_3x(tri_ref[...], x_ref[c * ch:(c + 1) * ch, :]) + carry
            o_ref[c * ch:(c + 1) * ch, :] = cc
            carry = cc[ch - 1:ch, :]

    return _call(body, (B,), [_rows(x, T), _full(tri)], [_rows_out(B * T, LANES, T)], name="cumsum_rows")[0]


def _out_proj(st, o, w_o, alpha, ln_g, ln_b, tm, name):
    def epi(acc, x, gate, g_, b_):
        return (_res_ln(alpha, acc, x, gate, g_, b_),)

    return _mm(o, _full(w_o.astype(BF16)), tm=tm, epi=epi,
               epi_ins=[_rows(st.x, tm), st.mod(2, tm), _full(ln_g), _full(ln_b)],
               outs=[_rows_out(st.M, D, tm)], name=name)[0]


def _fox_prompt(st, p, alpha, ln_g, ln_b):
    B, T, M = st.B, st.T, st.M
    tm = min(256, M)
    q, kv, logf = _fox_proj(st, p, tm)
    cum = _cumsum_rows(logf, B, T)
    ckT = cum.reshape(B, T, LANES)[:, :, :FOX_HEADS].transpose(0, 2, 1)
    tq = min(512, T)
    nq = T // tq
    cfg = dict(name="fox", tq=tq, hq=2, hk=2, fox=True, bias=False, aug=0, window=None, unroll=2)
    npair = FOX_HEADS // 2
    q_in = (q, pl.BlockSpec((tq, 2 * HD), lambda b, g, i: (b * nq + i, g)))
    k_in = (kv, pl.BlockSpec((T, 2 * HD), lambda b, g, i: (b, g)))
    v_in = (kv, pl.BlockSpec((T, 2 * HD), lambda b, g, i: (b, npair + g)))
    extra = [(cum, pl.BlockSpec((tq, LANES), lambda b, g, i: (b * nq + i, 0))),
             (ckT, pl.BlockSpec((None, FOX_HEADS, T), lambda b, g, i: (b, 0, 0)))]
    o = _flash(cfg, B, T, npair, q_in, k_in, v_in, extra, M)
    st.x = _out_proj(st, o, p["fox_w_o"], alpha, ln_g, ln_b, tm, "fox_out")
    return kv.reshape(B, T, 2, FOX_HEADS, HD), logf.reshape(B, T, LANES)[:, :, :FOX_HEADS]


def _page_ins(cache, page_shape, npages, first_of_step):
    nd = len(page_shape)
    return [(cache, pl.BlockSpec((None,) + tuple(page_shape),
                                 lambda b, s, pt, j=j: (pt[b, first_of_step(s) + j],) + (0,) * nd))
            for j in range(npages)]


def _fox_cum_sample(logf_new, cache_logf, page_table, Tn):
    B, npg = page_table.shape
    PAGE = cache_logf.shape[1]
    H = cache_logf.shape[2]
    cache_t = cache_logf.transpose(0, 2, 1)
    triu = jnp.asarray(np.triu(np.ones((PAGE, PAGE), np.float32)), BF16)

    def body(pt_ref, *refs):
        pages, new_ref, tri_ref, o_ref = refs[:npg], refs[npg], refs[npg + 1], refs[npg + 2]
        carry = jnp.zeros((H, 1), F32)
        for j in range(npg):
            cc = _d_x3(pages[j][...], tri_ref[...]) + carry
            o_ref[:, j * PAGE:(j + 1) * PAGE] = cc
            carry = cc[:, PAGE - 1:PAGE]
        xn = jnp.concatenate([new_ref[...], jnp.zeros((PAGE - Tn, LANES), F32)], axis=0).T[:H, :]
        o_ref[:, npg * PAGE:(npg + 1) * PAGE] = _d_x3(xn, tri_ref[...]) + carry

    ins = _page_ins(cache_t, (H, PAGE), npg, lambda s: 0)
    ins += [(logf_new, pl.BlockSpec((Tn, LANES), lambda b, s, pt: (b, 0))),
            (triu, pl.BlockSpec(triu.shape, lambda b, s, pt: (0, 0)))]
    Lp = (npg + 1) * PAGE
    outs = [((B, H, Lp), F32, pl.BlockSpec((None, H, Lp), lambda b, s, pt: (b, 0, 0)))]
    return _call(body, (B, 1), ins, outs, prefetch=[page_table], name="fox_cum_sample")[0]


def _rep_mat(n_rows, n_src, per):
    r = np.arange(n_rows)
    src = r // per if per else r % n_src
    return jnp.asarray((src[:, None] == np.arange(n_src)[None, :]).astype(np.float32), BF16)


def _fox_decode(q, kv_new, cache_kv, page_table, cq, ckT, Tn):
    B, npg = page_table.shape
    PAGE, H = cache_kv.shape[1], FOX_HEADS
    hw = H * HD
    R = H * Tn
    pps = PAGES_PER_STEP if npg % PAGES_PER_STEP == 0 else 1
    nsteps = npg // pps
    cache_t = cache_kv.transpose(0, 2, 3, 4, 1).reshape(cache_kv.shape[0], 2, hw, PAGE)
    rep_t = _rep_mat(R, Tn, 0)
    rep_h = _rep_mat(R, H, Tn)
    scale = HD ** -0.5

    def body(pt_ref, *refs):
        pages = refs[:pps]
        q_ref, new_ref, cq_ref, ck_ref, rt_ref, rh_ref, o_ref, qbd, m_s, l_s, acc = refs[pps:]
        s_id = pl.program_id(1)
        own = lax.broadcasted_iota(I32, (R, hw), 0) // Tn == lax.broadcasted_iota(I32, (R, hw), 1) // HD

        @pl.when(s_id == 0)
        def _():
            qbd[...] = jnp.where(own, _d(rt_ref[...], q_ref[...]) * scale, 0.0).astype(BF16)
            m_s[...] = jnp.full(m_s.shape, NEG, F32)
            l_s[...] = jnp.zeros(l_s.shape, F32)
            acc[...] = jnp.zeros(acc.shape, F32)

        cqv = cq_ref[...]

        def update(s, pv):
            m = m_s[:, 0:1]
            m2 = jnp.maximum(m, jnp.max(s, axis=-1, keepdims=True))
            a = jnp.exp(m - m2)
            pr = jnp.exp(s - m2)
            l_s[...] = jnp.broadcast_to(a * l_s[:, 0:1] + jnp.sum(pr, axis=-1, keepdims=True), l_s.shape)
            m_s[...] = jnp.broadcast_to(m2, m_s.shape)
            acc[...] = a * acc[...] + pv(pr)

        kt = jnp.concatenate([pages[j][0].astype(BF16) for j in range(pps)], axis=1)
        vt = jnp.concatenate([pages[j][1].astype(BF16) for j in range(pps)], axis=1)
        off = pl.multiple_of(s_id * (pps * PAGE), pps * PAGE)
        update(_d(qbd[...], kt) + cqv - _d_3x(rh_ref[...], ck_ref[:, pl.ds(off, pps * PAGE)]),
               lambda pr: _d_nt(pr, vt))

        @pl.when(s_id == nsteps - 1)
        def _():
            new = jnp.concatenate([new_ref[...], jnp.zeros((PAGE - Tn, 2 * hw), F32)], axis=0)
            t_row = lax.broadcasted_iota(I32, (R, PAGE), 0) % Tn
            col = lax.broadcasted_iota(I32, (R, PAGE), 1)
            s_new = _d_nt(qbd[...], new[:, :hw]) + cqv - _d_3x(rh_ref[...], ck_ref[:, pl.ds(npg * PAGE, PAGE)])
            update(jnp.where(col <= t_row, s_new, NEG), lambda pr: _d(pr, new[:, hw:]))
            of = jnp.where(own, acc[...] / l_s[:, 0:1], 0.0)
            out = of[0:Tn]
            for h in range(1, H):
                out = out + of[h * Tn:(h + 1) * Tn]
            o_ref[...] = out

    Lp = ckT.shape[2]
    const = lambda a: (a, pl.BlockSpec(a.shape, lambda b, s, pt: (0,) * a.ndim))
    ins = _page_ins(cache_t, (2, hw, PAGE), pps, lambda s: s * pps)
    ins += [(q, pl.BlockSpec((Tn, hw), lambda b, s, pt: (b, 0))),
            (kv_new, pl.BlockSpec((Tn, 2 * hw), lambda b, s, pt: (b, 0))),
            (cq, pl.BlockSpec((None, R, 1), lambda b, s, pt: (b, 0, 0))),
            (ckT, pl.BlockSpec((None, H, Lp), lambda b, s, pt: (b, 0, 0))),
            const(rep_t), const(rep_h)]
    outs = [((B * Tn, hw), F32, pl.BlockSpec((Tn, hw), lambda b, s, pt: (b, 0)))]
    scratch = [pltpu.VMEM((R, hw), BF16), pltpu.VMEM((R, LANES), F32), pltpu.VMEM((R, LANES), F32),
               pltpu.VMEM((R, hw), F32)]
    return _call(body, (B, nsteps), ins, outs, scratch=scratch, prefetch=[page_table], name="fox_decode")[0]


def _fox_sample(st, cache_kv, cache_logf, page_table, p, alpha, ln_g, ln_b):
    B, Tn, M = st.B, st.T, st.M
    tm = min(256, M)
    npg = page_table.shape[1]
    PAGE = cache_kv.shape[1]
    q, kv, logf = _fox_proj(st, p, tm)
    ckT = _fox_cum_sample(logf, cache_logf, page_table, Tn)
    cq = ckT[:, :, npg * PAGE:npg * PAGE + Tn].reshape(B, FOX_HEADS * Tn, 1)
    o = _fox_decode(q, kv, cache_kv, page_table, cq, ckT, Tn)
    st.x = _out_proj(st, o, p["fox_w_o"], alpha, ln_g, ln_b, tm, "fox_out")
    return kv.reshape(B, Tn, 2, FOX_HEADS, HD), logf.reshape(B, Tn, LANES)[:, :, :FOX_HEADS]


KVW = NSA_KVH * HD
HALF = CMP_BLK // 2


def _t5_bucket(dist):
    exact = REL_BUCKETS // 2
    d = jnp.maximum(dist, 0)
    far = exact + (jnp.log(jnp.maximum(d, 1).astype(F32) / exact) / math.log(REL_MAX_DIST / exact)
                   * (REL_BUCKETS - exact)).astype(I32)
    return jnp.where(d < exact, d, jnp.minimum(far, REL_BUCKETS - 1))


def _rel_bias(table, dist):
    return jnp.moveaxis(table[_t5_bucket(dist)], -1, 0)


def _nsa_proj(st, p, tm):
    qw = NSA_HEADS * HD
    w = _pad_cols(p["nsa_w_in"], qw + 6 * KVW + LANES).astype(BF16)

    def epi(acc):
        return (acc[:, :qw], acc[:, qw:qw + 2 * KVW], acc[:, qw + 2 * KVW:qw + 4 * KVW],
                acc[:, qw + 4 * KVW:qw + 6 * KVW], acc[:, qw + 6 * KVW:])

    outs = [_rows_out(st.M, qw, tm)] + [_rows_out(st.M, 2 * KVW, tm)] * 3 + [_rows_out(st.M, LANES, tm)]
    return _mm(st.x, _full(w), tm=tm, pro=_modulate, pro_ins=[st.mod(1, tm), st.mod(0, tm)], epi=epi, outs=outs,
               name="nsa_proj")


def _cmp_weights(p):
    eye = jnp.eye(NSA_KVH, dtype=F32)
    wk = jnp.einsum("ab,vlde->vladbe", eye, p["nsa_cmp_w1"]).reshape(2, CMP_BLK, KVW, KVW)
    wc = wk.reshape(2, 2, HALF, KVW, KVW).transpose(1, 2, 0, 3, 4)
    w2c = jnp.einsum("ab,vde->vadbe", eye, p["nsa_cmp_w2"]).reshape(2, KVW, KVW)
    b1 = jnp.tile(p["nsa_cmp_b1"][:, None, :], (1, NSA_KVH, 1)).reshape(1, 2 * KVW)
    return wc.astype(BF16), w2c.astype(BF16), b1


def _compress_body(nx, *refs):
    x_refs = refs[:nx]
    wc_ref, w2_ref, b1_ref, o_ref, ua, ub = refs[nx:]
    rows = ua.shape[1]
    nl = 2 * KVW // LANES
    acc = [[jnp.zeros((rows, KVW), F32) for _ in range(2)] for _ in range(2)]
    for l in range(HALF):
        for kv in range(2):
            lo = l * 2 * KVW + kv * KVW
            piece = [r[:, lo:lo + KVW] for r in x_refs]
            piece = (jnp.concatenate(piece, axis=0) if nx > 1 else piece[0]).astype(BF16)
            for half in range(2):
                acc[half][kv] = acc[half][kv] + _d(piece, wc_ref[half, l, kv])
    for scr, a in ((ua, acc[0]), (ub, acc[1])):
        full = jnp.concatenate(a, axis=1)
        for c in range(nl):
            scr[c] = full[:, c * LANES:(c + 1) * LANES]
    hid = jnp.concatenate([ua[c, pl.ds(0, rows // 2, stride=2), :] + ub[c, pl.ds(1, rows // 2, stride=2), :]
                           for c in range(nl)], axis=1)
    hid = _gelu_tanh(hid + b1_ref[...])
    o_ref[...] = jnp.concatenate([_d(hid[:, :KVW], w2_ref[0]), _d(hid[:, KVW:], w2_ref[1])], axis=1)


def _compress_dense(rows_kv, cw):
    wc, w2c, b1 = cw
    M = rows_kv.shape[0]
    x = rows_kv.reshape(M // HALF, HALF * 2 * KVW)
    nh = M // HALF
    th = _pick_tile(nh, cap=128)
    ins = [_rows(x, th), _full(wc), _full(w2c), _full(b1)]
    outs = [_rows_out(nh // 2, 2 * KVW, th // 2)]
    scratch = [pltpu.VMEM((2 * KVW // LANES, th, LANES), F32)] * 2
    return _call(functools.partial(_compress_body, 1), (nh // th,), ins, outs, scratch=scratch, name="nsa_compress")[0]


def _pair_mat(nc, ns):
    n = np.arange(nc)
    return jnp.asarray((n[:, None] // (SEL_BLK // CMP_BLK) == np.arange(ns)[None, :]).astype(np.float32), BF16)


def _top_blocks(score, n_sel):
    lane = lax.broadcasted_iota(I32, score.shape, 1)
    big = jnp.int32(1 << 20)
    sel = jnp.zeros(score.shape, jnp.bool_)
    work = score
    for _ in range(n_sel):
        m = jnp.max(work, axis=-1, keepdims=True)
        idx = jnp.min(jnp.where(work == m, lane, big), axis=-1, keepdims=True)
        hit = lane == idx
        sel = sel | hit
        work = jnp.where(hit, -3e38, work)
    return jnp.where(sel, 0.0, SEL_NEG)


def _masked_softmax(s, mask):
    s = jnp.where(mask, s, NEG)
    m = jnp.max(s, axis=-1, keepdims=True)
    p = jnp.where(mask, jnp.exp(s - m), 0.0)
    l = jnp.sum(p, axis=-1, keepdims=True)
    return p / jnp.where(l > 0.0, l, 1.0)


def _block_scores(imp, tpos, ns):
    blk = lax.broadcasted_iota(I32, imp.shape, 1)
    cur = tpos // SEL_BLK
    forced = (blk == 0) | (blk == cur) | (blk == cur - 1)
    score = jnp.where(forced, FORCE_SCORE, imp)
    return jnp.where(blk * SEL_BLK > tpos, -1.0, score)


CMP_NEAR_LO = -3
CMP_NEAR_N = 8


def _cmp_bias_pattern(table, tq):
    assert tq == LANES and CMP_BLK == 32 and REL_MAX_DIST == LANES
    r = jnp.arange(tq, dtype=I32)[:, None]
    m = CMP_NEAR_LO + jnp.arange(CMP_NEAR_N, dtype=I32)[None, :]
    near = _rel_bias(table, r - (CMP_BLK - 1) + CMP_BLK * m)
    far = jnp.broadcast_to(table[REL_BUCKETS - 1][:, None, None], (table.shape[1], tq, 1))
    return jnp.pad(jnp.concatenate([near, far], axis=2), ((0, 0), (0, 0), (0, LANES - CMP_NEAR_N - 1)))


def _nsa_cmp_prompt(q, kcvc, pat, B, T, tq):
    nc, ns = T // CMP_BLK, -(-T // SEL_BLK)
    n_sel = min(N_SEL, ns)
    nq = T // tq
    G = NSA_G
    pair = _pair_mat(nc, ns)
    scale = HD ** -0.5
    rb = tq // CMP_BLK

    def body(q_ref, kc_ref, vc_ref, b_ref, pair_ref, o_ref, mb_ref):
        qi = pl.program_id(1)
        q = q_ref[...]
        R = G * tq
        tpos = qi * tq + lax.broadcasted_iota(I32, (R, 1), 0) % tq
        cmp_end = lax.broadcasted_iota(I32, (R, nc), 1) * CMP_BLK + (CMP_BLK - 1)
        mask = cmp_end <= tpos
        j = lax.broadcasted_iota(I32, (LANES, nc), 0)
        m = rb * qi - lax.broadcasted_iota(I32, (LANES, nc), 1)
        sel = ((j < CMP_NEAR_N) & (m == j + CMP_NEAR_LO)) | ((j == CMP_NEAR_N) & (m >= CMP_NEAR_LO + CMP_NEAR_N))
        sel = jnp.where(sel, 1.0, 0.0).astype(BF16)
        outs = []
        for kv in range(NSA_KVH):
            qs = jnp.concatenate([q[:, (kv * G + gg) * HD:(kv * G + gg + 1) * HD] for gg in range(G)], axis=0) * scale
            s = _d_nt(qs, kc_ref[:, kv * HD:(kv + 1) * HD])
            s = s + _d_x3(jnp.concatenate([b_ref[kv * G + gg] for gg in range(G)], axis=0), sel)
            pc = _masked_softmax(s, mask)
            oc = _d(pc, vc_ref[:, kv * HD:(kv + 1) * HD])
            outs += [oc[gg * tq:(gg + 1) * tq] for gg in range(G)]
            imp = pc[0:tq]
            for gg in range(1, G):
                imp = imp + pc[gg * tq:(gg + 1) * tq]
            score = _block_scores(_d_x3(imp, pair_ref[...]), tpos[0:tq], ns)
            mb_ref[kv] = _top_blocks(score, n_sel).astype(mb_ref.dtype)
        o_ref[...] = jnp.concatenate(outs, axis=1)

    ins = [(q, pl.BlockSpec((tq, NSA_HEADS * HD), lambda b, i: (b * nq + i, 0))),
           (kcvc, pl.BlockSpec((nc, KVW), lambda b, i: (b, 0))),
           (kcvc, pl.BlockSpec((nc, KVW), lambda b, i: (b, 1))),
           (pat, pl.BlockSpec(pat.shape, lambda b, i: (0, 0, 0))),
           _full(pair)]
    outs = [((B * T, NSA_HEADS * HD), F32, pl.BlockSpec((tq, NSA_HEADS * HD), lambda b, i: (b * nq + i, 0))),
            ((B, NSA_KVH, T, ns), BF16, pl.BlockSpec((None, NSA_KVH, tq, ns), lambda b, i: (b, 0, i, 0)))]
    return _call(body, (B, nq), ins, outs, name="nsa_cmp_select")


def _gate_mats():
    hsn = _seg_np(HD)
    return [jnp.asarray(np.roll(hsn.T, br * NSA_HEADS, axis=0), BF16) for br in range(3)]


def _nsa_out(st, o_c, o_s, o_w, gates, p, alpha, ln_g, ln_b, tm):
    def pro(oc, os_, ow, gl, e0, e1, e2):
        sg = _sigmoid(gl)
        return _d_x3(sg, e0) * oc + _d_x3(sg, e1) * os_ + _d_x3(sg, e2) * ow

    def epi(acc, x, gate, g_, b_):
        return (_res_ln(alpha, acc, x, gate, g_, b_),)

    pro_ins = [_rows(o_s, tm), _rows(o_w, tm), _rows(gates, tm)] + [_full(e) for e in _gate_mats()]
    return _mm(o_c, _full(p["nsa_w_o"].astype(BF16)), tm=tm, pro=pro, pro_ins=pro_ins, epi=epi,
               epi_ins=[_rows(st.x, tm), st.mod(2, tm), _full(ln_g), _full(ln_b)],
               outs=[_rows_out(st.M, D, tm)], name="nsa_out")[0]


def _nsa_prompt(st, p, alpha, ln_g, ln_b):
    B, T, M = st.B, st.T, st.M
    tm = min(256, M)
    tq = ATT_T
    nq = T // tq
    table = p["rel_bias"]
    q, cmp_rows, slc_rows, win_rows, gates = _nsa_proj(st, p, tm)
    kcvc = _compress_dense(cmp_rows, _cmp_weights(p))
    nc, ns = T // CMP_BLK, -(-T // SEL_BLK)
    o_c, mb = _nsa_cmp_prompt(q, kcvc, _cmp_bias_pattern(table, tq), B, T, tq)
    r = jnp.arange(tq, dtype=I32)
    far = table[REL_BUCKETS - 1][:, None, None]
    tz = jnp.stack([_rel_bias(table, r[:, None] - r[None, :]) - far,
                    _rel_bias(table, tq + r[:, None] - r[None, :]) - far], axis=1)
    e_blk = jnp.asarray((np.arange(T)[:, None] // SEL_BLK == np.arange(ns)[None, :]).astype(np.float32), BF16)
    q_in = (q, pl.BlockSpec((tq, NSA_HEADS * HD), lambda b, g, i: (b * nq + i, 0)))
    kv_in = lambda a, c: (a, pl.BlockSpec((T, KVW), lambda b, g, i: (b, c)))
    tz_in = (tz, pl.BlockSpec(tz.shape, lambda b, g, i: (0, 0, 0, 0)))
    cfg = dict(name="nsa_slc", tq=tq, hq=NSA_HEADS, hk=NSA_KVH, fox=False, bias=True, aug=ns, window=None, unroll=4)
    extra = [(mb, pl.BlockSpec((None, NSA_KVH, tq, ns), lambda b, g, i: (b, 0, i, 0))),
             (e_blk, pl.BlockSpec(e_blk.shape, lambda b, g, i: (0, 0))), tz_in]
    o_s = _flash(cfg, B, T, 1, q_in, kv_in(slc_rows, 0), kv_in(slc_rows, 1), extra, M)
    cfg = dict(name="nsa_win", tq=tq, hq=NSA_HEADS, hk=NSA_KVH, fox=False, bias=True, aug=0, window=WINDOW, unroll=1)
    o_w = _flash(cfg, B, T, 1, q_in, kv_in(win_rows, 0), kv_in(win_rows, 1), [tz_in], M)
    st.x = _nsa_out(st, o_c, o_s, o_w, gates, p, alpha, ln_g, ln_b, tm)
    shp = (B, T, 2, NSA_KVH, HD)
    keep = min(WINDOW, T)
    return cmp_rows.reshape(shp), slc_rows.reshape(shp), win_rows.reshape(shp)[:, T - keep:]


def _compress_paged(cache_cmp, page_table, cw):
    wc, w2c, b1 = cw
    B, npg = page_table.shape
    PAGE = cache_cmp.shape[1]
    bpp = PAGE // CMP_BLK
    cache = cache_cmp.transpose(0, 2, 3, 4, 1).reshape(cache_cmp.shape[0], 2, KVW, PAGE)
    nb = 2 if B % 2 == 0 else 1
    npages = nb * npg
    nblocks = npages * bpp
    npan = KVW // LANES

    def body(pt_ref, *refs):
        pages = refs[:npages]
        wc_ref, w2_ref, b1_ref, o_ref, xs = refs[npages:]
        for j in range(npages):
            for kv in range(2):
                x = pages[j][kv].T
                for c in range(npan):
                    xs[kv, c, j * PAGE:(j + 1) * PAGE, :] = x[:, c * LANES:(c + 1) * LANES]
        acc = [jnp.zeros((nblocks, KVW), F32) for _ in range(2)]
        for l in range(CMP_BLK):
            for kv in range(2):
                rows = jnp.concatenate([xs[kv, c, pl.ds(l, nblocks, stride=CMP_BLK), :] for c in range(npan)], axis=1)
                acc[kv] = acc[kv] + _d(rows, wc_ref[l // HALF, l % HALF, kv])
        hid = _gelu_tanh(jnp.concatenate(acc, axis=1) + b1_ref[...])
        o_ref[...] = jnp.concatenate([_d(hid[:, :KVW], w2_ref[0]), _d(hid[:, KVW:], w2_ref[1])], axis=1)

    ins = [(cache, pl.BlockSpec((None, 2, KVW, PAGE), lambda g, pt, bb=bb, j=j: (pt[g * nb + bb, j], 0, 0, 0)))
           for bb in range(nb) for j in range(npg)]
    const = lambda a: (a, pl.BlockSpec(a.shape, lambda g, pt: (0,) * a.ndim))
    ins += [const(wc), const(w2c), const(b1)]
    outs = [((B * npg * bpp, 2 * KVW), F32, pl.BlockSpec((nblocks, 2 * KVW), lambda g, pt: (g, 0)))]
    scratch = [pltpu.VMEM((2, npan, npages * PAGE, LANES), F32)]
    return _call(body, (B // nb,), ins, outs, scratch=scratch, prefetch=[page_table], name="nsa_compress_paged")[0]


def _nsa_decode(q, slc_new, win_new, kcvc, cache_slc, cache_win, page_table, gcol, consts, Tn, offset):
    B, npg = page_table.shape
    PAGE = cache_slc.shape[1]
    Wb = cache_win.shape[1]
    R = NSA_HEADS * Tn
    L = offset + Tn
    nc, ns = L // CMP_BLK, -(-L // SEL_BLK)
    n_sel = min(N_SEL, ns)
    nck = npg + 1
    qw = NSA_HEADS * HD
    cache = cache_slc.transpose(0, 2, 3, 4, 1).reshape(cache_slc.shape[0], 2, KVW, PAGE)
    win = cache_win.transpose(0, 2, 3, 4, 1).reshape(B, 2, KVW, Wb)
    scale = HD ** -0.5
    names = ["rep_t", "fold", "unfold", "bias_c", "mask_c", "pair", "e_blk", "bias_s", "bias_w"]
    cvals = [consts[n] for n in names]

    nb = 2 if B % 2 == 0 else 1

    def body(pt_ref, *refs):
        all_pages = refs[:nb * npg]
        (q_ref, sn_ref, wn_ref, kc_ref, vc_ref, win_ref, g_ref, rt_ref, fold_ref, unfold_ref, bc_ref, mc_ref, pair_ref,
         e_ref, bs_ref, bw_ref, o_ref, wout_ref) = refs[nb * npg:]
        for bi in range(nb):
            one_sequence(bi, all_pages[bi * npg:(bi + 1) * npg], q_ref, sn_ref, wn_ref, kc_ref, vc_ref, win_ref, g_ref,
                         rt_ref, fold_ref, unfold_ref, bc_ref, mc_ref, pair_ref, e_ref, bs_ref, bw_ref, o_ref, wout_ref)

    def one_sequence(bi, pages, q_ref, sn_ref, wn_ref, kc_ref, vc_ref, win_ref, g_ref, rt_ref, fold_ref, unfold_ref,
                     bc_ref, mc_ref, pair_ref, e_ref, bs_ref, bw_ref, o_ref, wout_ref):
        rows = slice(bi * Tn, (bi + 1) * Tn)
        row_h = lax.broadcasted_iota(I32, (R, qw), 0) // Tn
        lane_h = lax.broadcasted_iota(I32, (R, qw), 1) // HD
        own = row_h == lane_h
        qrep = _d(rt_ref[...], q_ref[rows, :])
        qbd = (_d(jnp.where(own, qrep, 0.0), fold_ref[...]) * scale).astype(BF16)
        pad = lambda x: jnp.concatenate([x, jnp.zeros((PAGE - Tn, x.shape[1]), x.dtype)], axis=0)

        cblk = slice(bi * nc, (bi + 1) * nc)
        pc = _masked_softmax(_d_nt(qbd, kc_ref[cblk, :]) + bc_ref[...], mc_ref[...] > 0.0)
        o_c = _d(pc, vc_ref[cblk, :])
        imp = []
        for kv in range(NSA_KVH):
            a = pc[kv * NSA_G * Tn:(kv * NSA_G + 1) * Tn]
            for gg in range(1, NSA_G):
                a = a + pc[(kv * NSA_G + gg) * Tn:(kv * NSA_G + gg + 1) * Tn]
            imp.append(a)
        imp = jnp.concatenate(imp, axis=0)
        tpos = offset + lax.broadcasted_iota(I32, (NSA_KVH * Tn, 1), 0) % Tn
        mb = _top_blocks(_block_scores(_d_x3(imp, pair_ref[...]), tpos, ns), n_sel)
        mb = jnp.concatenate([mb[kv * Tn:(kv + 1) * Tn] for kv in range(NSA_KVH) for _ in range(NSA_G)], axis=0)

        sn = pad(sn_ref[rows, :])
        past = npg * PAGE
        kt = jnp.concatenate([pages[j][0].astype(BF16) for j in range(npg)], axis=1)
        vt = jnp.concatenate([pages[j][1].astype(BF16) for j in range(npg)], axis=1)
        s = jnp.concatenate([_d(qbd, kt), _d_nt(qbd, sn[:, :KVW])], axis=1) + _d(mb, e_ref[...]) + bs_ref[...]
        p = jnp.exp(s - jnp.max(s, axis=-1, keepdims=True))
        o_s = (_d_nt(p[:, :past], vt) + _d(p[:, past:], sn[:, KVW:])) / jnp.sum(p, axis=-1, keepdims=True)

        wk, wv = win_ref[bi, 0], win_ref[bi, 1]
        wn = pad(wn_ref[rows, :])
        s = jnp.concatenate([_d(qbd, wk), _d_nt(qbd, wn[:, :KVW])], axis=1) + bw_ref[...]
        p = jnp.exp(s - jnp.max(s, axis=-1, keepdims=True))
        o_w = (_d_nt(p[:, :Wb], wv) + _d(p[:, Wb:], wn[:, KVW:])) / jnp.sum(p, axis=-1, keepdims=True)
        lane = lax.broadcasted_iota(I32, (KVW, Wb), 1)
        for kv, old in enumerate((wk, wv)):
            nt = pltpu.roll(wn[:, kv * KVW:(kv + 1) * KVW].T, PAGE - Tn, axis=1)
            nt = jnp.concatenate([jnp.zeros((KVW, Wb - PAGE), F32), nt], axis=1)
            wout_ref[bi, kv] = jnp.where(lane >= Wb - Tn, nt, pltpu.roll(old, Wb - Tn, axis=1))

        sg = _sigmoid(g_ref[bi])
        o = sg[:, 0:1] * o_c + sg[:, 1:2] * o_s + sg[:, 2:3] * o_w
        of = jnp.where(own, _d_x3(o, unfold_ref[...]), 0.0)
        out = of[0:Tn]
        for h in range(1, NSA_HEADS):
            out = out + of[h * Tn:(h + 1) * Tn]
        o_ref[rows, :] = out

    c2 = lambda b, pt: (0, 0)
    ins = [(cache, pl.BlockSpec((None, 2, KVW, PAGE), lambda b, pt, bb=bb, j=j: (pt[b * nb + bb, j], 0, 0, 0)))
           for bb in range(nb) for j in range(npg)]
    ins += [(q, pl.BlockSpec((nb * Tn, qw), lambda b, pt: (b, 0))),
            (slc_new, pl.BlockSpec((nb * Tn, 2 * KVW), lambda b, pt: (b, 0))),
            (win_new, pl.BlockSpec((nb * Tn, 2 * KVW), lambda b, pt: (b, 0))),
            (kcvc, pl.BlockSpec((nb * nc, KVW), lambda b, pt: (b, 0))),
            (kcvc, pl.BlockSpec((nb * nc, KVW), lambda b, pt: (b, 1))),
            (win, pl.BlockSpec((nb, 2, KVW, Wb), lambda b, pt: (b, 0, 0, 0))),
            (gcol, pl.BlockSpec((nb, R, 3), lambda b, pt: (b, 0, 0)))]
    ins += [(a, pl.BlockSpec(a.shape, c2)) for a in cvals]
    outs = [((B * Tn, qw), F32, pl.BlockSpec((nb * Tn, qw), lambda b, pt: (b, 0))),
            ((B, 2, KVW, Wb), F32, pl.BlockSpec((nb, 2, KVW, Wb), lambda b, pt: (b, 0, 0, 0)))]
    return _call(body, (B // nb,), ins, outs, prefetch=[page_table], name="nsa_decode")


def _nsa_decode_consts(table, Tn, offset, npg, PAGE, Wb):
    R = NSA_HEADS * Tn
    L = offset + Tn
    nc, ns = L // CMP_BLK, -(-L // SEL_BLK)
    Lp = (npg + 1) * PAGE
    tpos = offset + jnp.arange(Tn, dtype=I32)
    rows = lambda x: x.reshape(R, x.shape[-1])
    cmp_end = jnp.arange(nc, dtype=I32) * CMP_BLK + CMP_BLK - 1
    dist_c = tpos[:, None] - cmp_end[None, :]
    spos = jnp.arange(Lp, dtype=I32)
    dist_s = tpos[:, None] - spos[None, :]
    ok_s = (dist_s >= 0) & (spos[None, :] < L)
    col = jnp.arange(Wb + PAGE, dtype=I32)
    wpos = offset - Wb + col
    dist_w = tpos[:, None] - wpos[None, :]
    ok_w = (dist_w >= 0) & (dist_w < WINDOW) & (wpos[None, :] >= 0) & (col[None, :] < Wb + Tn)
    tile = lambda m: jnp.tile(m[None], (NSA_HEADS, 1, 1))
    fold = np.zeros((NSA_HEADS, HD, NSA_KVH, HD), np.float32)
    for h in range(NSA_HEADS):
        fold[h, :, h // NSA_G, :] = np.eye(HD)
    fold = fold.reshape(NSA_HEADS * HD, KVW)
    return dict(
        rep_t=_rep_mat(R, Tn, 0), fold=jnp.asarray(fold, BF16), unfold=jnp.asarray(fold.T, BF16),
        bias_c=rows(_rel_bias(table, dist_c)), mask_c=rows(tile((dist_c >= 0).astype(F32))), pair=_pair_mat(nc, ns),
        e_blk=jnp.asarray((np.arange(ns)[:, None] == np.arange(Lp)[None, :] // SEL_BLK).astype(np.float32), BF16),
        bias_s=rows(_rel_bias(table, dist_s) + tile(jnp.where(ok_s, 0.0, NEG))),
        bias_w=rows(_rel_bias(table, dist_w) + tile(jnp.where(ok_w, 0.0, NEG))))


def _nsa_sample(st, cache_cmp, cache_slc, cache_win, page_table, p, alpha, ln_g, ln_b):
    B, Tn, M = st.B, st.T, st.M
    tm = min(256, M)
    npg = page_table.shape[1]
    PAGE = cache_slc.shape[1]
    Wb = cache_win.shape[1]
    offset = npg * PAGE
    assert offset % CMP_BLK == 0 and Tn < CMP_BLK and Wb == WINDOW and Tn % SUBLANES == 0
    q, cmp_rows, slc_rows, win_rows, gates = _nsa_proj(st, p, tm)
    kcvc = _compress_paged(cache_cmp, page_table, _cmp_weights(p))
    gcol = gates.reshape(B, Tn, LANES)[:, :, :3 * NSA_HEADS].reshape(B, Tn, 3, NSA_HEADS)
    gcol = gcol.transpose(0, 3, 1, 2).reshape(B, NSA_HEADS * Tn, 3)
    consts = _nsa_decode_consts(p["rel_bias"], Tn, offset, npg, PAGE, Wb)
    o, wout = _nsa_decode(q, slc_rows, win_rows, kcvc, cache_slc, cache_win, page_table, gcol, consts, Tn, offset)
    st.x = _out_proj(st, o, p["nsa_w_o"], alpha, ln_g, ln_b, tm, "nsa_out_s")
    shp = (B, Tn, 2, NSA_KVH, HD)
    win_keep = wout.reshape(B, 2, NSA_KVH, HD, Wb).transpose(0, 4, 1, 2, 3)
    return cmp_rows.reshape(shp), slc_rows.reshape(shp), win_keep


def kernel(x_prompt, x_sample, state_rwkv_wkv, state_rwkv_shift, cache_nsa_cmp, cache_nsa_slc, cache_nsa_win, cache_fox_kv, cache_fox_logf, state_gdn_S, state_gdn_conv, page_table, c_prompt, c_sample, w_mod, b_mod, ln_g, ln_b, moe_w_group, moe_b_group, moe_w_router, moe_b_router, moe_w1, moe_w3, moe_w2, rwkv_mu, rwkv_w_rkv, rwkv_w0, rwkv_w1, rwkv_w2, rwkv_a0, rwkv_a1, rwkv_a2, rwkv_g1, rwkv_g2, rwkv_k_k, rwkv_k_a, rwkv_r_k, rwkv_ln_w, rwkv_ln_b, rwkv_w_o, nsa_w_in, nsa_cmp_w1, nsa_cmp_b1, nsa_cmp_w2, nsa_w_o, rel_bias, fox_w_in, fox_b_f, fox_w_o, gdn_w_in, gdn_conv_w, gdn_A_log, gdn_dt_bias, gdn_norm_w, gdn_w_o):
    p = dict(locals())
    Bp, T, _ = x_prompt.shape
    Bs, Tn, _ = x_sample.shape
    depth = w_mod.shape[0]
    alpha = (2 * depth) ** 0.25
    sp = _Stream(x_prompt.reshape(Bp * T, D), Bp, T, min(512, T))
    ss = _Stream(x_sample.reshape(Bs * Tn, D), Bs, Tn, min(256, Bs * Tn))
    nc = Bp + Bs
    c_all = jnp.pad(jnp.concatenate([c_prompt, c_sample], axis=0), ((0, -nc % SUBLANES), (0, 0)))
    out = {}
    for layer in range(depth):
        m6 = _ada(c_all, w_mod, b_mod, layer)
        sp.set_mods(m6[:Bp])
        ss.set_mods(m6[Bp:nc])
        g0 = ln_g[layer, 0].reshape(1, D)
        b0 = ln_b[layer, 0].reshape(1, D)
        kind = layer % 4
        if kind == 0:
            nh = D // RWKV_HSZ
            out["wkv_p"], out["shift_p"] = _rwkv_layer(sp, jnp.zeros((Bp, D), F32),
                                                       jnp.zeros((Bp, nh, RWKV_HSZ, RWKV_HSZ), F32), p, alpha, g0, b0)
            out["wkv_s"], out["shift_s"] = _rwkv_layer(ss, state_rwkv_shift, state_rwkv_wkv, p, alpha, g0, b0)
        elif kind == 1:
            out["cmp_p"], out["slc_p"], out["win_p"] = _nsa_prompt(sp, p, alpha, g0, b0)
            out["cmp_s"], out["slc_s"], out["win_s"] = _nsa_sample(ss, cache_nsa_cmp, cache_nsa_slc, cache_nsa_win,
                                                                   page_table, p, alpha, g0, b0)
        elif kind == 2:
            out["kv_p"], out["logf_p"] = _fox_prompt(sp, p, alpha, g0, b0)
            out["kv_s"], out["logf_s"] = _fox_sample(ss, cache_fox_kv, cache_fox_logf, page_table, p, alpha, g0, b0)
        else:
            out["S_p"], out["conv_p"] = _gdn_layer(sp, jnp.zeros((Bp, GDN_CONV - 1, 3 * D), F32),
                                                   jnp.zeros((Bp, GDN_HEADS, GDN_HSZ, GDN_HSZ), F32), p, alpha, g0, b0)
            out["S_s"], out["conv_s"] = _gdn_layer(ss, state_gdn_conv, state_gdn_S, p, alpha, g0, b0)
        _moe_layer([sp, ss], layer, alpha, p)
    return (sp.x.reshape(Bp, T, D), ss.x.reshape(Bs, Tn, D), out["wkv_p"], out["wkv_s"], out["shift_p"], out["shift_s"],
            out["cmp_p"], out["cmp_s"], out["slc_p"], out["slc_s"], out["win_p"], out["win_s"],
            out["kv_p"], out["kv_s"], out["logf_p"], out["logf_s"], out["S_p"], out["S_s"], out["conv_p"], out["conv_s"])
```

```python
import functools
import math

import numpy as np
import jax
import jax.numpy as jnp
from jax import lax
from jax.experimental import pallas as pl
from jax.experimental.pallas import tpu as pltpu

F32 = jnp.float32
BF16 = jnp.bfloat16
I32 = jnp.int32
NEG = -1e30
LN_EPS = 1e-5
D = 1024
LANES = 128
SUBLANES = 8
MXU_TILE = 256
VMEM_LIMIT_MB = 56

RWKV_HSZ = 64
RWKV_GN_EPS = 64e-5
NSA_HEADS, NSA_KVH, HD = 16, 4, 64
NSA_G = NSA_HEADS // NSA_KVH
CMP_BLK, SEL_BLK, N_SEL, WINDOW = 32, 64, 16, 512
FORCE_SCORE = 1e4
REL_BUCKETS, REL_MAX_DIST = 32, 128
FOX_HEADS = 16
GDN_HEADS, GDN_HSZ, GDN_CONV = 8, 128, 4
MOE_GROUPS, MOE_EPG, MOE_BLK = 4, 8, 256
MOE_EXPERTS = MOE_GROUPS * MOE_EPG
ATT_T = 128
SEL_NEG = -65536.0
PAGES_PER_STEP = 8


def _d(a, b):
    return jnp.dot(a.astype(BF16), b.astype(BF16), preferred_element_type=F32)


def _d_nt(a, b):
    return lax.dot_general(a.astype(BF16), b.astype(BF16), (((1,), (1,)), ((), ())),
                           preferred_element_type=F32)


def _split3(x):
    h = x.astype(BF16)
    r1 = x - h.astype(F32)
    m = r1.astype(BF16)
    l = (r1 - m.astype(F32)).astype(BF16)
    return h, m, l


def _d_x3(x, sel):
    h, m, l = _split3(x)
    return _d(h, sel) + _d(m, sel) + _d(l, sel)


def _d_3x(sel, x):
    h, m, l = _split3(x)
    return _d(sel, h) + _d(sel, m) + _d(sel, l)


def _d_f32(x, w):
    xh = x.astype(BF16)
    xl = (x - xh.astype(F32)).astype(BF16)
    wh = w.astype(BF16)
    wl = (w - wh.astype(F32)).astype(BF16)
    return _d(xh, wh) + (_d(xh, wl) + _d(xl, wh))


def _sigmoid(x):
    return 1.0 / (1.0 + jnp.exp(-x))


def _softplus(x):
    return jnp.maximum(x, 0.0) + jnp.log(1.0 + jnp.exp(-jnp.abs(x)))


def _silu(x):
    return x * _sigmoid(x)


def _gelu_tanh(x):
    return 0.5 * x * (1.0 + jnp.tanh(math.sqrt(2.0 / math.pi) * (x + 0.044715 * (x * x * x))))


def _layer_norm(z, g, b):
    mu = jnp.mean(z, axis=-1, keepdims=True)
    zc = z - mu
    var = jnp.mean(zc * zc, axis=-1, keepdims=True)
    return zc * lax.rsqrt(var + LN_EPS) * g + b


def _modulate(x, sc, sh):
    return x * (1.0 + sc) + sh


def _res_ln(alpha, y, xres, gate, g, b):
    return _layer_norm(alpha * xres + (1.0 + gate) * y, g, b)


@functools.lru_cache(maxsize=None)
def _seg_np(hsz):
    head = np.arange(D) // hsz
    hs = (head[:, None] == np.arange(LANES)[None, :]).astype(np.float32)
    return hs


def _seg_consts(hsz):
    hs = _seg_np(hsz)
    return jnp.asarray(hs, BF16), jnp.asarray(hs.T, BF16)


def _scan_consts(hsz):
    nh = D // hsz
    head = np.arange(D) // hsz
    slot_j = np.arange(LANES) // 16
    slot_h = np.arange(LANES) % 16
    hexp = np.zeros((SUBLANES, LANES, D), np.float32)
    for j in range(SUBLANES):
        hexp[j] = ((slot_j[:, None] == j) & (slot_h[:, None] == head[None, :]) & (slot_h[:, None] < nh))
    hsum = np.transpose(hexp, (0, 2, 1))
    blk = np.arange(MXU_TILE) // hsz
    bd = (blk[:, None] == blk[None, :]).astype(np.float32)
    return jnp.asarray(hexp, BF16), jnp.asarray(hsum, BF16), jnp.asarray(bd, BF16)


def _call(body, grid, ins, outs, scratch=(), name=None, prefetch=None, aliases=None):
    arrays = [a for a, _ in ins]
    in_specs = [s for _, s in ins]
    out_shape = [jax.ShapeDtypeStruct(s, d) for s, d, _ in outs]
    out_specs = [s for _, _, s in outs]
    params = pltpu.CompilerParams(dimension_semantics=("arbitrary",) * len(grid),
                                  vmem_limit_bytes=VMEM_LIMIT_MB << 20)
    kw = {}
    if aliases:
        kw["input_output_aliases"] = aliases
    if prefetch is None:
        fn = pl.pallas_call(body, grid=grid, in_specs=in_specs, out_specs=out_specs, out_shape=out_shape,
                            scratch_shapes=list(scratch), compiler_params=params, name=name, **kw)
        res = fn(*arrays)
    else:
        gs = pltpu.PrefetchScalarGridSpec(num_scalar_prefetch=len(prefetch), grid=grid, in_specs=in_specs,
                                          out_specs=out_specs, scratch_shapes=list(scratch))
        fn = pl.pallas_call(body, grid_spec=gs, out_shape=out_shape, compiler_params=params, name=name, **kw)
        res = fn(*prefetch, *arrays)
    return list(res)


def _full(a):
    nd = a.ndim
    return (a, pl.BlockSpec(a.shape, lambda *_: (0,) * nd))


def _rows(a, tm):
    return (a, pl.BlockSpec((tm, a.shape[1]), lambda i, *_: (i, 0)))


def _rows_out(M, C, tm, dtype=F32):
    return ((M, C), dtype, pl.BlockSpec((tm, C), lambda i, *_: (i, 0)))


class _Stream:
    def __init__(self, x, B, T, tm):
        self.x, self.B, self.T, self.tm = x, B, T, tm
        self.M = B * T
        self.m6 = None

    def set_mods(self, m6):
        self.m6 = m6
        self.rep = jnp.repeat(m6, self.T, axis=0) if self.T < self.tm else None

    def mod(self, c, tm=None):
        tm = tm or self.tm
        if self.T % tm == 0:
            tpb = self.T // tm
            a = self.m6[:, c * D:(c + 1) * D].reshape(self.B, 1, D)
            return (a, pl.BlockSpec((None, 1, D), lambda i, *_: (i // tpb, 0, 0)))
        assert tm % self.T == 0 and self.M % tm == 0
        a = self.rep[:, c * D:(c + 1) * D].reshape(self.M // tm, tm, D)
        return (a, pl.BlockSpec((None, tm, D), lambda i, *_: (i, 0, 0)))


def _mm(x, w_in, *, tm, pro=None, pro_ins=(), epi=None, epi_ins=(), outs=None, name="mm"):
    M, K = x.shape
    n_pro, n_epi = len(pro_ins), len(epi_ins)
    n_out = len(outs)

    def body(*refs):
        x_ref = refs[0]
        pro_refs = refs[1:1 + n_pro]
        w_ref = refs[1 + n_pro]
        epi_refs = refs[2 + n_pro:2 + n_pro + n_epi]
        out_refs = refs[2 + n_pro + n_epi:2 + n_pro + n_epi + n_out]
        a = x_ref[...]
        if pro is not None:
            a = pro(a, *[r[...] for r in pro_refs])
        acc = _d(a, w_ref[...])
        res = epi(acc, *[r[...] for r in epi_refs]) if epi is not None else (acc,)
        for o, r in zip(out_refs, res):
            o[...] = r.astype(o.dtype)

    ins = [_rows(x, tm)] + list(pro_ins) + [w_in] + list(epi_ins)
    return _call(body, (M // tm,), ins, outs, name=name)


def _ada(c_all, w_mod, b_mod, layer):
    Mp = c_all.shape[0]
    N = w_mod.shape[2]
    tn = 1536

    def body(c_ref, w_ref, b_ref, o_ref):
        o_ref[...] = _d(_silu(c_ref[...]), w_ref[...]) + b_ref[...]

    ins = [(c_all, pl.BlockSpec((Mp, D), lambda j: (0, 0))),
           (w_mod, pl.BlockSpec((None, D, tn), lambda j: (layer, 0, j))),
           (b_mod.reshape(b_mod.shape[0], 1, N), pl.BlockSpec((None, 1, tn), lambda j: (layer, 0, j)))]
    outs = [((Mp, N), F32, pl.BlockSpec((Mp, tn), lambda j: (0, j)))]
    return _call(body, (N // tn,), ins, outs, name="ada_mod")[0]


def _pick_tile(*sizes, cap=512):
    t = cap
    while t > SUBLANES and any(s % t for s in sizes):
        t //= 2
    assert all(s % t == 0 for s in sizes), sizes
    return t


def _route(lg):
    lane = lax.broadcasted_iota(I32, lg.shape, 1)
    big = jnp.int32(1 << 20)
    isg = lane < MOE_GROUPS
    gl = jnp.where(isg, lg, NEG)
    gmax = jnp.max(gl, axis=-1, keepdims=True)
    gsel = jnp.min(jnp.where(gl == gmax, lane, big), axis=-1, keepdims=True)
    gsum = jnp.sum(jnp.where(isg, jnp.exp(gl - gmax), 0.0), axis=-1, keepdims=True)
    gw = 1.0 / gsum
    lo = MOE_GROUPS + MOE_EPG * gsel
    ise = (lane >= lo) & (lane < lo + MOE_EPG)
    el = jnp.where(ise, lg, NEG)
    emax = jnp.max(el, axis=-1, keepdims=True)
    ep = jnp.where(ise, jnp.exp(el - emax), 0.0)
    prob = ep / jnp.sum(ep, axis=-1, keepdims=True)
    pm = jnp.where(ise, prob, -1.0)
    p1 = jnp.max(pm, axis=-1, keepdims=True)
    i1 = jnp.min(jnp.where(pm == p1, lane, big), axis=-1, keepdims=True)
    pm2 = jnp.where(lane == i1, -1.0, pm)
    p2 = jnp.max(pm2, axis=-1, keepdims=True)
    i2 = jnp.min(jnp.where(pm2 == p2, lane, big), axis=-1, keepdims=True)
    den = p1 + p2
    w1 = gw * p1 / den
    w2 = gw * p2 / den
    e1 = (i1 - MOE_GROUPS).astype(F32)
    e2 = (i2 - MOE_GROUPS).astype(F32)
    return jnp.where(lane == 0, e1, jnp.where(lane == 1, e2, jnp.where(lane == 2, w1, jnp.where(lane == 3, w2, 0.0))))


NSEG = D // LANES


def _to_tiles(ref, x):
    for s in range(NSEG):
        ref[:, s, :] = x[:, s * LANES:(s + 1) * LANES]


def _from_tiles(ref):
    return jnp.concatenate([ref[:, s, :] for s in range(NSEG)], axis=1)


def _moe_router(st, wgr, bgr, h_all, off):
    tm = _pick_tile(st.M, off, cap=st.tm)
    b0 = off // tm

    def body(x_ref, sc_ref, sh_ref, w_ref, b_ref, hin_ref, h_ref, r_ref):
        h = _modulate(x_ref[...], sc_ref[...], sh_ref[...])
        _to_tiles(h_ref, h)
        r_ref[...] = _route(_d_f32(h, w_ref[...]) + b_ref[...])

    ins = [_rows(st.x, tm), st.mod(4, tm), st.mod(3, tm), _full(wgr), _full(bgr),
           (h_all, pl.BlockSpec(memory_space=pl.ANY))]
    outs = [(h_all.shape, F32, pl.BlockSpec((tm, NSEG, LANES), lambda i: (b0 + i, 0, 0))),
            _rows_out(st.M, LANES, tm)]
    return _call(body, (st.M // tm,), ins, outs, aliases={5: 0}, name="moe_router")


def _moe_counts(rinfo, R):
    Mtot = rinfo.shape[0]
    nt = Mtot // R

    def body(r_ref, o_ref):
        j = pl.program_id(0)
        t = pl.program_id(1)

        @pl.when((j == 0) & (t == 0))
        def _():
            o_ref[...] = jnp.zeros_like(o_ref)

        xt = r_ref[...].T
        row = jnp.where(j == 0, xt[0:1, :], xt[1:2, :])
        sub = lax.broadcasted_iota(I32, (LANES, R), 0).astype(F32)
        oh = jnp.where(sub == row, 1.0, 0.0)
        o_ref[...] += jnp.sum(oh, axis=1, keepdims=True)

    ins = [(rinfo, pl.BlockSpec((R, LANES), lambda j, t: (t, 0)))]
    outs = [((LANES, LANES), F32, pl.BlockSpec((LANES, LANES), lambda j, t: (0, 0)))]
    return _call(body, (2, nt), ins, outs, name="moe_counts")[0]


def _moe_dest(rinfo, pstart, R):
    Mtot = rinfo.shape[0]
    nt = Mtot // R
    upper = jnp.asarray(np.triu(np.ones((R, R), np.float32), 1), BF16)

    def body(r_ref, p_ref, u_ref, o_ref, carry):
        j = pl.program_id(0)
        t = pl.program_id(1)

        @pl.when((j == 0) & (t == 0))
        def _():
            carry[...] = jnp.zeros_like(carry)

        xt = r_ref[...].T
        row = jnp.where(j == 0, xt[0:1, :], xt[1:2, :])
        sub = lax.broadcasted_iota(I32, (LANES, R), 0).astype(F32)
        oh = jnp.where(sub == row, 1.0, 0.0)
        cum = _d(oh, u_ref[...])
        base = carry[:, 0:1] + p_ref[:, 0:1]
        dest = jnp.sum(oh * (cum + base), axis=0, keepdims=True)
        o_ref[...] = dest.astype(I32)
        carry[...] += jnp.sum(oh, axis=1, keepdims=True)

    ins = [(rinfo, pl.BlockSpec((R, LANES), lambda j, t: (t, 0))), _full(pstart), _full(upper)]
    outs = [((2 * nt, 1, R), I32, pl.BlockSpec((None, 1, R), lambda j, t: (j * nt + t, 0, 0)))]
    return _call(body, (2, nt), ins, outs, scratch=[pltpu.VMEM((LANES, LANES), F32)], name="moe_dest")[0]


def _moe_ffn(h_all, slots, blk_expert, nvalid, w1, w3, w2, layer, Mtot):
    nblk = slots.shape[0] // MOE_BLK
    FF = w1.shape[-1]
    any_spec = pl.BlockSpec(memory_space=pl.ANY)
    GRP = SUBLANES

    def body(be_ref, nv_ref, slot_ref, h_ref, w1_ref, w3_ref, w2_ref, y_ref, xbuf, ybuf, sem_in, sem_out):
        i = pl.program_id(0)
        nv = nv_ref[0]
        buf = i % 2

        def row_copy(kind, blk, b, r):
            s = slot_ref[blk * MOE_BLK + r]
            if kind == "gather":
                tok = jnp.where(s >= 2 * Mtot, s - 2 * Mtot, jnp.where(s >= Mtot, s - Mtot, s))
                return pltpu.make_async_copy(h_ref.at[tok], xbuf.at[b, r], sem_in.at[b])
            return pltpu.make_async_copy(ybuf.at[b, r], y_ref.at[s], sem_out.at[b])

        def each_row(kind, blk, b, start):
            for r in range(MOE_BLK):
                cp = row_copy(kind, blk, b, r)
                cp.start() if start else cp.wait()

        @pl.when(i < nv)
        def _():
            @pl.when(i == 0)
            def _():
                ybuf[...] = jnp.zeros_like(ybuf)
                for b in range(2):
                    cp = pltpu.make_async_copy(ybuf.at[b], y_ref.at[pl.ds(2 * Mtot + b * MOE_BLK, MOE_BLK)],
                                               sem_out.at[b])
                    cp.start()
                    cp.wait()
                each_row("gather", 0, 0, True)

            nxt = jnp.minimum(i + 1, nv - 1)
            each_row("gather", i, buf, False)
            x = _from_tiles(xbuf.at[buf]).astype(BF16)
            each_row("gather", nxt, 1 - buf, True)
            a = _d(x, w1_ref[...])
            b = _d(x, w3_ref[...])
            y = _d(_silu(a) * b, w2_ref[...])

            @pl.when(i >= 2)
            def _():
                each_row("scatter", i - 2, buf, False)

            _to_tiles(ybuf.at[buf], y)
            each_row("scatter", i, buf, True)

            @pl.when(i == nv - 1)
            def _():
                each_row("gather", nxt, 1 - buf, False)

                @pl.when(i >= 1)
                def _():
                    each_row("scatter", i - 1, 1 - buf, False)

                each_row("scatter", i, buf, False)

    def blk(i, be, nv, sl):
        return be[jnp.minimum(i, nv[0] - 1)]

    ins = [(h_all, any_spec),
           (w1, pl.BlockSpec((None, None, D, FF), lambda i, be, nv, sl: (layer, blk(i, be, nv, sl), 0, 0))),
           (w3, pl.BlockSpec((None, None, D, FF), lambda i, be, nv, sl: (layer, blk(i, be, nv, sl), 0, 0))),
           (w2, pl.BlockSpec((None, None, FF, D), lambda i, be, nv, sl: (layer, blk(i, be, nv, sl), 0, 0)))]
    outs = [((2 * Mtot + 2 * MOE_BLK, NSEG, LANES), F32, any_spec)]
    scratch = [pltpu.VMEM((2, MOE_BLK, NSEG, LANES), F32), pltpu.VMEM((2, MOE_BLK, NSEG, LANES), F32),
               pltpu.SemaphoreType.DMA((2,)), pltpu.SemaphoreType.DMA((2,))]
    return _call(body, (nblk,), ins, outs, scratch=scratch, prefetch=[blk_expert, nvalid, slots], name="moe_ffn")[0]


def _moe_combine(st, yslot, rinfo_all, off, Mtot, alpha, ln_g, ln_b):
    tm = _pick_tile(st.M, off, Mtot, cap=st.tm)
    b0, b1, br = off // tm, (Mtot + off) // tm, off // tm

    def body(y0_ref, y1_ref, r_ref, x_ref, gate_ref, g_ref, b_ref, o_ref):
        r = r_ref[...]
        y = r[:, 2:3] * _from_tiles(y0_ref) + r[:, 3:4] * _from_tiles(y1_ref)
        o_ref[...] = _res_ln(alpha, y, x_ref[...], gate_ref[...], g_ref[...], b_ref[...])

    ins = [(yslot, pl.BlockSpec((tm, NSEG, LANES), lambda i: (b0 + i, 0, 0))),
           (yslot, pl.BlockSpec((tm, NSEG, LANES), lambda i: (b1 + i, 0, 0))),
           (rinfo_all, pl.BlockSpec((tm, LANES), lambda i: (br + i, 0))),
           _rows(st.x, tm), st.mod(5, tm), _full(ln_g), _full(ln_b)]
    return _call(body, (st.M // tm,), ins, [_rows_out(st.M, D, tm)], name="moe_combine")[0]


def _moe_layer(streams, layer, alpha, p):
    wgr = jnp.zeros((D, LANES), F32).at[:, :MOE_GROUPS].set(p["moe_w_group"][layer])
    wgr = wgr.at[:, MOE_GROUPS:MOE_GROUPS + MOE_EXPERTS].set(p["moe_w_router"][layer])
    bgr = jnp.zeros((1, LANES), F32).at[0, :MOE_GROUPS].set(p["moe_b_group"][layer])
    bgr = bgr.at[0, MOE_GROUPS:MOE_GROUPS + MOE_EXPERTS].set(p["moe_b_router"][layer])
    Mtot = sum(st.M for st in streams)
    h_all = jnp.zeros((Mtot, NSEG, LANES), F32)
    rs, off = [], 0
    for st in streams:
        h_all, r = _moe_router(st, wgr, bgr, h_all, off)
        rs.append(r)
        off += st.M
    rinfo = jnp.concatenate(rs, axis=0)
    R = _pick_tile(Mtot)
    counts = _moe_counts(rinfo, R)[:MOE_EXPERTS, 0].astype(I32)
    padded = (counts + MOE_BLK - 1) // MOE_BLK * MOE_BLK
    pad_end = jnp.cumsum(padded)
    pstart = jnp.zeros((LANES,), F32).at[:MOE_EXPERTS].set((pad_end - padded).astype(F32))
    pstart = jnp.broadcast_to(pstart[:, None], (LANES, LANES))
    nblk = -(-2 * Mtot // MOE_BLK) + MOE_EXPERTS
    blk_first = jnp.arange(nblk, dtype=I32) * MOE_BLK
    blk_expert = jnp.minimum(jnp.sum((pad_end[None, :] <= blk_first[:, None]).astype(I32), axis=1), MOE_EXPERTS - 1)
    nvalid = (pad_end[-1:] // MOE_BLK).astype(I32)
    dest = _moe_dest(rinfo, pstart, R).reshape(-1)
    spare = 2 * Mtot + jnp.arange(nblk * MOE_BLK, dtype=I32) % (2 * MOE_BLK)
    slots = spare.at[dest].set(jnp.arange(2 * Mtot, dtype=I32))
    yslot = _moe_ffn(h_all, slots, blk_expert, nvalid, p["moe_w1"], p["moe_w3"], p["moe_w2"], layer, Mtot)
    ln_g = p["ln_g"][layer, 1].reshape(1, D)
    ln_b = p["ln_b"][layer, 1].reshape(1, D)
    off = 0
    for st in streams:
        st.x = _moe_combine(st, yslot, rinfo, off, Mtot, alpha, ln_g, ln_b)
        off += st.M


def _scan_body(nbg, nv, tb, w_ref, kkn_ref, b_ref, k_ref, r_ref, vt_ref, s0_ref, hexp_ref, hsum_ref, bd_ref,
               o_ref, sf_ref, s_scr):
    t = pl.program_id(1)

    @pl.when(t == 0)
    def _():
        s_scr[...] = s0_ref[...]

    def sub(sb, carry):
        base = pl.multiple_of(sb * SUBLANES, SUBLANES)
        rows = [[ref[bb, pl.ds(base, SUBLANES), :] for bb in range(nbg)]
                for ref in (w_ref, kkn_ref, b_ref, k_ref, r_ref)]
        vt = vt_ref[:, sb].reshape(nbg * nv, LANES).astype(BF16)
        oacc = jnp.zeros((nbg * nv, LANES), F32)
        for j in range(SUBLANES):
            S = [s_scr[bb] for bb in range(nbg)]
            P = jnp.concatenate([S[bb] * rows[1][bb][j:j + 1] for bb in range(nbg)], axis=0).astype(BF16)
            sa = jnp.concatenate([_d(P[:, c0:c0 + MXU_TILE], bd_ref[...]) for c0 in range(0, D, MXU_TILE)], axis=1)
            vb = _d(vt, hexp_ref[j])
            P2 = []
            for bb in range(nbg):
                sl = slice(bb * nv, (bb + 1) * nv)
                Sn = S[bb] * rows[0][bb][j:j + 1] + sa[sl] * rows[2][bb][j:j + 1] + vb[sl] * rows[3][bb][j:j + 1]
                s_scr[bb] = Sn
                P2.append(Sn * rows[4][bb][j:j + 1])
            oacc = oacc + _d(jnp.concatenate(P2, axis=0), hsum_ref[j])
        o_ref[:, sb] = oacc.reshape(nbg, nv, LANES)
        return carry

    lax.fori_loop(0, tb // SUBLANES, sub, 0)

    @pl.when(t == pl.num_programs(1) - 1)
    def _():
        sf_ref[...] = s_scr[...]


def _delta_scan(w, kkn, b, k, r, v, S0, B, T, hsz):
    nh = D // hsz
    nv = hsz
    nbg = 4 if B % 4 == 0 else (2 if B % 2 == 0 else 1)
    tb = min(64, T)
    hexp, hsum, bd = _scan_consts(hsz)
    vt = v.reshape(B, T // SUBLANES, SUBLANES, nh, nv).transpose(0, 1, 4, 2, 3)
    vt = jnp.pad(vt, ((0, 0),) * 4 + ((0, 16 - nh),)).reshape(B, T // SUBLANES, nv, LANES)
    seq = lambda a: (a.reshape(B, T, D), pl.BlockSpec((nbg, tb, D), lambda g, t: (g, t, 0)))
    ins = [seq(w), seq(kkn), seq(b), seq(k), seq(r),
           (vt, pl.BlockSpec((nbg, tb // SUBLANES, nv, LANES), lambda g, t: (g, t, 0, 0))),
           (S0, pl.BlockSpec((nbg, nv, D), lambda g, t: (g, 0, 0))),
           _full(hexp), _full(hsum), _full(bd)]
    outs = [((B, T // SUBLANES, nv, LANES), F32,
             pl.BlockSpec((nbg, tb // SUBLANES, nv, LANES), lambda g, t: (g, t, 0, 0))),
            ((B, nv, D), F32, pl.BlockSpec((nbg, nv, D), lambda g, t: (g, 0, 0)))]
    body = functools.partial(_scan_body, nbg, nv, tb)
    op, sf = _call(body, (B // nbg, T // tb), ins, outs, scratch=[pltpu.VMEM((nbg, nv, D), F32)], name="delta_scan")
    o = op.reshape(B, T // SUBLANES, nv, SUBLANES, 16)[..., :nh].transpose(0, 1, 3, 4, 2).reshape(B * T, D)
    return o, sf


def _shifted_rows(h, first, period, shift=1):
    row = lax.broadcasted_iota(I32, h.shape, 0)
    return jnp.where(row % period < shift, first, pltpu.roll(h, shift, axis=0))


def _rwkv_prep(st, shift_prev, p, tm):
    long_seq = st.T % tm == 0
    tpb = st.T // tm if long_seq else 1
    hs, he = _seg_consts(RWKV_HSZ)
    row = lambda a: _full(a.reshape(1, D))
    wts = [_full(p["rwkv_mu"]), _full(p["rwkv_w_rkv"].astype(BF16)),
           _full(p["rwkv_w1"].astype(BF16)), _full(p["rwkv_w2"].astype(BF16)),
           _full(p["rwkv_a1"].astype(BF16)), _full(p["rwkv_a2"].astype(BF16)),
           _full(p["rwkv_g1"].astype(BF16)), _full(p["rwkv_g2"].astype(BF16)),
           row(p["rwkv_w0"]), row(p["rwkv_a0"]), row(p["rwkv_k_k"]), row(p["rwkv_k_a"]), _full(hs), _full(he)]
    if long_seq:
        nsub = tm // SUBLANES
        first_ins = [(st.x, pl.BlockSpec((SUBLANES, D), lambda i: (jnp.maximum(i * nsub - 1, 0), 0))),
                     (shift_prev.reshape(st.B, 1, D), pl.BlockSpec((None, 1, D), lambda i: (i // tpb, 0, 0)))]
    else:
        first_ins = [_rows(jnp.repeat(shift_prev, st.T, axis=0), tm)]
    nf = len(first_ins)

    def body(x_ref, sc_ref, sh_ref, *refs):
        first_refs, refs = refs[:nf], refs[nf:]
        (mu_ref, wrkv_ref, w1_ref, w2_ref, a1_ref, a2_ref, g1_ref, g2_ref, w0_ref, a0_ref, kk_ref, ka_ref,
         hs_ref, he_ref) = refs[:14]
        h_ref, r_ref, w_ref, k_ref, v_ref, kkn_ref, b_ref, g_ref = refs[14:]
        sc, sh = sc_ref[...], sh_ref[...]
        h = _modulate(x_ref[...], sc, sh)
        if long_seq:
            hh = _modulate(first_refs[0][...], sc, sh)[SUBLANES - 1:SUBLANES]
            first = jnp.where(pl.program_id(0) % tpb == 0, first_refs[1][...], hh)
            hprev = _shifted_rows(h, first, tm)
        else:
            hprev = _shifted_rows(h, first_refs[0][...], st.T)
        xx = hprev - h
        mu = mu_ref[...]
        xr, xw, xk, xv, xa, xg = [h + xx * mu[i:i + 1] for i in range(6)]
        r = _d(xr, wrkv_ref[0])
        k = _d(xk, wrkv_ref[1])
        v = _d(xv, wrkv_ref[2])
        logw = -_softplus(-(w0_ref[...] + _d(jnp.tanh(_d(xw, w1_ref[...])), w2_ref[...]))) - 0.5
        a = _sigmoid(a0_ref[...] + _d(_d(xa, a1_ref[...]), a2_ref[...]))
        g = _d(_sigmoid(_d(xg, g1_ref[...])), g2_ref[...])
        kk = k * kk_ref[...]
        inv = lax.rsqrt(_d_x3(kk * kk, hs_ref[...]) + 1e-6)
        kk = kk * _d_x3(inv, he_ref[...])
        h_ref[...] = h
        r_ref[...] = r
        w_ref[...] = jnp.exp(-jnp.exp(logw))
        k_ref[...] = k * (1.0 + (a - 1.0) * ka_ref[...])
        v_ref[...] = v
        kkn_ref[...] = -kk
        b_ref[...] = kk * a
        g_ref[...] = g

    ins = [_rows(st.x, tm), st.mod(1, tm), st.mod(0, tm)] + first_ins + wts
    outs = [_rows_out(st.M, D, tm) for _ in range(8)]
    return _call(body, (st.M // tm,), ins, outs, name="rwkv_prep")


def _rwkv_out(st, o, r, kmod, v, g, p, alpha, ln_g, ln_b, tm):
    hs, he = _seg_consts(RWKV_HSZ)
    inv_n = 1.0 / RWKV_HSZ

    def pro(o, r, k, v, g, lw, lb, rk, hs, he):
        mean = _d_x3(_d_x3(o, hs) * inv_n, he)
        c = o - mean
        rstd = lax.rsqrt(_d_x3(c * c, hs) * inv_n + RWKV_GN_EPS)
        on = c * _d_x3(rstd, he) * lw + lb
        bonus = _d_x3(_d_x3(r * k * rk, hs), he) * v
        return (on + bonus) * g

    def epi(acc, x, gate, g_, b_):
        return (_res_ln(alpha, acc, x, gate, g_, b_),)

    row = lambda a: _full(a.reshape(1, D))
    pro_ins = [_rows(a, tm) for a in (r, kmod, v, g)] + [row(p["rwkv_ln_w"]), row(p["rwkv_ln_b"]),
                                                       row(p["rwkv_r_k"]), _full(hs), _full(he)]
    epi_ins = [_rows(st.x, tm), st.mod(2, tm), _full(ln_g), _full(ln_b)]
    return _mm(o, _full(p["rwkv_w_o"].astype(BF16)), tm=tm, pro=pro, pro_ins=pro_ins, epi=epi, epi_ins=epi_ins,
               outs=[_rows_out(st.M, D, tm)], name="rwkv_out")[0]


def _rwkv_layer(st, shift_prev, wkv0, p, alpha, ln_g, ln_b):
    B, T = st.B, st.T
    tm = min(256, st.M)
    nh = D // RWKV_HSZ
    h, r, w, kmod, v, kkn, b, g = _rwkv_prep(st, shift_prev, p, tm)
    S0 = wkv0.transpose(0, 2, 1, 3).reshape(B, RWKV_HSZ, D)
    o, sf = _delta_scan(w, kkn, b, kmod, r, v, S0, B, T, RWKV_HSZ)
    st.x = _rwkv_out(st, o, r, kmod, v, g, p, alpha, ln_g, ln_b, tm)
    wkv = sf.reshape(B, RWKV_HSZ, nh, RWKV_HSZ).transpose(0, 2, 1, 3)
    return wkv, h.reshape(B, T, D)[:, -1]


def _pad_cols(w, n):
    return jnp.pad(w, ((0, 0), (0, n - w.shape[1])))


def _gdn_proj(st, p, tm):
    C = 3 * D
    w = _pad_cols(p["gdn_w_in"], C + D + LANES).astype(BF16)

    def epi(acc):
        return acc[:, :C], acc[:, C:C + D], acc[:, C + D:]

    outs = [_rows_out(st.M, C, tm), _rows_out(st.M, D, tm), _rows_out(st.M, LANES, tm)]
    return _mm(st.x, _full(w), tm=tm, pro=_modulate, pro_ins=[st.mod(1, tm), st.mod(0, tm)], epi=epi, outs=outs,
               name="gdn_proj")


def _gdn_conv(st, pre, ba, conv_buf, p, tm, chunked):
    C = 3 * D
    H = GDN_HEADS
    long_seq = st.T % tm == 0
    tpb = st.T // tm if long_seq else 1
    hs, he = _seg_consts(GDN_HSZ)
    hsn = _seg_np(GDN_HSZ)
    he_b = jnp.asarray(hsn.T, BF16)
    he_a = jnp.asarray(np.roll(hsn.T, H, axis=0), BF16)
    alog = jnp.zeros((1, LANES), F32).at[0, H:2 * H].set(p["gdn_A_log"])
    dtb = jnp.zeros((1, LANES), F32).at[0, H:2 * H].set(p["gdn_dt_bias"])
    if long_seq:
        nsub = tm // SUBLANES
        init8 = jnp.pad(conv_buf, ((0, 0), (SUBLANES - (GDN_CONV - 1), 0), (0, 0)))
        first_ins = [(pre, pl.BlockSpec((SUBLANES, C), lambda i: (jnp.maximum(i * nsub - 1, 0), 0))),
                     (init8, pl.BlockSpec((None, SUBLANES, C), lambda i: (i // tpb, 0, 0)))]
    else:
        padded = jnp.pad(conv_buf, ((0, 0), (0, st.T), (0, 0)))
        first_ins = [_rows(padded[:, GDN_CONV - 1 - j:GDN_CONV - 1 - j + st.T].reshape(st.M, C), tm)
                     for j in range(1, GDN_CONV)]
    nf = len(first_ins)

    def body(pre_ref, ba_ref, *refs):
        first_refs, refs = refs[:nf], refs[nf:]
        cw_ref, alog_ref, dtb_ref, hs_ref, he_ref, heb_ref, hea_ref = refs[:7]
        w_ref, kkn_ref, k_ref, q_ref, v_ref = refs[7:]
        x = pre_ref[...]
        if long_seq:
            halo = jnp.where(pl.program_id(0) % tpb == 0, first_refs[1][...], first_refs[0][...])
            big = jnp.concatenate([halo, x], axis=0)
            sh = [pltpu.roll(big, j, axis=0)[SUBLANES:] for j in range(1, GDN_CONV)]
        else:
            sh = [_shifted_rows(x, first_refs[j - 1][...], st.T, j) for j in range(1, GDN_CONV)]
        cw = cw_ref[...]
        conv = sh[2] * cw[0:1]
        conv = conv + sh[1] * cw[1:2]
        conv = conv + sh[0] * cw[2:3]
        conv = conv + x * cw[3:4]
        c = _silu(conv)
        q, k, v = c[:, :D], c[:, D:2 * D], c[:, 2 * D:]
        qn = q * _d_x3(lax.rsqrt(_d_x3(q * q, hs_ref[...]) + 1e-6), he_ref[...]) * (GDN_HSZ ** -0.5)
        kn = k * _d_x3(lax.rsqrt(_d_x3(k * k, hs_ref[...]) + 1e-6), he_ref[...])
        ba = ba_ref[...]
        logdecay = -jnp.exp(alog_ref[...]) * _softplus(ba + dtb_ref[...])
        if chunked:
            w_ref[...] = _sigmoid(ba)
            kkn_ref[...] = logdecay
            k_ref[...] = kn
            q_ref[...] = qn
            v_ref[...] = v
        else:
            beta = _d_x3(_sigmoid(ba), heb_ref[...])
            a = _d_x3(jnp.exp(logdecay), hea_ref[...])
            w_ref[...] = a
            kkn_ref[...] = -(a * beta) * kn
            k_ref[...] = kn
            q_ref[...] = qn
            v_ref[...] = beta * v

    ins = [_rows(pre, tm), _rows(ba, tm)] + first_ins + [_full(p["gdn_conv_w"]), _full(alog), _full(dtb), _full(hs),
                                                         _full(he), _full(he_b), _full(he_a)]
    small = LANES if chunked else D
    outs = [_rows_out(st.M, small, tm), _rows_out(st.M, small, tm)] + [_rows_out(st.M, D, tm) for _ in range(3)]
    return _call(body, (st.M // tm,), ins, outs, name="gdn_conv")


GDN_CHUNK = 64


def _gdn_chunk_scan(q, k, v, beta, g, S0, B, T):
    C = GDN_CHUNK
    H, N = GDN_HEADS, GDN_HSZ
    nchunk = T // C
    tril = _lower_tri(C)
    triu = jnp.asarray(np.triu(np.ones((C, C), np.float32)), BF16)

    def body(q_ref, k_ref, v_ref, b_ref, g_ref, s0_ref, tril_ref, triu_ref, o_ref, sf_ref, s_scr):
        c = pl.program_id(1)

        @pl.when(c == 0)
        def _():
            s_scr[...] = s0_ref[...]

        gblk = g_ref[...]
        gc = _d_3x(tril_ref[...], gblk)
        gh, gm, gl = _split3(gblk)
        tn = lambda a: lax.dot_general(a, triu_ref[...], (((0,), (0,)), ((), ())), preferred_element_type=F32)
        gct = tn(gh) + tn(gm) + tn(gl)
        bblk = b_ref[...]
        ri = lax.broadcasted_iota(I32, (C, C), 0)
        ci = lax.broadcasted_iota(I32, (C, C), 1)
        lower, strict = ri >= ci, ri > ci
        eye = jnp.where(ri == ci, 1.0, 0.0)
        heads = range(H)
        sl = [slice(h * N, (h + 1) * N) for h in heads]
        bcol = [bblk[:, h:h + 1] for h in heads]
        gcol = [gc[:, H + h:H + h + 1] for h in heads]
        gamma = [jnp.where(lower, jnp.exp(jnp.minimum(gcol[h] - gct[H + h:H + h + 1, :], 0.0)), 0.0) for h in heads]
        kb = [k_ref[:, sl[h]] * bcol[h] for h in heads]
        pw = [jnp.where(strict, _d_nt(kb[h], k_ref[:, sl[h]]) * gamma[h], 0.0) for h in heads]
        t_inv = [eye - pw[h] for h in heads]
        for _ in range(int(math.log2(C)) - 1):
            pw = [_d(pw[h], pw[h]) for h in heads]
            t_inv = [t_inv[h] + _d(t_inv[h], pw[h]) for h in heads]
        eg = [jnp.exp(gcol[h]) for h in heads]
        u = [_d(t_inv[h], v_ref[:, sl[h]] * bcol[h]) for h in heads]
        w = [_d(t_inv[h], kb[h] * eg[h]) for h in heads]
        qk = [jnp.where(lower, _d_nt(q_ref[:, sl[h]], k_ref[:, sl[h]]) * gamma[h], 0.0) for h in heads]
        v_new = [u[h] - _d(w[h], s_scr[h]) for h in heads]
        for h in heads:
            o_ref[:, sl[h]] = _d(q_ref[:, sl[h]] * eg[h], s_scr[h]) + _d(qk[h], v_new[h])
        for h in heads:
            g_last = gcol[h][C - 1:C]
            kd = (k_ref[:, sl[h]] * jnp.exp(g_last - gcol[h])).astype(BF16)
            s_scr[h] = s_scr[h] * jnp.exp(g_last) + lax.dot_general(kd, v_new[h].astype(BF16), (((0,), (0,)), ((), ())),
                                                                   preferred_element_type=F32)

        @pl.when(c == nchunk - 1)
        def _():
            sf_ref[...] = s_scr[...]

    seq = lambda a, w: (a, pl.BlockSpec((C, w), lambda b, c: (b * nchunk + c, 0)))
    ins = [seq(q, D), seq(k, D), seq(v, D), seq(beta, LANES), seq(g, LANES),
           (S0, pl.BlockSpec((None, H, N, N), lambda b, c: (b, 0, 0, 0))),
           (tril, pl.BlockSpec(tril.shape, lambda b, c: (0, 0))), (triu, pl.BlockSpec(triu.shape, lambda b, c: (0, 0)))]
    outs = [((B * T, D), F32, pl.BlockSpec((C, D), lambda b, c: (b * nchunk + c, 0))),
            ((B, H, N, N), F32, pl.BlockSpec((None, H, N, N), lambda b, c: (b, 0, 0, 0)))]
    return _call(body, (B, nchunk), ins, outs, scratch=[pltpu.VMEM((H, N, N), F32)], name="gdn_chunk_scan")


def _gdn_out(st, o, z, p, alpha, ln_g, ln_b, tm):
    hs, he = _seg_consts(GDN_HSZ)
    nw = jnp.tile(p["gdn_norm_w"], GDN_HEADS).reshape(1, D)

    def pro(o, z, nw, hs, he):
        rstd = lax.rsqrt(_d_x3(o * o, hs) * (1.0 / GDN_HSZ) + 1e-6)
        return o * _d_x3(rstd, he) * nw * _silu(z)

    def epi(acc, x, gate, g_, b_):
        return (_res_ln(alpha, acc, x, gate, g_, b_),)

    return _mm(o, _full(p["gdn_w_o"].astype(BF16)), tm=tm, pro=pro, pro_ins=[_rows(z, tm), _full(nw), _full(hs), _full(he)],
               epi=epi, epi_ins=[_rows(st.x, tm), st.mod(2, tm), _full(ln_g), _full(ln_b)],
               outs=[_rows_out(st.M, D, tm)], name="gdn_out")[0]


def _gdn_layer(st, conv_buf, S0, p, alpha, ln_g, ln_b):
    B, T = st.B, st.T
    tm = min(256, st.M)
    pre, z, ba = _gdn_proj(st, p, tm)
    chunked = T % GDN_CHUNK == 0
    if chunked:
        beta, g, kn, qn, v = _gdn_conv(st, pre, ba, conv_buf, p, tm, True)
        o, S = _gdn_chunk_scan(qn, kn, v, beta, g, S0, B, T)
    else:
        w, kkn, kn, qn, vb = _gdn_conv(st, pre, ba, conv_buf, p, tm, False)
        S0t = S0.transpose(0, 3, 1, 2).reshape(B, GDN_HSZ, D)
        o, sf = _delta_scan(w, kkn, kn, kn, qn, vb, S0t, B, T, GDN_HSZ)
        S = sf.reshape(B, GDN_HSZ, GDN_HEADS, GDN_HSZ).transpose(0, 2, 3, 1)
    st.x = _gdn_out(st, o, z, p, alpha, ln_g, ln_b, tm)
    xpad = jnp.concatenate([conv_buf, pre.reshape(B, T, 3 * D)[:, -(GDN_CONV - 1):]], axis=1)
    return S, xpad[:, -(GDN_CONV - 1):]


def _flash_body(cfg, *refs):
    tq, hq, hk = cfg["tq"], cfg["hq"], cfg["hk"]
    fox, bias, aug, window = cfg["fox"], cfg["bias"], cfg["aug"], cfg["window"]
    tk = tq
    G = hq // hk
    Kc = HD + aug
    R = G * tq
    refs = list(refs)
    q_ref, k_ref, v_ref = refs[:3]
    pos = 3
    if aug:
        mb_ref, e_ref = refs[pos:pos + 2]
        pos += 2
    if bias:
        tz_ref = refs[pos]
        pos += 1
    if fox:
        cq_ref, ck_ref = refs[pos:pos + 2]
        pos += 2
    o_ref, kb, vb, s_scr = refs[pos:pos + 4]
    g = pl.program_id(1)
    qi = pl.program_id(2)

    @pl.when(qi == 0)
    def _():
        vb[...] = v_ref[...].astype(BF16)
        if aug:
            k = k_ref[...]
            kb[...] = jnp.concatenate(
                [jnp.concatenate([k[:, kv * HD:(kv + 1) * HD].astype(BF16), e_ref[...]], axis=1) for kv in range(hk)],
                axis=1)
        else:
            kb[...] = k_ref[...].astype(BF16)

    scale = HD ** -0.5
    q = q_ref[...]
    row_t = lax.broadcasted_iota(I32, (R, tk), 0) % tq
    col_s = lax.broadcasted_iota(I32, (R, tk), 1)
    qs, cqs = [], []
    for kv in range(hk):
        x = jnp.concatenate([q[:, (kv * G + gg) * HD:(kv * G + gg + 1) * HD] for gg in range(G)], axis=0) * scale
        if aug:
            x = jnp.concatenate([x.astype(BF16), jnp.concatenate([mb_ref[kv]] * G, axis=0)], axis=1)
        qs.append(x.astype(BF16))
        if fox:
            lane = lax.broadcasted_iota(I32, (tq, LANES), 1)
            cqs.append(jnp.sum(jnp.where(lane == g * hk + kv, cq_ref[...], 0.0), axis=-1, keepdims=True))

    def logits(kv, c, rel, valid):
        off = pl.multiple_of(c * tk, tk)
        if cfg.get("kt"):
            s = _d(qs[kv], kb[kv * Kc:(kv + 1) * Kc, pl.ds(off, tk)])
        else:
            s = _d_nt(qs[kv], kb[pl.ds(off, tk), kv * Kc:(kv + 1) * Kc])
        if fox:
            sub = lax.broadcasted_iota(I32, (FOX_HEADS, tk), 0)
            ck = jnp.sum(jnp.where(sub == g * hk + kv, ck_ref[:, pl.ds(off, tk)], 0.0), axis=0, keepdims=True)
            s = s + cqs[kv] - ck
        if bias and rel in (0, 1):
            s = s + jnp.concatenate([tz_ref[kv * G + gg, rel] for gg in range(G)], axis=0)
        if rel == 0:
            s = jnp.where(row_t >= col_s, s, NEG)
        if rel == 3:
            s = jnp.where(col_s > row_t, s, NEG)
        if valid is not None:
            s = jnp.where(valid, s, NEG)
        return s

    nt = tk // LANES

    def lane_tiles(x):
        return [x[:, j * LANES:(j + 1) * LANES] for j in range(nt)]

    if window is not None:
        nch = window // tk
        static = [(qi, 0, None)] + [(jnp.maximum(qi - dc, 0), 1 if dc == 1 else (3 if dc == nch else 2), qi - dc >= 0)
                                    for dc in range(1, nch + 1)]
        n_far = 0
    elif bias:
        static = [(qi, 0, None), (jnp.maximum(qi - 1, 0), 1, qi >= 1)]
        n_far = jnp.maximum(qi - 1, 0)
    else:
        static = [(qi, 0, None)]
        n_far = qi
    n_static = len(static)
    unroll = cfg["unroll"]

    def far_loop(fn, carry):
        ng = n_far // unroll

        def group(gi, cr):
            for u in range(unroll):
                cr = fn(gi * unroll + u, cr)
            return cr

        carry = lax.fori_loop(0, ng, group, carry)
        return lax.fori_loop(ng * unroll, n_far, fn, carry)

    outs = []
    for kv in range(hk):
        def score(c, rel, valid, slot, m128):
            s = logits(kv, c, rel, valid)
            s_scr[slot] = s
            for t in lane_tiles(s):
                m128 = jnp.maximum(m128, t)
            return m128

        m128 = jnp.full((R, LANES), NEG, F32)
        for slot, (c, rel, valid) in enumerate(static):
            m128 = score(c, rel, valid, slot, m128)
        if window is None:
            m128 = far_loop(lambda c, m: score(c, 2, None, n_static + c, m), m128)
        mrep = jnp.broadcast_to(jnp.max(m128, axis=-1, keepdims=True), (R, LANES))

        def accumulate(c, slot, carry):
            l128, acc = carry
            p = [jnp.exp(t - mrep) for t in lane_tiles(s_scr[slot])]
            for t in p:
                l128 = l128 + t
            off = pl.multiple_of(c * tk, tk)
            pm = jnp.concatenate(p, axis=1) if nt > 1 else p[0]
            if cfg.get("kt"):
                return l128, acc + _d_nt(pm, vb[kv * HD:(kv + 1) * HD, pl.ds(off, tk)])
            return l128, acc + _d(pm, vb[pl.ds(off, tk), kv * HD:(kv + 1) * HD])

        carry = (jnp.zeros((R, LANES), F32), jnp.zeros((R, HD), F32))
        for slot, (c, rel, valid) in enumerate(static):
            carry = accumulate(c, slot, carry)
        if window is None:
            carry = far_loop(lambda c, cr: accumulate(c, n_static + c, cr), carry)
        l128, acc = carry
        o = acc / jnp.sum(l128, axis=-1, keepdims=True)
        outs += [o[gg * tq:(gg + 1) * tq] for gg in range(G)]
    o_ref[...] = jnp.concatenate(outs, axis=1)


def _flash(cfg, B, T, ngroups, q_in, k_in, v_in, extra_ins, M):
    tq, hq, hk = cfg["tq"], cfg["hq"], cfg["hk"]
    Kc = HD + cfg["aug"]
    ins = [q_in, k_in, v_in] + list(extra_ins)
    nq = T // tq
    R = (hq // hk) * tq
    if cfg["window"] is not None:
        nslots = cfg["window"] // tq + 1
    else:
        nslots = nq + (1 if cfg["bias"] else 0)
    outs = [((M, ngroups * hq * HD), F32, pl.BlockSpec((tq, hq * HD), lambda b, g, i: (b * nq + i, g)))]
    kshape, vshape = ((hk * Kc, T), (hk * HD, T)) if cfg.get("kt") else ((T, hk * Kc), (T, hk * HD))
    scratch = [pltpu.VMEM(kshape, BF16), pltpu.VMEM(vshape, BF16), pltpu.VMEM((nslots, R, tq), F32)]
    return _call(functools.partial(_flash_body, cfg), (B, ngroups, nq), ins, outs, scratch=scratch,
                 name="flash_" + cfg["name"])[0]


def _log_sigmoid(x):
    return -_softplus(-x)


def _fox_proj(st, p, tm):
    hw = FOX_HEADS * HD
    w = _pad_cols(p["fox_w_in"], 3 * hw + LANES).astype(BF16)
    bf = jnp.zeros((1, LANES), F32).at[0, :FOX_HEADS].set(p["fox_b_f"])

    def epi(acc, bf):
        return acc[:, :hw], acc[:, hw:3 * hw], _log_sigmoid(acc[:, 3 * hw:] + bf)

    outs = [_rows_out(st.M, hw, tm), _rows_out(st.M, 2 * hw, tm), _rows_out(st.M, LANES, tm)]
    return _mm(st.x, _full(w), tm=tm, pro=_modulate, pro_ins=[st.mod(1, tm), st.mod(0, tm)], epi=epi,
               epi_ins=[_full(bf)], outs=outs, name="fox_proj")


def _fox_proj_t(st, p, tm):
    hw = FOX_HEADS * HD
    B, T = st.B, st.T
    tpb = T // tm
    w_in = p["fox_w_in"]
    wq = w_in[:, :hw].astype(BF16)
    wkvt = w_in[:, hw:3 * hw].T.astype(BF16)
    wf = _pad_cols(w_in[:, 3 * hw:], LANES).astype(BF16)
    bf = jnp.zeros((1, LANES), F32).at[0, :FOX_HEADS].set(p["fox_b_f"])

    def body(x_ref, sc_ref, sh_ref, wq_ref, wkv_ref, wf_ref, bf_ref, q_ref, kvt_ref, f_ref):
        a = _modulate(x_ref[...], sc_ref[...], sh_ref[...]).astype(BF16)
        q_ref[...] = _d(a, wq_ref[...])
        kvt_ref[...] = _d_nt(wkv_ref[...], a)
        f_ref[...] = _log_sigmoid(_d(a, wf_ref[...]) + bf_ref[...])

    ins = [_rows(st.x, tm), st.mod(1, tm), st.mod(0, tm), _full(wq), _full(wkvt), _full(wf), _full(bf)]
    outs = [_rows_out(st.M, hw, tm),
            ((B, 2 * hw, T), F32, pl.BlockSpec((None, 2 * hw, tm), lambda i: (i // tpb, 0, i % tpb))),
            _rows_out(st.M, LANES, tm)]
    return _call(body, (st.M // tm,), ins, outs, name="fox_proj_t")


def _lower_tri(n):
    return jnp.asarray(np.tril(np.ones((n, n), np.float32)), BF16)


def _cumsum_rows(x, B, T):
    ch = _pick_tile(T)
    tri = _lower_tri(ch)

    def body(x_ref, tri_ref, o_ref):
        carry = jnp.zeros((1, LANES), F32)
        for c in range(T // ch):
            cc = _d_3x(tri_ref[...], x_ref[c * ch:(c + 1) * ch, :]) + carry
            o_ref[c * ch:(c + 1) * ch, :] = cc
            carry = cc[ch - 1:ch, :]

    return _call(body, (B,), [_rows(x, T), _full(tri)], [_rows_out(B * T, LANES, T)], name="cumsum_rows")[0]


def _out_proj(st, o, w_o, alpha, ln_g, ln_b, tm, name):
    def epi(acc, x, gate, g_, b_):
        return (_res_ln(alpha, acc, x, gate, g_, b_),)

    return _mm(o, _full(w_o.astype(BF16)), tm=tm, epi=epi,
               epi_ins=[_rows(st.x, tm), st.mod(2, tm), _full(ln_g), _full(ln_b)],
               outs=[_rows_out(st.M, D, tm)], name=name)[0]


def _fox_prompt(st, p, alpha, ln_g, ln_b):
    B, T, M = st.B, st.T, st.M
    tm = min(256, M)
    q, kvt, logf = _fox_proj_t(st, p, tm)
    cum = _cumsum_rows(logf, B, T)
    ckT = cum.reshape(B, T, LANES)[:, :, :FOX_HEADS].transpose(0, 2, 1)
    tq = min(512, T)
    nq = T // tq
    cfg = dict(name="fox", tq=tq, hq=2, hk=2, fox=True, bias=False, aug=0, window=None, unroll=2, kt=True)
    npair = FOX_HEADS // 2
    q_in = (q, pl.BlockSpec((tq, 2 * HD), lambda b, g, i: (b * nq + i, g)))
    k_in = (kvt, pl.BlockSpec((None, 2 * HD, T), lambda b, g, i: (b, g, 0)))
    v_in = (kvt, pl.BlockSpec((None, 2 * HD, T), lambda b, g, i: (b, npair + g, 0)))
    extra = [(cum, pl.BlockSpec((tq, LANES), lambda b, g, i: (b * nq + i, 0))),
             (ckT, pl.BlockSpec((None, FOX_HEADS, T), lambda b, g, i: (b, 0, 0)))]
    o = _flash(cfg, B, T, npair, q_in, k_in, v_in, extra, M)
    st.x = _out_proj(st, o, p["fox_w_o"], alpha, ln_g, ln_b, tm, "fox_out")
    kv = kvt.reshape(B, 2, FOX_HEADS, HD, T).transpose(0, 4, 1, 2, 3)
    return kv, logf.reshape(B, T, LANES)[:, :, :FOX_HEADS]


def _page_ins(cache, page_shape, npages, first_of_step):
    nd = len(page_shape)
    return [(cache, pl.BlockSpec((None,) + tuple(page_shape),
                                 lambda b, s, pt, j=j: (pt[b, first_of_step(s) + j],) + (0,) * nd))
            for j in range(npages)]


def _fox_cum_sample(logf_new, cache_logf, page_table, Tn):
    B, npg = page_table.shape
    PAGE = cache_logf.shape[1]
    H = cache_logf.shape[2]
    cache_t = cache_logf.transpose(0, 2, 1)
    triu = jnp.asarray(np.triu(np.ones((PAGE, PAGE), np.float32)), BF16)

    def body(pt_ref, *refs):
        pages, new_ref, tri_ref, o_ref = refs[:npg], refs[npg], refs[npg + 1], refs[npg + 2]
        carry = jnp.zeros((H, 1), F32)
        for j in range(npg):
            cc = _d_x3(pages[j][...], tri_ref[...]) + carry
            o_ref[:, j * PAGE:(j + 1) * PAGE] = cc
            carry = cc[:, PAGE - 1:PAGE]
        xn = jnp.concatenate([new_ref[...], jnp.zeros((PAGE - Tn, LANES), F32)], axis=0).T[:H, :]
        o_ref[:, npg * PAGE:(npg + 1) * PAGE] = _d_x3(xn, tri_ref[...]) + carry

    ins = _page_ins(cache_t, (H, PAGE), npg, lambda s: 0)
    ins += [(logf_new, pl.BlockSpec((Tn, LANES), lambda b, s, pt: (b, 0))),
            (triu, pl.BlockSpec(triu.shape, lambda b, s, pt: (0, 0)))]
    Lp = (npg + 1) * PAGE
    outs = [((B, H, Lp), F32, pl.BlockSpec((None, H, Lp), lambda b, s, pt: (b, 0, 0)))]
    return _call(body, (B, 1), ins, outs, prefetch=[page_table], name="fox_cum_sample")[0]


def _rep_mat(n_rows, n_src, per):
    r = np.arange(n_rows)
    src = r // per if per else r % n_src
    return jnp.asarray((src[:, None] == np.arange(n_src)[None, :]).astype(np.float32), BF16)


def _fox_decode(q, kv_new, cache_kv, page_table, cq, ckT, Tn):
    B, npg = page_table.shape
    PAGE, H = cache_kv.shape[1], FOX_HEADS
    hw = H * HD
    R = H * Tn
    pps = PAGES_PER_STEP if npg % PAGES_PER_STEP == 0 else 1
    nsteps = npg // pps
    cache_t = cache_kv.transpose(0, 2, 3, 4, 1).reshape(cache_kv.shape[0], 2, hw, PAGE)
    rep_t = _rep_mat(R, Tn, 0)
    rep_h = _rep_mat(R, H, Tn)
    scale = HD ** -0.5

    def body(pt_ref, *refs):
        pages = refs[:pps]
        q_ref, new_ref, cq_ref, ck_ref, rt_ref, rh_ref, o_ref, qbd, m_s, l_s, acc = refs[pps:]
        s_id = pl.program_id(1)
        own = lax.broadcasted_iota(I32, (R, hw), 0) // Tn == lax.broadcasted_iota(I32, (R, hw), 1) // HD

        @pl.when(s_id == 0)
        def _():
            qbd[...] = jnp.where(own, _d(rt_ref[...], q_ref[...]) * scale, 0.0).astype(BF16)
            m_s[...] = jnp.full(m_s.shape, NEG, F32)
            l_s[...] = jnp.zeros(l_s.shape, F32)
            acc[...] = jnp.zeros(acc.shape, F32)

        cqv = cq_ref[...]

        def update(s, pv):
            m = m_s[:, 0:1]
            m2 = jnp.maximum(m, jnp.max(s, axis=-1, keepdims=True))
            a = jnp.exp(m - m2)
            pr = jnp.exp(s - m2)
            l_s[...] = jnp.broadcast_to(a * l_s[:, 0:1] + jnp.sum(pr, axis=-1, keepdims=True), l_s.shape)
            m_s[...] = jnp.broadcast_to(m2, m_s.shape)
            acc[...] = a * acc[...] + pv(pr)

        kt = jnp.concatenate([pages[j][0].astype(BF16) for j in range(pps)], axis=1)
        vt = jnp.concatenate([pages[j][1].astype(BF16) for j in range(pps)], axis=1)
        off = pl.multiple_of(s_id * (pps * PAGE), pps * PAGE)
        update(_d(qbd[...], kt) + cqv - _d_3x(rh_ref[...], ck_ref[:, pl.ds(off, pps * PAGE)]),
               lambda pr: _d_nt(pr, vt))

        @pl.when(s_id == nsteps - 1)
        def _():
            new = jnp.concatenate([new_ref[...], jnp.zeros((PAGE - Tn, 2 * hw), F32)], axis=0)
            t_row = lax.broadcasted_iota(I32, (R, PAGE), 0) % Tn
            col = lax.broadcasted_iota(I32, (R, PAGE), 1)
            s_new = _d_nt(qbd[...], new[:, :hw]) + cqv - _d_3x(rh_ref[...], ck_ref[:, pl.ds(npg * PAGE, PAGE)])
            update(jnp.where(col <= t_row, s_new, NEG), lambda pr: _d(pr, new[:, hw:]))
            of = jnp.where(own, acc[...] / l_s[:, 0:1], 0.0)
            out = of[0:Tn]
            for h in range(1, H):
                out = out + of[h * Tn:(h + 1) * Tn]
            o_ref[...] = out

    Lp = ckT.shape[2]
    const = lambda a: (a, pl.BlockSpec(a.shape, lambda b, s, pt: (0,) * a.ndim))
    ins = _page_ins(cache_t, (2, hw, PAGE), pps, lambda s: s * pps)
    ins += [(q, pl.BlockSpec((Tn, hw), lambda b, s, pt: (b, 0))),
            (kv_new, pl.BlockSpec((Tn, 2 * hw), lambda b, s, pt: (b, 0))),
            (cq, pl.BlockSpec((None, R, 1), lambda b, s, pt: (b, 0, 0))),
            (ckT, pl.BlockSpec((None, H, Lp), lambda b, s, pt: (b, 0, 0))),
            const(rep_t), const(rep_h)]
    outs = [((B * Tn, hw), F32, pl.BlockSpec((Tn, hw), lambda b, s, pt: (b, 0)))]
    scratch = [pltpu.VMEM((R, hw), BF16), pltpu.VMEM((R, LANES), F32), pltpu.VMEM((R, LANES), F32),
               pltpu.VMEM((R, hw), F32)]
    return _call(body, (B, nsteps), ins, outs, scratch=scratch, prefetch=[page_table], name="fox_decode")[0]


def _fox_sample(st, cache_kv, cache_logf, page_table, p, alpha, ln_g, ln_b):
    B, Tn, M = st.B, st.T, st.M
    tm = min(256, M)
    npg = page_table.shape[1]
    PAGE = cache_kv.shape[1]
    q, kv, logf = _fox_proj(st, p, tm)
    ckT = _fox_cum_sample(logf, cache_logf, page_table, Tn)
    cq = ckT[:, :, npg * PAGE:npg * PAGE + Tn].reshape(B, FOX_HEADS * Tn, 1)
    o = _fox_decode(q, kv, cache_kv, page_table, cq, ckT, Tn)
    st.x = _out_proj(st, o, p["fox_w_o"], alpha, ln_g, ln_b, tm, "fox_out")
    return kv.reshape(B, Tn, 2, FOX_HEADS, HD), logf.reshape(B, Tn, LANES)[:, :, :FOX_HEADS]


KVW = NSA_KVH * HD
HALF = CMP_BLK // 2


def _t5_bucket(dist):
    exact = REL_BUCKETS // 2
    d = jnp.maximum(dist, 0)
    far = exact + (jnp.log(jnp.maximum(d, 1).astype(F32) / exact) / math.log(REL_MAX_DIST / exact)
                   * (REL_BUCKETS - exact)).astype(I32)
    return jnp.where(d < exact, d, jnp.minimum(far, REL_BUCKETS - 1))


def _rel_bias(table, dist):
    return jnp.moveaxis(table[_t5_bucket(dist)], -1, 0)


def _nsa_proj(st, p, tm):
    qw = NSA_HEADS * HD
    w = _pad_cols(p["nsa_w_in"], qw + 6 * KVW + LANES).astype(BF16)

    def epi(acc):
        return (acc[:, :qw], acc[:, qw:qw + 2 * KVW], acc[:, qw + 2 * KVW:qw + 4 * KVW],
                acc[:, qw + 4 * KVW:qw + 6 * KVW], acc[:, qw + 6 * KVW:])

    outs = [_rows_out(st.M, qw, tm)] + [_rows_out(st.M, 2 * KVW, tm)] * 3 + [_rows_out(st.M, LANES, tm)]
    return _mm(st.x, _full(w), tm=tm, pro=_modulate, pro_ins=[st.mod(1, tm), st.mod(0, tm)], epi=epi, outs=outs,
               name="nsa_proj")


def _cmp_weights(p):
    eye = jnp.eye(NSA_KVH, dtype=F32)
    wk = jnp.einsum("ab,vlde->vladbe", eye, p["nsa_cmp_w1"]).reshape(2, CMP_BLK, KVW, KVW)
    wc = wk.reshape(2, 2, HALF, KVW, KVW).transpose(1, 2, 0, 3, 4)
    w2c = jnp.einsum("ab,vde->vadbe", eye, p["nsa_cmp_w2"]).reshape(2, KVW, KVW)
    b1 = jnp.tile(p["nsa_cmp_b1"][:, None, :], (1, NSA_KVH, 1)).reshape(1, 2 * KVW)
    return wc.astype(BF16), w2c.astype(BF16), b1


def _compress_body(nx, *refs):
    x_refs = refs[:nx]
    wc_ref, w2_ref, b1_ref, o_ref, ua, ub = refs[nx:]
    rows = ua.shape[1]
    nl = 2 * KVW // LANES
    acc = [[jnp.zeros((rows, KVW), F32) for _ in range(2)] for _ in range(2)]
    for l in range(HALF):
        for kv in range(2):
            lo = l * 2 * KVW + kv * KVW
            piece = [r[:, lo:lo + KVW] for r in x_refs]
            piece = (jnp.concatenate(piece, axis=0) if nx > 1 else piece[0]).astype(BF16)
            for half in range(2):
                acc[half][kv] = acc[half][kv] + _d(piece, wc_ref[half, l, kv])
    for scr, a in ((ua, acc[0]), (ub, acc[1])):
        full = jnp.concatenate(a, axis=1)
        for c in range(nl):
            scr[c] = full[:, c * LANES:(c + 1) * LANES]
    hid = jnp.concatenate([ua[c, pl.ds(0, rows // 2, stride=2), :] + ub[c, pl.ds(1, rows // 2, stride=2), :]
                           for c in range(nl)], axis=1)
    hid = _gelu_tanh(hid + b1_ref[...])
    o_ref[...] = jnp.concatenate([_d(hid[:, :KVW], w2_ref[0]), _d(hid[:, KVW:], w2_ref[1])], axis=1)


def _compress_dense(rows_kv, cw):
    wc, w2c, b1 = cw
    M = rows_kv.shape[0]
    x = rows_kv.reshape(M // HALF, HALF * 2 * KVW)
    nh = M // HALF
    th = _pick_tile(nh, cap=128)
    ins = [_rows(x, th), _full(wc), _full(w2c), _full(b1)]
    outs = [_rows_out(nh // 2, 2 * KVW, th // 2)]
    scratch = [pltpu.VMEM((2 * KVW // LANES, th, LANES), F32)] * 2
    return _call(functools.partial(_compress_body, 1), (nh // th,), ins, outs, scratch=scratch, name="nsa_compress")[0]


def _pair_mat(nc, ns):
    n = np.arange(nc)
    return jnp.asarray((n[:, None] // (SEL_BLK // CMP_BLK) == np.arange(ns)[None, :]).astype(np.float32), BF16)


def _top_blocks(score, n_sel):
    lane = lax.broadcasted_iota(I32, score.shape, 1)
    big = jnp.int32(1 << 20)
    taken = -3e38
    work = score
    for _ in range(n_sel):
        m = jnp.max(work, axis=-1, keepdims=True)
        idx = jnp.min(jnp.where(work == m, lane, big), axis=-1, keepdims=True)
        work = jnp.where(lane == idx, taken, work)
    return jnp.where(work == taken, 0.0, SEL_NEG)


def _masked_softmax(s, mask):
    s = jnp.where(mask, s, NEG)
    m = jnp.max(s, axis=-1, keepdims=True)
    p = jnp.where(mask, jnp.exp(s - m), 0.0)
    l = jnp.sum(p, axis=-1, keepdims=True)
    return p / jnp.where(l > 0.0, l, 1.0)


def _block_scores(imp, tpos, ns):
    blk = lax.broadcasted_iota(I32, imp.shape, 1)
    cur = tpos // SEL_BLK
    forced = (blk == 0) | (blk == cur) | (blk == cur - 1)
    score = jnp.where(forced, FORCE_SCORE, imp)
    return jnp.where(blk * SEL_BLK > tpos, -1.0, score)


CMP_NEAR_LO = -3
CMP_NEAR_N = 8


def _cmp_bias_pattern(table, tq):
    assert tq == LANES and CMP_BLK == 32 and REL_MAX_DIST == LANES
    r = jnp.arange(tq, dtype=I32)[:, None]
    m = CMP_NEAR_LO + jnp.arange(CMP_NEAR_N, dtype=I32)[None, :]
    near = _rel_bias(table, r - (CMP_BLK - 1) + CMP_BLK * m)
    far = jnp.broadcast_to(table[REL_BUCKETS - 1][:, None, None], (table.shape[1], tq, 1))
    return jnp.pad(jnp.concatenate([near, far], axis=2), ((0, 0), (0, 0), (0, LANES - CMP_NEAR_N - 1)))


def _nsa_cmp_prompt(q, kcvc, pat, B, T, tq):
    nc, ns = T // CMP_BLK, -(-T // SEL_BLK)
    n_sel = min(N_SEL, ns)
    nq = T // tq
    G = NSA_G
    pair = _pair_mat(nc, ns)
    scale = HD ** -0.5
    rb = tq // CMP_BLK

    def body(q_ref, kc_ref, vc_ref, b_ref, pair_ref, o_ref, mb_ref):
        qi = pl.program_id(1)
        q = q_ref[...]
        R = G * tq
        tpos = qi * tq + lax.broadcasted_iota(I32, (R, 1), 0) % tq
        cmp_end = lax.broadcasted_iota(I32, (R, nc), 1) * CMP_BLK + (CMP_BLK - 1)
        mask = cmp_end <= tpos
        j = lax.broadcasted_iota(I32, (LANES, nc), 0)
        m = rb * qi - lax.broadcasted_iota(I32, (LANES, nc), 1)
        sel = ((j < CMP_NEAR_N) & (m == j + CMP_NEAR_LO)) | ((j == CMP_NEAR_N) & (m >= CMP_NEAR_LO + CMP_NEAR_N))
        sel = jnp.where(sel, 1.0, 0.0).astype(BF16)
        outs = []
        for kv in range(NSA_KVH):
            qs = jnp.concatenate([q[:, (kv * G + gg) * HD:(kv * G + gg + 1) * HD] for gg in range(G)], axis=0) * scale
            s = _d_nt(qs, kc_ref[:, kv * HD:(kv + 1) * HD])
            s = s + _d_x3(jnp.concatenate([b_ref[kv * G + gg] for gg in range(G)], axis=0), sel)
            pc = _masked_softmax(s, mask)
            oc = _d(pc, vc_ref[:, kv * HD:(kv + 1) * HD])
            outs += [oc[gg * tq:(gg + 1) * tq] for gg in range(G)]
            imp = pc[0:tq]
            for gg in range(1, G):
                imp = imp + pc[gg * tq:(gg + 1) * tq]
            score = _block_scores(_d_x3(imp, pair_ref[...]), tpos[0:tq], ns)
            mb_ref[kv] = _top_blocks(score, n_sel).astype(mb_ref.dtype)
        o_ref[...] = jnp.concatenate(outs, axis=1)

    ins = [(q, pl.BlockSpec((tq, NSA_HEADS * HD), lambda b, i: (b * nq + i, 0))),
           (kcvc, pl.BlockSpec((nc, KVW), lambda b, i: (b, 0))),
           (kcvc, pl.BlockSpec((nc, KVW), lambda b, i: (b, 1))),
           (pat, pl.BlockSpec(pat.shape, lambda b, i: (0, 0, 0))),
           _full(pair)]
    outs = [((B * T, NSA_HEADS * HD), F32, pl.BlockSpec((tq, NSA_HEADS * HD), lambda b, i: (b * nq + i, 0))),
            ((B, NSA_KVH, T, ns), BF16, pl.BlockSpec((None, NSA_KVH, tq, ns), lambda b, i: (b, 0, i, 0)))]
    return _call(body, (B, nq), ins, outs, name="nsa_cmp_select")


def _gate_mats():
    hsn = _seg_np(HD)
    return [jnp.asarray(np.roll(hsn.T, br * NSA_HEADS, axis=0), BF16) for br in range(3)]


def _nsa_out(st, o_c, o_s, o_w, gates, p, alpha, ln_g, ln_b, tm):
    def pro(oc, os_, ow, gl, e0, e1, e2):
        sg = _sigmoid(gl)
        return _d_x3(sg, e0) * oc + _d_x3(sg, e1) * os_ + _d_x3(sg, e2) * ow

    def epi(acc, x, gate, g_, b_):
        return (_res_ln(alpha, acc, x, gate, g_, b_),)

    pro_ins = [_rows(o_s, tm), _rows(o_w, tm), _rows(gates, tm)] + [_full(e) for e in _gate_mats()]
    return _mm(o_c, _full(p["nsa_w_o"].astype(BF16)), tm=tm, pro=pro, pro_ins=pro_ins, epi=epi,
               epi_ins=[_rows(st.x, tm), st.mod(2, tm), _full(ln_g), _full(ln_b)],
               outs=[_rows_out(st.M, D, tm)], name="nsa_out")[0]


def _nsa_prompt(st, p, alpha, ln_g, ln_b):
    B, T, M = st.B, st.T, st.M
    tm = min(256, M)
    tq = ATT_T
    nq = T // tq
    table = p["rel_bias"]
    q, cmp_rows, slc_rows, win_rows, gates = _nsa_proj(st, p, tm)
    kcvc = _compress_dense(cmp_rows, _cmp_weights(p))
    nc, ns = T // CMP_BLK, -(-T // SEL_BLK)
    o_c, mb = _nsa_cmp_prompt(q, kcvc, _cmp_bias_pattern(table, tq), B, T, tq)
    r = jnp.arange(tq, dtype=I32)
    far = table[REL_BUCKETS - 1][:, None, None]
    tz = jnp.stack([_rel_bias(table, r[:, None] - r[None, :]) - far,
                    _rel_bias(table, tq + r[:, None] - r[None, :]) - far], axis=1)
    e_blk = jnp.asarray((np.arange(T)[:, None] // SEL_BLK == np.arange(ns)[None, :]).astype(np.float32), BF16)
    q_in = (q, pl.BlockSpec((tq, NSA_HEADS * HD), lambda b, g, i: (b * nq + i, 0)))
    kv_in = lambda a, c: (a, pl.BlockSpec((T, KVW), lambda b, g, i: (b, c)))
    tz_in = (tz, pl.BlockSpec(tz.shape, lambda b, g, i: (0, 0, 0, 0)))
    cfg = dict(name="nsa_slc", tq=tq, hq=NSA_HEADS, hk=NSA_KVH, fox=False, bias=True, aug=ns, window=None, unroll=4)
    extra = [(mb, pl.BlockSpec((None, NSA_KVH, tq, ns), lambda b, g, i: (b, 0, i, 0))),
             (e_blk, pl.BlockSpec(e_blk.shape, lambda b, g, i: (0, 0))), tz_in]
    o_s = _flash(cfg, B, T, 1, q_in, kv_in(slc_rows, 0), kv_in(slc_rows, 1), extra, M)
    cfg = dict(name="nsa_win", tq=tq, hq=NSA_HEADS, hk=NSA_KVH, fox=False, bias=True, aug=0, window=WINDOW, unroll=1)
    o_w = _flash(cfg, B, T, 1, q_in, kv_in(win_rows, 0), kv_in(win_rows, 1), [tz_in], M)
    st.x = _nsa_out(st, o_c, o_s, o_w, gates, p, alpha, ln_g, ln_b, tm)
    shp = (B, T, 2, NSA_KVH, HD)
    keep = min(WINDOW, T)
    return cmp_rows.reshape(shp), slc_rows.reshape(shp), win_rows.reshape(shp)[:, T - keep:]


def _compress_paged(cache_cmp, page_table, cw):
    wc, w2c, b1 = cw
    B, npg = page_table.shape
    PAGE = cache_cmp.shape[1]
    bpp = PAGE // CMP_BLK
    cache = cache_cmp.transpose(0, 2, 3, 4, 1).reshape(cache_cmp.shape[0], 2, KVW, PAGE)
    nb = 2 if B % 2 == 0 else 1
    npages = nb * npg
    nblocks = npages * bpp
    npan = KVW // LANES

    def body(pt_ref, *refs):
        pages = refs[:npages]
        wc_ref, w2_ref, b1_ref, o_ref, xs = refs[npages:]
        for j in range(npages):
            for kv in range(2):
                x = pages[j][kv].T
                for c in range(npan):
                    xs[kv, c, j * PAGE:(j + 1) * PAGE, :] = x[:, c * LANES:(c + 1) * LANES]
        acc = [jnp.zeros((nblocks, KVW), F32) for _ in range(2)]
        for l in range(CMP_BLK):
            for kv in range(2):
                rows = jnp.concatenate([xs[kv, c, pl.ds(l, nblocks, stride=CMP_BLK), :] for c in range(npan)], axis=1)
                acc[kv] = acc[kv] + _d(rows, wc_ref[l // HALF, l % HALF, kv])
        hid = _gelu_tanh(jnp.concatenate(acc, axis=1) + b1_ref[...])
        o_ref[...] = jnp.concatenate([_d(hid[:, :KVW], w2_ref[0]), _d(hid[:, KVW:], w2_ref[1])], axis=1)

    ins = [(cache, pl.BlockSpec((None, 2, KVW, PAGE), lambda g, pt, bb=bb, j=j: (pt[g * nb + bb, j], 0, 0, 0)))
           for bb in range(nb) for j in range(npg)]
    const = lambda a: (a, pl.BlockSpec(a.shape, lambda g, pt: (0,) * a.ndim))
    ins += [const(wc), const(w2c), const(b1)]
    outs = [((B * npg * bpp, 2 * KVW), F32, pl.BlockSpec((nblocks, 2 * KVW), lambda g, pt: (g, 0)))]
    scratch = [pltpu.VMEM((2, npan, npages * PAGE, LANES), F32)]
    return _call(body, (B // nb,), ins, outs, scratch=scratch, prefetch=[page_table], name="nsa_compress_paged")[0]


def _nsa_decode(q, slc_new, win_new, kcvc, cache_slc, cache_win, page_table, gcol, consts, Tn, offset):
    B, npg = page_table.shape
    PAGE = cache_slc.shape[1]
    Wb = cache_win.shape[1]
    R = NSA_HEADS * Tn
    L = offset + Tn
    nc, ns = L // CMP_BLK, -(-L // SEL_BLK)
    n_sel = min(N_SEL, ns)
    nck = npg + 1
    qw = NSA_HEADS * HD
    cache = cache_slc.transpose(0, 2, 3, 4, 1).reshape(cache_slc.shape[0], 2, KVW, PAGE)
    win = cache_win.transpose(0, 2, 3, 4, 1).reshape(B, 2, KVW, Wb)
    scale = HD ** -0.5
    names = ["rep_t", "fold", "unfold", "bias_c", "mask_c", "pair", "e_blk", "bias_s", "bias_w"]
    cvals = [consts[n] for n in names]

    nb = 2 if B % 2 == 0 else 1

    def body(pt_ref, *refs):
        all_pages = refs[:nb * npg]
        (q_ref, sn_ref, wn_ref, kc_ref, vc_ref, win_ref, g_ref, rt_ref, fold_ref, unfold_ref, bc_ref, mc_ref, pair_ref,
         e_ref, bs_ref, bw_ref, o_ref, wout_ref) = refs[nb * npg:]
        for bi in range(nb):
            one_sequence(bi, all_pages[bi * npg:(bi + 1) * npg], q_ref, sn_ref, wn_ref, kc_ref, vc_ref, win_ref, g_ref,
                         rt_ref, fold_ref, unfold_ref, bc_ref, mc_ref, pair_ref, e_ref, bs_ref, bw_ref, o_ref, wout_ref)

    def one_sequence(bi, pages, q_ref, sn_ref, wn_ref, kc_ref, vc_ref, win_ref, g_ref, rt_ref, fold_ref, unfold_ref,
                     bc_ref, mc_ref, pair_ref, e_ref, bs_ref, bw_ref, o_ref, wout_ref):
        rows = slice(bi * Tn, (bi + 1) * Tn)
        row_h = lax.broadcasted_iota(I32, (R, qw), 0) // Tn
        lane_h = lax.broadcasted_iota(I32, (R, qw), 1) // HD
        own = row_h == lane_h
        qrep = _d(rt_ref[...], q_ref[rows, :])
        qbd = (_d(jnp.where(own, qrep, 0.0), fold_ref[...]) * scale).astype(BF16)
        pad = lambda x: jnp.concatenate([x, jnp.zeros((PAGE - Tn, x.shape[1]), x.dtype)], axis=0)

        cblk = slice(bi * nc, (bi + 1) * nc)
        pc = _masked_softmax(_d_nt(qbd, kc_ref[cblk, :]) + bc_ref[...], mc_ref[...] > 0.0)
        o_c = _d(pc, vc_ref[cblk, :])
        imp = []
        for kv in range(NSA_KVH):
            a = pc[kv * NSA_G * Tn:(kv * NSA_G + 1) * Tn]
            for gg in range(1, NSA_G):
                a = a + pc[(kv * NSA_G + gg) * Tn:(kv * NSA_G + gg + 1) * Tn]
            imp.append(a)
        imp = jnp.concatenate(imp, axis=0)
        tpos = offset + lax.broadcasted_iota(I32, (NSA_KVH * Tn, 1), 0) % Tn
        mb = _top_blocks(_block_scores(_d_x3(imp, pair_ref[...]), tpos, ns), n_sel)
        mb = jnp.concatenate([mb[kv * Tn:(kv + 1) * Tn] for kv in range(NSA_KVH) for _ in range(NSA_G)], axis=0)

        sn = pad(sn_ref[rows, :])
        past = npg * PAGE
        kt = jnp.concatenate([pages[j][0].astype(BF16) for j in range(npg)], axis=1)
        vt = jnp.concatenate([pages[j][1].astype(BF16) for j in range(npg)], axis=1)
        s = jnp.concatenate([_d(qbd, kt), _d_nt(qbd, sn[:, :KVW])], axis=1) + _d(mb, e_ref[...]) + bs_ref[...]
        p = jnp.exp(s - jnp.max(s, axis=-1, keepdims=True))
        o_s = (_d_nt(p[:, :past], vt) + _d(p[:, past:], sn[:, KVW:])) / jnp.sum(p, axis=-1, keepdims=True)

        wk, wv = win_ref[bi, 0], win_ref[bi, 1]
        wn = pad(wn_ref[rows, :])
        s = jnp.concatenate([_d(qbd, wk), _d_nt(qbd, wn[:, :KVW])], axis=1) + bw_ref[...]
        p = jnp.exp(s - jnp.max(s, axis=-1, keepdims=True))
        o_w = (_d_nt(p[:, :Wb], wv) + _d(p[:, Wb:], wn[:, KVW:])) / jnp.sum(p, axis=-1, keepdims=True)
        lane = lax.broadcasted_iota(I32, (KVW, Wb), 1)
        for kv, old in enumerate((wk, wv)):
            nt = pltpu.roll(wn[:, kv * KVW:(kv + 1) * KVW].T, PAGE - Tn, axis=1)
            nt = jnp.concatenate([jnp.zeros((KVW, Wb - PAGE), F32), nt], axis=1)
            wout_ref[bi, kv] = jnp.where(lane >= Wb - Tn, nt, pltpu.roll(old, Wb - Tn, axis=1))

        sg = _sigmoid(g_ref[bi])
        o = sg[:, 0:1] * o_c + sg[:, 1:2] * o_s + sg[:, 2:3] * o_w
        of = jnp.where(own, _d_x3(o, unfold_ref[...]), 0.0)
        out = of[0:Tn]
        for h in range(1, NSA_HEADS):
            out = out + of[h * Tn:(h + 1) * Tn]
        o_ref[rows, :] = out

    c2 = lambda b, pt: (0, 0)
    ins = [(cache, pl.BlockSpec((None, 2, KVW, PAGE), lambda b, pt, bb=bb, j=j: (pt[b * nb + bb, j], 0, 0, 0)))
           for bb in range(nb) for j in range(npg)]
    ins += [(q, pl.BlockSpec((nb * Tn, qw), lambda b, pt: (b, 0))),
            (slc_new, pl.BlockSpec((nb * Tn, 2 * KVW), lambda b, pt: (b, 0))),
            (win_new, pl.BlockSpec((nb * Tn, 2 * KVW), lambda b, pt: (b, 0))),
            (kcvc, pl.BlockSpec((nb * nc, KVW), lambda b, pt: (b, 0))),
            (kcvc, pl.BlockSpec((nb * nc, KVW), lambda b, pt: (b, 1))),
            (win, pl.BlockSpec((nb, 2, KVW, Wb), lambda b, pt: (b, 0, 0, 0))),
            (gcol, pl.BlockSpec((nb, R, 3), lambda b, pt: (b, 0, 0)))]
    ins += [(a, pl.BlockSpec(a.shape, c2)) for a in cvals]
    outs = [((B * Tn, qw), F32, pl.BlockSpec((nb * Tn, qw), lambda b, pt: (b, 0))),
            ((B, 2, KVW, Wb), F32, pl.BlockSpec((nb, 2, KVW, Wb), lambda b, pt: (b, 0, 0, 0)))]
    return _call(body, (B // nb,), ins, outs, prefetch=[page_table], name="nsa_decode")


def _nsa_decode_consts(table, Tn, offset, npg, PAGE, Wb):
    R = NSA_HEADS * Tn
    L = offset + Tn
    nc, ns = L // CMP_BLK, -(-L // SEL_BLK)
    Lp = (npg + 1) * PAGE
    tpos = offset + jnp.arange(Tn, dtype=I32)
    rows = lambda x: x.reshape(R, x.shape[-1])
    cmp_end = jnp.arange(nc, dtype=I32) * CMP_BLK + CMP_BLK - 1
    dist_c = tpos[:, None] - cmp_end[None, :]
    spos = jnp.arange(Lp, dtype=I32)
    dist_s = tpos[:, None] - spos[None, :]
    ok_s = (dist_s >= 0) & (spos[None, :] < L)
    col = jnp.arange(Wb + PAGE, dtype=I32)
    wpos = offset - Wb + col
    dist_w = tpos[:, None] - wpos[None, :]
    ok_w = (dist_w >= 0) & (dist_w < WINDOW) & (wpos[None, :] >= 0) & (col[None, :] < Wb + Tn)
    tile = lambda m: jnp.tile(m[None], (NSA_HEADS, 1, 1))
    fold = np.zeros((NSA_HEADS, HD, NSA_KVH, HD), np.float32)
    for h in range(NSA_HEADS):
        fold[h, :, h // NSA_G, :] = np.eye(HD)
    fold = fold.reshape(NSA_HEADS * HD, KVW)
    return dict(
        rep_t=_rep_mat(R, Tn, 0), fold=jnp.asarray(fold, BF16), unfold=jnp.asarray(fold.T, BF16),
        bias_c=rows(_rel_bias(table, dist_c)), mask_c=rows(tile((dist_c >= 0).astype(F32))), pair=_pair_mat(nc, ns),
        e_blk=jnp.asarray((np.arange(ns)[:, None] == np.arange(Lp)[None, :] // SEL_BLK).astype(np.float32), BF16),
        bias_s=rows(_rel_bias(table, dist_s) + tile(jnp.where(ok_s, 0.0, NEG))),
        bias_w=rows(_rel_bias(table, dist_w) + tile(jnp.where(ok_w, 0.0, NEG))))


def _nsa_sample(st, cache_cmp, cache_slc, cache_win, page_table, p, alpha, ln_g, ln_b):
    B, Tn, M = st.B, st.T, st.M
    tm = min(256, M)
    npg = page_table.shape[1]
    PAGE = cache_slc.shape[1]
    Wb = cache_win.shape[1]
    offset = npg * PAGE
    assert offset % CMP_BLK == 0 and Tn < CMP_BLK and Wb == WINDOW and Tn % SUBLANES == 0
    q, cmp_rows, slc_rows, win_rows, gates = _nsa_proj(st, p, tm)
    kcvc = _compress_paged(cache_cmp, page_table, _cmp_weights(p))
    gcol = gates.reshape(B, Tn, LANES)[:, :, :3 * NSA_HEADS].reshape(B, Tn, 3, NSA_HEADS)
    gcol = gcol.transpose(0, 3, 1, 2).reshape(B, NSA_HEADS * Tn, 3)
    consts = _nsa_decode_consts(p["rel_bias"], Tn, offset, npg, PAGE, Wb)
    o, wout = _nsa_decode(q, slc_rows, win_rows, kcvc, cache_slc, cache_win, page_table, gcol, consts, Tn, offset)
    st.x = _out_proj(st, o, p["nsa_w_o"], alpha, ln_g, ln_b, tm, "nsa_out_s")
    shp = (B, Tn, 2, NSA_KVH, HD)
    win_keep = wout.reshape(B, 2, NSA_KVH, HD, Wb).transpose(0, 4, 1, 2, 3)
    return cmp_rows.reshape(shp), slc_rows.reshape(shp), win_keep


def kernel(x_prompt, x_sample, state_rwkv_wkv, state_rwkv_shift, cache_nsa_cmp, cache_nsa_slc, cache_nsa_win, cache_fox_kv, cache_fox_logf, state_gdn_S, state_gdn_conv, page_table, c_prompt, c_sample, w_mod, b_mod, ln_g, ln_b, moe_w_group, moe_b_group, moe_w_router, moe_b_router, moe_w1, moe_w3, moe_w2, rwkv_mu, rwkv_w_rkv, rwkv_w0, rwkv_w1, rwkv_w2, rwkv_a0, rwkv_a1, rwkv_a2, rwkv_g1, rwkv_g2, rwkv_k_k, rwkv_k_a, rwkv_r_k, rwkv_ln_w, rwkv_ln_b, rwkv_w_o, nsa_w_in, nsa_cmp_w1, nsa_cmp_b1, nsa_cmp_w2, nsa_w_o, rel_bias, fox_w_in, fox_b_f, fox_w_o, gdn_w_in, gdn_conv_w, gdn_A_log, gdn_dt_bias, gdn_norm_w, gdn_w_o):
    p = dict(locals())
    Bp, T, _ = x_prompt.shape
    Bs, Tn, _ = x_sample.shape
    depth = w_mod.shape[0]
    alpha = (2 * depth) ** 0.25
    sp = _Stream(x_prompt.reshape(Bp * T, D), Bp, T, min(512, T))
    ss = _Stream(x_sample.reshape(Bs * Tn, D), Bs, Tn, min(256, Bs * Tn))
    nc = Bp + Bs
    c_all = jnp.pad(jnp.concatenate([c_prompt, c_sample], axis=0), ((0, -nc % SUBLANES), (0, 0)))
    out = {}
    for layer in range(depth):
        m6 = _ada(c_all, w_mod, b_mod, layer)
        sp.set_mods(m6[:Bp])
        ss.set_mods(m6[Bp:nc])
        g0 = ln_g[layer, 0].reshape(1, D)
        b0 = ln_b[layer, 0].reshape(1, D)
        kind = layer % 4
        if kind == 0:
            nh = D // RWKV_HSZ
            out["wkv_p"], out["shift_p"] = _rwkv_layer(sp, jnp.zeros((Bp, D), F32),
                                                       jnp.zeros((Bp, nh, RWKV_HSZ, RWKV_HSZ), F32), p, alpha, g0, b0)
            out["wkv_s"], out["shift_s"] = _rwkv_layer(ss, state_rwkv_shift, state_rwkv_wkv, p, alpha, g0, b0)
        elif kind == 1:
            out["cmp_p"], out["slc_p"], out["win_p"] = _nsa_prompt(sp, p, alpha, g0, b0)
            out["cmp_s"], out["slc_s"], out["win_s"] = _nsa_sample(ss, cache_nsa_cmp, cache_nsa_slc, cache_nsa_win,
                                                                   page_table, p, alpha, g0, b0)
        elif kind == 2:
            out["kv_p"], out["logf_p"] = _fox_prompt(sp, p, alpha, g0, b0)
            out["kv_s"], out["logf_s"] = _fox_sample(ss, cache_fox_kv, cache_fox_logf, page_table, p, alpha, g0, b0)
        else:
            out["S_p"], out["conv_p"] = _gdn_layer(sp, jnp.zeros((Bp, GDN_CONV - 1, 3 * D), F32),
                                                   jnp.zeros((Bp, GDN_HEADS, GDN_HSZ, GDN_HSZ), F32), p, alpha, g0, b0)
            out["S_s"], out["conv_s"] = _gdn_layer(ss, state_gdn_conv, state_gdn_S, p, alpha, g0, b0)
        _moe_layer([sp, ss], layer, alpha, p)
    return (sp.x.reshape(Bp, T, D), ss.x.reshape(Bs, Tn, D), out["wkv_p"], out["wkv_s"], out["shift_p"], out["shift_s"],
            out["cmp_p"], out["cmp_s"], out["slc_p"], out["slc_s"], out["win_p"], out["win_s"],
            out["kv_p"], out["kv_s"], out["logf_p"], out["logf_s"], out["S_p"], out["S_s"], out["conv_p"], out["conv_s"])
```

```python
import functools
import math

import numpy as np
import jax
import jax.numpy as jnp
from jax import lax
from jax.experimental import pallas as pl
from jax.experimental.pallas import tpu as pltpu

F32 = jnp.float32
BF16 = jnp.bfloat16
I32 = jnp.int32
NEG = -1e30
LN_EPS = 1e-5
D = 1024
LANES = 128
SUBLANES = 8
MXU_TILE = 256
VMEM_LIMIT_MB = 56

RWKV_HSZ = 64
RWKV_GN_EPS = 64e-5
NSA_HEADS, NSA_KVH, HD = 16, 4, 64
NSA_G = NSA_HEADS // NSA_KVH
CMP_BLK, SEL_BLK, N_SEL, WINDOW = 32, 64, 16, 512
FORCE_SCORE = 1e4
REL_BUCKETS, REL_MAX_DIST = 32, 128
FOX_HEADS = 16
GDN_HEADS, GDN_HSZ, GDN_CONV = 8, 128, 4
MOE_GROUPS, MOE_EPG, MOE_BLK = 4, 8, 256
MOE_EXPERTS = MOE_GROUPS * MOE_EPG
ATT_T = 128
SEL_NEG = -65536.0
PAGES_PER_STEP = 8


def _d(a, b):
    return jnp.dot(a.astype(BF16), b.astype(BF16), preferred_element_type=F32)


def _d_nt(a, b):
    return lax.dot_general(a.astype(BF16), b.astype(BF16), (((1,), (1,)), ((), ())),
                           preferred_element_type=F32)


def _split3(x):
    h = x.astype(BF16)
    r1 = x - h.astype(F32)
    m = r1.astype(BF16)
    l = (r1 - m.astype(F32)).astype(BF16)
    return h, m, l


def _d_x3(x, sel):
    h, m, l = _split3(x)
    return _d(h, sel) + _d(m, sel) + _d(l, sel)


def _d_3x(sel, x):
    h, m, l = _split3(x)
    return _d(sel, h) + _d(sel, m) + _d(sel, l)


def _d_f32(x, w):
    xh = x.astype(BF16)
    xl = (x - xh.astype(F32)).astype(BF16)
    wh = w.astype(BF16)
    wl = (w - wh.astype(F32)).astype(BF16)
    return _d(xh, wh) + (_d(xh, wl) + _d(xl, wh))


def _sigmoid(x):
    return 1.0 / (1.0 + jnp.exp(-x))


def _softplus(x):
    return jnp.maximum(x, 0.0) + jnp.log(1.0 + jnp.exp(-jnp.abs(x)))


def _silu(x):
    return x * _sigmoid(x)


def _gelu_tanh(x):
    return 0.5 * x * (1.0 + jnp.tanh(math.sqrt(2.0 / math.pi) * (x + 0.044715 * (x * x * x))))


def _layer_norm(z, g, b):
    mu = jnp.mean(z, axis=-1, keepdims=True)
    zc = z - mu
    var = jnp.mean(zc * zc, axis=-1, keepdims=True)
    return zc * lax.rsqrt(var + LN_EPS) * g + b


def _modulate(x, sc, sh):
    return x * (1.0 + sc) + sh


def _res_ln(alpha, y, xres, gate, g, b):
    return _layer_norm(alpha * xres + (1.0 + gate) * y, g, b)


@functools.lru_cache(maxsize=None)
def _seg_np(hsz):
    head = np.arange(D) // hsz
    hs = (head[:, None] == np.arange(LANES)[None, :]).astype(np.float32)
    return hs


def _seg_consts(hsz):
    hs = _seg_np(hsz)
    return jnp.asarray(hs, BF16), jnp.asarray(hs.T, BF16)


def _scan_consts(hsz):
    nh = D // hsz
    head = np.arange(D) // hsz
    slot_j = np.arange(LANES) // 16
    slot_h = np.arange(LANES) % 16
    hexp = np.zeros((SUBLANES, LANES, D), np.float32)
    for j in range(SUBLANES):
        hexp[j] = ((slot_j[:, None] == j) & (slot_h[:, None] == head[None, :]) & (slot_h[:, None] < nh))
    hsum = np.transpose(hexp, (0, 2, 1))
    blk = np.arange(MXU_TILE) // hsz
    bd = (blk[:, None] == blk[None, :]).astype(np.float32)
    return jnp.asarray(hexp, BF16), jnp.asarray(hsum, BF16), jnp.asarray(bd, BF16)


def _call(body, grid, ins, outs, scratch=(), name=None, prefetch=None, aliases=None):
    arrays = [a for a, _ in ins]
    in_specs = [s for _, s in ins]
    out_shape = [jax.ShapeDtypeStruct(s, d) for s, d, _ in outs]
    out_specs = [s for _, _, s in outs]
    params = pltpu.CompilerParams(dimension_semantics=("arbitrary",) * len(grid),
                                  vmem_limit_bytes=VMEM_LIMIT_MB << 20)
    kw = {}
    if aliases:
        kw["input_output_aliases"] = aliases
    if prefetch is None:
        fn = pl.pallas_call(body, grid=grid, in_specs=in_specs, out_specs=out_specs, out_shape=out_shape,
                            scratch_shapes=list(scratch), compiler_params=params, name=name, **kw)
        res = fn(*arrays)
    else:
        gs = pltpu.PrefetchScalarGridSpec(num_scalar_prefetch=len(prefetch), grid=grid, in_specs=in_specs,
                                          out_specs=out_specs, scratch_shapes=list(scratch))
        fn = pl.pallas_call(body, grid_spec=gs, out_shape=out_shape, compiler_params=params, name=name, **kw)
        res = fn(*prefetch, *arrays)
    return list(res)


def _full(a):
    nd = a.ndim
    return (a, pl.BlockSpec(a.shape, lambda *_: (0,) * nd))


def _rows(a, tm):
    return (a, pl.BlockSpec((tm, a.shape[1]), lambda i, *_: (i, 0)))


def _rows_out(M, C, tm, dtype=F32):
    return ((M, C), dtype, pl.BlockSpec((tm, C), lambda i, *_: (i, 0)))


class _Stream:
    def __init__(self, x, B, T, tm):
        self.x, self.B, self.T, self.tm = x, B, T, tm
        self.M = B * T
        self.m6 = None

    def set_mods(self, m6):
        self.m6 = m6
        self.rep = jnp.repeat(m6, self.T, axis=0) if self.T < self.tm else None

    def mod(self, c, tm=None):
        tm = tm or self.tm
        if self.T % tm == 0:
            tpb = self.T // tm
            a = self.m6[:, c * D:(c + 1) * D].reshape(self.B, 1, D)
            return (a, pl.BlockSpec((None, 1, D), lambda i, *_: (i // tpb, 0, 0)))
        assert tm % self.T == 0 and self.M % tm == 0
        a = self.rep[:, c * D:(c + 1) * D].reshape(self.M // tm, tm, D)
        return (a, pl.BlockSpec((None, tm, D), lambda i, *_: (i, 0, 0)))


def _mm(x, w_in, *, tm, pro=None, pro_ins=(), epi=None, epi_ins=(), outs=None, name="mm"):
    M, K = x.shape
    n_pro, n_epi = len(pro_ins), len(epi_ins)
    n_out = len(outs)

    def body(*refs):
        x_ref = refs[0]
        pro_refs = refs[1:1 + n_pro]
        w_ref = refs[1 + n_pro]
        epi_refs = refs[2 + n_pro:2 + n_pro + n_epi]
        out_refs = refs[2 + n_pro + n_epi:2 + n_pro + n_epi + n_out]
        a = x_ref[...]
        if pro is not None:
            a = pro(a, *[r[...] for r in pro_refs])
        acc = _d(a, w_ref[...])
        res = epi(acc, *[r[...] for r in epi_refs]) if epi is not None else (acc,)
        for o, r in zip(out_refs, res):
            o[...] = r.astype(o.dtype)

    ins = [_rows(x, tm)] + list(pro_ins) + [w_in] + list(epi_ins)
    return _call(body, (M // tm,), ins, outs, name=name)


def _ada(c_all, w_mod, b_mod, layer):
    Mp = c_all.shape[0]
    N = w_mod.shape[2]
    tn = 1536

    def body(c_ref, w_ref, b_ref, o_ref):
        o_ref[...] = _d(_silu(c_ref[...]), w_ref[...]) + b_ref[...]

    ins = [(c_all, pl.BlockSpec((Mp, D), lambda j: (0, 0))),
           (w_mod, pl.BlockSpec((None, D, tn), lambda j: (layer, 0, j))),
           (b_mod.reshape(b_mod.shape[0], 1, N), pl.BlockSpec((None, 1, tn), lambda j: (layer, 0, j)))]
    outs = [((Mp, N), F32, pl.BlockSpec((Mp, tn), lambda j: (0, j)))]
    return _call(body, (N // tn,), ins, outs, name="ada_mod")[0]


def _pick_tile(*sizes, cap=512):
    t = cap
    while t > SUBLANES and any(s % t for s in sizes):
        t //= 2
    assert all(s % t == 0 for s in sizes), sizes
    return t


def _route(lg):
    lane = lax.broadcasted_iota(I32, lg.shape, 1)
    big = jnp.int32(1 << 20)
    isg = lane < MOE_GROUPS
    gl = jnp.where(isg, lg, NEG)
    gmax = jnp.max(gl, axis=-1, keepdims=True)
    gsel = jnp.min(jnp.where(gl == gmax, lane, big), axis=-1, keepdims=True)
    gsum = jnp.sum(jnp.where(isg, jnp.exp(gl - gmax), 0.0), axis=-1, keepdims=True)
    gw = 1.0 / gsum
    lo = MOE_GROUPS + MOE_EPG * gsel
    ise = (lane >= lo) & (lane < lo + MOE_EPG)
    el = jnp.where(ise, lg, NEG)
    emax = jnp.max(el, axis=-1, keepdims=True)
    ep = jnp.where(ise, jnp.exp(el - emax), 0.0)
    prob = ep / jnp.sum(ep, axis=-1, keepdims=True)
    pm = jnp.where(ise, prob, -1.0)
    p1 = jnp.max(pm, axis=-1, keepdims=True)
    i1 = jnp.min(jnp.where(pm == p1, lane, big), axis=-1, keepdims=True)
    pm2 = jnp.where(lane == i1, -1.0, pm)
    p2 = jnp.max(pm2, axis=-1, keepdims=True)
    i2 = jnp.min(jnp.where(pm2 == p2, lane, big), axis=-1, keepdims=True)
    den = p1 + p2
    w1 = gw * p1 / den
    w2 = gw * p2 / den
    e1 = (i1 - MOE_GROUPS).astype(F32)
    e2 = (i2 - MOE_GROUPS).astype(F32)
    return jnp.where(lane == 0, e1, jnp.where(lane == 1, e2, jnp.where(lane == 2, w1, jnp.where(lane == 3, w2, 0.0))))


NSEG = D // LANES


def _to_tiles(ref, x):
    for s in range(NSEG):
        ref[:, s, :] = x[:, s * LANES:(s + 1) * LANES]


def _from_tiles(ref):
    return jnp.concatenate([ref[:, s, :] for s in range(NSEG)], axis=1)


def _moe_router(st, wgr, bgr, h_all, off):
    tm = _pick_tile(st.M, off, cap=st.tm)
    b0 = off // tm

    def body(x_ref, sc_ref, sh_ref, w_ref, b_ref, hin_ref, h_ref, r_ref):
        h = _modulate(x_ref[...], sc_ref[...], sh_ref[...])
        _to_tiles(h_ref, h)
        r_ref[...] = _route(_d_f32(h, w_ref[...]) + b_ref[...])

    ins = [_rows(st.x, tm), st.mod(4, tm), st.mod(3, tm), _full(wgr), _full(bgr),
           (h_all, pl.BlockSpec(memory_space=pl.ANY))]
    outs = [(h_all.shape, F32, pl.BlockSpec((tm, NSEG, LANES), lambda i: (b0 + i, 0, 0))),
            _rows_out(st.M, LANES, tm)]
    return _call(body, (st.M // tm,), ins, outs, aliases={5: 0}, name="moe_router")


def _moe_counts(rinfo, R):
    Mtot = rinfo.shape[0]
    nt = Mtot // R

    def body(r_ref, o_ref):
        j = pl.program_id(0)
        t = pl.program_id(1)

        @pl.when((j == 0) & (t == 0))
        def _():
            o_ref[...] = jnp.zeros_like(o_ref)

        xt = r_ref[...].T
        row = jnp.where(j == 0, xt[0:1, :], xt[1:2, :])
        sub = lax.broadcasted_iota(I32, (LANES, R), 0).astype(F32)
        oh = jnp.where(sub == row, 1.0, 0.0)
        o_ref[...] += jnp.sum(oh, axis=1, keepdims=True)

    ins = [(rinfo, pl.BlockSpec((R, LANES), lambda j, t: (t, 0)))]
    outs = [((LANES, LANES), F32, pl.BlockSpec((LANES, LANES), lambda j, t: (0, 0)))]
    return _call(body, (2, nt), ins, outs, name="moe_counts")[0]


def _moe_dest(rinfo, pstart, R):
    Mtot = rinfo.shape[0]
    nt = Mtot // R
    upper = jnp.asarray(np.triu(np.ones((R, R), np.float32), 1), BF16)

    def body(r_ref, p_ref, u_ref, o_ref, carry):
        j = pl.program_id(0)
        t = pl.program_id(1)

        @pl.when((j == 0) & (t == 0))
        def _():
            carry[...] = jnp.zeros_like(carry)

        xt = r_ref[...].T
        row = jnp.where(j == 0, xt[0:1, :], xt[1:2, :])
        sub = lax.broadcasted_iota(I32, (LANES, R), 0).astype(F32)
        oh = jnp.where(sub == row, 1.0, 0.0)
        cum = _d(oh, u_ref[...])
        base = carry[:, 0:1] + p_ref[:, 0:1]
        dest = jnp.sum(oh * (cum + base), axis=0, keepdims=True)
        o_ref[...] = dest.astype(I32)
        carry[...] += jnp.sum(oh, axis=1, keepdims=True)

    ins = [(rinfo, pl.BlockSpec((R, LANES), lambda j, t: (t, 0))), _full(pstart), _full(upper)]
    outs = [((2 * nt, 1, R), I32, pl.BlockSpec((None, 1, R), lambda j, t: (j * nt + t, 0, 0)))]
    return _call(body, (2, nt), ins, outs, scratch=[pltpu.VMEM((LANES, LANES), F32)], name="moe_dest")[0]


def _moe_ffn(h_all, slots, blk_expert, nvalid, w1, w3, w2, layer, Mtot):
    nblk = slots.shape[0] // MOE_BLK
    FF = w1.shape[-1]
    any_spec = pl.BlockSpec(memory_space=pl.ANY)
    GRP = SUBLANES

    def body(be_ref, nv_ref, slot_ref, h_ref, w1_ref, w3_ref, w2_ref, y_ref, xbuf, ybuf, sem_in, sem_out):
        i = pl.program_id(0)
        nv = nv_ref[0]
        buf = i % 2

        def row_copy(kind, blk, b, r):
            s = slot_ref[blk * MOE_BLK + r]
            if kind == "gather":
                tok = jnp.where(s >= 2 * Mtot, s - 2 * Mtot, jnp.where(s >= Mtot, s - Mtot, s))
                return pltpu.make_async_copy(h_ref.at[tok], xbuf.at[b, r], sem_in.at[b])
            return pltpu.make_async_copy(ybuf.at[b, r], y_ref.at[s], sem_out.at[b])

        def each_row(kind, blk, b, start):
            for r in range(MOE_BLK):
                cp = row_copy(kind, blk, b, r)
                cp.start() if start else cp.wait()

        @pl.when(i < nv)
        def _():
            @pl.when(i == 0)
            def _():
                ybuf[...] = jnp.zeros_like(ybuf)
                for b in range(2):
                    cp = pltpu.make_async_copy(ybuf.at[b], y_ref.at[pl.ds(2 * Mtot + b * MOE_BLK, MOE_BLK)],
                                               sem_out.at[b])
                    cp.start()
                    cp.wait()
                each_row("gather", 0, 0, True)

            nxt = jnp.minimum(i + 1, nv - 1)
            each_row("gather", i, buf, False)
            x = _from_tiles(xbuf.at[buf]).astype(BF16)
            each_row("gather", nxt, 1 - buf, True)
            a = _d(x, w1_ref[...])
            b = _d(x, w3_ref[...])
            y = _d(_silu(a) * b, w2_ref[...])

            @pl.when(i >= 2)
            def _():
                each_row("scatter", i - 2, buf, False)

            _to_tiles(ybuf.at[buf], y)
            each_row("scatter", i, buf, True)

            @pl.when(i == nv - 1)
            def _():
                each_row("gather", nxt, 1 - buf, False)

                @pl.when(i >= 1)
                def _():
                    each_row("scatter", i - 1, 1 - buf, False)

                each_row("scatter", i, buf, False)

    def blk(i, be, nv, sl):
        return be[jnp.minimum(i, nv[0] - 1)]

    ins = [(h_all, any_spec),
           (w1, pl.BlockSpec((None, None, D, FF), lambda i, be, nv, sl: (layer, blk(i, be, nv, sl), 0, 0))),
           (w3, pl.BlockSpec((None, None, D, FF), lambda i, be, nv, sl: (layer, blk(i, be, nv, sl), 0, 0))),
           (w2, pl.BlockSpec((None, None, FF, D), lambda i, be, nv, sl: (layer, blk(i, be, nv, sl), 0, 0)))]
    outs = [((2 * Mtot + 2 * MOE_BLK, NSEG, LANES), F32, any_spec)]
    scratch = [pltpu.VMEM((2, MOE_BLK, NSEG, LANES), F32), pltpu.VMEM((2, MOE_BLK, NSEG, LANES), F32),
               pltpu.SemaphoreType.DMA((2,)), pltpu.SemaphoreType.DMA((2,))]
    return _call(body, (nblk,), ins, outs, scratch=scratch, prefetch=[blk_expert, nvalid, slots], name="moe_ffn")[0]


def _moe_combine(st, yslot, rinfo_all, off, Mtot, alpha, ln_g, ln_b):
    tm = _pick_tile(st.M, off, Mtot, cap=st.tm)
    b0, b1, br = off // tm, (Mtot + off) // tm, off // tm

    def body(y0_ref, y1_ref, r_ref, x_ref, gate_ref, g_ref, b_ref, o_ref):
        r = r_ref[...]
        y = r[:, 2:3] * _from_tiles(y0_ref) + r[:, 3:4] * _from_tiles(y1_ref)
        o_ref[...] = _res_ln(alpha, y, x_ref[...], gate_ref[...], g_ref[...], b_ref[...])

    ins = [(yslot, pl.BlockSpec((tm, NSEG, LANES), lambda i: (b0 + i, 0, 0))),
           (yslot, pl.BlockSpec((tm, NSEG, LANES), lambda i: (b1 + i, 0, 0))),
           (rinfo_all, pl.BlockSpec((tm, LANES), lambda i: (br + i, 0))),
           _rows(st.x, tm), st.mod(5, tm), _full(ln_g), _full(ln_b)]
    return _call(body, (st.M // tm,), ins, [_rows_out(st.M, D, tm)], name="moe_combine")[0]


def _moe_layer(streams, layer, alpha, p):
    wgr = jnp.zeros((D, LANES), F32).at[:, :MOE_GROUPS].set(p["moe_w_group"][layer])
    wgr = wgr.at[:, MOE_GROUPS:MOE_GROUPS + MOE_EXPERTS].set(p["moe_w_router"][layer])
    bgr = jnp.zeros((1, LANES), F32).at[0, :MOE_GROUPS].set(p["moe_b_group"][layer])
    bgr = bgr.at[0, MOE_GROUPS:MOE_GROUPS + MOE_EXPERTS].set(p["moe_b_router"][layer])
    Mtot = sum(st.M for st in streams)
    h_all = jnp.zeros((Mtot, NSEG, LANES), F32)
    rs, off = [], 0
    for st in streams:
        h_all, r = _moe_router(st, wgr, bgr, h_all, off)
        rs.append(r)
        off += st.M
    rinfo = jnp.concatenate(rs, axis=0)
    R = _pick_tile(Mtot)
    counts = _moe_counts(rinfo, R)[:MOE_EXPERTS, 0].astype(I32)
    padded = (counts + MOE_BLK - 1) // MOE_BLK * MOE_BLK
    pad_end = jnp.cumsum(padded)
    pstart = jnp.zeros((LANES,), F32).at[:MOE_EXPERTS].set((pad_end - padded).astype(F32))
    pstart = jnp.broadcast_to(pstart[:, None], (LANES, LANES))
    nblk = -(-2 * Mtot // MOE_BLK) + MOE_EXPERTS
    blk_first = jnp.arange(nblk, dtype=I32) * MOE_BLK
    blk_expert = jnp.minimum(jnp.sum((pad_end[None, :] <= blk_first[:, None]).astype(I32), axis=1), MOE_EXPERTS - 1)
    nvalid = (pad_end[-1:] // MOE_BLK).astype(I32)
    dest = _moe_dest(rinfo, pstart, R).reshape(-1)
    spare = 2 * Mtot + jnp.arange(nblk * MOE_BLK, dtype=I32) % (2 * MOE_BLK)
    slots = spare.at[dest].set(jnp.arange(2 * Mtot, dtype=I32))
    yslot = _moe_ffn(h_all, slots, blk_expert, nvalid, p["moe_w1"], p["moe_w3"], p["moe_w2"], layer, Mtot)
    ln_g = p["ln_g"][layer, 1].reshape(1, D)
    ln_b = p["ln_b"][layer, 1].reshape(1, D)
    off = 0
    for st in streams:
        st.x = _moe_combine(st, yslot, rinfo, off, Mtot, alpha, ln_g, ln_b)
        off += st.M


def _scan_body(nbg, nv, tb, w_ref, kkn_ref, b_ref, k_ref, r_ref, vt_ref, s0_ref, hexp_ref, hsum_ref, bd_ref,
               o_ref, sf_ref, s_scr):
    t = pl.program_id(1)

    @pl.when(t == 0)
    def _():
        s_scr[...] = s0_ref[...]

    def sub(sb, carry):
        base = pl.multiple_of(sb * SUBLANES, SUBLANES)
        rows = [[ref[bb, pl.ds(base, SUBLANES), :] for bb in range(nbg)]
                for ref in (w_ref, kkn_ref, b_ref, k_ref, r_ref)]
        vt = vt_ref[:, sb].reshape(nbg * nv, LANES).astype(BF16)
        oacc = jnp.zeros((nbg * nv, LANES), F32)
        for j in range(SUBLANES):
            S = [s_scr[bb] for bb in range(nbg)]
            P = jnp.concatenate([S[bb] * rows[1][bb][j:j + 1] for bb in range(nbg)], axis=0).astype(BF16)
            sa = jnp.concatenate([_d(P[:, c0:c0 + MXU_TILE], bd_ref[...]) for c0 in range(0, D, MXU_TILE)], axis=1)
            vb = _d(vt, hexp_ref[j])
            P2 = []
            for bb in range(nbg):
                sl = slice(bb * nv, (bb + 1) * nv)
                Sn = S[bb] * rows[0][bb][j:j + 1] + sa[sl] * rows[2][bb][j:j + 1] + vb[sl] * rows[3][bb][j:j + 1]
                s_scr[bb] = Sn
                P2.append(Sn * rows[4][bb][j:j + 1])
            oacc = oacc + _d(jnp.concatenate(P2, axis=0), hsum_ref[j])
        o_ref[:, sb] = oacc.reshape(nbg, nv, LANES)
        return carry

    lax.fori_loop(0, tb // SUBLANES, sub, 0)

    @pl.when(t == pl.num_programs(1) - 1)
    def _():
        sf_ref[...] = s_scr[...]


def _delta_scan(w, kkn, b, k, r, v, S0, B, T, hsz):
    nh = D // hsz
    nv = hsz
    nbg = 4 if B % 4 == 0 else (2 if B % 2 == 0 else 1)
    tb = min(64, T)
    hexp, hsum, bd = _scan_consts(hsz)
    vt = v.reshape(B, T // SUBLANES, SUBLANES, nh, nv).transpose(0, 1, 4, 2, 3)
    vt = jnp.pad(vt, ((0, 0),) * 4 + ((0, 16 - nh),)).reshape(B, T // SUBLANES, nv, LANES)
    seq = lambda a: (a.reshape(B, T, D), pl.BlockSpec((nbg, tb, D), lambda g, t: (g, t, 0)))
    ins = [seq(w), seq(kkn), seq(b), seq(k), seq(r),
           (vt, pl.BlockSpec((nbg, tb // SUBLANES, nv, LANES), lambda g, t: (g, t, 0, 0))),
           (S0, pl.BlockSpec((nbg, nv, D), lambda g, t: (g, 0, 0))),
           _full(hexp), _full(hsum), _full(bd)]
    outs = [((B, T // SUBLANES, nv, LANES), F32,
             pl.BlockSpec((nbg, tb // SUBLANES, nv, LANES), lambda g, t: (g, t, 0, 0))),
            ((B, nv, D), F32, pl.BlockSpec((nbg, nv, D), lambda g, t: (g, 0, 0)))]
    body = functools.partial(_scan_body, nbg, nv, tb)
    op, sf = _call(body, (B // nbg, T // tb), ins, outs, scratch=[pltpu.VMEM((nbg, nv, D), F32)], name="delta_scan")
    o = op.reshape(B, T // SUBLANES, nv, SUBLANES, 16)[..., :nh].transpose(0, 1, 3, 4, 2).reshape(B * T, D)
    return o, sf


def _shifted_rows(h, first, period, shift=1):
    row = lax.broadcasted_iota(I32, h.shape, 0)
    return jnp.where(row % period < shift, first, pltpu.roll(h, shift, axis=0))


def _rwkv_prep(st, shift_prev, p, tm):
    long_seq = st.T % tm == 0
    tpb = st.T // tm if long_seq else 1
    hs, he = _seg_consts(RWKV_HSZ)
    row = lambda a: _full(a.reshape(1, D))
    wts = [_full(p["rwkv_mu"]), _full(p["rwkv_w_rkv"].astype(BF16)),
           _full(p["rwkv_w1"].astype(BF16)), _full(p["rwkv_w2"].astype(BF16)),
           _full(p["rwkv_a1"].astype(BF16)), _full(p["rwkv_a2"].astype(BF16)),
           _full(p["rwkv_g1"].astype(BF16)), _full(p["rwkv_g2"].astype(BF16)),
           row(p["rwkv_w0"]), row(p["rwkv_a0"]), row(p["rwkv_k_k"]), row(p["rwkv_k_a"]), _full(hs), _full(he)]
    if long_seq:
        nsub = tm // SUBLANES
        first_ins = [(st.x, pl.BlockSpec((SUBLANES, D), lambda i: (jnp.maximum(i * nsub - 1, 0), 0))),
                     (shift_prev.reshape(st.B, 1, D), pl.BlockSpec((None, 1, D), lambda i: (i // tpb, 0, 0)))]
    else:
        first_ins = [_rows(jnp.repeat(shift_prev, st.T, axis=0), tm)]
    nf = len(first_ins)

    def body(x_ref, sc_ref, sh_ref, *refs):
        first_refs, refs = refs[:nf], refs[nf:]
        (mu_ref, wrkv_ref, w1_ref, w2_ref, a1_ref, a2_ref, g1_ref, g2_ref, w0_ref, a0_ref, kk_ref, ka_ref,
         hs_ref, he_ref) = refs[:14]
        h_ref, r_ref, w_ref, k_ref, v_ref, kkn_ref, b_ref, g_ref = refs[14:]
        sc, sh = sc_ref[...], sh_ref[...]
        h = _modulate(x_ref[...], sc, sh)
        if long_seq:
            hh = _modulate(first_refs[0][...], sc, sh)[SUBLANES - 1:SUBLANES]
            first = jnp.where(pl.program_id(0) % tpb == 0, first_refs[1][...], hh)
            hprev = _shifted_rows(h, first, tm)
        else:
            hprev = _shifted_rows(h, first_refs[0][...], st.T)
        xx = hprev - h
        mu = mu_ref[...]
        xr, xw, xk, xv, xa, xg = [h + xx * mu[i:i + 1] for i in range(6)]
        r = _d(xr, wrkv_ref[0])
        k = _d(xk, wrkv_ref[1])
        v = _d(xv, wrkv_ref[2])
        logw = -_softplus(-(w0_ref[...] + _d(jnp.tanh(_d(xw, w1_ref[...])), w2_ref[...]))) - 0.5
        a = _sigmoid(a0_ref[...] + _d(_d(xa, a1_ref[...]), a2_ref[...]))
        g = _d(_sigmoid(_d(xg, g1_ref[...])), g2_ref[...])
        kk = k * kk_ref[...]
        inv = lax.rsqrt(_d_x3(kk * kk, hs_ref[...]) + 1e-6)
        kk = kk * _d_x3(inv, he_ref[...])
        h_ref[...] = h
        r_ref[...] = r
        w_ref[...] = jnp.exp(-jnp.exp(logw))
        k_ref[...] = k * (1.0 + (a - 1.0) * ka_ref[...])
        v_ref[...] = v
        kkn_ref[...] = -kk
        b_ref[...] = kk * a
        g_ref[...] = g

    ins = [_rows(st.x, tm), st.mod(1, tm), st.mod(0, tm)] + first_ins + wts
    outs = [_rows_out(st.M, D, tm) for _ in range(8)]
    return _call(body, (st.M // tm,), ins, outs, name="rwkv_prep")


def _rwkv_out(st, o, r, kmod, v, g, p, alpha, ln_g, ln_b, tm):
    hs, he = _seg_consts(RWKV_HSZ)
    inv_n = 1.0 / RWKV_HSZ

    def pro(o, r, k, v, g, lw, lb, rk, hs, he):
        mean = _d_x3(_d_x3(o, hs) * inv_n, he)
        c = o - mean
        rstd = lax.rsqrt(_d_x3(c * c, hs) * inv_n + RWKV_GN_EPS)
        on = c * _d_x3(rstd, he) * lw + lb
        bonus = _d_x3(_d_x3(r * k * rk, hs), he) * v
        return (on + bonus) * g

    def epi(acc, x, gate, g_, b_):
        return (_res_ln(alpha, acc, x, gate, g_, b_),)

    row = lambda a: _full(a.reshape(1, D))
    pro_ins = [_rows(a, tm) for a in (r, kmod, v, g)] + [row(p["rwkv_ln_w"]), row(p["rwkv_ln_b"]),
                                                       row(p["rwkv_r_k"]), _full(hs), _full(he)]
    epi_ins = [_rows(st.x, tm), st.mod(2, tm), _full(ln_g), _full(ln_b)]
    return _mm(o, _full(p["rwkv_w_o"].astype(BF16)), tm=tm, pro=pro, pro_ins=pro_ins, epi=epi, epi_ins=epi_ins,
               outs=[_rows_out(st.M, D, tm)], name="rwkv_out")[0]


def _rwkv_layer(st, shift_prev, wkv0, p, alpha, ln_g, ln_b):
    B, T = st.B, st.T
    tm = min(256, st.M)
    nh = D // RWKV_HSZ
    h, r, w, kmod, v, kkn, b, g = _rwkv_prep(st, shift_prev, p, tm)
    S0 = wkv0.transpose(0, 2, 1, 3).reshape(B, RWKV_HSZ, D)
    o, sf = _delta_scan(w, kkn, b, kmod, r, v, S0, B, T, RWKV_HSZ)
    st.x = _rwkv_out(st, o, r, kmod, v, g, p, alpha, ln_g, ln_b, tm)
    wkv = sf.reshape(B, RWKV_HSZ, nh, RWKV_HSZ).transpose(0, 2, 1, 3)
    return wkv, h.reshape(B, T, D)[:, -1]


def _pad_cols(w, n):
    return jnp.pad(w, ((0, 0), (0, n - w.shape[1])))


def _gdn_proj(st, p, tm):
    C = 3 * D
    w = _pad_cols(p["gdn_w_in"], C + D + LANES).astype(BF16)

    def epi(acc):
        return acc[:, :C], acc[:, C:C + D], acc[:, C + D:]

    outs = [_rows_out(st.M, C, tm), _rows_out(st.M, D, tm), _rows_out(st.M, LANES, tm)]
    return _mm(st.x, _full(w), tm=tm, pro=_modulate, pro_ins=[st.mod(1, tm), st.mod(0, tm)], epi=epi, outs=outs,
               name="gdn_proj")


def _gdn_conv(st, pre, ba, conv_buf, p, tm, chunked):
    C = 3 * D
    H = GDN_HEADS
    long_seq = st.T % tm == 0
    tpb = st.T // tm if long_seq else 1
    hs, he = _seg_consts(GDN_HSZ)
    hsn = _seg_np(GDN_HSZ)
    he_b = jnp.asarray(hsn.T, BF16)
    he_a = jnp.asarray(np.roll(hsn.T, H, axis=0), BF16)
    alog = jnp.zeros((1, LANES), F32).at[0, H:2 * H].set(p["gdn_A_log"])
    dtb = jnp.zeros((1, LANES), F32).at[0, H:2 * H].set(p["gdn_dt_bias"])
    if long_seq:
        nsub = tm // SUBLANES
        init8 = jnp.pad(conv_buf, ((0, 0), (SUBLANES - (GDN_CONV - 1), 0), (0, 0)))
        first_ins = [(pre, pl.BlockSpec((SUBLANES, C), lambda i: (jnp.maximum(i * nsub - 1, 0), 0))),
                     (init8, pl.BlockSpec((None, SUBLANES, C), lambda i: (i // tpb, 0, 0)))]
    else:
        padded = jnp.pad(conv_buf, ((0, 0), (0, st.T), (0, 0)))
        first_ins = [_rows(padded[:, GDN_CONV - 1 - j:GDN_CONV - 1 - j + st.T].reshape(st.M, C), tm)
                     for j in range(1, GDN_CONV)]
    nf = len(first_ins)

    def body(pre_ref, ba_ref, *refs):
        first_refs, refs = refs[:nf], refs[nf:]
        cw_ref, alog_ref, dtb_ref, hs_ref, he_ref, heb_ref, hea_ref = refs[:7]
        w_ref, kkn_ref, k_ref, q_ref, v_ref = refs[7:]
        x = pre_ref[...]
        if long_seq:
            halo = jnp.where(pl.program_id(0) % tpb == 0, first_refs[1][...], first_refs[0][...])
            big = jnp.concatenate([halo, x], axis=0)
            sh = [pltpu.roll(big, j, axis=0)[SUBLANES:] for j in range(1, GDN_CONV)]
        else:
            sh = [_shifted_rows(x, first_refs[j - 1][...], st.T, j) for j in range(1, GDN_CONV)]
        cw = cw_ref[...]
        conv = sh[2] * cw[0:1]
        conv = conv + sh[1] * cw[1:2]
        conv = conv + sh[0] * cw[2:3]
        conv = conv + x * cw[3:4]
        c = _silu(conv)
        q, k, v = c[:, :D], c[:, D:2 * D], c[:, 2 * D:]
        qn = q * _d_x3(lax.rsqrt(_d_x3(q * q, hs_ref[...]) + 1e-6), he_ref[...]) * (GDN_HSZ ** -0.5)
        kn = k * _d_x3(lax.rsqrt(_d_x3(k * k, hs_ref[...]) + 1e-6), he_ref[...])
        ba = ba_ref[...]
        logdecay = -jnp.exp(alog_ref[...]) * _softplus(ba + dtb_ref[...])
        if chunked:
            w_ref[...] = _sigmoid(ba)
            kkn_ref[...] = logdecay
            k_ref[...] = kn
            q_ref[...] = qn
            v_ref[...] = v
        else:
            beta = _d_x3(_sigmoid(ba), heb_ref[...])
            a = _d_x3(jnp.exp(logdecay), hea_ref[...])
            w_ref[...] = a
            kkn_ref[...] = -(a * beta) * kn
            k_ref[...] = kn
            q_ref[...] = qn
            v_ref[...] = beta * v

    ins = [_rows(pre, tm), _rows(ba, tm)] + first_ins + [_full(p["gdn_conv_w"]), _full(alog), _full(dtb), _full(hs),
                                                         _full(he), _full(he_b), _full(he_a)]
    small = LANES if chunked else D
    outs = [_rows_out(st.M, small, tm), _rows_out(st.M, small, tm)] + [_rows_out(st.M, D, tm) for _ in range(3)]
    return _call(body, (st.M // tm,), ins, outs, name="gdn_conv")


GDN_CHUNK = 64


def _gdn_chunk_scan(q, k, v, beta, g, S0, B, T):
    C = GDN_CHUNK
    H, N = GDN_HEADS, GDN_HSZ
    nchunk = T // C
    tril = _lower_tri(C)
    triu = jnp.asarray(np.triu(np.ones((C, C), np.float32)), BF16)

    def body(q_ref, k_ref, v_ref, b_ref, g_ref, s0_ref, tril_ref, triu_ref, o_ref, sf_ref, s_scr):
        c = pl.program_id(1)

        @pl.when(c == 0)
        def _():
            s_scr[...] = s0_ref[...]

        gblk = g_ref[...]
        gc = _d_3x(tril_ref[...], gblk)
        gh, gm, gl = _split3(gblk)
        tn = lambda a: lax.dot_general(a, triu_ref[...], (((0,), (0,)), ((), ())), preferred_element_type=F32)
        gct = tn(gh) + tn(gm) + tn(gl)
        bblk = b_ref[...]
        ri = lax.broadcasted_iota(I32, (C, C), 0)
        ci = lax.broadcasted_iota(I32, (C, C), 1)
        lower, strict = ri >= ci, ri > ci
        eye = jnp.where(ri == ci, 1.0, 0.0)
        heads = range(H)
        sl = [slice(h * N, (h + 1) * N) for h in heads]
        bcol = [bblk[:, h:h + 1] for h in heads]
        gcol = [gc[:, H + h:H + h + 1] for h in heads]
        gamma = [jnp.where(lower, jnp.exp(jnp.minimum(gcol[h] - gct[H + h:H + h + 1, :], 0.0)), 0.0) for h in heads]
        kb = [k_ref[:, sl[h]] * bcol[h] for h in heads]
        pw = [jnp.where(strict, _d_nt(kb[h], k_ref[:, sl[h]]) * gamma[h], 0.0) for h in heads]
        t_inv = [eye - pw[h] for h in heads]
        for _ in range(int(math.log2(C)) - 1):
            pw = [_d(pw[h], pw[h]) for h in heads]
            t_inv = [t_inv[h] + _d(t_inv[h], pw[h]) for h in heads]
        eg = [jnp.exp(gcol[h]) for h in heads]
        u = [_d(t_inv[h], v_ref[:, sl[h]] * bcol[h]) for h in heads]
        w = [_d(t_inv[h], kb[h] * eg[h]) for h in heads]
        qk = [jnp.where(lower, _d_nt(q_ref[:, sl[h]], k_ref[:, sl[h]]) * gamma[h], 0.0) for h in heads]
        v_new = [u[h] - _d(w[h], s_scr[h]) for h in heads]
        for h in heads:
            o_ref[:, sl[h]] = _d(q_ref[:, sl[h]] * eg[h], s_scr[h]) + _d(qk[h], v_new[h])
        for h in heads:
            g_last = gcol[h][C - 1:C]
            kd = (k_ref[:, sl[h]] * jnp.exp(g_last - gcol[h])).astype(BF16)
            s_scr[h] = s_scr[h] * jnp.exp(g_last) + lax.dot_general(kd, v_new[h].astype(BF16), (((0,), (0,)), ((), ())),
                                                                   preferred_element_type=F32)

        @pl.when(c == nchunk - 1)
        def _():
            sf_ref[...] = s_scr[...]

    seq = lambda a, w: (a, pl.BlockSpec((C, w), lambda b, c: (b * nchunk + c, 0)))
    ins = [seq(q, D), seq(k, D), seq(v, D), seq(beta, LANES), seq(g, LANES),
           (S0, pl.BlockSpec((None, H, N, N), lambda b, c: (b, 0, 0, 0))),
           (tril, pl.BlockSpec(tril.shape, lambda b, c: (0, 0))), (triu, pl.BlockSpec(triu.shape, lambda b, c: (0, 0)))]
    outs = [((B * T, D), F32, pl.BlockSpec((C, D), lambda b, c: (b * nchunk + c, 0))),
            ((B, H, N, N), F32, pl.BlockSpec((None, H, N, N), lambda b, c: (b, 0, 0, 0)))]
    return _call(body, (B, nchunk), ins, outs, scratch=[pltpu.VMEM((H, N, N), F32)], name="gdn_chunk_scan")


def _gdn_out(st, o, z, p, alpha, ln_g, ln_b, tm):
    hs, he = _seg_consts(GDN_HSZ)
    nw = jnp.tile(p["gdn_norm_w"], GDN_HEADS).reshape(1, D)

    def pro(o, z, nw, hs, he):
        rstd = lax.rsqrt(_d_x3(o * o, hs) * (1.0 / GDN_HSZ) + 1e-6)
        return o * _d_x3(rstd, he) * nw * _silu(z)

    def epi(acc, x, gate, g_, b_):
        return (_res_ln(alpha, acc, x, gate, g_, b_),)

    return _mm(o, _full(p["gdn_w_o"].astype(BF16)), tm=tm, pro=pro, pro_ins=[_rows(z, tm), _full(nw), _full(hs), _full(he)],
               epi=epi, epi_ins=[_rows(st.x, tm), st.mod(2, tm), _full(ln_g), _full(ln_b)],
               outs=[_rows_out(st.M, D, tm)], name="gdn_out")[0]


def _gdn_layer(st, conv_buf, S0, p, alpha, ln_g, ln_b):
    B, T = st.B, st.T
    tm = min(256, st.M)
    pre, z, ba = _gdn_proj(st, p, tm)
    chunked = T % GDN_CHUNK == 0
    if chunked:
        beta, g, kn, qn, v = _gdn_conv(st, pre, ba, conv_buf, p, tm, True)
        o, S = _gdn_chunk_scan(qn, kn, v, beta, g, S0, B, T)
    else:
        w, kkn, kn, qn, vb = _gdn_conv(st, pre, ba, conv_buf, p, tm, False)
        S0t = S0.transpose(0, 3, 1, 2).reshape(B, GDN_HSZ, D)
        o, sf = _delta_scan(w, kkn, kn, kn, qn, vb, S0t, B, T, GDN_HSZ)
        S = sf.reshape(B, GDN_HSZ, GDN_HEADS, GDN_HSZ).transpose(0, 2, 3, 1)
    st.x = _gdn_out(st, o, z, p, alpha, ln_g, ln_b, tm)
    xpad = jnp.concatenate([conv_buf, pre.reshape(B, T, 3 * D)[:, -(GDN_CONV - 1):]], axis=1)
    return S, xpad[:, -(GDN_CONV - 1):]


def _flash_body(cfg, *refs):
    tq, hq, hk = cfg["tq"], cfg["hq"], cfg["hk"]
    fox, bias, aug, window = cfg["fox"], cfg["bias"], cfg["aug"], cfg["window"]
    tk = tq
    G = hq // hk
    Kc = HD + aug
    R = G * tq
    refs = list(refs)
    q_ref, k_ref, v_ref = refs[:3]
    pos = 3
    if aug:
        mb_ref, e_ref = refs[pos:pos + 2]
        pos += 2
    if bias:
        tz_ref = refs[pos]
        pos += 1
    if fox:
        cq_ref, ck_ref = refs[pos:pos + 2]
        pos += 2
    o_ref, kb, vb, s_scr = refs[pos:pos + 4]
    g = pl.program_id(1)
    qi = pl.program_id(2)

    @pl.when(qi == 0)
    def _():
        vb[...] = v_ref[...].astype(BF16)
        if aug and cfg.get("kt"):
            k = k_ref[...]
            kb[...] = jnp.concatenate(
                [jnp.concatenate([k[kv * HD:(kv + 1) * HD].astype(BF16), e_ref[...]], axis=0) for kv in range(hk)],
                axis=0)
        elif aug:
            k = k_ref[...]
            kb[...] = jnp.concatenate(
                [jnp.concatenate([k[:, kv * HD:(kv + 1) * HD].astype(BF16), e_ref[...]], axis=1) for kv in range(hk)],
                axis=1)
        else:
            kb[...] = k_ref[...].astype(BF16)

    scale = HD ** -0.5
    q = q_ref[...]
    row_t = lax.broadcasted_iota(I32, (R, tk), 0) % tq
    col_s = lax.broadcasted_iota(I32, (R, tk), 1)
    qs, cqs = [], []
    for kv in range(hk):
        x = jnp.concatenate([q[:, (kv * G + gg) * HD:(kv * G + gg + 1) * HD] for gg in range(G)], axis=0) * scale
        if aug:
            x = jnp.concatenate([x.astype(BF16), jnp.concatenate([mb_ref[kv]] * G, axis=0)], axis=1)
        qs.append(x.astype(BF16))
        if fox:
            lane = lax.broadcasted_iota(I32, (tq, LANES), 1)
            cqs.append(jnp.sum(jnp.where(lane == g * hk + kv, cq_ref[...], 0.0), axis=-1, keepdims=True))

    def logits(kv, c, rel, valid):
        off = pl.multiple_of(c * tk, tk)
        if cfg.get("kt"):
            s = _d(qs[kv], kb[kv * Kc:(kv + 1) * Kc, pl.ds(off, tk)])
        else:
            s = _d_nt(qs[kv], kb[pl.ds(off, tk), kv * Kc:(kv + 1) * Kc])
        if fox:
            sub = lax.broadcasted_iota(I32, (FOX_HEADS, tk), 0)
            ck = jnp.sum(jnp.where(sub == g * hk + kv, ck_ref[:, pl.ds(off, tk)], 0.0), axis=0, keepdims=True)
            s = s + cqs[kv] - ck
        if bias and rel in (0, 1):
            s = s + jnp.concatenate([tz_ref[kv * G + gg, rel] for gg in range(G)], axis=0)
        if rel == 0:
            s = jnp.where(row_t >= col_s, s, NEG)
        if rel == 3:
            s = jnp.where(col_s > row_t, s, NEG)
        if valid is not None:
            s = jnp.where(valid, s, NEG)
        return s

    nt = tk // LANES

    def lane_tiles(x):
        return [x[:, j * LANES:(j + 1) * LANES] for j in range(nt)]

    if window is not None:
        nch = window // tk
        static = [(qi, 0, None)] + [(jnp.maximum(qi - dc, 0), 1 if dc == 1 else (3 if dc == nch else 2), qi - dc >= 0)
                                    for dc in range(1, nch + 1)]
        n_far = 0
    elif bias:
        static = [(qi, 0, None), (jnp.maximum(qi - 1, 0), 1, qi >= 1)]
        n_far = jnp.maximum(qi - 1, 0)
    else:
        static = [(qi, 0, None)]
        n_far = qi
    n_static = len(static)
    unroll = cfg["unroll"]

    def far_loop(fn, carry):
        ng = n_far // unroll

        def group(gi, cr):
            for u in range(unroll):
                cr = fn(gi * unroll + u, cr)
            return cr

        carry = lax.fori_loop(0, ng, group, carry)
        return lax.fori_loop(ng * unroll, n_far, fn, carry)

    outs = []
    for kv in range(hk):
        def score(c, rel, valid, slot, m128):
            s = logits(kv, c, rel, valid)
            s_scr[slot] = s
            for t in lane_tiles(s):
                m128 = jnp.maximum(m128, t)
            return m128

        m128 = jnp.full((R, LANES), NEG, F32)
        for slot, (c, rel, valid) in enumerate(static):
            m128 = score(c, rel, valid, slot, m128)
        if window is None:
            m128 = far_loop(lambda c, m: score(c, 2, None, n_static + c, m), m128)
        mrep = jnp.broadcast_to(jnp.max(m128, axis=-1, keepdims=True), (R, LANES))

        def accumulate(c, slot, carry):
            l128, acc = carry
            p = [jnp.exp(t - mrep) for t in lane_tiles(s_scr[slot])]
            for t in p:
                l128 = l128 + t
            off = pl.multiple_of(c * tk, tk)
            pm = jnp.concatenate(p, axis=1) if nt > 1 else p[0]
            if cfg.get("kt"):
                return l128, acc + _d_nt(pm, vb[kv * HD:(kv + 1) * HD, pl.ds(off, tk)])
            return l128, acc + _d(pm, vb[pl.ds(off, tk), kv * HD:(kv + 1) * HD])

        carry = (jnp.zeros((R, LANES), F32), jnp.zeros((R, HD), F32))
        for slot, (c, rel, valid) in enumerate(static):
            carry = accumulate(c, slot, carry)
        if window is None:
            carry = far_loop(lambda c, cr: accumulate(c, n_static + c, cr), carry)
        l128, acc = carry
        o = acc / jnp.sum(l128, axis=-1, keepdims=True)
        outs += [o[gg * tq:(gg + 1) * tq] for gg in range(G)]
    o_ref[...] = jnp.concatenate(outs, axis=1)


def _flash(cfg, B, T, ngroups, q_in, k_in, v_in, extra_ins, M):
    tq, hq, hk = cfg["tq"], cfg["hq"], cfg["hk"]
    Kc = HD + cfg["aug"]
    ins = [q_in, k_in, v_in] + list(extra_ins)
    nq = T // tq
    R = (hq // hk) * tq
    if cfg["window"] is not None:
        nslots = cfg["window"] // tq + 1
    else:
        nslots = nq + (1 if cfg["bias"] else 0)
    outs = [((M, ngroups * hq * HD), F32, pl.BlockSpec((tq, hq * HD), lambda b, g, i: (b * nq + i, g)))]
    kshape, vshape = ((hk * Kc, T), (hk * HD, T)) if cfg.get("kt") else ((T, hk * Kc), (T, hk * HD))
    scratch = [pltpu.VMEM(kshape, BF16), pltpu.VMEM(vshape, BF16), pltpu.VMEM((nslots, R, tq), F32)]
    return _call(functools.partial(_flash_body, cfg), (B, ngroups, nq), ins, outs, scratch=scratch,
                 name="flash_" + cfg["name"])[0]


def _log_sigmoid(x):
    return -_softplus(-x)


def _fox_proj(st, p, tm):
    hw = FOX_HEADS * HD
    w = _pad_cols(p["fox_w_in"], 3 * hw + LANES).astype(BF16)
    bf = jnp.zeros((1, LANES), F32).at[0, :FOX_HEADS].set(p["fox_b_f"])

    def epi(acc, bf):
        return acc[:, :hw], acc[:, hw:3 * hw], _log_sigmoid(acc[:, 3 * hw:] + bf)

    outs = [_rows_out(st.M, hw, tm), _rows_out(st.M, 2 * hw, tm), _rows_out(st.M, LANES, tm)]
    return _mm(st.x, _full(w), tm=tm, pro=_modulate, pro_ins=[st.mod(1, tm), st.mod(0, tm)], epi=epi,
               epi_ins=[_full(bf)], outs=outs, name="fox_proj")


def _fox_proj_t(st, p, tm):
    hw = FOX_HEADS * HD
    B, T = st.B, st.T
    tpb = T // tm
    w_in = p["fox_w_in"]
    wq = w_in[:, :hw].astype(BF16)
    wkvt = w_in[:, hw:3 * hw].T.astype(BF16)
    wf = _pad_cols(w_in[:, 3 * hw:], LANES).astype(BF16)
    bf = jnp.zeros((1, LANES), F32).at[0, :FOX_HEADS].set(p["fox_b_f"])

    def body(x_ref, sc_ref, sh_ref, wq_ref, wkv_ref, wf_ref, bf_ref, q_ref, kvt_ref, f_ref):
        a = _modulate(x_ref[...], sc_ref[...], sh_ref[...]).astype(BF16)
        q_ref[...] = _d(a, wq_ref[...])
        kvt_ref[...] = _d_nt(wkv_ref[...], a)
        f_ref[...] = _log_sigmoid(_d(a, wf_ref[...]) + bf_ref[...])

    ins = [_rows(st.x, tm), st.mod(1, tm), st.mod(0, tm), _full(wq), _full(wkvt), _full(wf), _full(bf)]
    outs = [_rows_out(st.M, hw, tm),
            ((B, 2 * hw, T), F32, pl.BlockSpec((None, 2 * hw, tm), lambda i: (i // tpb, 0, i % tpb))),
            _rows_out(st.M, LANES, tm)]
    return _call(body, (st.M // tm,), ins, outs, name="fox_proj_t")


def _lower_tri(n):
    return jnp.asarray(np.tril(np.ones((n, n), np.float32)), BF16)


def _cumsum_rows(x, B, T):
    ch = _pick_tile(T)
    tri = _lower_tri(ch)

    def body(x_ref, tri_ref, o_ref):
        carry = jnp.zeros((1, LANES), F32)
        for c in range(T // ch):
            cc = _d_3x(tri_ref[...], x_ref[c * ch:(c + 1) * ch, :]) + carry
            o_ref[c * ch:(c + 1) * ch, :] = cc
            carry = cc[ch - 1:ch, :]

    return _call(body, (B,), [_rows(x, T), _full(tri)], [_rows_out(B * T, LANES, T)], name="cumsum_rows")[0]


def _out_proj(st, o, w_o, alpha, ln_g, ln_b, tm, name):
    def epi(acc, x, gate, g_, b_):
        return (_res_ln(alpha, acc, x, gate, g_, b_),)

    return _mm(o, _full(w_o.astype(BF16)), tm=tm, epi=epi,
               epi_ins=[_rows(st.x, tm), st.mod(2, tm), _full(ln_g), _full(ln_b)],
               outs=[_rows_out(st.M, D, tm)], name=name)[0]


def _fox_prompt(st, p, alpha, ln_g, ln_b):
    B, T, M = st.B, st.T, st.M
    tm = min(256, M)
    q, kvt, logf = _fox_proj_t(st, p, tm)
    cum = _cumsum_rows(logf, B, T)
    ckT = cum.reshape(B, T, LANES)[:, :, :FOX_HEADS].transpose(0, 2, 1)
    tq = min(512, T)
    nq = T // tq
    cfg = dict(name="fox", tq=tq, hq=2, hk=2, fox=True, bias=False, aug=0, window=None, unroll=2, kt=True)
    npair = FOX_HEADS // 2
    q_in = (q, pl.BlockSpec((tq, 2 * HD), lambda b, g, i: (b * nq + i, g)))
    k_in = (kvt, pl.BlockSpec((None, 2 * HD, T), lambda b, g, i: (b, g, 0)))
    v_in = (kvt, pl.BlockSpec((None, 2 * HD, T), lambda b, g, i: (b, npair + g, 0)))
    extra = [(cum, pl.BlockSpec((tq, LANES), lambda b, g, i: (b * nq + i, 0))),
             (ckT, pl.BlockSpec((None, FOX_HEADS, T), lambda b, g, i: (b, 0, 0)))]
    o = _flash(cfg, B, T, npair, q_in, k_in, v_in, extra, M)
    st.x = _out_proj(st, o, p["fox_w_o"], alpha, ln_g, ln_b, tm, "fox_out")
    kv = kvt.reshape(B, 2, FOX_HEADS, HD, T).transpose(0, 4, 1, 2, 3)
    return kv, logf.reshape(B, T, LANES)[:, :, :FOX_HEADS]


def _page_ins(cache, page_shape, npages, first_of_step):
    nd = len(page_shape)
    return [(cache, pl.BlockSpec((None,) + tuple(page_shape),
                                 lambda b, s, pt, j=j: (pt[b, first_of_step(s) + j],) + (0,) * nd))
            for j in range(npages)]


def _fox_cum_sample(logf_new, cache_logf, page_table, Tn):
    B, npg = page_table.shape
    PAGE = cache_logf.shape[1]
    H = cache_logf.shape[2]
    cache_t = cache_logf.transpose(0, 2, 1)
    triu = jnp.asarray(np.triu(np.ones((PAGE, PAGE), np.float32)), BF16)

    def body(pt_ref, *refs):
        pages, new_ref, tri_ref, o_ref = refs[:npg], refs[npg], refs[npg + 1], refs[npg + 2]
        carry = jnp.zeros((H, 1), F32)
        for j in range(npg):
            cc = _d_x3(pages[j][...], tri_ref[...]) + carry
            o_ref[:, j * PAGE:(j + 1) * PAGE] = cc
            carry = cc[:, PAGE - 1:PAGE]
        xn = jnp.concatenate([new_ref[...], jnp.zeros((PAGE - Tn, LANES), F32)], axis=0).T[:H, :]
        o_ref[:, npg * PAGE:(npg + 1) * PAGE] = _d_x3(xn, tri_ref[...]) + carry

    ins = _page_ins(cache_t, (H, PAGE), npg, lambda s: 0)
    ins += [(logf_new, pl.BlockSpec((Tn, LANES), lambda b, s, pt: (b, 0))),
            (triu, pl.BlockSpec(triu.shape, lambda b, s, pt: (0, 0)))]
    Lp = (npg + 1) * PAGE
    outs = [((B, H, Lp), F32, pl.BlockSpec((None, H, Lp), lambda b, s, pt: (b, 0, 0)))]
    return _call(body, (B, 1), ins, outs, prefetch=[page_table], name="fox_cum_sample")[0]


def _rep_mat(n_rows, n_src, per):
    r = np.arange(n_rows)
    src = r // per if per else r % n_src
    return jnp.asarray((src[:, None] == np.arange(n_src)[None, :]).astype(np.float32), BF16)


def _fox_decode(q, kv_new, cache_kv, page_table, cq, ckT, Tn):
    B, npg = page_table.shape
    PAGE, H = cache_kv.shape[1], FOX_HEADS
    hw = H * HD
    R = H * Tn
    pps = PAGES_PER_STEP if npg % PAGES_PER_STEP == 0 else 1
    nsteps = npg // pps
    cache_t = cache_kv.transpose(0, 2, 3, 4, 1).reshape(cache_kv.shape[0], 2, hw, PAGE)
    rep_t = _rep_mat(R, Tn, 0)
    rep_h = _rep_mat(R, H, Tn)
    scale = HD ** -0.5

    def body(pt_ref, *refs):
        pages = refs[:pps]
        q_ref, new_ref, cq_ref, ck_ref, rt_ref, rh_ref, o_ref, qbd, m_s, l_s, acc = refs[pps:]
        s_id = pl.program_id(1)
        own = lax.broadcasted_iota(I32, (R, hw), 0) // Tn == lax.broadcasted_iota(I32, (R, hw), 1) // HD

        @pl.when(s_id == 0)
        def _():
            qbd[...] = jnp.where(own, _d(rt_ref[...], q_ref[...]) * scale, 0.0).astype(BF16)
            m_s[...] = jnp.full(m_s.shape, NEG, F32)
            l_s[...] = jnp.zeros(l_s.shape, F32)
            acc[...] = jnp.zeros(acc.shape, F32)

        cqv = cq_ref[...]

        def update(s, pv):
            m = m_s[:, 0:1]
            m2 = jnp.maximum(m, jnp.max(s, axis=-1, keepdims=True))
            a = jnp.exp(m - m2)
            pr = jnp.exp(s - m2)
            l_s[...] = jnp.broadcast_to(a * l_s[:, 0:1] + jnp.sum(pr, axis=-1, keepdims=True), l_s.shape)
            m_s[...] = jnp.broadcast_to(m2, m_s.shape)
            acc[...] = a * acc[...] + pv(pr)

        kt = jnp.concatenate([pages[j][0].astype(BF16) for j in range(pps)], axis=1)
        vt = jnp.concatenate([pages[j][1].astype(BF16) for j in range(pps)], axis=1)
        off = pl.multiple_of(s_id * (pps * PAGE), pps * PAGE)
        update(_d(qbd[...], kt) + cqv - _d_3x(rh_ref[...], ck_ref[:, pl.ds(off, pps * PAGE)]),
               lambda pr: _d_nt(pr, vt))

        @pl.when(s_id == nsteps - 1)
        def _():
            new = jnp.concatenate([new_ref[...], jnp.zeros((PAGE - Tn, 2 * hw), F32)], axis=0)
            t_row = lax.broadcasted_iota(I32, (R, PAGE), 0) % Tn
            col = lax.broadcasted_iota(I32, (R, PAGE), 1)
            s_new = _d_nt(qbd[...], new[:, :hw]) + cqv - _d_3x(rh_ref[...], ck_ref[:, pl.ds(npg * PAGE, PAGE)])
            update(jnp.where(col <= t_row, s_new, NEG), lambda pr: _d(pr, new[:, hw:]))
            of = jnp.where(own, acc[...] / l_s[:, 0:1], 0.0)
            out = of[0:Tn]
            for h in range(1, H):
                out = out + of[h * Tn:(h + 1) * Tn]
            o_ref[...] = out

    Lp = ckT.shape[2]
    const = lambda a: (a, pl.BlockSpec(a.shape, lambda b, s, pt: (0,) * a.ndim))
    ins = _page_ins(cache_t, (2, hw, PAGE), pps, lambda s: s * pps)
    ins += [(q, pl.BlockSpec((Tn, hw), lambda b, s, pt: (b, 0))),
            (kv_new, pl.BlockSpec((Tn, 2 * hw), lambda b, s, pt: (b, 0))),
            (cq, pl.BlockSpec((None, R, 1), lambda b, s, pt: (b, 0, 0))),
            (ckT, pl.BlockSpec((None, H, Lp), lambda b, s, pt: (b, 0, 0))),
            const(rep_t), const(rep_h)]
    outs = [((B * Tn, hw), F32, pl.BlockSpec((Tn, hw), lambda b, s, pt: (b, 0)))]
    scratch = [pltpu.VMEM((R, hw), BF16), pltpu.VMEM((R, LANES), F32), pltpu.VMEM((R, LANES), F32),
               pltpu.VMEM((R, hw), F32)]
    return _call(body, (B, nsteps), ins, outs, scratch=scratch, prefetch=[page_table], name="fox_decode")[0]


def _fox_sample(st, cache_kv, cache_logf, page_table, p, alpha, ln_g, ln_b):
    B, Tn, M = st.B, st.T, st.M
    tm = min(256, M)
    npg = page_table.shape[1]
    PAGE = cache_kv.shape[1]
    q, kv, logf = _fox_proj(st, p, tm)
    ckT = _fox_cum_sample(logf, cache_logf, page_table, Tn)
    cq = ckT[:, :, npg * PAGE:npg * PAGE + Tn].reshape(B, FOX_HEADS * Tn, 1)
    o = _fox_decode(q, kv, cache_kv, page_table, cq, ckT, Tn)
    st.x = _out_proj(st, o, p["fox_w_o"], alpha, ln_g, ln_b, tm, "fox_out")
    return kv.reshape(B, Tn, 2, FOX_HEADS, HD), logf.reshape(B, Tn, LANES)[:, :, :FOX_HEADS]


KVW = NSA_KVH * HD
HALF = CMP_BLK // 2


def _t5_bucket(dist):
    exact = REL_BUCKETS // 2
    d = jnp.maximum(dist, 0)
    far = exact + (jnp.log(jnp.maximum(d, 1).astype(F32) / exact) / math.log(REL_MAX_DIST / exact)
                   * (REL_BUCKETS - exact)).astype(I32)
    return jnp.where(d < exact, d, jnp.minimum(far, REL_BUCKETS - 1))


def _rel_bias(table, dist):
    return jnp.moveaxis(table[_t5_bucket(dist)], -1, 0)


def _nsa_proj(st, p, tm):
    qw = NSA_HEADS * HD
    w = _pad_cols(p["nsa_w_in"], qw + 6 * KVW + LANES).astype(BF16)

    def epi(acc):
        return (acc[:, :qw], acc[:, qw:qw + 2 * KVW], acc[:, qw + 2 * KVW:qw + 4 * KVW],
                acc[:, qw + 4 * KVW:qw + 6 * KVW], acc[:, qw + 6 * KVW:])

    outs = [_rows_out(st.M, qw, tm)] + [_rows_out(st.M, 2 * KVW, tm)] * 3 + [_rows_out(st.M, LANES, tm)]
    return _mm(st.x, _full(w), tm=tm, pro=_modulate, pro_ins=[st.mod(1, tm), st.mod(0, tm)], epi=epi, outs=outs,
               name="nsa_proj")


def _nsa_proj_t(st, p, tm):
    qw = NSA_HEADS * HD
    B, T = st.B, st.T
    tpb = T // tm
    w_in = p["nsa_w_in"]
    wqc = w_in[:, :qw + 2 * KVW].astype(BF16)
    wswt = w_in[:, qw + 2 * KVW:qw + 6 * KVW].T.astype(BF16)
    wg = _pad_cols(w_in[:, qw + 6 * KVW:], LANES).astype(BF16)

    def body(x_ref, sc_ref, sh_ref, wqc_ref, wsw_ref, wg_ref, q_ref, c_ref, s_ref, w_ref, g_ref):
        a = _modulate(x_ref[...], sc_ref[...], sh_ref[...]).astype(BF16)
        acc = _d(a, wqc_ref[...])
        q_ref[...] = acc[:, :qw]
        c_ref[...] = acc[:, qw:]
        swt = _d_nt(wsw_ref[...], a)
        s_ref[...] = swt[:2 * KVW]
        w_ref[...] = swt[2 * KVW:]
        g_ref[...] = _d(a, wg_ref[...])

    tspec = pl.BlockSpec((None, 2 * KVW, tm), lambda i: (i // tpb, 0, i % tpb))
    ins = [_rows(st.x, tm), st.mod(1, tm), st.mod(0, tm), _full(wqc), _full(wswt), _full(wg)]
    outs = [_rows_out(st.M, qw, tm), _rows_out(st.M, 2 * KVW, tm), ((B, 2 * KVW, T), F32, tspec),
            ((B, 2 * KVW, T), F32, tspec), _rows_out(st.M, LANES, tm)]
    return _call(body, (st.M // tm,), ins, outs, name="nsa_proj_t")


def _cmp_weights(p):
    eye = jnp.eye(NSA_KVH, dtype=F32)
    wk = jnp.einsum("ab,vlde->vladbe", eye, p["nsa_cmp_w1"]).reshape(2, CMP_BLK, KVW, KVW)
    wc = wk.reshape(2, 2, HALF, KVW, KVW).transpose(1, 2, 0, 3, 4)
    w2c = jnp.einsum("ab,vde->vadbe", eye, p["nsa_cmp_w2"]).reshape(2, KVW, KVW)
    b1 = jnp.tile(p["nsa_cmp_b1"][:, None, :], (1, NSA_KVH, 1)).reshape(1, 2 * KVW)
    return wc.astype(BF16), w2c.astype(BF16), b1


def _compress_body(nx, *refs):
    x_refs = refs[:nx]
    wc_ref, w2_ref, b1_ref, o_ref, ua, ub = refs[nx:]
    rows = ua.shape[1]
    nl = 2 * KVW // LANES
    acc = [[jnp.zeros((rows, KVW), F32) for _ in range(2)] for _ in range(2)]
    for l in range(HALF):
        for kv in range(2):
            lo = l * 2 * KVW + kv * KVW
            piece = [r[:, lo:lo + KVW] for r in x_refs]
            piece = (jnp.concatenate(piece, axis=0) if nx > 1 else piece[0]).astype(BF16)
            for half in range(2):
                acc[half][kv] = acc[half][kv] + _d(piece, wc_ref[half, l, kv])
    for scr, a in ((ua, acc[0]), (ub, acc[1])):
        full = jnp.concatenate(a, axis=1)
        for c in range(nl):
            scr[c] = full[:, c * LANES:(c + 1) * LANES]
    hid = jnp.concatenate([ua[c, pl.ds(0, rows // 2, stride=2), :] + ub[c, pl.ds(1, rows // 2, stride=2), :]
                           for c in range(nl)], axis=1)
    hid = _gelu_tanh(hid + b1_ref[...])
    o_ref[...] = jnp.concatenate([_d(hid[:, :KVW], w2_ref[0]), _d(hid[:, KVW:], w2_ref[1])], axis=1)


def _compress_dense(rows_kv, cw):
    wc, w2c, b1 = cw
    M = rows_kv.shape[0]
    x = rows_kv.reshape(M // HALF, HALF * 2 * KVW)
    nh = M // HALF
    th = _pick_tile(nh, cap=128)
    ins = [_rows(x, th), _full(wc), _full(w2c), _full(b1)]
    outs = [_rows_out(nh // 2, 2 * KVW, th // 2)]
    scratch = [pltpu.VMEM((2 * KVW // LANES, th, LANES), F32)] * 2
    return _call(functools.partial(_compress_body, 1), (nh // th,), ins, outs, scratch=scratch, name="nsa_compress")[0]


def _pair_mat(nc, ns):
    n = np.arange(nc)
    return jnp.asarray((n[:, None] // (SEL_BLK // CMP_BLK) == np.arange(ns)[None, :]).astype(np.float32), BF16)


def _top_blocks(score, n_sel):
    lane = lax.broadcasted_iota(I32, score.shape, 1)
    big = jnp.int32(1 << 20)
    taken = -3e38
    work = score
    for _ in range(n_sel):
        m = jnp.max(work, axis=-1, keepdims=True)
        idx = jnp.min(jnp.where(work == m, lane, big), axis=-1, keepdims=True)
        work = jnp.where(lane == idx, taken, work)
    return jnp.where(work == taken, 0.0, SEL_NEG)


def _masked_softmax(s, mask):
    s = jnp.where(mask, s, NEG)
    m = jnp.max(s, axis=-1, keepdims=True)
    p = jnp.where(mask, jnp.exp(s - m), 0.0)
    l = jnp.sum(p, axis=-1, keepdims=True)
    return p / jnp.where(l > 0.0, l, 1.0)


def _block_scores(imp, tpos, ns):
    blk = lax.broadcasted_iota(I32, imp.shape, 1)
    cur = tpos // SEL_BLK
    forced = (blk == 0) | (blk == cur) | (blk == cur - 1)
    score = jnp.where(forced, FORCE_SCORE, imp)
    return jnp.where(blk * SEL_BLK > tpos, -1.0, score)


CMP_NEAR_LO = -3
CMP_NEAR_N = 8


def _cmp_bias_pattern(table, tq):
    assert tq == LANES and CMP_BLK == 32 and REL_MAX_DIST == LANES
    r = jnp.arange(tq, dtype=I32)[:, None]
    m = CMP_NEAR_LO + jnp.arange(CMP_NEAR_N, dtype=I32)[None, :]
    near = _rel_bias(table, r - (CMP_BLK - 1) + CMP_BLK * m)
    far = jnp.broadcast_to(table[REL_BUCKETS - 1][:, None, None], (table.shape[1], tq, 1))
    return jnp.pad(jnp.concatenate([near, far], axis=2), ((0, 0), (0, 0), (0, LANES - CMP_NEAR_N - 1)))


def _nsa_cmp_prompt(q, kcvc, pat, B, T, tq):
    nc, ns = T // CMP_BLK, -(-T // SEL_BLK)
    n_sel = min(N_SEL, ns)
    nq = T // tq
    G = NSA_G
    pair = _pair_mat(nc, ns)
    scale = HD ** -0.5
    rb = tq // CMP_BLK

    def body(q_ref, kc_ref, vc_ref, b_ref, pair_ref, o_ref, mb_ref):
        qi = pl.program_id(1)
        q = q_ref[...]
        R = G * tq
        tpos = qi * tq + lax.broadcasted_iota(I32, (R, 1), 0) % tq
        cmp_end = lax.broadcasted_iota(I32, (R, nc), 1) * CMP_BLK + (CMP_BLK - 1)
        mask = cmp_end <= tpos
        j = lax.broadcasted_iota(I32, (LANES, nc), 0)
        m = rb * qi - lax.broadcasted_iota(I32, (LANES, nc), 1)
        sel = ((j < CMP_NEAR_N) & (m == j + CMP_NEAR_LO)) | ((j == CMP_NEAR_N) & (m >= CMP_NEAR_LO + CMP_NEAR_N))
        sel = jnp.where(sel, 1.0, 0.0).astype(BF16)
        outs = []
        for kv in range(NSA_KVH):
            qs = jnp.concatenate([q[:, (kv * G + gg) * HD:(kv * G + gg + 1) * HD] for gg in range(G)], axis=0) * scale
            s = _d_nt(qs, kc_ref[:, kv * HD:(kv + 1) * HD])
            s = s + _d_x3(jnp.concatenate([b_ref[kv * G + gg] for gg in range(G)], axis=0), sel)
            pc = _masked_softmax(s, mask)
            oc = _d(pc, vc_ref[:, kv * HD:(kv + 1) * HD])
            outs += [oc[gg * tq:(gg + 1) * tq] for gg in range(G)]
            imp = pc[0:tq]
            for gg in range(1, G):
                imp = imp + pc[gg * tq:(gg + 1) * tq]
            score = _block_scores(_d_x3(imp, pair_ref[...]), tpos[0:tq], ns)
            mb_ref[kv] = _top_blocks(score, n_sel).astype(mb_ref.dtype)
        o_ref[...] = jnp.concatenate(outs, axis=1)

    ins = [(q, pl.BlockSpec((tq, NSA_HEADS * HD), lambda b, i: (b * nq + i, 0))),
           (kcvc, pl.BlockSpec((nc, KVW), lambda b, i: (b, 0))),
           (kcvc, pl.BlockSpec((nc, KVW), lambda b, i: (b, 1))),
           (pat, pl.BlockSpec(pat.shape, lambda b, i: (0, 0, 0))),
           _full(pair)]
    outs = [((B * T, NSA_HEADS * HD), F32, pl.BlockSpec((tq, NSA_HEADS * HD), lambda b, i: (b * nq + i, 0))),
            ((B, NSA_KVH, T, ns), BF16, pl.BlockSpec((None, NSA_KVH, tq, ns), lambda b, i: (b, 0, i, 0)))]
    return _call(body, (B, nq), ins, outs, name="nsa_cmp_select")


def _gate_mats():
    hsn = _seg_np(HD)
    return [jnp.asarray(np.roll(hsn.T, br * NSA_HEADS, axis=0), BF16) for br in range(3)]


def _nsa_out(st, o_c, o_s, o_w, gates, p, alpha, ln_g, ln_b, tm):
    def pro(oc, os_, ow, gl, e0, e1, e2):
        sg = _sigmoid(gl)
        return _d_x3(sg, e0) * oc + _d_x3(sg, e1) * os_ + _d_x3(sg, e2) * ow

    def epi(acc, x, gate, g_, b_):
        return (_res_ln(alpha, acc, x, gate, g_, b_),)

    pro_ins = [_rows(o_s, tm), _rows(o_w, tm), _rows(gates, tm)] + [_full(e) for e in _gate_mats()]
    return _mm(o_c, _full(p["nsa_w_o"].astype(BF16)), tm=tm, pro=pro, pro_ins=pro_ins, epi=epi,
               epi_ins=[_rows(st.x, tm), st.mod(2, tm), _full(ln_g), _full(ln_b)],
               outs=[_rows_out(st.M, D, tm)], name="nsa_out")[0]


def _nsa_prompt(st, p, alpha, ln_g, ln_b):
    B, T, M = st.B, st.T, st.M
    tm = min(256, M)
    tq = ATT_T
    nq = T // tq
    table = p["rel_bias"]
    q, cmp_rows, slc_t, win_t, gates = _nsa_proj_t(st, p, tm)
    kcvc = _compress_dense(cmp_rows, _cmp_weights(p))
    nc, ns = T // CMP_BLK, -(-T // SEL_BLK)
    o_c, mb = _nsa_cmp_prompt(q, kcvc, _cmp_bias_pattern(table, tq), B, T, tq)
    r = jnp.arange(tq, dtype=I32)
    far = table[REL_BUCKETS - 1][:, None, None]
    tz = jnp.stack([_rel_bias(table, r[:, None] - r[None, :]) - far,
                    _rel_bias(table, tq + r[:, None] - r[None, :]) - far], axis=1)
    e_blk = jnp.asarray((np.arange(ns)[:, None] == np.arange(T)[None, :] // SEL_BLK).astype(np.float32), BF16)
    q_in = (q, pl.BlockSpec((tq, NSA_HEADS * HD), lambda b, g, i: (b * nq + i, 0)))
    kv_in = lambda a, c: (a, pl.BlockSpec((None, KVW, T), lambda b, g, i: (b, c, 0)))
    tz_in = (tz, pl.BlockSpec(tz.shape, lambda b, g, i: (0, 0, 0, 0)))
    cfg = dict(name="nsa_slc", tq=tq, hq=NSA_HEADS, hk=NSA_KVH, fox=False, bias=True, aug=ns, window=None, unroll=4,
               kt=True)
    extra = [(mb, pl.BlockSpec((None, NSA_KVH, tq, ns), lambda b, g, i: (b, 0, i, 0))),
             (e_blk, pl.BlockSpec(e_blk.shape, lambda b, g, i: (0, 0))), tz_in]
    o_s = _flash(cfg, B, T, 1, q_in, kv_in(slc_t, 0), kv_in(slc_t, 1), extra, M)
    cfg = dict(name="nsa_win", tq=tq, hq=NSA_HEADS, hk=NSA_KVH, fox=False, bias=True, aug=0, window=WINDOW, unroll=1,
               kt=True)
    o_w = _flash(cfg, B, T, 1, q_in, kv_in(win_t, 0), kv_in(win_t, 1), [tz_in], M)
    st.x = _nsa_out(st, o_c, o_s, o_w, gates, p, alpha, ln_g, ln_b, tm)
    keep = min(WINDOW, T)
    rows = lambda t: t.reshape(B, 2, NSA_KVH, HD, t.shape[-1]).transpose(0, 4, 1, 2, 3)
    return cmp_rows.reshape(B, T, 2, NSA_KVH, HD), rows(slc_t), rows(win_t[:, :, T - keep:])


def _compress_paged(cache_cmp, page_table, cw):
    wc, w2c, b1 = cw
    B, npg = page_table.shape
    PAGE = cache_cmp.shape[1]
    bpp = PAGE // CMP_BLK
    cache = cache_cmp.transpose(0, 2, 3, 4, 1).reshape(cache_cmp.shape[0], 2, KVW, PAGE)
    nb = 2 if B % 2 == 0 else 1
    npages = nb * npg
    nblocks = npages * bpp
    npan = KVW // LANES

    def body(pt_ref, *refs):
        pages = refs[:npages]
        wc_ref, w2_ref, b1_ref, o_ref, xs = refs[npages:]
        for j in range(npages):
            for kv in range(2):
                x = pages[j][kv].T
                for c in range(npan):
                    xs[kv, c, j * PAGE:(j + 1) * PAGE, :] = x[:, c * LANES:(c + 1) * LANES]
        acc = [jnp.zeros((nblocks, KVW), F32) for _ in range(2)]
        for l in range(CMP_BLK):
            for kv in range(2):
                rows = jnp.concatenate([xs[kv, c, pl.ds(l, nblocks, stride=CMP_BLK), :] for c in range(npan)], axis=1)
                acc[kv] = acc[kv] + _d(rows, wc_ref[l // HALF, l % HALF, kv])
        hid = _gelu_tanh(jnp.concatenate(acc, axis=1) + b1_ref[...])
        o_ref[...] = jnp.concatenate([_d(hid[:, :KVW], w2_ref[0]), _d(hid[:, KVW:], w2_ref[1])], axis=1)

    ins = [(cache, pl.BlockSpec((None, 2, KVW, PAGE), lambda g, pt, bb=bb, j=j: (pt[g * nb + bb, j], 0, 0, 0)))
           for bb in range(nb) for j in range(npg)]
    const = lambda a: (a, pl.BlockSpec(a.shape, lambda g, pt: (0,) * a.ndim))
    ins += [const(wc), const(w2c), const(b1)]
    outs = [((B * npg * bpp, 2 * KVW), F32, pl.BlockSpec((nblocks, 2 * KVW), lambda g, pt: (g, 0)))]
    scratch = [pltpu.VMEM((2, npan, npages * PAGE, LANES), F32)]
    return _call(body, (B // nb,), ins, outs, scratch=scratch, prefetch=[page_table], name="nsa_compress_paged")[0]


def _nsa_decode(q, slc_new, win_new, kcvc, cache_slc, cache_win, page_table, gcol, consts, Tn, offset):
    B, npg = page_table.shape
    PAGE = cache_slc.shape[1]
    Wb = cache_win.shape[1]
    R = NSA_HEADS * Tn
    L = offset + Tn
    nc, ns = L // CMP_BLK, -(-L // SEL_BLK)
    n_sel = min(N_SEL, ns)
    nck = npg + 1
    qw = NSA_HEADS * HD
    cache = cache_slc.transpose(0, 2, 3, 4, 1).reshape(cache_slc.shape[0], 2, KVW, PAGE)
    win = cache_win.transpose(0, 2, 3, 4, 1).reshape(B, 2, KVW, Wb)
    scale = HD ** -0.5
    names = ["rep_t", "fold", "unfold", "bias_c", "mask_c", "pair", "e_blk", "bias_s", "bias_w"]
    cvals = [consts[n] for n in names]

    nb = 2 if B % 2 == 0 else 1

    def body(pt_ref, *refs):
        all_pages = refs[:nb * npg]
        (q_ref, sn_ref, wn_ref, kc_ref, vc_ref, win_ref, g_ref, rt_ref, fold_ref, unfold_ref, bc_ref, mc_ref, pair_ref,
         e_ref, bs_ref, bw_ref, o_ref, wout_ref) = refs[nb * npg:]
        for bi in range(nb):
            one_sequence(bi, all_pages[bi * npg:(bi + 1) * npg], q_ref, sn_ref, wn_ref, kc_ref, vc_ref, win_ref, g_ref,
                         rt_ref, fold_ref, unfold_ref, bc_ref, mc_ref, pair_ref, e_ref, bs_ref, bw_ref, o_ref, wout_ref)

    def one_sequence(bi, pages, q_ref, sn_ref, wn_ref, kc_ref, vc_ref, win_ref, g_ref, rt_ref, fold_ref, unfold_ref,
                     bc_ref, mc_ref, pair_ref, e_ref, bs_ref, bw_ref, o_ref, wout_ref):
        rows = slice(bi * Tn, (bi + 1) * Tn)
        row_h = lax.broadcasted_iota(I32, (R, qw), 0) // Tn
        lane_h = lax.broadcasted_iota(I32, (R, qw), 1) // HD
        own = row_h == lane_h
        qrep = _d(rt_ref[...], q_ref[rows, :])
        qbd = (_d(jnp.where(own, qrep, 0.0), fold_ref[...]) * scale).astype(BF16)
        pad = lambda x: jnp.concatenate([x, jnp.zeros((PAGE - Tn, x.shape[1]), x.dtype)], axis=0)

        cblk = slice(bi * nc, (bi + 1) * nc)
        pc = _masked_softmax(_d_nt(qbd, kc_ref[cblk, :]) + bc_ref[...], mc_ref[...] > 0.0)
        o_c = _d(pc, vc_ref[cblk, :])
        imp = []
        for kv in range(NSA_KVH):
            a = pc[kv * NSA_G * Tn:(kv * NSA_G + 1) * Tn]
            for gg in range(1, NSA_G):
                a = a + pc[(kv * NSA_G + gg) * Tn:(kv * NSA_G + gg + 1) * Tn]
            imp.append(a)
        imp = jnp.concatenate(imp, axis=0)
        tpos = offset + lax.broadcasted_iota(I32, (NSA_KVH * Tn, 1), 0) % Tn
        mb = _top_blocks(_block_scores(_d_x3(imp, pair_ref[...]), tpos, ns), n_sel)
        mb = jnp.concatenate([mb[kv * Tn:(kv + 1) * Tn] for kv in range(NSA_KVH) for _ in range(NSA_G)], axis=0)

        sn = pad(sn_ref[rows, :])
        past = npg * PAGE
        kt = jnp.concatenate([pages[j][0].astype(BF16) for j in range(npg)], axis=1)
        vt = jnp.concatenate([pages[j][1].astype(BF16) for j in range(npg)], axis=1)
        s = jnp.concatenate([_d(qbd, kt), _d_nt(qbd, sn[:, :KVW])], axis=1) + _d(mb, e_ref[...]) + bs_ref[...]
        p = jnp.exp(s - jnp.max(s, axis=-1, keepdims=True))
        o_s = (_d_nt(p[:, :past], vt) + _d(p[:, past:], sn[:, KVW:])) / jnp.sum(p, axis=-1, keepdims=True)

        wk, wv = win_ref[bi, 0], win_ref[bi, 1]
        wn = pad(wn_ref[rows, :])
        s = jnp.concatenate([_d(qbd, wk), _d_nt(qbd, wn[:, :KVW])], axis=1) + bw_ref[...]
        p = jnp.exp(s - jnp.max(s, axis=-1, keepdims=True))
        o_w = (_d_nt(p[:, :Wb], wv) + _d(p[:, Wb:], wn[:, KVW:])) / jnp.sum(p, axis=-1, keepdims=True)
        lane = lax.broadcasted_iota(I32, (KVW, Wb), 1)
        for kv, old in enumerate((wk, wv)):
            nt = pltpu.roll(wn[:, kv * KVW:(kv + 1) * KVW].T, PAGE - Tn, axis=1)
            nt = jnp.concatenate([jnp.zeros((KVW, Wb - PAGE), F32), nt], axis=1)
            wout_ref[bi, kv] = jnp.where(lane >= Wb - Tn, nt, pltpu.roll(old, Wb - Tn, axis=1))

        sg = _sigmoid(g_ref[bi])
        o = sg[:, 0:1] * o_c + sg[:, 1:2] * o_s + sg[:, 2:3] * o_w
        of = jnp.where(own, _d_x3(o, unfold_ref[...]), 0.0)
        out = of[0:Tn]
        for h in range(1, NSA_HEADS):
            out = out + of[h * Tn:(h + 1) * Tn]
        o_ref[rows, :] = out

    c2 = lambda b, pt: (0, 0)
    ins = [(cache, pl.BlockSpec((None, 2, KVW, PAGE), lambda b, pt, bb=bb, j=j: (pt[b * nb + bb, j], 0, 0, 0)))
           for bb in range(nb) for j in range(npg)]
    ins += [(q, pl.BlockSpec((nb * Tn, qw), lambda b, pt: (b, 0))),
            (slc_new, pl.BlockSpec((nb * Tn, 2 * KVW), lambda b, pt: (b, 0))),
            (win_new, pl.BlockSpec((nb * Tn, 2 * KVW), lambda b, pt: (b, 0))),
            (kcvc, pl.BlockSpec((nb * nc, KVW), lambda b, pt: (b, 0))),
            (kcvc, pl.BlockSpec((nb * nc, KVW), lambda b, pt: (b, 1))),
            (win, pl.BlockSpec((nb, 2, KVW, Wb), lambda b, pt: (b, 0, 0, 0))),
            (gcol, pl.BlockSpec((nb, R, 3), lambda b, pt: (b, 0, 0)))]
    ins += [(a, pl.BlockSpec(a.shape, c2)) for a in cvals]
    outs = [((B * Tn, qw), F32, pl.BlockSpec((nb * Tn, qw), lambda b, pt: (b, 0))),
            ((B, 2, KVW, Wb), F32, pl.BlockSpec((nb, 2, KVW, Wb), lambda b, pt: (b, 0, 0, 0)))]
    return _call(body, (B // nb,), ins, outs, prefetch=[page_table], name="nsa_decode")


def _nsa_decode_consts(table, Tn, offset, npg, PAGE, Wb):
    R = NSA_HEADS * Tn
    L = offset + Tn
    nc, ns = L // CMP_BLK, -(-L // SEL_BLK)
    Lp = (npg + 1) * PAGE
    tpos = offset + jnp.arange(Tn, dtype=I32)
    rows = lambda x: x.reshape(R, x.shape[-1])
    cmp_end = jnp.arange(nc, dtype=I32) * CMP_BLK + CMP_BLK - 1
    dist_c = tpos[:, None] - cmp_end[None, :]
    spos = jnp.arange(Lp, dtype=I32)
    dist_s = tpos[:, None] - spos[None, :]
    ok_s = (dist_s >= 0) & (spos[None, :] < L)
    col = jnp.arange(Wb + PAGE, dtype=I32)
    wpos = offset - Wb + col
    dist_w = tpos[:, None] - wpos[None, :]
    ok_w = (dist_w >= 0) & (dist_w < WINDOW) & (wpos[None, :] >= 0) & (col[None, :] < Wb + Tn)
    tile = lambda m: jnp.tile(m[None], (NSA_HEADS, 1, 1))
    fold = np.zeros((NSA_HEADS, HD, NSA_KVH, HD), np.float32)
    for h in range(NSA_HEADS):
        fold[h, :, h // NSA_G, :] = np.eye(HD)
    fold = fold.reshape(NSA_HEADS * HD, KVW)
    return dict(
        rep_t=_rep_mat(R, Tn, 0), fold=jnp.asarray(fold, BF16), unfold=jnp.asarray(fold.T, BF16),
        bias_c=rows(_rel_bias(table, dist_c)), mask_c=rows(tile((dist_c >= 0).astype(F32))), pair=_pair_mat(nc, ns),
        e_blk=jnp.asarray((np.arange(ns)[:, None] == np.arange(Lp)[None, :] // SEL_BLK).astype(np.float32), BF16),
        bias_s=rows(_rel_bias(table, dist_s) + tile(jnp.where(ok_s, 0.0, NEG))),
        bias_w=rows(_rel_bias(table, dist_w) + tile(jnp.where(ok_w, 0.0, NEG))))


def _nsa_sample(st, cache_cmp, cache_slc, cache_win, page_table, p, alpha, ln_g, ln_b):
    B, Tn, M = st.B, st.T, st.M
    tm = min(256, M)
    npg = page_table.shape[1]
    PAGE = cache_slc.shape[1]
    Wb = cache_win.shape[1]
    offset = npg * PAGE
    assert offset % CMP_BLK == 0 and Tn < CMP_BLK and Wb == WINDOW and Tn % SUBLANES == 0
    q, cmp_rows, slc_rows, win_rows, gates = _nsa_proj(st, p, tm)
    kcvc = _compress_paged(cache_cmp, page_table, _cmp_weights(p))
    gcol = gates.reshape(B, Tn, LANES)[:, :, :3 * NSA_HEADS].reshape(B, Tn, 3, NSA_HEADS)
    gcol = gcol.transpose(0, 3, 1, 2).reshape(B, NSA_HEADS * Tn, 3)
    consts = _nsa_decode_consts(p["rel_bias"], Tn, offset, npg, PAGE, Wb)
    o, wout = _nsa_decode(q, slc_rows, win_rows, kcvc, cache_slc, cache_win, page_table, gcol, consts, Tn, offset)
    st.x = _out_proj(st, o, p["nsa_w_o"], alpha, ln_g, ln_b, tm, "nsa_out_s")
    shp = (B, Tn, 2, NSA_KVH, HD)
    win_keep = wout.reshape(B, 2, NSA_KVH, HD, Wb).transpose(0, 4, 1, 2, 3)
    return cmp_rows.reshape(shp), slc_rows.reshape(shp), win_keep


def kernel(x_prompt, x_sample, state_rwkv_wkv, state_rwkv_shift, cache_nsa_cmp, cache_nsa_slc, cache_nsa_win, cache_fox_kv, cache_fox_logf, state_gdn_S, state_gdn_conv, page_table, c_prompt, c_sample, w_mod, b_mod, ln_g, ln_b, moe_w_group, moe_b_group, moe_w_router, moe_b_router, moe_w1, moe_w3, moe_w2, rwkv_mu, rwkv_w_rkv, rwkv_w0, rwkv_w1, rwkv_w2, rwkv_a0, rwkv_a1, rwkv_a2, rwkv_g1, rwkv_g2, rwkv_k_k, rwkv_k_a, rwkv_r_k, rwkv_ln_w, rwkv_ln_b, rwkv_w_o, nsa_w_in, nsa_cmp_w1, nsa_cmp_b1, nsa_cmp_w2, nsa_w_o, rel_bias, fox_w_in, fox_b_f, fox_w_o, gdn_w_in, gdn_conv_w, gdn_A_log, gdn_dt_bias, gdn_norm_w, gdn_w_o):
    p = dict(locals())
    Bp, T, _ = x_prompt.shape
    Bs, Tn, _ = x_sample.shape
    depth = w_mod.shape[0]
    alpha = (2 * depth) ** 0.25
    sp = _Stream(x_prompt.reshape(Bp * T, D), Bp, T, min(512, T))
    ss = _Stream(x_sample.reshape(Bs * Tn, D), Bs, Tn, min(256, Bs * Tn))
    nc = Bp + Bs
    c_all = jnp.pad(jnp.concatenate([c_prompt, c_sample], axis=0), ((0, -nc % SUBLANES), (0, 0)))
    out = {}
    for layer in range(depth):
        m6 = _ada(c_all, w_mod, b_mod, layer)
        sp.set_mods(m6[:Bp])
        ss.set_mods(m6[Bp:nc])
        g0 = ln_g[layer, 0].reshape(1, D)
        b0 = ln_b[layer, 0].reshape(1, D)
        kind = layer % 4
        if kind == 0:
            nh = D // RWKV_HSZ
            out["wkv_p"], out["shift_p"] = _rwkv_layer(sp, jnp.zeros((Bp, D), F32),
                                                       jnp.zeros((Bp, nh, RWKV_HSZ, RWKV_HSZ), F32), p, alpha, g0, b0)
            out["wkv_s"], out["shift_s"] = _rwkv_layer(ss, state_rwkv_shift, state_rwkv_wkv, p, alpha, g0, b0)
        elif kind == 1:
            out["cmp_p"], out["slc_p"], out["win_p"] = _nsa_prompt(sp, p, alpha, g0, b0)
            out["cmp_s"], out["slc_s"], out["win_s"] = _nsa_sample(ss, cache_nsa_cmp, cache_nsa_slc, cache_nsa_win,
                                                                   page_table, p, alpha, g0, b0)
        elif kind == 2:
            out["kv_p"], out["logf_p"] = _fox_prompt(sp, p, alpha, g0, b0)
            out["kv_s"], out["logf_s"] = _fox_sample(ss, cache_fox_kv, cache_fox_logf, page_table, p, alpha, g0, b0)
        else:
            out["S_p"], out["conv_p"] = _gdn_layer(sp, jnp.zeros((Bp, GDN_CONV - 1, 3 * D), F32),
                                                   jnp.zeros((Bp, GDN_HEADS, GDN_HSZ, GDN_HSZ), F32), p, alpha, g0, b0)
            out["S_s"], out["conv_s"] = _gdn_layer(ss, state_gdn_conv, state_gdn_S, p, alpha, g0, b0)
        _moe_layer([sp, ss], layer, alpha, p)
    return (sp.x.reshape(Bp, T, D), ss.x.reshape(Bs, Tn, D), out["wkv_p"], out["wkv_s"], out["shift_p"], out["shift_s"],
            out["cmp_p"], out["cmp_s"], out["slc_p"], out["slc_s"], out["win_p"], out["win_s"],
            out["kv_p"], out["kv_s"], out["logf_p"], out["logf_s"], out["S_p"], out["S_s"], out["conv_p"], out["conv_s"])
```
